```python
import jax
import jax.numpy as jnp
from jax import lax
import numpy as np

D_MODEL = 1024
BATCH = 8
SEQ = 4096
DEPTH = 4

MEM_LEN = 256
ROPE_THETA = 10000.0
MAX_POS_OFFSET = 1024

RET_HEADS = 4
RET_DK = 64
RET_DV = 64
RET_CHUNK = 128
RET_W = RET_HEADS * RET_DV

SGU_GROUPS = 4
SGU_DIM = 64
SGU_CHUNK = 128
SGU_W = SGU_GROUPS * SGU_DIM

MLA_HEADS = 8
MLA_Q_RANK = 256
MLA_KV_RANK = 128
MLA_NOPE = 64
MLA_ROPE = 32
MLA_V = 64
MLA_W = MLA_HEADS * MLA_V
ATTN_BLOCK = 128

D_MIX = RET_W + SGU_W + MLA_W
IN_COLS = 4 * RET_W + 2 * SGU_W + MLA_Q_RANK + MLA_KV_RANK + MLA_ROPE

XA_HEADS = 4
XA_DIM = 128
XA_W = XA_HEADS * XA_DIM

N_EXPERTS = 16
N_GROUPS = 4
EXPERTS_PER_GROUP = N_EXPERTS // N_GROUPS
TOP_K = 2
D_EXPERT = 256

DEEPNORM_ALPHA = (2 * DEPTH) ** 0.25
DEEPNORM_BETA = (8 * DEPTH) ** -0.25
LN_EPS = 1e-5

kernel_name = 'hybrid_retention_sgu_mla_moe'


def layer_norm(x, g, b):
    xf = x.astype(jnp.float32)
    mu = jnp.mean(xf, axis=-1, keepdims=True)
    var = jnp.mean(jnp.square(xf - mu), axis=-1, keepdims=True)
    return ((xf - mu) * lax.rsqrt(var + LN_EPS)).astype(x.dtype) * g + b


def rms_norm(x, g):
    xf = x.astype(jnp.float32)
    ms = jnp.mean(jnp.square(xf), axis=-1, keepdims=True)
    return (xf * lax.rsqrt(ms + LN_EPS)).astype(x.dtype) * g


def head_group_norm(x):
    xf = x.astype(jnp.float32)
    mu = jnp.mean(xf, axis=-1, keepdims=True)
    var = jnp.mean(jnp.square(xf - mu), axis=-1, keepdims=True)
    return ((xf - mu) * lax.rsqrt(var + LN_EPS)).astype(x.dtype)


def rope_cos_sin(positions, dim):
    inv_freq = ROPE_THETA ** (-jnp.arange(0, dim, 2, dtype=jnp.float32) / dim)
    ang = positions.astype(jnp.float32)[..., None] * inv_freq
    return jnp.cos(ang)[:, :, None, :], jnp.sin(ang)[:, :, None, :]


def apply_rope(x, cos, sin):
    x1, x2 = jnp.split(x, 2, axis=-1)
    cos = cos.astype(x.dtype)
    sin = sin.astype(x.dtype)
    return jnp.concatenate([x1 * cos - x2 * sin, x1 * sin + x2 * cos], axis=-1)


def retention_chunkwise(q, k, v):
    b, s, h, dk = q.shape
    dv = v.shape[-1]
    c = RET_CHUNK
    nc = s // c
    log_gamma = jnp.log1p(-(2.0 ** (-5.0 - jnp.arange(h, dtype=jnp.float32))))
    pos = jnp.arange(c, dtype=jnp.float32)
    diff = pos[:, None] - pos[None, :]
    intra_decay = jnp.where(diff >= 0,
                            jnp.exp(log_gamma[:, None, None] * jnp.maximum(diff, 0.0)),
                            0.0).astype(q.dtype)
    inner_decay = jnp.exp(log_gamma[None, :] * (c - 1 - pos)[:, None]).astype(q.dtype)
    query_decay = jnp.exp(log_gamma[None, :] * (pos + 1)[:, None]).astype(q.dtype)
    chunk_decay = jnp.exp(log_gamma * c)
    qc = q.reshape(b, nc, c, h, dk)
    kc = (k * dk ** -0.5).reshape(b, nc, c, h, dk)
    vc = v.reshape(b, nc, c, h, dv)
    scores = jnp.einsum('bcnhd,bcmhd->bchnm', qc, kc) * intra_decay
    intra = jnp.einsum('bchnm,bcmhv->bcnhv', scores, vc)
    chunk_kv = jnp.einsum('bcmhd,mh,bcmhv->bchdv', kc, inner_decay, vc)

    def step(state, kv_c):
        return state * chunk_decay[:, None, None].astype(state.dtype) + kv_c, state

    _, prev = lax.scan(step, jnp.zeros_like(chunk_kv[:, 0]), jnp.moveaxis(chunk_kv, 1, 0))
    prev = jnp.moveaxis(prev, 0, 1)
    cross = jnp.einsum('bcnhd,nh,bchdv->bcnhv', qc, query_decay, prev)
    return (intra + cross).reshape(b, s, h, dv)


def spatial_gating(u, v, w_s, b_s):
    b, s, _ = u.shape
    c = SGU_CHUNK
    nc = s // c
    causal = jnp.tril(jnp.ones((c, c), dtype=bool))
    w = jnp.where(causal[None], w_s, 0.0)
    vc = v.reshape(b, nc, c, SGU_GROUPS, SGU_DIM)
    mixed = jnp.einsum('gts,bcsgd->bctgd', w, vc) + b_s.T[:, :, None]
    return u * mixed.reshape(b, s, SGU_W)


def causal_block_attention(q, k, v, scale):
    b, s, h, d = q.shape
    dv = v.shape[-1]
    nb = s // ATTN_BLOCK
    qb = jnp.moveaxis(q.reshape(b, nb, ATTN_BLOCK, h, d), 1, 0)
    key_pos = jnp.arange(s)

    def one_block(args):
        q_blk, i = args
        sc = jnp.einsum('bqhd,bkhd->bhqk', q_blk, k).astype(jnp.float32) * scale
        q_pos = i * ATTN_BLOCK + jnp.arange(ATTN_BLOCK)
        mask = key_pos[None, :] <= q_pos[:, None]
        p = jax.nn.softmax(jnp.where(mask, sc, -jnp.inf), axis=-1).astype(v.dtype)
        return jnp.einsum('bhqk,bkhd->bqhd', p, v)

    out = lax.map(one_block, (qb, jnp.arange(nb)))
    return jnp.moveaxis(out, 0, 1).reshape(b, s, h, dv)


def latent_attention(c_q, c_kv, k_rope, cos_m, sin_m, q_norm_g, w_uq, kv_norm_g, w_ukv):
    b, s, _ = c_q.shape
    q = (rms_norm(c_q, q_norm_g) @ w_uq).reshape(b, s, MLA_HEADS, MLA_NOPE + MLA_ROPE)
    q_nope, q_rope = jnp.split(q, [MLA_NOPE], axis=-1)
    q = jnp.concatenate([q_nope, apply_rope(q_rope, cos_m, sin_m)], axis=-1)
    kv = (rms_norm(c_kv, kv_norm_g) @ w_ukv).reshape(b, s, MLA_HEADS, MLA_NOPE + MLA_V)
    k_nope, v = jnp.split(kv, [MLA_NOPE], axis=-1)
    k_rope = apply_rope(k_rope[:, :, None, :], cos_m, sin_m)
    k = jnp.concatenate([k_nope, jnp.broadcast_to(k_rope, (b, s, MLA_HEADS, MLA_ROPE))], axis=-1)
    o = causal_block_attention(q, k, v, (MLA_NOPE + MLA_ROPE) ** -0.5)
    return o.reshape(b, s, MLA_W)


def hybrid_mixer(x, cos_r, sin_r, cos_m, sin_m, w_in, w_out, sgu_ln_g, sgu_ln_b, sgu_w, sgu_b,
                 q_norm_g, w_uq, kv_norm_g, w_ukv):
    b, s, _ = x.shape
    sizes = (RET_W, RET_W, RET_W, RET_W, SGU_W, SGU_W, MLA_Q_RANK, MLA_KV_RANK, MLA_ROPE)
    split_points = [int(p) for p in np.cumsum(sizes)[:-1]]
    rq, rk, rv, rg, su, sv, cq, ckv, kr = jnp.split(x @ w_in, split_points, axis=-1)
    rq = apply_rope(rq.reshape(b, s, RET_HEADS, RET_DK), cos_r, sin_r)
    rk = apply_rope(rk.reshape(b, s, RET_HEADS, RET_DK), cos_r, sin_r)
    rv = rv.reshape(b, s, RET_HEADS, RET_DV)
    ret = head_group_norm(retention_chunkwise(rq, rk, rv)).reshape(b, s, RET_W)
    ret = jax.nn.silu(rg) * ret
    sgu = spatial_gating(su, layer_norm(sv, sgu_ln_g, sgu_ln_b), sgu_w, sgu_b)
    mla = latent_attention(cq, ckv, kr, cos_m, sin_m, q_norm_g, w_uq, kv_norm_g, w_ukv)
    return jnp.concatenate([ret, sgu, mla], axis=-1) @ w_out


def memory_cross_attention(x, mem, wq, wk, wv, wo):
    b, s, _ = x.shape
    m = mem.shape[1]
    q = (x @ wq).reshape(b, s, XA_HEADS, XA_DIM)
    k = (mem @ wk).reshape(b, m, XA_HEADS, XA_DIM)
    v = (mem @ wv).reshape(b, m, XA_HEADS, XA_DIM)
    sc = jnp.einsum('bshd,bmhd->bhsm', q, k).astype(jnp.float32) * XA_DIM ** -0.5
    p = jax.nn.softmax(sc, axis=-1).astype(v.dtype)
    o = jnp.einsum('bhsm,bmhd->bshd', p, v).reshape(b, s, XA_W)
    return o @ wo


def grouped_moe(x, router_w, router_bias, w_gate, w_up, w_down):
    b, s, d = x.shape
    t = x.reshape(b * s, d)
    scores = jax.nn.sigmoid((t @ router_w).astype(jnp.float32))
    biased = (scores + router_bias.astype(jnp.float32)).reshape(-1, N_GROUPS, EXPERTS_PER_GROUP)
    group_score = jnp.sum(lax.top_k(biased, TOP_K)[0], axis=-1)
    sel_group = jnp.argmax(group_score, axis=-1)
    in_group = jnp.take_along_axis(biased, sel_group[:, None, None], axis=1)[:, 0]
    _, local_idx = lax.top_k(in_group, TOP_K)
    expert_idx = sel_group[:, None] * EXPERTS_PER_GROUP + local_idx
    gate = jnp.take_along_axis(scores, expert_idx, axis=1)
    gate = gate / jnp.sum(gate, axis=-1, keepdims=True)
    combine = jnp.sum(jax.nn.one_hot(expert_idx, N_EXPERTS, dtype=jnp.float32) * gate[..., None],
                      axis=1).astype(t.dtype)
    y = jnp.zeros_like(t)
    for e in range(N_EXPERTS):
        h = jax.nn.silu(t @ w_gate[e]) * (t @ w_up[e])
        y = y + combine[:, e:e + 1] * (h @ w_down[e])
    return y.reshape(b, s, d)


def setup_inputs(seed: int = 0) -> dict:
    key = jax.random.key(seed)
    ks = iter(jax.random.split(key, 40))

    def nrm(shape, scale):
        return jax.random.normal(next(ks), shape, jnp.float32) * scale

    L = DEPTH
    x = nrm((BATCH, SEQ, D_MODEL), 1.0)
    mem = nrm((BATCH, MEM_LEN, D_MODEL), 1.0)
    offset = jax.random.randint(next(ks), (BATCH, 1), 0, MAX_POS_OFFSET, dtype=jnp.int32)
    positions = offset + jnp.arange(SEQ, dtype=jnp.int32)[None, :]
    return {
        'x': x,
        'mem': mem,
        'positions': positions,
        'w_in': nrm((L, D_MODEL, IN_COLS), D_MODEL ** -0.5),
        'w_out': nrm((L, D_MIX, D_MODEL), DEEPNORM_BETA * D_MIX ** -0.5),
        'sgu_ln_g': 1.0 + nrm((L, SGU_W), 0.01),
        'sgu_ln_b': nrm((L, SGU_W), 0.01),
        'sgu_w': nrm((L, SGU_GROUPS, SGU_CHUNK, SGU_CHUNK), SGU_CHUNK ** -0.5),
        'sgu_b': 1.0 + nrm((L, SGU_GROUPS, SGU_CHUNK), 0.01),
        'mla_q_norm_g': 1.0 + nrm((L, MLA_Q_RANK), 0.01),
        'mla_w_uq': nrm((L, MLA_Q_RANK, MLA_HEADS * (MLA_NOPE + MLA_ROPE)), MLA_Q_RANK ** -0.5),
        'mla_kv_norm_g': 1.0 + nrm((L, MLA_KV_RANK), 0.01),
        'mla_w_ukv': nrm((L, MLA_KV_RANK, MLA_HEADS * (MLA_NOPE + MLA_V)), MLA_KV_RANK ** -0.5),
        'xa_wq': nrm((L, D_MODEL, XA_W), D_MODEL ** -0.5),
        'xa_wk': nrm((L, D_MODEL, XA_W), D_MODEL ** -0.5),
        'xa_wv': nrm((L, D_MODEL, XA_W), D_MODEL ** -0.5),
        'xa_wo': nrm((L, XA_W, D_MODEL), DEEPNORM_BETA * XA_W ** -0.5),
        'ln_mix_g': 1.0 + nrm((L, D_MODEL), 0.01),
        'ln_mix_b': nrm((L, D_MODEL), 0.01),
        'ln_xa_g': 1.0 + nrm((L, D_MODEL), 0.01),
        'ln_xa_b': nrm((L, D_MODEL), 0.01),
        'ln_moe_g': 1.0 + nrm((L, D_MODEL), 0.01),
        'ln_moe_b': nrm((L, D_MODEL), 0.01),
        'router_w': nrm((D_MODEL, N_EXPERTS), D_MODEL ** -0.5),
        'router_bias': nrm((N_EXPERTS,), 0.01),
        'expert_w_gate': nrm((L, N_EXPERTS, D_MODEL, D_EXPERT), D_MODEL ** -0.5),
        'expert_w_up': nrm((L, N_EXPERTS, D_MODEL, D_EXPERT), D_MODEL ** -0.5),
        'expert_w_down': nrm((L, N_EXPERTS, D_EXPERT, D_MODEL), DEEPNORM_BETA * D_EXPERT ** -0.5),
    }


def reference(x, mem, positions, w_in, w_out, sgu_ln_g, sgu_ln_b, sgu_w, sgu_b,
              mla_q_norm_g, mla_w_uq, mla_kv_norm_g, mla_w_ukv,
              xa_wq, xa_wk, xa_wv, xa_wo,
              ln_mix_g, ln_mix_b, ln_xa_g, ln_xa_b, ln_moe_g, ln_moe_b,
              router_w, router_bias, expert_w_gate, expert_w_up, expert_w_down):
    cos_r, sin_r = rope_cos_sin(positions, RET_DK)
    cos_m, sin_m = rope_cos_sin(positions, MLA_ROPE)
    for l in range(DEPTH):
        mix = hybrid_mixer(x, cos_r, sin_r, cos_m, sin_m, w_in[l], w_out[l],
                           sgu_ln_g[l], sgu_ln_b[l], sgu_w[l], sgu_b[l],
                           mla_q_norm_g[l], mla_w_uq[l], mla_kv_norm_g[l], mla_w_ukv[l])
        x = layer_norm(DEEPNORM_ALPHA * x + mix, ln_mix_g[l], ln_mix_b[l])
        xa = memory_cross_attention(x, mem, xa_wq[l], xa_wk[l], xa_wv[l], xa_wo[l])
        x = layer_norm(DEEPNORM_ALPHA * x + xa, ln_xa_g[l], ln_xa_b[l])
        ffn = grouped_moe(x, router_w, router_bias, expert_w_gate[l], expert_w_up[l], expert_w_down[l])
        x = layer_norm(DEEPNORM_ALPHA * x + ffn, ln_moe_g[l], ln_moe_b[l])
    return x
```

```python
import functools
import math

import jax
import jax.numpy as jnp
from jax import lax
from jax.experimental import pallas as pl
from jax.experimental.pallas import tpu as pltpu

F32 = jnp.float32
BF16 = jnp.bfloat16

D_MODEL = 1024
DEPTH = 4
MEM_LEN = 256
ROPE_THETA = 10000.0

RET_HEADS = 4
RET_DK = 64
RET_W = 256
CHUNK = 128

SGU_GROUPS = 4
SGU_W = 256

MLA_HEADS = 8
MLA_Q_RANK = 256
MLA_KV_RANK = 128
MLA_NOPE = 64
MLA_ROPE = 32
MLA_V = 64
MLA_W = MLA_HEADS * MLA_V
HEAD_PAD = 128

XA_HEADS = 4
XA_DIM = 128
XA_W = XA_HEADS * XA_DIM

N_EXPERTS = 16
N_GROUPS = 4
EXPERTS_PER_GROUP = 4
D_EXPERT = 256

ALPHA = (2 * DEPTH) ** 0.25
LN_EPS = 1e-5
IN_PAD = 2048
NEG_BIG = -1e30

VMEM_LIMIT = 56 * 1024 * 1024


def _cparams(*sem):
    return pltpu.CompilerParams(dimension_semantics=sem, vmem_limit_bytes=VMEM_LIMIT)


def _dot(a, b):
    return jnp.dot(a, b, preferred_element_type=F32)


def _dot_nt(a, b):
    return lax.dot_general(a, b, (((1,), (1,)), ((), ())), preferred_element_type=F32)


def _dot_tn(a, b):
    return lax.dot_general(a, b, (((0,), (0,)), ((), ())), preferred_element_type=F32)


def _layer_norm(z, g, b):
    mu = jnp.mean(z, axis=-1, keepdims=True)
    zc = z - mu
    var = jnp.mean(zc * zc, axis=-1, keepdims=True)
    return zc * lax.rsqrt(var + LN_EPS) * g + b


def _rms_norm(z, g):
    ms = jnp.mean(z * z, axis=-1, keepdims=True)
    return z * lax.rsqrt(ms + LN_EPS) * g


def _silu(z):
    return z / (1.0 + jnp.exp(-z))


def _rope(x, cos, sin_signed, half):
    w = x.shape[-1]
    lane = lax.broadcasted_iota(jnp.int32, x.shape, 1)
    rot = jnp.where((lane & half) == 0, pltpu.roll(x, w - half, 1), pltpu.roll(x, half, 1))
    return x * cos + rot * sin_signed


def _split_bf16(x):
    hi = x.astype(BF16)
    lo = (x - hi.astype(F32)).astype(BF16)
    return hi, lo


def _tables_kernel(pos_ref, invf_ref, cr_ref, sr_ref, cm_ref, sm_ref):
    ang = pos_ref[...].astype(F32) * invf_ref[...]
    c = jnp.cos(ang)
    s = jnp.sin(ang)
    lane = lax.broadcasted_iota(jnp.int32, c.shape, 1)

    def tile32(v):
        v0 = jnp.where(lane < 32, v, 0.0)
        v1 = v0 + pltpu.roll(v0, 32, 1)
        return v1 + pltpu.roll(v1, 64, 1)

    ct = tile32(c)
    st = tile32(s) * jnp.where((lane & 32) == 0, -1.0, 1.0)
    cr_ref[...] = jnp.concatenate([ct, ct], axis=1)
    sr_ref[...] = jnp.concatenate([st, st], axis=1)
    in_src = (lane >= 32) & (lane < 48)
    cmv = jnp.where(in_src, c, 0.0)
    smv = jnp.where(in_src, s, 0.0)
    in_dst = (lane >= 64) & (lane < 96)
    cm_ref[...] = jnp.where(in_dst, pltpu.roll(cmv, 32, 1) + pltpu.roll(cmv, 48, 1), 1.0)
    sm_ref[...] = jnp.where(in_dst, pltpu.roll(smv, 48, 1) - pltpu.roll(smv, 32, 1), 0.0)


def _rope_tables(positions, tm):
    t = positions.size
    pos = positions.reshape(t, 1)
    fr = ROPE_THETA ** (-jnp.arange(0, RET_DK, 2, dtype=F32) / RET_DK)
    fm = ROPE_THETA ** (-jnp.arange(0, MLA_ROPE, 2, dtype=F32) / MLA_ROPE)
    invf = jnp.concatenate([fr, fm, jnp.zeros((128 - 48,), F32)]).reshape(1, 128)
    return pl.pallas_call(
        _tables_kernel,
        grid=(t // tm,),
        in_specs=[pl.BlockSpec((tm, 1), lambda i: (i, 0)),
                  pl.BlockSpec((1, 128), lambda i: (0, 0))],
        out_specs=[pl.BlockSpec((tm, 256), lambda i: (i, 0)),
                   pl.BlockSpec((tm, 256), lambda i: (i, 0)),
                   pl.BlockSpec((tm, 128), lambda i: (i, 0)),
                   pl.BlockSpec((tm, 128), lambda i: (i, 0))],
        out_shape=[jax.ShapeDtypeStruct((t, 256), F32), jax.ShapeDtypeStruct((t, 256), F32),
                   jax.ShapeDtypeStruct((t, 128), F32), jax.ShapeDtypeStruct((t, 128), F32)],
        compiler_params=_cparams("parallel"),
        name="rope_tables",
    )(pos, invf)


def _proj_kernel(x_ref, win_ref, wuq_ref, wukv_ref, lng_ref, lnb_ref, qg_ref, kvg_ref,
                 cr_ref, sr_ref, cm_ref, sm_ref, rs_ref, q_ref, k_ref, v_ref, *, q_scale):
    h = _dot(x_ref[...].astype(BF16), win_ref[...])
    cr = cr_ref[...]
    sr = sr_ref[...]
    rs_ref[:, 0:256] = _rope(h[:, 0:256], cr, sr, 32).astype(BF16)
    rs_ref[:, 256:512] = (_rope(h[:, 256:512], cr, sr, 32) * (RET_DK ** -0.5)).astype(BF16)
    rs_ref[:, 512:768] = h[:, 512:768].astype(BF16)
    rs_ref[:, 768:1024] = _silu(h[:, 768:1024]).astype(BF16)
    rs_ref[:, 1024:1280] = h[:, 1024:1280].astype(BF16)
    rs_ref[:, 1280:1536] = _layer_norm(h[:, 1280:1536], lng_ref[...], lnb_ref[...]).astype(BF16)

    cm = cm_ref[...]
    sm = sm_ref[...]
    cm8 = jnp.concatenate([cm] * MLA_HEADS, axis=1)
    sm8 = jnp.concatenate([sm] * MLA_HEADS, axis=1)
    cq = _rms_norm(h[:, 1536:1792], qg_ref[...]).astype(BF16)
    q = _rope(_dot(cq, wuq_ref[...]), cm8, sm8, 16)
    q_ref[...] = (q * q_scale).astype(BF16)
    ckv = _rms_norm(h[:, 1792:1920], kvg_ref[...]).astype(BF16)
    kv = _dot(ckv, wukv_ref[...])
    kr = _rope(h[:, 1920:2048], cm, sm, 16)
    k_ref[...] = (kv[:, 0:1024] + jnp.concatenate([kr] * MLA_HEADS, axis=1)).astype(BF16)
    v_ref[...] = kv[:, 1024:1536].astype(BF16)


def _projections(x2d, l, w_in, w_uq, w_ukv, sgu_g, sgu_b, q_g, kv_g, cr, sr, cm, sm, tm):
    t = x2d.shape[0]
    row = lambda w: pl.BlockSpec((tm, w), lambda i: (i, 0))
    lay = lambda a: pl.BlockSpec((None,) + a.shape[1:], lambda i: (l,) + (0,) * (a.ndim - 1))
    q_scale = (MLA_NOPE + MLA_ROPE) ** -0.5 * math.log2(math.e)
    return pl.pallas_call(
        functools.partial(_proj_kernel, q_scale=q_scale),
        grid=(t // tm,),
        in_specs=[row(D_MODEL), lay(w_in), lay(w_uq), lay(w_ukv), lay(sgu_g), lay(sgu_b),
                  lay(q_g), lay(kv_g), row(256), row(256), row(128), row(128)],
        out_specs=[row(1536), row(1024), row(1024), row(512)],
        out_shape=[jax.ShapeDtypeStruct((t, 1536), BF16), jax.ShapeDtypeStruct((t, 1024), BF16),
                   jax.ShapeDtypeStruct((t, 1024), BF16), jax.ShapeDtypeStruct((t, 512), BF16)],
        compiler_params=_cparams("parallel"),
        name="projections",
    )(x2d, w_in, w_uq, w_ukv, sgu_g, sgu_b, q_g, kv_g, cr, sr, cm, sm)


def _retsgu_kernel(rs_ref, dmat_ref, qdec_ref, kdec_ref, cdec_ref, gavg_ref, sw_ref, sb_ref,
                   out_ref, state_ref, *, n_chunks):
    @pl.when(pl.program_id(1) == 0)
    def _():
        state_ref[...] = jnp.zeros_like(state_ref)

    lane = lax.broadcasted_iota(jnp.int32, (CHUNK, RET_W), 1)
    head_of_lane = lane // RET_DK
    hmask = [head_of_lane == h for h in range(RET_HEADS)]
    row_i = lax.broadcasted_iota(jnp.int32, (CHUNK, SGU_GROUPS * CHUNK), 0)
    col_i = lax.broadcasted_iota(jnp.int32, (CHUNK, SGU_GROUPS * CHUNK), 1)
    sw = jnp.where(row_i >= (col_i & (CHUNK - 1)), sw_ref[...], 0.0).astype(BF16)
    blk = (lax.broadcasted_iota(jnp.int32, (RET_W, RET_W), 0) // RET_DK
           == lax.broadcasted_iota(jnp.int32, (RET_W, RET_W), 1) // RET_DK)
    gavg = gavg_ref[...]
    zero = jnp.zeros((), BF16)

    def group_mean(y):
        hi, lo = _split_bf16(y)
        return _dot(jnp.concatenate([hi, lo], axis=1), gavg)

    def chunk(c, carry):
        r0 = pl.multiple_of(c * CHUNK, CHUNK)
        rq = rs_ref[pl.ds(r0, CHUNK), 0:256]
        rk = rs_ref[pl.ds(r0, CHUNK), 256:512]
        rv = rs_ref[pl.ds(r0, CHUNK), 512:768]
        gate = rs_ref[pl.ds(r0, CHUNK), 768:1024].astype(F32)
        su = rs_ref[pl.ds(r0, CHUNK), 1024:1280].astype(F32)
        sv = rs_ref[pl.ds(r0, CHUNK), 1280:1536]

        q_heads = jnp.concatenate([jnp.where(m, rq, zero) for m in hmask], axis=0)
        scores = _dot_nt(q_heads, rk) * dmat_ref[...]
        p_cat = jnp.concatenate(
            [scores[h * CHUNK:(h + 1) * CHUNK, :] for h in range(RET_HEADS)], axis=1).astype(BF16)
        v_heads = jnp.concatenate([jnp.where(m, rv, zero) for m in hmask], axis=0)
        intra = _dot(p_cat, v_heads)
        state = state_ref[...]
        cross = _dot((rq.astype(F32) * qdec_ref[...]).astype(BF16), state.astype(BF16))
        kv = _dot_tn((rk.astype(F32) * kdec_ref[...]).astype(BF16), rv)
        state_ref[...] = state * cdec_ref[...] + jnp.where(blk, kv, 0.0)

        y = intra + cross
        yc = y - group_mean(y)
        var = group_mean(yc * yc)
        ret = gate * (yc * lax.rsqrt(var + LN_EPS))

        sv_groups = jnp.concatenate([jnp.where(m, sv, zero) for m in hmask], axis=0)
        mixed = _dot(sw, sv_groups) + sb_ref[...]
        out_ref[pl.ds(r0, CHUNK), 0:256] = ret.astype(BF16)
        out_ref[pl.ds(r0, CHUNK), 256:512] = (su * mixed).astype(BF16)
        return carry

    lax.fori_loop(0, n_chunks, chunk, 0)


def _retention_sgu(rs, l, consts, sgu_w_cat, sgu_b_lane, batch, seq, tb):
    dmat, qdec, kdec, cdec, gavg = consts
    n_blocks = seq // tb
    full = lambda a: pl.BlockSpec(a.shape, lambda b, i: (0,) * a.ndim)
    lay = lambda a: pl.BlockSpec((None,) + a.shape[1:], lambda b, i: (l,) + (0,) * (a.ndim - 1))
    return pl.pallas_call(
        functools.partial(_retsgu_kernel, n_chunks=tb // CHUNK),
        grid=(batch, n_blocks),
        in_specs=[pl.BlockSpec((tb, 1536), lambda b, i: (b * n_blocks + i, 0)),
                  full(dmat), full(qdec), full(kdec), full(cdec), full(gavg),
                  lay(sgu_w_cat), lay(sgu_b_lane)],
        out_specs=pl.BlockSpec((tb, 512), lambda b, i: (b * n_blocks + i, 0)),
        out_shape=jax.ShapeDtypeStruct((batch * seq, 512), BF16),
        scratch_shapes=[pltpu.VMEM((RET_W, RET_W), F32)],
        compiler_params=_cparams("parallel", "arbitrary"),
        name="retention_sgu",
    )(rs, dmat, qdec, kdec, cdec, gavg, sgu_w_cat, sgu_b_lane)


def _retention_consts():
    h = jnp.arange(RET_HEADS, dtype=F32)
    log_gamma = jnp.log1p(-(2.0 ** (-5.0 - h)))
    pos = jnp.arange(CHUNK, dtype=F32)
    diff = pos[:, None] - pos[None, :]
    intra = jnp.where(diff >= 0, jnp.exp(log_gamma[:, None, None] * jnp.maximum(diff, 0.0)), 0.0)
    dmat = intra.reshape(RET_HEADS * CHUNK, CHUNK)
    inner = jnp.exp(log_gamma[None, :] * (CHUNK - 1 - pos)[:, None])
    query = jnp.exp(log_gamma[None, :] * (pos + 1)[:, None])
    kdec = jnp.repeat(inner, RET_DK, axis=1)
    qdec = jnp.repeat(query, RET_DK, axis=1)
    chunk_decay = jnp.repeat(jnp.exp(log_gamma * CHUNK), RET_DK)
    blk = jnp.arange(RET_W)[:, None] // RET_DK == jnp.arange(RET_W)[None, :] // RET_DK
    cdec = jnp.where(blk, chunk_decay[:, None], 0.0)
    gavg = jnp.where(blk, 1.0 / RET_DK, 0.0).astype(BF16)
    return dmat, qdec, kdec, cdec, jnp.concatenate([gavg, gavg], axis=0)


def _flash_kernel(q_ref, k_ref, v_ref, o_ref, *, tq):
    i = pl.program_id(2)
    lane = lax.broadcasted_iota(jnp.int32, (tq, 2 * MLA_V), 1)
    first = lane < MLA_V
    outs = []
    for hh in range(2):
        q = q_ref[:, hh * HEAD_PAD:(hh + 1) * HEAD_PAD]

        def block(j, carry, masked):
            m, l, acc = carry
            r0 = pl.multiple_of(j * tq, tq)
            kb = k_ref[pl.ds(r0, tq), hh * HEAD_PAD:(hh + 1) * HEAD_PAD]
            vb = v_ref[pl.ds(r0, tq), :]
            s = _dot_nt(q, kb)
            if masked:
                rr = lax.broadcasted_iota(jnp.int32, s.shape, 0)
                cc = lax.broadcasted_iota(jnp.int32, s.shape, 1)
                s = jnp.where(cc <= rr, s, NEG_BIG)
            m_new = jnp.maximum(m, jnp.max(s, axis=-1, keepdims=True))
            a = jnp.exp2(m - m_new)
            p = jnp.exp2(s - m_new)
            l_new = a * l + jnp.sum(p, axis=-1, keepdims=True)
            acc_new = a * acc + _dot(p.astype(BF16), vb)
            return m_new, l_new, acc_new

        init = (jnp.full((tq, 1), NEG_BIG, F32), jnp.zeros((tq, 1), F32),
                jnp.zeros((tq, 2 * MLA_V), F32))
        carry = lax.fori_loop(0, i, functools.partial(block, masked=False), init)
        m, l, acc = block(i, carry, True)
        outs.append(acc / l)
    o_ref[...] = jnp.where(first, outs[0], outs[1]).astype(BF16)


def _flash_attention(q, k, v, batch, seq, tq):
    nq = seq // tq
    pairs = MLA_HEADS // 2
    return pl.pallas_call(
        functools.partial(_flash_kernel, tq=tq),
        grid=(batch, pairs, nq),
        in_specs=[pl.BlockSpec((tq, 2 * HEAD_PAD), lambda b, p, i: (b * nq + i, p)),
                  pl.BlockSpec((seq, 2 * HEAD_PAD), lambda b, p, i: (b, p)),
                  pl.BlockSpec((seq, 2 * MLA_V), lambda b, p, i: (b, p))],
        out_specs=pl.BlockSpec((tq, 2 * MLA_V), lambda b, p, i: (b * nq + i, p)),
        out_shape=jax.ShapeDtypeStruct((batch * seq, MLA_W), BF16),
        compiler_params=_cparams("parallel", "parallel", "arbitrary"),
        name="flash_attention",
    )(q, k, v)


def _memkv_kernel(mem_ref, wk_ref, wv_ref, k_ref, v_ref):
    m = mem_ref[...].astype(BF16)
    k_ref[...] = _dot(m, wk_ref[...]).astype(BF16)
    v_ref[...] = _dot(m, wv_ref[...]).astype(BF16)


def _memory_kv(mem2d, l, wk, wv, batch):
    lay = lambda a: pl.BlockSpec((None,) + a.shape[1:], lambda b: (l,) + (0,) * (a.ndim - 1))
    blk = pl.BlockSpec((MEM_LEN, XA_W), lambda b: (b, 0))
    return pl.pallas_call(
        _memkv_kernel,
        grid=(batch,),
        in_specs=[pl.BlockSpec((MEM_LEN, D_MODEL), lambda b: (b, 0)), lay(wk), lay(wv)],
        out_specs=[blk, blk],
        out_shape=[jax.ShapeDtypeStruct((batch * MEM_LEN, XA_W), BF16)] * 2,
        compiler_params=_cparams("parallel"),
        name="memory_kv",
    )(mem2d, wk, wv)


def _route_rows(scores, biased):
    s = [scores[e:e + 1, :] for e in range(N_EXPERTS)]
    b = [biased[e:e + 1, :] for e in range(N_EXPERTS)]
    group_scores = []
    for g in range(N_GROUPS):
        b0, b1, b2, b3 = b[4 * g:4 * g + 4]
        hi01, lo01 = jnp.maximum(b0, b1), jnp.minimum(b0, b1)
        hi23, lo23 = jnp.maximum(b2, b3), jnp.minimum(b2, b3)
        top1 = jnp.maximum(hi01, hi23)
        top2 = jnp.maximum(jnp.minimum(hi01, hi23), jnp.maximum(lo01, lo23))
        group_scores.append(top1 + top2)
    best = group_scores[0]
    sel = jnp.zeros_like(best, dtype=jnp.int32)
    for g in range(1, N_GROUPS):
        upd = group_scores[g] > best
        sel = jnp.where(upd, g, sel)
        best = jnp.where(upd, group_scores[g], best)

    def pick(rows, j):
        out = rows[j]
        for g in range(1, N_GROUPS):
            out = jnp.where(sel == g, rows[4 * g + j], out)
        return out

    ib = [pick(b, j) for j in range(EXPERTS_PER_GROUP)]
    isc = [pick(s, j) for j in range(EXPERTS_PER_GROUP)]

    def argmax4(vals):
        bv, bi = vals[0], jnp.zeros_like(sel)
        for j in range(1, EXPERTS_PER_GROUP):
            upd = vals[j] > bv
            bi = jnp.where(upd, j, bi)
            bv = jnp.where(upd, vals[j], bv)
        return bi

    i1 = argmax4(ib)
    i2 = argmax4([jnp.where(i1 == j, -jnp.inf, ib[j]) for j in range(EXPERTS_PER_GROUP)])

    def take(vals, idx):
        out = vals[0]
        for j in range(1, EXPERTS_PER_GROUP):
            out = jnp.where(idx == j, vals[j], out)
        return out

    g1, g2 = take(isc, i1), take(isc, i2)
    den = g1 + g2
    g1, g2 = g1 / den, g2 / den
    e1 = sel * EXPERTS_PER_GROUP + i1
    e2 = sel * EXPERTS_PER_GROUP + i2
    rows = [jnp.where(e1 == e, g1, 0.0) + jnp.where(e2 == e, g2, 0.0) for e in range(N_EXPERTS)]
    return jnp.concatenate(rows, axis=0)


def _mix_xa_kernel(x_ref, rs_ref, at_ref, wout_ref, g1_ref, b1_ref, wq_ref, km_ref, vm_ref, wo_ref,
                   g2_ref, b2_ref, rw_ref, rb_ref, x2_ref, comb_ref, *, tm):
    mix = _dot(rs_ref[...], wout_ref[0:512, :]) + _dot(at_ref[...], wout_ref[512:1024, :])
    x1 = _layer_norm(ALPHA * x_ref[...] + mix, g1_ref[...], b1_ref[...])
    q = (_dot(x1.astype(BF16), wq_ref[...]) * (XA_DIM ** -0.5 * math.log2(math.e))).astype(BF16)
    heads = []
    for h in range(XA_HEADS):
        sl = slice(h * XA_DIM, (h + 1) * XA_DIM)
        s = _dot_nt(q[:, sl], km_ref[:, sl])
        p = jnp.exp2(s - jnp.max(s, axis=-1, keepdims=True))
        o = _dot(p.astype(BF16), vm_ref[:, sl])
        heads.append(o / jnp.sum(p, axis=-1, keepdims=True))
    xa = _dot(jnp.concatenate(heads, axis=1).astype(BF16), wo_ref[...])
    x2 = _layer_norm(ALPHA * x1 + xa, g2_ref[...], b2_ref[...])
    x2_ref[...] = x2

    x_hi, x_lo = _split_bf16(x2)
    w_hi, w_lo = _split_bf16(rw_ref[...])
    logits = _dot_nt(w_hi, x_hi) + (_dot_nt(w_hi, x_lo) + _dot_nt(w_lo, x_hi))
    scores = 1.0 / (1.0 + jnp.exp(-logits))
    comb = _route_rows(scores, scores + rb_ref[...])
    comb_ref[...] = jnp.concatenate([comb, jnp.zeros((128 - N_EXPERTS, tm), F32)], axis=0).T


def _mix_xa(x2d, rs, at, km, vm, l, w_out, g1, b1, wq, wo, g2, b2, rw_t, rb_col, batch, seq, tm):
    t = x2d.shape[0]
    nb = seq // tm
    row = lambda w: pl.BlockSpec((tm, w), lambda b, i: (b * nb + i, 0))
    lay = lambda a: pl.BlockSpec((None,) + a.shape[1:], lambda b, i: (l,) + (0,) * (a.ndim - 1))
    full = lambda a: pl.BlockSpec(a.shape, lambda b, i: (0,) * a.ndim)
    memb = pl.BlockSpec((MEM_LEN, XA_W), lambda b, i: (b, 0))
    return pl.pallas_call(
        functools.partial(_mix_xa_kernel, tm=tm),
        grid=(batch, nb),
        in_specs=[row(D_MODEL), row(512), row(512), lay(w_out), lay(g1), lay(b1), lay(wq), memb, memb,
                  lay(wo), lay(g2), lay(b2), full(rw_t), full(rb_col)],
        out_specs=[row(D_MODEL), row(128)],
        out_shape=[jax.ShapeDtypeStruct((t, D_MODEL), F32), jax.ShapeDtypeStruct((t, 128), F32)],
        compiler_params=_cparams("parallel", "parallel"),
        name="mix_xattn_router",
    )(x2d, rs, at, w_out, g1, b1, wq, km, vm, wo, g2, b2, rw_t, rb_col)


def _moe_kernel(x_ref, comb_ref, wg_ref, wu_ref, wd_ref, g_ref, b_ref, o_ref, acc_ref):
    e = pl.program_id(1)

    @pl.when(e == 0)
    def _():
        acc_ref[...] = jnp.zeros_like(acc_ref)

    xb = x_ref[...].astype(BF16)
    comb = comb_ref[...]
    lane = lax.broadcasted_iota(jnp.int32, comb.shape, 1)
    c = jnp.sum(jnp.where(lane == e, comb, 0.0), axis=-1, keepdims=True)
    hid = _silu(_dot(xb, wg_ref[...])) * _dot(xb, wu_ref[...])
    acc_ref[...] += _dot((hid * c).astype(BF16), wd_ref[...])

    @pl.when(e == N_EXPERTS - 1)
    def _():
        o_ref[...] = _layer_norm(ALPHA * x_ref[...] + acc_ref[...], g_ref[...], b_ref[...])


def _moe(x2, comb, l, wg, wu, wd, g, b, tm):
    t = x2.shape[0]
    row = lambda w: pl.BlockSpec((tm, w), lambda i, e: (i, 0))
    lay = lambda a: pl.BlockSpec((None,) + a.shape[1:], lambda i, e: (l,) + (0,) * (a.ndim - 1))
    exp = lambda a: pl.BlockSpec((None, None) + a.shape[2:], lambda i, e: (l, e, 0, 0))
    return pl.pallas_call(
        _moe_kernel,
        grid=(t // tm, N_EXPERTS),
        in_specs=[row(D_MODEL), row(128), exp(wg), exp(wu), exp(wd), lay(g), lay(b)],
        out_specs=row(D_MODEL),
        out_shape=jax.ShapeDtypeStruct((t, D_MODEL), F32),
        scratch_shapes=[pltpu.VMEM((tm, D_MODEL), F32)],
        compiler_params=_cparams("parallel", "arbitrary"),
        name="moe_experts",
    )(x2, comb, wg, wu, wd, g, b)


def _tile(n, pref):
    t = min(n, pref)
    assert n % t == 0, (n, t)
    return t


def kernel(x, mem, positions, w_in, w_out, sgu_ln_g, sgu_ln_b, sgu_w, sgu_b, mla_q_norm_g, mla_w_uq, mla_kv_norm_g, mla_w_ukv, xa_wq, xa_wk, xa_wv, xa_wo, ln_mix_g, ln_mix_b, ln_xa_g, ln_xa_b, ln_moe_g, ln_moe_b, router_w, router_bias, expert_w_gate, expert_w_up, expert_w_down):
    batch, seq, _ = x.shape
    depth = w_in.shape[0]
    t = batch * seq
    assert seq % CHUNK == 0

    w_in_p = jnp.concatenate(
        [w_in[:, :, :1920], jnp.zeros((depth, D_MODEL, 64), F32), w_in[:, :, 1920:1952],
         jnp.zeros((depth, D_MODEL, 32), F32)], axis=2).astype(BF16)
    w_uq_p = jnp.pad(mla_w_uq.reshape(depth, MLA_Q_RANK, MLA_HEADS, MLA_NOPE + MLA_ROPE),
                     ((0, 0), (0, 0), (0, 0), (0, HEAD_PAD - MLA_NOPE - MLA_ROPE))
                     ).reshape(depth, MLA_Q_RANK, MLA_HEADS * HEAD_PAD).astype(BF16)
    ukv = mla_w_ukv.reshape(depth, MLA_KV_RANK, MLA_HEADS, MLA_NOPE + MLA_V)
    w_uk_p = jnp.pad(ukv[..., :MLA_NOPE], ((0, 0), (0, 0), (0, 0), (0, HEAD_PAD - MLA_NOPE))
                     ).reshape(depth, MLA_KV_RANK, MLA_HEADS * HEAD_PAD)
    w_uv = ukv[..., MLA_NOPE:].reshape(depth, MLA_KV_RANK, MLA_W)
    w_ukv_p = jnp.concatenate([w_uk_p, w_uv], axis=2).astype(BF16)
    w_out_b = w_out.astype(BF16)
    wq_b, wk_b, wv_b, wo_b = (a.astype(BF16) for a in (xa_wq, xa_wk, xa_wv, xa_wo))
    wg_b, wu_b, wd_b = (a.astype(BF16) for a in (expert_w_gate, expert_w_up, expert_w_down))
    vec = lambda a: a.reshape(depth, 1, a.shape[-1])
    sgu_w_cat = jnp.transpose(sgu_w, (0, 2, 1, 3)).reshape(depth, CHUNK, SGU_GROUPS * CHUNK)
    sgu_b_lane = jnp.repeat(jnp.transpose(sgu_b, (0, 2, 1)), SGU_W // SGU_GROUPS, axis=2)
    rw_t = router_w.T
    rb_col = router_bias.reshape(N_EXPERTS, 1)
    consts = _retention_consts()

    x2d = x.reshape(t, D_MODEL)
    mem2d = mem.reshape(batch * MEM_LEN, D_MODEL)
    cr, sr, cm, sm = _rope_tables(positions, _tile(t, 512))

    tm_proj = _tile(t, 512)
    tb = _tile(seq, 512)
    tq = _tile(seq, 512)
    tm_mix = _tile(seq, 256)
    tm_moe = _tile(t, 1024)
    for l in range(depth):
        rs, q, k, v = _projections(x2d, l, w_in_p, w_uq_p, w_ukv_p, vec(sgu_ln_g), vec(sgu_ln_b),
                                   vec(mla_q_norm_g), vec(mla_kv_norm_g), cr, sr, cm, sm, tm_proj)
        retsgu = _retention_sgu(rs, l, consts, sgu_w_cat, sgu_b_lane, batch, seq, tb)
        attn = _flash_attention(q, k, v, batch, seq, tq)
        km, vm = _memory_kv(mem2d, l, wk_b, wv_b, batch)
        x2, comb = _mix_xa(x2d, retsgu, attn, km, vm, l, w_out_b, vec(ln_mix_g), vec(ln_mix_b), wq_b, wo_b,
                           vec(ln_xa_g), vec(ln_xa_b), rw_t, rb_col, batch, seq, tm_mix)
        x2d = _moe(x2, comb, l, wg_b, wu_b, wd_b, vec(ln_moe_g), vec(ln_moe_b), tm_moe)
    return x2d.reshape(batch, seq, D_MODEL)
```

```python
import functools
import math

import jax
import jax.numpy as jnp
from jax import lax
from jax.experimental import pallas as pl
from jax.experimental.pallas import tpu as pltpu

F32 = jnp.float32
BF16 = jnp.bfloat16

D_MODEL = 1024
DEPTH = 4
MEM_LEN = 256
ROPE_THETA = 10000.0

RET_HEADS = 4
RET_DK = 64
RET_W = 256
CHUNK = 128

SGU_GROUPS = 4
SGU_W = 256

MLA_HEADS = 8
MLA_Q_RANK = 256
MLA_KV_RANK = 128
MLA_NOPE = 64
MLA_ROPE = 32
MLA_V = 64
MLA_W = MLA_HEADS * MLA_V
HEAD_PAD = 128

XA_HEADS = 4
XA_DIM = 128
XA_W = XA_HEADS * XA_DIM

N_EXPERTS = 16
N_GROUPS = 4
EXPERTS_PER_GROUP = 4
D_EXPERT = 256

ALPHA = (2 * DEPTH) ** 0.25
LN_EPS = 1e-5
IN_PAD = 2048
NEG_BIG = -1e30

VMEM_LIMIT = 56 * 1024 * 1024


def _cparams(*sem):
    return pltpu.CompilerParams(dimension_semantics=sem, vmem_limit_bytes=VMEM_LIMIT)


def _dot(a, b):
    return jnp.dot(a, b, preferred_element_type=F32)


def _dot_nt(a, b):
    return lax.dot_general(a, b, (((1,), (1,)), ((), ())), preferred_element_type=F32)


def _dot_tn(a, b):
    return lax.dot_general(a, b, (((0,), (0,)), ((), ())), preferred_element_type=F32)


def _layer_norm(z, g, b):
    mu = jnp.mean(z, axis=-1, keepdims=True)
    zc = z - mu
    var = jnp.mean(zc * zc, axis=-1, keepdims=True)
    return zc * lax.rsqrt(var + LN_EPS) * g + b


def _rms_norm(z, g):
    ms = jnp.mean(z * z, axis=-1, keepdims=True)
    return z * lax.rsqrt(ms + LN_EPS) * g


def _silu(z):
    return z / (1.0 + jnp.exp(-z))


def _rope(x, cos, sin_signed, half):
    w = x.shape[-1]
    lane = lax.broadcasted_iota(jnp.int32, x.shape, 1)
    rot = jnp.where((lane & half) == 0, pltpu.roll(x, w - half, 1), pltpu.roll(x, half, 1))
    return x * cos + rot * sin_signed


def _split_bf16(x):
    hi = x.astype(BF16)
    lo = (x - hi.astype(F32)).astype(BF16)
    return hi, lo


def _tables_kernel(pos_ref, invf_ref, cr_ref, sr_ref, cm_ref, sm_ref):
    ang = pos_ref[...].astype(F32) * invf_ref[...]
    c = jnp.cos(ang)
    s = jnp.sin(ang)
    lane = lax.broadcasted_iota(jnp.int32, c.shape, 1)

    def tile32(v):
        v0 = jnp.where(lane < 32, v, 0.0)
        v1 = v0 + pltpu.roll(v0, 32, 1)
        return v1 + pltpu.roll(v1, 64, 1)

    ct = tile32(c)
    st = tile32(s) * jnp.where((lane & 32) == 0, -1.0, 1.0)
    cr_ref[...] = jnp.concatenate([ct, ct], axis=1)
    sr_ref[...] = jnp.concatenate([st, st], axis=1)
    in_src = (lane >= 32) & (lane < 48)
    cmv = jnp.where(in_src, c, 0.0)
    smv = jnp.where(in_src, s, 0.0)
    in_dst = (lane >= 64) & (lane < 96)
    cm_ref[...] = jnp.where(in_dst, pltpu.roll(cmv, 32, 1) + pltpu.roll(cmv, 48, 1), 1.0)
    sm_ref[...] = jnp.where(in_dst, pltpu.roll(smv, 48, 1) - pltpu.roll(smv, 32, 1), 0.0)


def _rope_tables(positions, tm):
    t = positions.size
    pos = positions.reshape(t, 1)
    fr = ROPE_THETA ** (-jnp.arange(0, RET_DK, 2, dtype=F32) / RET_DK)
    fm = ROPE_THETA ** (-jnp.arange(0, MLA_ROPE, 2, dtype=F32) / MLA_ROPE)
    invf = jnp.concatenate([fr, fm, jnp.zeros((128 - 48,), F32)]).reshape(1, 128)
    return pl.pallas_call(
        _tables_kernel,
        grid=(t // tm,),
        in_specs=[pl.BlockSpec((tm, 1), lambda i: (i, 0)),
                  pl.BlockSpec((1, 128), lambda i: (0, 0))],
        out_specs=[pl.BlockSpec((tm, 256), lambda i: (i, 0)),
                   pl.BlockSpec((tm, 256), lambda i: (i, 0)),
                   pl.BlockSpec((tm, 128), lambda i: (i, 0)),
                   pl.BlockSpec((tm, 128), lambda i: (i, 0))],
        out_shape=[jax.ShapeDtypeStruct((t, 256), F32), jax.ShapeDtypeStruct((t, 256), F32),
                   jax.ShapeDtypeStruct((t, 128), F32), jax.ShapeDtypeStruct((t, 128), F32)],
        compiler_params=_cparams("parallel"),
        name="rope_tables",
    )(pos, invf)


def _proj_kernel(x_ref, win_ref, wuq_ref, wukv_ref, lng_ref, lnb_ref, qg_ref, kvg_ref,
                 cr_ref, sr_ref, cm_ref, sm_ref, rs_ref, q_ref, k_ref, v_ref, *, q_scale):
    h = _dot(x_ref[...].astype(BF16), win_ref[...])
    cr = cr_ref[...]
    sr = sr_ref[...]
    rs_ref[:, 0:256] = _rope(h[:, 0:256], cr, sr, 32).astype(BF16)
    rs_ref[:, 256:512] = (_rope(h[:, 256:512], cr, sr, 32) * (RET_DK ** -0.5)).astype(BF16)
    rs_ref[:, 512:768] = h[:, 512:768].astype(BF16)
    rs_ref[:, 768:1024] = _silu(h[:, 768:1024]).astype(BF16)
    rs_ref[:, 1024:1280] = h[:, 1024:1280].astype(BF16)
    rs_ref[:, 1280:1536] = _layer_norm(h[:, 1280:1536], lng_ref[...], lnb_ref[...]).astype(BF16)

    cm = cm_ref[...]
    sm = sm_ref[...]
    cm8 = jnp.concatenate([cm] * MLA_HEADS, axis=1)
    sm8 = jnp.concatenate([sm] * MLA_HEADS, axis=1)
    cq = _rms_norm(h[:, 1536:1792], qg_ref[...]).astype(BF16)
    q = _rope(_dot(cq, wuq_ref[...]), cm8, sm8, 16)
    q_ref[...] = (q * q_scale).astype(BF16)
    ckv = _rms_norm(h[:, 1792:1920], kvg_ref[...]).astype(BF16)
    kv = _dot(ckv, wukv_ref[...])
    kr = _rope(h[:, 1920:2048], cm, sm, 16)
    k_ref[...] = (kv[:, 0:1024] + jnp.concatenate([kr] * MLA_HEADS, axis=1)).astype(BF16)
    lane = lax.broadcasted_iota(jnp.int32, (kv.shape[0], MLA_HEADS * HEAD_PAD), 1)
    ones_lane = jnp.where((lane & (HEAD_PAD - 1)) == MLA_V, 1.0, 0.0)
    v_ref[...] = (kv[:, 1024:2048] + ones_lane).astype(BF16)


def _projections(x2d, l, w_in, w_uq, w_ukv, sgu_g, sgu_b, q_g, kv_g, cr, sr, cm, sm, tm):
    t = x2d.shape[0]
    row = lambda w: pl.BlockSpec((tm, w), lambda i: (i, 0))
    lay = lambda a: pl.BlockSpec((None,) + a.shape[1:], lambda i: (l,) + (0,) * (a.ndim - 1))
    q_scale = (MLA_NOPE + MLA_ROPE) ** -0.5 * math.log2(math.e)
    return pl.pallas_call(
        functools.partial(_proj_kernel, q_scale=q_scale),
        grid=(t // tm,),
        in_specs=[row(D_MODEL), lay(w_in), lay(w_uq), lay(w_ukv), lay(sgu_g), lay(sgu_b),
                  lay(q_g), lay(kv_g), row(256), row(256), row(128), row(128)],
        out_specs=[row(1536), row(1024), row(1024), row(1024)],
        out_shape=[jax.ShapeDtypeStruct((t, 1536), BF16)] + [jax.ShapeDtypeStruct((t, 1024), BF16)] * 3,
        compiler_params=_cparams("parallel"),
        name="projections",
    )(x2d, w_in, w_uq, w_ukv, sgu_g, sgu_b, q_g, kv_g, cr, sr, cm, sm)


def _retsgu_kernel(rs_ref, dmat_ref, qdec_ref, kdec_ref, cdec_ref, gavg_ref, sw_ref, sb_ref,
                   out_ref, state_ref, *, n_chunks):
    @pl.when(pl.program_id(1) == 0)
    def _():
        state_ref[...] = jnp.zeros_like(state_ref)

    lane = lax.broadcasted_iota(jnp.int32, (CHUNK, RET_W), 1)
    head_of_lane = lane // RET_DK
    hmask = [head_of_lane == h for h in range(RET_HEADS)]
    row_i = lax.broadcasted_iota(jnp.int32, (CHUNK, SGU_GROUPS * CHUNK), 0)
    col_i = lax.broadcasted_iota(jnp.int32, (CHUNK, SGU_GROUPS * CHUNK), 1)
    sw = jnp.where(row_i >= (col_i & (CHUNK - 1)), sw_ref[...], 0.0).astype(BF16)
    blk = (lax.broadcasted_iota(jnp.int32, (RET_W, RET_W), 0) // RET_DK
           == lax.broadcasted_iota(jnp.int32, (RET_W, RET_W), 1) // RET_DK)
    gavg = gavg_ref[...]
    zero = jnp.zeros((), BF16)

    def group_mean(y):
        hi, lo = _split_bf16(y)
        return _dot(jnp.concatenate([hi, lo], axis=1), gavg)

    def chunk(c, carry):
        r0 = pl.multiple_of(c * CHUNK, CHUNK)
        rq = rs_ref[pl.ds(r0, CHUNK), 0:256]
        rk = rs_ref[pl.ds(r0, CHUNK), 256:512]
        rv = rs_ref[pl.ds(r0, CHUNK), 512:768]
        gate = rs_ref[pl.ds(r0, CHUNK), 768:1024].astype(F32)
        su = rs_ref[pl.ds(r0, CHUNK), 1024:1280].astype(F32)
        sv = rs_ref[pl.ds(r0, CHUNK), 1280:1536]

        q_heads = jnp.concatenate([jnp.where(m, rq, zero) for m in hmask], axis=0)
        scores = _dot_nt(q_heads, rk) * dmat_ref[...]
        p_cat = jnp.concatenate(
            [scores[h * CHUNK:(h + 1) * CHUNK, :] for h in range(RET_HEADS)], axis=1).astype(BF16)
        v_heads = jnp.concatenate([jnp.where(m, rv, zero) for m in hmask], axis=0)
        intra = _dot(p_cat, v_heads)
        state = state_ref[...]
        cross = _dot((rq.astype(F32) * qdec_ref[...]).astype(BF16), state.astype(BF16))
        kv = _dot_tn((rk.astype(F32) * kdec_ref[...]).astype(BF16), rv)
        state_ref[...] = state * cdec_ref[...] + jnp.where(blk, kv, 0.0)

        y = intra + cross
        yc = y - group_mean(y)
        var = group_mean(yc * yc)
        ret = gate * (yc * lax.rsqrt(var + LN_EPS))

        sv_groups = jnp.concatenate([jnp.where(m, sv, zero) for m in hmask], axis=0)
        mixed = _dot(sw, sv_groups) + sb_ref[...]
        out_ref[pl.ds(r0, CHUNK), 0:256] = ret.astype(BF16)
        out_ref[pl.ds(r0, CHUNK), 256:512] = (su * mixed).astype(BF16)
        return carry

    lax.fori_loop(0, n_chunks, chunk, 0)


def _retention_sgu(rs, l, consts, sgu_w_cat, sgu_b_lane, batch, seq, tb):
    dmat, qdec, kdec, cdec, gavg = consts
    n_blocks = seq // tb
    full = lambda a: pl.BlockSpec(a.shape, lambda b, i: (0,) * a.ndim)
    lay = lambda a: pl.BlockSpec((None,) + a.shape[1:], lambda b, i: (l,) + (0,) * (a.ndim - 1))
    return pl.pallas_call(
        functools.partial(_retsgu_kernel, n_chunks=tb // CHUNK),
        grid=(batch, n_blocks),
        in_specs=[pl.BlockSpec((tb, 1536), lambda b, i: (b * n_blocks + i, 0)),
                  full(dmat), full(qdec), full(kdec), full(cdec), full(gavg),
                  lay(sgu_w_cat), lay(sgu_b_lane)],
        out_specs=pl.BlockSpec((tb, 512), lambda b, i: (b * n_blocks + i, 0)),
        out_shape=jax.ShapeDtypeStruct((batch * seq, 512), BF16),
        scratch_shapes=[pltpu.VMEM((RET_W, RET_W), F32)],
        compiler_params=_cparams("parallel", "arbitrary"),
        name="retention_sgu",
    )(rs, dmat, qdec, kdec, cdec, gavg, sgu_w_cat, sgu_b_lane)


def _retention_consts():
    h = jnp.arange(RET_HEADS, dtype=F32)
    log_gamma = jnp.log1p(-(2.0 ** (-5.0 - h)))
    pos = jnp.arange(CHUNK, dtype=F32)
    diff = pos[:, None] - pos[None, :]
    intra = jnp.where(diff >= 0, jnp.exp(log_gamma[:, None, None] * jnp.maximum(diff, 0.0)), 0.0)
    dmat = intra.reshape(RET_HEADS * CHUNK, CHUNK)
    inner = jnp.exp(log_gamma[None, :] * (CHUNK - 1 - pos)[:, None])
    query = jnp.exp(log_gamma[None, :] * (pos + 1)[:, None])
    kdec = jnp.repeat(inner, RET_DK, axis=1)
    qdec = jnp.repeat(query, RET_DK, axis=1)
    chunk_decay = jnp.repeat(jnp.exp(log_gamma * CHUNK), RET_DK)
    blk = jnp.arange(RET_W)[:, None] // RET_DK == jnp.arange(RET_W)[None, :] // RET_DK
    cdec = jnp.where(blk, chunk_decay[:, None], 0.0)
    gavg = jnp.where(blk, 1.0 / RET_DK, 0.0).astype(BF16)
    return dmat, qdec, kdec, cdec, jnp.concatenate([gavg, gavg], axis=0)


def _flash_kernel(q_ref, k_ref, v_ref, o_ref, sa_ref, sb_ref, m_ref, acc_ref, *, tq, hps):
    i = pl.program_id(2)
    tk = tq // 2
    heads = range(hps)
    cols = [slice(hh * HEAD_PAD, (hh + 1) * HEAD_PAD) for hh in heads]

    def scores(j, hh, rows=slice(None)):
        r0 = pl.multiple_of(j * tk, tk)
        return _dot_nt(q_ref[rows, cols[hh]], k_ref[pl.ds(r0, tk), cols[hh]])

    def update(j, hh, s, rows=slice(None)):
        r0 = pl.multiple_of(j * tk, tk)
        m = m_ref[hh, rows, :]
        s_max = s[:, 0:HEAD_PAD]
        for c in range(1, tk // HEAD_PAD):
            s_max = jnp.maximum(s_max, s[:, c * HEAD_PAD:(c + 1) * HEAD_PAD])
        m_new = jnp.maximum(m, jnp.max(s_max, axis=-1, keepdims=True))
        p = jnp.exp2(s - jnp.concatenate([m_new] * (tk // HEAD_PAD), axis=1)).astype(BF16)
        acc_ref[hh, rows, :] = (jnp.exp2(m - m_new) * acc_ref[hh, rows, :]
                                + _dot(p, v_ref[pl.ds(r0, tk), cols[hh]]))
        m_ref[hh, rows, :] = m_new

    m_ref[...] = jnp.full(m_ref.shape, NEG_BIG, F32)
    acc_ref[...] = jnp.zeros(acc_ref.shape, F32)
    for hh in heads:
        sa_ref[hh] = scores(0, hh)

    def pair(t, carry):
        for hh in heads:
            sb_ref[hh] = scores(2 * t + 1, hh)
            update(2 * t, hh, sa_ref[hh])
        for hh in heads:
            sa_ref[hh] = scores(2 * t + 2, hh)
            update(2 * t + 1, hh, sb_ref[hh])
        return carry

    lax.fori_loop(0, i, pair, 0)
    low = slice(tk, tq)
    visible = (lax.broadcasted_iota(jnp.int32, (tq, tk), 1) <= lax.broadcasted_iota(jnp.int32, (tq, tk), 0))
    visible_low = (lax.broadcasted_iota(jnp.int32, (tk, tk), 1) <= lax.broadcasted_iota(jnp.int32, (tk, tk), 0))
    for hh in heads:
        sb_ref[hh, low, :] = scores(2 * i + 1, hh, low)
        update(2 * i, hh, jnp.where(visible, sa_ref[hh], NEG_BIG))
    for hh in heads:
        update(2 * i + 1, hh, jnp.where(visible_low, sb_ref[hh, low, :], NEG_BIG), low)
    lane = lax.broadcasted_iota(jnp.int32, (tq, HEAD_PAD), 1)
    for pr in range(hps // 2):
        o = []
        for hh in (2 * pr, 2 * pr + 1):
            acc = acc_ref[hh]
            o.append(acc / acc[:, MLA_V:MLA_V + 1])
        o_ref[:, pr * HEAD_PAD:(pr + 1) * HEAD_PAD] = jnp.where(
            lane < MLA_V, o[0], pltpu.roll(o[1], MLA_V, 1)).astype(BF16)


def _flash_attention(q, k, v, batch, seq, tq, hps):
    nq = seq // tq
    return pl.pallas_call(
        functools.partial(_flash_kernel, tq=tq, hps=hps),
        grid=(batch, MLA_HEADS // hps, nq),
        in_specs=[pl.BlockSpec((tq, hps * HEAD_PAD), lambda b, p, i: (b * nq + i, p)),
                  pl.BlockSpec((seq, hps * HEAD_PAD), lambda b, p, i: (b, p)),
                  pl.BlockSpec((seq, hps * HEAD_PAD), lambda b, p, i: (b, p))],
        out_specs=pl.BlockSpec((tq, hps * MLA_V), lambda b, p, i: (b * nq + i, p)),
        out_shape=jax.ShapeDtypeStruct((batch * seq, MLA_W), BF16),
        scratch_shapes=[pltpu.VMEM((hps, tq, tq // 2), F32), pltpu.VMEM((hps, tq, tq // 2), F32),
                        pltpu.VMEM((hps, tq, HEAD_PAD), F32), pltpu.VMEM((hps, tq, HEAD_PAD), F32)],
        compiler_params=_cparams("parallel", "parallel", "arbitrary"),
        name="flash_attention",
    )(q, k, v)


def _memkv_kernel(mem_ref, wk_ref, wv_ref, k_ref, v_ref):
    m = mem_ref[...].astype(BF16)
    k_ref[...] = _dot(m, wk_ref[...]).astype(BF16)
    v_ref[...] = _dot(m, wv_ref[...]).astype(BF16)


def _memory_kv(mem2d, l, wk, wv, batch):
    lay = lambda a: pl.BlockSpec((None,) + a.shape[1:], lambda b: (l,) + (0,) * (a.ndim - 1))
    blk = pl.BlockSpec((MEM_LEN, XA_W), lambda b: (b, 0))
    return pl.pallas_call(
        _memkv_kernel,
        grid=(batch,),
        in_specs=[pl.BlockSpec((MEM_LEN, D_MODEL), lambda b: (b, 0)), lay(wk), lay(wv)],
        out_specs=[blk, blk],
        out_shape=[jax.ShapeDtypeStruct((batch * MEM_LEN, XA_W), BF16)] * 2,
        compiler_params=_cparams("parallel"),
        name="memory_kv",
    )(mem2d, wk, wv)


def _route_rows(scores, biased):
    s = [scores[e:e + 1, :] for e in range(N_EXPERTS)]
    b = [biased[e:e + 1, :] for e in range(N_EXPERTS)]
    group_scores = []
    for g in range(N_GROUPS):
        b0, b1, b2, b3 = b[4 * g:4 * g + 4]
        hi01, lo01 = jnp.maximum(b0, b1), jnp.minimum(b0, b1)
        hi23, lo23 = jnp.maximum(b2, b3), jnp.minimum(b2, b3)
        top1 = jnp.maximum(hi01, hi23)
        top2 = jnp.maximum(jnp.minimum(hi01, hi23), jnp.maximum(lo01, lo23))
        group_scores.append(top1 + top2)
    best = group_scores[0]
    sel = jnp.zeros_like(best, dtype=jnp.int32)
    for g in range(1, N_GROUPS):
        upd = group_scores[g] > best
        sel = jnp.where(upd, g, sel)
        best = jnp.where(upd, group_scores[g], best)

    def pick(rows, j):
        out = rows[j]
        for g in range(1, N_GROUPS):
            out = jnp.where(sel == g, rows[4 * g + j], out)
        return out

    ib = [pick(b, j) for j in range(EXPERTS_PER_GROUP)]
    isc = [pick(s, j) for j in range(EXPERTS_PER_GROUP)]

    def argmax4(vals):
        bv, bi = vals[0], jnp.zeros_like(sel)
        for j in range(1, EXPERTS_PER_GROUP):
            upd = vals[j] > bv
            bi = jnp.where(upd, j, bi)
            bv = jnp.where(upd, vals[j], bv)
        return bi

    i1 = argmax4(ib)
    i2 = argmax4([jnp.where(i1 == j, -jnp.inf, ib[j]) for j in range(EXPERTS_PER_GROUP)])

    def take(vals, idx):
        out = vals[0]
        for j in range(1, EXPERTS_PER_GROUP):
            out = jnp.where(idx == j, vals[j], out)
        return out

    g1, g2 = take(isc, i1), take(isc, i2)
    den = g1 + g2
    g1, g2 = g1 / den, g2 / den
    e1 = sel * EXPERTS_PER_GROUP + i1
    e2 = sel * EXPERTS_PER_GROUP + i2
    rows = [jnp.where(e1 == e, g1, 0.0) + jnp.where(e2 == e, g2, 0.0) for e in range(N_EXPERTS)]
    return jnp.concatenate(rows, axis=0)


def _mix_xa_kernel(x_ref, rs_ref, at_ref, wout_ref, g1_ref, b1_ref, wq_ref, km_ref, vm_ref, wo_ref,
                   g2_ref, b2_ref, rw_ref, rb_ref, x2_ref, comb_ref, *, tm):
    mix = _dot(rs_ref[...], wout_ref[0:512, :]) + _dot(at_ref[...], wout_ref[512:1024, :])
    x1 = _layer_norm(ALPHA * x_ref[...] + mix, g1_ref[...], b1_ref[...])
    q = (_dot(x1.astype(BF16), wq_ref[...]) * (XA_DIM ** -0.5 * math.log2(math.e))).astype(BF16)
    heads = []
    for h in range(XA_HEADS):
        sl = slice(h * XA_DIM, (h + 1) * XA_DIM)
        s = _dot_nt(q[:, sl], km_ref[:, sl])
        p = jnp.exp2(s - jnp.max(s, axis=-1, keepdims=True))
        o = _dot(p.astype(BF16), vm_ref[:, sl])
        heads.append(o / jnp.sum(p, axis=-1, keepdims=True))
    xa = _dot(jnp.concatenate(heads, axis=1).astype(BF16), wo_ref[...])
    x2 = _layer_norm(ALPHA * x1 + xa, g2_ref[...], b2_ref[...])
    x2_ref[...] = x2

    x_hi, x_lo = _split_bf16(x2)
    w_hi, w_lo = _split_bf16(rw_ref[...])
    logits = _dot_nt(w_hi, x_hi) + (_dot_nt(w_hi, x_lo) + _dot_nt(w_lo, x_hi))
    scores = 1.0 / (1.0 + jnp.exp(-logits))
    comb = _route_rows(scores, scores + rb_ref[...])
    comb_ref[...] = jnp.concatenate([comb, jnp.zeros((128 - N_EXPERTS, tm), F32)], axis=0).T


def _mix_xa(x2d, rs, at, km, vm, l, w_out, g1, b1, wq, wo, g2, b2, rw_t, rb_col, batch, seq, tm):
    t = x2d.shape[0]
    nb = seq // tm
    row = lambda w: pl.BlockSpec((tm, w), lambda b, i: (b * nb + i, 0))
    lay = lambda a: pl.BlockSpec((None,) + a.shape[1:], lambda b, i: (l,) + (0,) * (a.ndim - 1))
    full = lambda a: pl.BlockSpec(a.shape, lambda b, i: (0,) * a.ndim)
    memb = pl.BlockSpec((MEM_LEN, XA_W), lambda b, i: (b, 0))
    return pl.pallas_call(
        functools.partial(_mix_xa_kernel, tm=tm),
        grid=(batch, nb),
        in_specs=[row(D_MODEL), row(512), row(512), lay(w_out), lay(g1), lay(b1), lay(wq), memb, memb,
                  lay(wo), lay(g2), lay(b2), full(rw_t), full(rb_col)],
        out_specs=[row(D_MODEL), row(128)],
        out_shape=[jax.ShapeDtypeStruct((t, D_MODEL), F32), jax.ShapeDtypeStruct((t, 128), F32)],
        compiler_params=_cparams("parallel", "parallel"),
        name="mix_xattn_router",
    )(x2d, rs, at, w_out, g1, b1, wq, km, vm, wo, g2, b2, rw_t, rb_col)


def _moe_kernel(x_ref, comb_ref, wg_ref, wu_ref, wd_ref, g_ref, b_ref, o_ref, acc_ref):
    e = pl.program_id(1)

    @pl.when(e == 0)
    def _():
        acc_ref[...] = jnp.zeros_like(acc_ref)

    xb = x_ref[...].astype(BF16)
    comb = comb_ref[...]
    lane = lax.broadcasted_iota(jnp.int32, comb.shape, 1)
    c = jnp.sum(jnp.where(lane == e, comb, 0.0), axis=-1, keepdims=True)
    hid = _silu(_dot(xb, wg_ref[...])) * _dot(xb, wu_ref[...])
    acc_ref[...] += _dot((hid * c).astype(BF16), wd_ref[...])

    @pl.when(e == N_EXPERTS - 1)
    def _():
        o_ref[...] = _layer_norm(ALPHA * x_ref[...] + acc_ref[...], g_ref[...], b_ref[...])


def _moe(x2, comb, l, wg, wu, wd, g, b, tm):
    t = x2.shape[0]
    row = lambda w: pl.BlockSpec((tm, w), lambda i, e: (i, 0))
    lay = lambda a: pl.BlockSpec((None,) + a.shape[1:], lambda i, e: (l,) + (0,) * (a.ndim - 1))
    exp = lambda a: pl.BlockSpec((None, None) + a.shape[2:], lambda i, e: (l, e, 0, 0))
    return pl.pallas_call(
        _moe_kernel,
        grid=(t // tm, N_EXPERTS),
        in_specs=[row(D_MODEL), row(128), exp(wg), exp(wu), exp(wd), lay(g), lay(b)],
        out_specs=row(D_MODEL),
        out_shape=jax.ShapeDtypeStruct((t, D_MODEL), F32),
        scratch_shapes=[pltpu.VMEM((tm, D_MODEL), F32)],
        compiler_params=_cparams("parallel", "arbitrary"),
        name="moe_experts",
    )(x2, comb, wg, wu, wd, g, b)


def _tile(n, pref):
    t = min(n, pref)
    assert n % t == 0, (n, t)
    return t


def kernel(x, mem, positions, w_in, w_out, sgu_ln_g, sgu_ln_b, sgu_w, sgu_b, mla_q_norm_g, mla_w_uq, mla_kv_norm_g, mla_w_ukv, xa_wq, xa_wk, xa_wv, xa_wo, ln_mix_g, ln_mix_b, ln_xa_g, ln_xa_b, ln_moe_g, ln_moe_b, router_w, router_bias, expert_w_gate, expert_w_up, expert_w_down):
    batch, seq, _ = x.shape
    depth = w_in.shape[0]
    t = batch * seq
    assert seq % CHUNK == 0

    w_in_p = jnp.concatenate(
        [w_in[:, :, :1920], jnp.zeros((depth, D_MODEL, 64), F32), w_in[:, :, 1920:1952],
         jnp.zeros((depth, D_MODEL, 32), F32)], axis=2).astype(BF16)
    w_uq_p = jnp.pad(mla_w_uq.reshape(depth, MLA_Q_RANK, MLA_HEADS, MLA_NOPE + MLA_ROPE),
                     ((0, 0), (0, 0), (0, 0), (0, HEAD_PAD - MLA_NOPE - MLA_ROPE))
                     ).reshape(depth, MLA_Q_RANK, MLA_HEADS * HEAD_PAD).astype(BF16)
    ukv = mla_w_ukv.reshape(depth, MLA_KV_RANK, MLA_HEADS, MLA_NOPE + MLA_V)
    w_uk_p = jnp.pad(ukv[..., :MLA_NOPE], ((0, 0), (0, 0), (0, 0), (0, HEAD_PAD - MLA_NOPE))
                     ).reshape(depth, MLA_KV_RANK, MLA_HEADS * HEAD_PAD)
    w_uv_p = jnp.pad(ukv[..., MLA_NOPE:], ((0, 0), (0, 0), (0, 0), (0, HEAD_PAD - MLA_V))
                     ).reshape(depth, MLA_KV_RANK, MLA_HEADS * HEAD_PAD)
    w_ukv_p = jnp.concatenate([w_uk_p, w_uv_p], axis=2).astype(BF16)
    w_out_b = w_out.astype(BF16)
    wq_b, wk_b, wv_b, wo_b = (a.astype(BF16) for a in (xa_wq, xa_wk, xa_wv, xa_wo))
    wg_b, wu_b, wd_b = (a.astype(BF16) for a in (expert_w_gate, expert_w_up, expert_w_down))
    vec = lambda a: a.reshape(depth, 1, a.shape[-1])
    sgu_w_cat = jnp.transpose(sgu_w, (0, 2, 1, 3)).reshape(depth, CHUNK, SGU_GROUPS * CHUNK)
    sgu_b_lane = jnp.repeat(jnp.transpose(sgu_b, (0, 2, 1)), SGU_W // SGU_GROUPS, axis=2)
    rw_t = router_w.T
    rb_col = router_bias.reshape(N_EXPERTS, 1)
    consts = _retention_consts()

    x2d = x.reshape(t, D_MODEL)
    mem2d = mem.reshape(batch * MEM_LEN, D_MODEL)
    cr, sr, cm, sm = _rope_tables(positions, _tile(t, 512))

    tm_proj = _tile(t, 512)
    tb = _tile(seq, 512)
    tq = _tile(seq, 1024)
    tm_mix = _tile(seq, 256)
    tm_moe = _tile(t, 1024)
    for l in range(depth):
        rs, q, k, v = _projections(x2d, l, w_in_p, w_uq_p, w_ukv_p, vec(sgu_ln_g), vec(sgu_ln_b),
                                   vec(mla_q_norm_g), vec(mla_kv_norm_g), cr, sr, cm, sm, tm_proj)
        retsgu = _retention_sgu(rs, l, consts, sgu_w_cat, sgu_b_lane, batch, seq, tb)
        attn = _flash_attention(q, k, v, batch, seq, tq, 2)
        km, vm = _memory_kv(mem2d, l, wk_b, wv_b, batch)
        x2, comb = _mix_xa(x2d, retsgu, attn, km, vm, l, w_out_b, vec(ln_mix_g), vec(ln_mix_b), wq_b, wo_b,
                           vec(ln_xa_g), vec(ln_xa_b), rw_t, rb_col, batch, seq, tm_mix)
        x2d = _moe(x2, comb, l, wg_b, wu_b, wd_b, vec(ln_moe_g), vec(ln_moe_b), tm_moe)
    return x2d.reshape(batch, seq, D_MODEL)
```

```python
import functools
import math

import jax
import jax.numpy as jnp
from jax import lax
from jax.experimental import pallas as pl
from jax.experimental.pallas import tpu as pltpu

F32 = jnp.float32
BF16 = jnp.bfloat16

D_MODEL = 1024
DEPTH = 4
MEM_LEN = 256
ROPE_THETA = 10000.0

RET_HEADS = 4
RET_DK = 64
RET_W = 256
CHUNK = 128

SGU_GROUPS = 4
SGU_W = 256

MLA_HEADS = 8
MLA_Q_RANK = 256
MLA_KV_RANK = 128
MLA_NOPE = 64
MLA_ROPE = 32
MLA_V = 64
MLA_W = MLA_HEADS * MLA_V
HEAD_PAD = 128

XA_HEADS = 4
XA_DIM = 128
XA_W = XA_HEADS * XA_DIM

N_EXPERTS = 16
N_GROUPS = 4
EXPERTS_PER_GROUP = 4
D_EXPERT = 256

ALPHA = (2 * DEPTH) ** 0.25
LN_EPS = 1e-5
IN_PAD = 2048
NEG_BIG = -1e30
MOE_TILE = 128
RANK_BITS = 16
RANK_RADIX = 1 << RANK_BITS

VMEM_LIMIT = 56 * 1024 * 1024


def _cparams(*sem):
    return pltpu.CompilerParams(dimension_semantics=sem, vmem_limit_bytes=VMEM_LIMIT)


def _dot(a, b):
    return jnp.dot(a, b, preferred_element_type=F32)


def _dot_nt(a, b):
    return lax.dot_general(a, b, (((1,), (1,)), ((), ())), preferred_element_type=F32)


def _dot_tn(a, b):
    return lax.dot_general(a, b, (((0,), (0,)), ((), ())), preferred_element_type=F32)


def _layer_norm(z, g, b):
    mu = jnp.mean(z, axis=-1, keepdims=True)
    zc = z - mu
    var = jnp.mean(zc * zc, axis=-1, keepdims=True)
    return zc * lax.rsqrt(var + LN_EPS) * g + b


def _rms_norm(z, g):
    ms = jnp.mean(z * z, axis=-1, keepdims=True)
    return z * lax.rsqrt(ms + LN_EPS) * g


def _silu(z):
    return z / (1.0 + jnp.exp(-z))


def _rope(x, cos, sin_signed, half):
    w = x.shape[-1]
    lane = lax.broadcasted_iota(jnp.int32, x.shape, 1)
    rot = jnp.where((lane & half) == 0, pltpu.roll(x, w - half, 1), pltpu.roll(x, half, 1))
    return x * cos + rot * sin_signed


def _split_bf16(x):
    hi = x.astype(BF16)
    lo = (x - hi.astype(F32)).astype(BF16)
    return hi, lo


def _tables_kernel(pos_ref, invf_ref, cr_ref, sr_ref, cm_ref, sm_ref):
    ang = pos_ref[...].astype(F32) * invf_ref[...]
    c = jnp.cos(ang)
    s = jnp.sin(ang)
    lane = lax.broadcasted_iota(jnp.int32, c.shape, 1)

    def tile32(v):
        v0 = jnp.where(lane < 32, v, 0.0)
        v1 = v0 + pltpu.roll(v0, 32, 1)
        return v1 + pltpu.roll(v1, 64, 1)

    ct = tile32(c)
    st = tile32(s) * jnp.where((lane & 32) == 0, -1.0, 1.0)
    cr_ref[...] = jnp.concatenate([ct, ct], axis=1)
    sr_ref[...] = jnp.concatenate([st, st], axis=1)
    in_src = (lane >= 32) & (lane < 48)
    cmv = jnp.where(in_src, c, 0.0)
    smv = jnp.where(in_src, s, 0.0)
    in_dst = (lane >= 64) & (lane < 96)
    cm_ref[...] = jnp.where(in_dst, pltpu.roll(cmv, 32, 1) + pltpu.roll(cmv, 48, 1), 1.0)
    sm_ref[...] = jnp.where(in_dst, pltpu.roll(smv, 48, 1) - pltpu.roll(smv, 32, 1), 0.0)


def _rope_tables(positions, tm):
    t = positions.size
    pos = positions.reshape(t, 1)
    fr = ROPE_THETA ** (-jnp.arange(0, RET_DK, 2, dtype=F32) / RET_DK)
    fm = ROPE_THETA ** (-jnp.arange(0, MLA_ROPE, 2, dtype=F32) / MLA_ROPE)
    invf = jnp.concatenate([fr, fm, jnp.zeros((128 - 48,), F32)]).reshape(1, 128)
    return pl.pallas_call(
        _tables_kernel,
        grid=(t // tm,),
        in_specs=[pl.BlockSpec((tm, 1), lambda i: (i, 0)),
                  pl.BlockSpec((1, 128), lambda i: (0, 0))],
        out_specs=[pl.BlockSpec((tm, 256), lambda i: (i, 0)),
                   pl.BlockSpec((tm, 256), lambda i: (i, 0)),
                   pl.BlockSpec((tm, 128), lambda i: (i, 0)),
                   pl.BlockSpec((tm, 128), lambda i: (i, 0))],
        out_shape=[jax.ShapeDtypeStruct((t, 256), F32), jax.ShapeDtypeStruct((t, 256), F32),
                   jax.ShapeDtypeStruct((t, 128), F32), jax.ShapeDtypeStruct((t, 128), F32)],
        compiler_params=_cparams("parallel"),
        name="rope_tables",
    )(pos, invf)


def _proj_kernel(x_ref, win_ref, wuq_ref, wukv_ref, lng_ref, lnb_ref, qg_ref, kvg_ref,
                 cr_ref, sr_ref, cm_ref, sm_ref, rs_ref, q_ref, k_ref, v_ref, *, q_scale):
    h = _dot(x_ref[...].astype(BF16), win_ref[...])
    cr = cr_ref[...]
    sr = sr_ref[...]
    rs_ref[:, 0:256] = _rope(h[:, 0:256], cr, sr, 32).astype(BF16)
    rs_ref[:, 256:512] = (_rope(h[:, 256:512], cr, sr, 32) * (RET_DK ** -0.5)).astype(BF16)
    rs_ref[:, 512:768] = h[:, 512:768].astype(BF16)
    rs_ref[:, 768:1024] = _silu(h[:, 768:1024]).astype(BF16)
    rs_ref[:, 1024:1280] = h[:, 1024:1280].astype(BF16)
    rs_ref[:, 1280:1536] = _layer_norm(h[:, 1280:1536], lng_ref[...], lnb_ref[...]).astype(BF16)

    cm = cm_ref[...]
    sm = sm_ref[...]
    cm8 = jnp.concatenate([cm] * MLA_HEADS, axis=1)
    sm8 = jnp.concatenate([sm] * MLA_HEADS, axis=1)
    cq = _rms_norm(h[:, 1536:1792], qg_ref[...]).astype(BF16)
    q = _rope(_dot(cq, wuq_ref[...]), cm8, sm8, 16)
    q_ref[...] = (q * q_scale).astype(BF16)
    ckv = _rms_norm(h[:, 1792:1920], kvg_ref[...]).astype(BF16)
    kv = _dot(ckv, wukv_ref[...])
    kr = _rope(h[:, 1920:2048], cm, sm, 16)
    k_ref[...] = (kv[:, 0:1024] + jnp.concatenate([kr] * MLA_HEADS, axis=1)).astype(BF16)
    lane = lax.broadcasted_iota(jnp.int32, (kv.shape[0], MLA_HEADS * HEAD_PAD), 1)
    ones_lane = jnp.where((lane & (HEAD_PAD - 1)) == MLA_V, 1.0, 0.0)
    v_ref[...] = (kv[:, 1024:2048] + ones_lane).astype(BF16)


def _projections(x2d, l, w_in, w_uq, w_ukv, sgu_g, sgu_b, q_g, kv_g, cr, sr, cm, sm, tm):
    t = x2d.shape[0]
    row = lambda w: pl.BlockSpec((tm, w), lambda i: (i, 0))
    lay = lambda a: pl.BlockSpec((None,) + a.shape[1:], lambda i: (l,) + (0,) * (a.ndim - 1))
    q_scale = (MLA_NOPE + MLA_ROPE) ** -0.5 * math.log2(math.e)
    return pl.pallas_call(
        functools.partial(_proj_kernel, q_scale=q_scale),
        grid=(t // tm,),
        in_specs=[row(D_MODEL), lay(w_in), lay(w_uq), lay(w_ukv), lay(sgu_g), lay(sgu_b),
                  lay(q_g), lay(kv_g), row(256), row(256), row(128), row(128)],
        out_specs=[row(1536), row(1024), row(1024), row(1024)],
        out_shape=[jax.ShapeDtypeStruct((t, 1536), BF16)] + [jax.ShapeDtypeStruct((t, 1024), BF16)] * 3,
        compiler_params=_cparams("parallel"),
        name="projections",
    )(x2d, w_in, w_uq, w_ukv, sgu_g, sgu_b, q_g, kv_g, cr, sr, cm, sm)


def _retsgu_kernel(rs_ref, dmat_ref, qdec_ref, kdec_ref, cdec_ref, gavg_ref, sw_ref, sb_ref,
                   out_ref, state_ref, *, n_chunks):
    @pl.when(pl.program_id(1) == 0)
    def _():
        state_ref[...] = jnp.zeros_like(state_ref)

    lane = lax.broadcasted_iota(jnp.int32, (CHUNK, RET_W), 1)
    head_of_lane = lane // RET_DK
    hmask = [head_of_lane == h for h in range(RET_HEADS)]
    row_i = lax.broadcasted_iota(jnp.int32, (CHUNK, SGU_GROUPS * CHUNK), 0)
    col_i = lax.broadcasted_iota(jnp.int32, (CHUNK, SGU_GROUPS * CHUNK), 1)
    sw = jnp.where(row_i >= (col_i & (CHUNK - 1)), sw_ref[...], 0.0).astype(BF16)
    blk = (lax.broadcasted_iota(jnp.int32, (RET_W, RET_W), 0) // RET_DK
           == lax.broadcasted_iota(jnp.int32, (RET_W, RET_W), 1) // RET_DK)
    gavg = gavg_ref[...]
    zero = jnp.zeros((), BF16)

    def group_mean(y):
        hi, lo = _split_bf16(y)
        return _dot(jnp.concatenate([hi, lo], axis=1), gavg)

    def chunk(c, carry):
        r0 = pl.multiple_of(c * CHUNK, CHUNK)
        rq = rs_ref[pl.ds(r0, CHUNK), 0:256]
        rk = rs_ref[pl.ds(r0, CHUNK), 256:512]
        rv = rs_ref[pl.ds(r0, CHUNK), 512:768]
        gate = rs_ref[pl.ds(r0, CHUNK), 768:1024].astype(F32)
        su = rs_ref[pl.ds(r0, CHUNK), 1024:1280].astype(F32)
        sv = rs_ref[pl.ds(r0, CHUNK), 1280:1536]

        q_heads = jnp.concatenate([jnp.where(m, rq, zero) for m in hmask], axis=0)
        scores = _dot_nt(q_heads, rk) * dmat_ref[...]
        p_cat = jnp.concatenate(
            [scores[h * CHUNK:(h + 1) * CHUNK, :] for h in range(RET_HEADS)], axis=1).astype(BF16)
        v_heads = jnp.concatenate([jnp.where(m, rv, zero) for m in hmask], axis=0)
        intra = _dot(p_cat, v_heads)
        state = state_ref[...]
        cross = _dot((rq.astype(F32) * qdec_ref[...]).astype(BF16), state.astype(BF16))
        kv = _dot_tn((rk.astype(F32) * kdec_ref[...]).astype(BF16), rv)
        state_ref[...] = state * cdec_ref[...] + jnp.where(blk, kv, 0.0)

        y = intra + cross
        yc = y - group_mean(y)
        var = group_mean(yc * yc)
        ret = gate * (yc * lax.rsqrt(var + LN_EPS))

        sv_groups = jnp.concatenate([jnp.where(m, sv, zero) for m in hmask], axis=0)
        mixed = _dot(sw, sv_groups) + sb_ref[...]
        out_ref[pl.ds(r0, CHUNK), 0:256] = ret.astype(BF16)
        out_ref[pl.ds(r0, CHUNK), 256:512] = (su * mixed).astype(BF16)
        return carry

    lax.fori_loop(0, n_chunks, chunk, 0)


def _retention_sgu(rs, l, consts, sgu_w_cat, sgu_b_lane, batch, seq, tb):
    dmat, qdec, kdec, cdec, gavg = consts
    n_blocks = seq // tb
    full = lambda a: pl.BlockSpec(a.shape, lambda b, i: (0,) * a.ndim)
    lay = lambda a: pl.BlockSpec((None,) + a.shape[1:], lambda b, i: (l,) + (0,) * (a.ndim - 1))
    return pl.pallas_call(
        functools.partial(_retsgu_kernel, n_chunks=tb // CHUNK),
        grid=(batch, n_blocks),
        in_specs=[pl.BlockSpec((tb, 1536), lambda b, i: (b * n_blocks + i, 0)),
                  full(dmat), full(qdec), full(kdec), full(cdec), full(gavg),
                  lay(sgu_w_cat), lay(sgu_b_lane)],
        out_specs=pl.BlockSpec((tb, 512), lambda b, i: (b * n_blocks + i, 0)),
        out_shape=jax.ShapeDtypeStruct((batch * seq, 512), BF16),
        scratch_shapes=[pltpu.VMEM((RET_W, RET_W), F32)],
        compiler_params=_cparams("parallel", "arbitrary"),
        name="retention_sgu",
    )(rs, dmat, qdec, kdec, cdec, gavg, sgu_w_cat, sgu_b_lane)


def _retention_consts():
    h = jnp.arange(RET_HEADS, dtype=F32)
    log_gamma = jnp.log1p(-(2.0 ** (-5.0 - h)))
    pos = jnp.arange(CHUNK, dtype=F32)
    diff = pos[:, None] - pos[None, :]
    intra = jnp.where(diff >= 0, jnp.exp(log_gamma[:, None, None] * jnp.maximum(diff, 0.0)), 0.0)
    dmat = intra.reshape(RET_HEADS * CHUNK, CHUNK)
    inner = jnp.exp(log_gamma[None, :] * (CHUNK - 1 - pos)[:, None])
    query = jnp.exp(log_gamma[None, :] * (pos + 1)[:, None])
    kdec = jnp.repeat(inner, RET_DK, axis=1)
    qdec = jnp.repeat(query, RET_DK, axis=1)
    chunk_decay = jnp.repeat(jnp.exp(log_gamma * CHUNK), RET_DK)
    blk = jnp.arange(RET_W)[:, None] // RET_DK == jnp.arange(RET_W)[None, :] // RET_DK
    cdec = jnp.where(blk, chunk_decay[:, None], 0.0)
    gavg = jnp.where(blk, 1.0 / RET_DK, 0.0).astype(BF16)
    return dmat, qdec, kdec, cdec, jnp.concatenate([gavg, gavg], axis=0)


def _flash_kernel(q_ref, k_ref, v_ref, o_ref, sa_ref, sb_ref, m_ref, acc_ref, *, tq, hps):
    i = pl.program_id(2)
    tk = tq // 2
    heads = range(hps)
    cols = [slice(hh * HEAD_PAD, (hh + 1) * HEAD_PAD) for hh in heads]

    def scores(j, hh, rows=slice(None)):
        r0 = pl.multiple_of(j * tk, tk)
        return _dot_nt(q_ref[rows, cols[hh]], k_ref[pl.ds(r0, tk), cols[hh]])

    def update(j, hh, s, rows=slice(None)):
        r0 = pl.multiple_of(j * tk, tk)
        m = m_ref[hh, rows, :]
        s_max = s[:, 0:HEAD_PAD]
        for c in range(1, tk // HEAD_PAD):
            s_max = jnp.maximum(s_max, s[:, c * HEAD_PAD:(c + 1) * HEAD_PAD])
        m_new = jnp.maximum(m, jnp.max(s_max, axis=-1, keepdims=True))
        p = jnp.exp2(s - jnp.concatenate([m_new] * (tk // HEAD_PAD), axis=1)).astype(BF16)
        acc_ref[hh, rows, :] = (jnp.exp2(m - m_new) * acc_ref[hh, rows, :]
                                + _dot(p, v_ref[pl.ds(r0, tk), cols[hh]]))
        m_ref[hh, rows, :] = m_new

    m_ref[...] = jnp.full(m_ref.shape, NEG_BIG, F32)
    acc_ref[...] = jnp.zeros(acc_ref.shape, F32)
    for hh in heads:
        sa_ref[hh] = scores(0, hh)

    def pair(t, carry):
        for hh in heads:
            sb_ref[hh] = scores(2 * t + 1, hh)
            update(2 * t, hh, sa_ref[hh])
        for hh in heads:
            sa_ref[hh] = scores(2 * t + 2, hh)
            update(2 * t + 1, hh, sb_ref[hh])
        return carry

    lax.fori_loop(0, i, pair, 0)
    low = slice(tk, tq)
    visible = (lax.broadcasted_iota(jnp.int32, (tq, tk), 1) <= lax.broadcasted_iota(jnp.int32, (tq, tk), 0))
    visible_low = (lax.broadcasted_iota(jnp.int32, (tk, tk), 1) <= lax.broadcasted_iota(jnp.int32, (tk, tk), 0))
    for hh in heads:
        sb_ref[hh, low, :] = scores(2 * i + 1, hh, low)
        update(2 * i, hh, jnp.where(visible, sa_ref[hh], NEG_BIG))
    for hh in heads:
        update(2 * i + 1, hh, jnp.where(visible_low, sb_ref[hh, low, :], NEG_BIG), low)
    lane = lax.broadcasted_iota(jnp.int32, (tq, HEAD_PAD), 1)
    for pr in range(hps // 2):
        o = []
        for hh in (2 * pr, 2 * pr + 1):
            acc = acc_ref[hh]
            o.append(acc / acc[:, MLA_V:MLA_V + 1])
        o_ref[:, pr * HEAD_PAD:(pr + 1) * HEAD_PAD] = jnp.where(
            lane < MLA_V, o[0], pltpu.roll(o[1], MLA_V, 1)).astype(BF16)


def _flash_attention(q, k, v, batch, seq, tq, hps):
    nq = seq // tq
    return pl.pallas_call(
        functools.partial(_flash_kernel, tq=tq, hps=hps),
        grid=(batch, MLA_HEADS // hps, nq),
        in_specs=[pl.BlockSpec((tq, hps * HEAD_PAD), lambda b, p, i: (b * nq + i, p)),
                  pl.BlockSpec((seq, hps * HEAD_PAD), lambda b, p, i: (b, p)),
                  pl.BlockSpec((seq, hps * HEAD_PAD), lambda b, p, i: (b, p))],
        out_specs=pl.BlockSpec((tq, hps * MLA_V), lambda b, p, i: (b * nq + i, p)),
        out_shape=jax.ShapeDtypeStruct((batch * seq, MLA_W), BF16),
        scratch_shapes=[pltpu.VMEM((hps, tq, tq // 2), F32), pltpu.VMEM((hps, tq, tq // 2), F32),
                        pltpu.VMEM((hps, tq, HEAD_PAD), F32), pltpu.VMEM((hps, tq, HEAD_PAD), F32)],
        compiler_params=_cparams("parallel", "parallel", "arbitrary"),
        name="flash_attention",
    )(q, k, v)


def _memkv_kernel(mem_ref, wk_ref, wv_ref, k_ref, v_ref):
    m = mem_ref[...].astype(BF16)
    k_ref[...] = _dot(m, wk_ref[...]).astype(BF16)
    v_ref[...] = _dot(m, wv_ref[...]).astype(BF16)


def _memory_kv(mem2d, l, wk, wv, batch):
    lay = lambda a: pl.BlockSpec((None,) + a.shape[1:], lambda b: (l,) + (0,) * (a.ndim - 1))
    blk = pl.BlockSpec((MEM_LEN, XA_W), lambda b: (b, 0))
    return pl.pallas_call(
        _memkv_kernel,
        grid=(batch,),
        in_specs=[pl.BlockSpec((MEM_LEN, D_MODEL), lambda b: (b, 0)), lay(wk), lay(wv)],
        out_specs=[blk, blk],
        out_shape=[jax.ShapeDtypeStruct((batch * MEM_LEN, XA_W), BF16)] * 2,
        compiler_params=_cparams("parallel"),
        name="memory_kv",
    )(mem2d, wk, wv)


def _route_rows(scores, biased):
    s = [scores[e:e + 1, :] for e in range(N_EXPERTS)]
    b = [biased[e:e + 1, :] for e in range(N_EXPERTS)]
    group_scores = []
    for g in range(N_GROUPS):
        b0, b1, b2, b3 = b[4 * g:4 * g + 4]
        hi01, lo01 = jnp.maximum(b0, b1), jnp.minimum(b0, b1)
        hi23, lo23 = jnp.maximum(b2, b3), jnp.minimum(b2, b3)
        top1 = jnp.maximum(hi01, hi23)
        top2 = jnp.maximum(jnp.minimum(hi01, hi23), jnp.maximum(lo01, lo23))
        group_scores.append(top1 + top2)
    best = group_scores[0]
    sel = jnp.zeros_like(best, dtype=jnp.int32)
    for g in range(1, N_GROUPS):
        upd = group_scores[g] > best
        sel = jnp.where(upd, g, sel)
        best = jnp.where(upd, group_scores[g], best)

    def pick(rows, j):
        out = rows[j]
        for g in range(1, N_GROUPS):
            out = jnp.where(sel == g, rows[4 * g + j], out)
        return out

    ib = [pick(b, j) for j in range(EXPERTS_PER_GROUP)]
    isc = [pick(s, j) for j in range(EXPERTS_PER_GROUP)]

    def argmax4(vals):
        bv, bi = vals[0], jnp.zeros_like(sel)
        for j in range(1, EXPERTS_PER_GROUP):
            upd = vals[j] > bv
            bi = jnp.where(upd, j, bi)
            bv = jnp.where(upd, vals[j], bv)
        return bi

    i1 = argmax4(ib)
    i2 = argmax4([jnp.where(i1 == j, -jnp.inf, ib[j]) for j in range(EXPERTS_PER_GROUP)])

    def take(vals, idx):
        out = vals[0]
        for j in range(1, EXPERTS_PER_GROUP):
            out = jnp.where(idx == j, vals[j], out)
        return out

    g1, g2 = take(isc, i1), take(isc, i2)
    den = g1 + g2
    g1, g2 = g1 / den, g2 / den
    e1 = sel * EXPERTS_PER_GROUP + i1
    e2 = sel * EXPERTS_PER_GROUP + i2
    return e1, e2, g1, g2


def _mix_xa_kernel(x_ref, rs_ref, at_ref, wout_ref, g1_ref, b1_ref, wq_ref, km_ref, vm_ref, wo_ref,
                   g2_ref, b2_ref, rw_ref, rb_ref, tri_ref, x2_ref, pk1_ref, pk2_ref, gt1_ref, gt2_ref,
                   cnt_ref, base_ref, *, tm, tiles_per_block):
    @pl.when(pl.program_id(1) % tiles_per_block == 0)
    def _():
        base_ref[...] = jnp.zeros_like(base_ref)

    mix = _dot(rs_ref[...], wout_ref[0:512, :]) + _dot(at_ref[...], wout_ref[512:1024, :])
    x1 = _layer_norm(ALPHA * x_ref[...] + mix, g1_ref[...], b1_ref[...])
    q = (_dot(x1.astype(BF16), wq_ref[...]) * (XA_DIM ** -0.5 * math.log2(math.e))).astype(BF16)
    heads = []
    for h in range(XA_HEADS):
        sl = slice(h * XA_DIM, (h + 1) * XA_DIM)
        s = _dot_nt(q[:, sl], km_ref[:, sl])
        p = jnp.exp2(s - jnp.max(s, axis=-1, keepdims=True))
        o = _dot(p.astype(BF16), vm_ref[:, sl])
        heads.append(o / jnp.sum(p, axis=-1, keepdims=True))
    xa = _dot(jnp.concatenate(heads, axis=1).astype(BF16), wo_ref[...])
    x2 = _layer_norm(ALPHA * x1 + xa, g2_ref[...], b2_ref[...])
    x2_ref[...] = x2

    x_hi, x_lo = _split_bf16(x2)
    w_hi, w_lo = _split_bf16(rw_ref[...])
    logits = _dot_nt(w_hi, x_hi) + (_dot_nt(w_hi, x_lo) + _dot_nt(w_lo, x_hi))
    scores = 1.0 / (1.0 + jnp.exp(-logits))
    e1, e2, gate1, gate2 = _route_rows(scores, scores + rb_ref[...])
    expert = lax.broadcasted_iota(jnp.int32, (N_EXPERTS, tm), 0)
    hit1 = expert == e1
    hit2 = expert == e2
    cnt = jnp.where(hit1, 1.0, jnp.where(hit2, 1.0, 0.0))
    base = base_ref[...]
    before = _dot(cnt.astype(BF16), tri_ref[...]) - cnt + base[:, 0:1]
    rank1 = jnp.sum(jnp.where(hit1, before, 0.0), axis=0, keepdims=True).astype(jnp.int32)
    rank2 = jnp.sum(jnp.where(hit2, before, 0.0), axis=0, keepdims=True).astype(jnp.int32)
    base = base + jnp.sum(cnt, axis=1, keepdims=True)
    base_ref[...] = base
    cnt_ref[...] = base
    pk1_ref[...] = e1 * RANK_RADIX + rank1
    pk2_ref[...] = e2 * RANK_RADIX + rank2
    gt1_ref[...] = gate1
    gt2_ref[...] = gate2


def _mix_xa(x2d, rs, at, km, vm, l, w_out, g1, b1, wq, wo, g2, b2, rw_t, rb_col, tri, batch, seq, tm, nblk):
    t = x2d.shape[0]
    nb = seq // tm
    n_tiles = t // tm
    row = lambda w: pl.BlockSpec((tm, w), lambda b, i: (b * nb + i, 0))
    lay = lambda a: pl.BlockSpec((None,) + a.shape[1:], lambda b, i: (l,) + (0,) * (a.ndim - 1))
    full = lambda a: pl.BlockSpec(a.shape, lambda b, i: (0,) * a.ndim)
    memb = pl.BlockSpec((MEM_LEN, XA_W), lambda b, i: (b, 0))
    tok = pl.BlockSpec((None, 1, tm), lambda b, i: (b * nb + i, 0, 0))
    tok_i = jax.ShapeDtypeStruct((n_tiles, 1, tm), jnp.int32)
    tok_f = jax.ShapeDtypeStruct((n_tiles, 1, tm), F32)
    return pl.pallas_call(
        functools.partial(_mix_xa_kernel, tm=tm, tiles_per_block=nblk // tm),
        grid=(batch, nb),
        in_specs=[row(D_MODEL), row(512), row(512), lay(w_out), lay(g1), lay(b1), lay(wq), memb, memb,
                  lay(wo), lay(g2), lay(b2), full(rw_t), full(rb_col), full(tri)],
        out_specs=[row(D_MODEL), tok, tok, tok, tok,
                   pl.BlockSpec((None, N_EXPERTS, 128), lambda b, i: (b * nb + i, 0, 0))],
        out_shape=[jax.ShapeDtypeStruct((t, D_MODEL), F32), tok_i, tok_i, tok_f, tok_f,
                   jax.ShapeDtypeStruct((n_tiles, N_EXPERTS, 128), F32)],
        scratch_shapes=[pltpu.VMEM((N_EXPERTS, 128), F32)],
        compiler_params=_cparams("parallel", "arbitrary"),
        name="mix_xattn_router",
    )(x2d, rs, at, w_out, g1, b1, wq, km, vm, wo, g2, b2, rw_t, rb_col, tri)


def _moe_kernel(cnt_ref, pk1_ref, pk2_ref, gt1_ref, gt2_ref, x_ref, wg_ref, wu_ref, wd_ref, g_ref, b_ref,
                o_ref, xg_ref, xs_ref, ys_ref, og_ref, off_ref, ntile_ref, pos1_ref, pos2_ref,
                *, ts, k, mp):
    s = pl.program_id(1)
    sg = ts + 8
    unroll = 8

    @pl.when(s == 0)
    def _():
        xs_ref[...] = jnp.zeros_like(xs_ref)
        start = jnp.int32(0)
        for e in range(N_EXPERTS):
            tiles = (cnt_ref[0, e] + (MOE_TILE - 1)) // MOE_TILE
            off_ref[e] = start
            ntile_ref[e] = tiles
            start = start + tiles * MOE_TILE

    @pl.when(s < k)
    def _():
        x = x_ref[...]
        half = D_MODEL // 2
        hi = pltpu.bitcast(x[:, :half].astype(BF16).astype(F32), jnp.uint32)
        lo = pltpu.bitcast(x[:, half:].astype(BF16).astype(F32), jnp.uint32)
        words = hi | (lo >> 16)
        for c in range(4):
            xg_ref[c * sg:c * sg + ts, :] = words[:, c * 128:(c + 1) * 128]

        def dispatch(tt, carry):
            for u in range(unroll):
                tl = tt * unroll + u
                tg = s * ts + tl
                a = pk1_ref[tg]
                b = pk2_ref[tg]
                p1 = off_ref[a >> RANK_BITS] + (a & (RANK_RADIX - 1))
                p2 = off_ref[b >> RANK_BITS] + (b & (RANK_RADIX - 1))
                pos1_ref[tg] = p1
                pos2_ref[tg] = p2
                row = xg_ref[pl.ds(tl, 4, stride=sg), :]
                xs_ref[pl.ds(p1, 4, stride=mp), :] = row
                xs_ref[pl.ds(p2, 4, stride=mp), :] = row
            return carry

        lax.fori_loop(0, ts // unroll, dispatch, 0)

    @pl.when((s >= k) & (s < k + N_EXPERTS))
    def _():
        e = s - k
        seg = off_ref[e]
        n_tiles = ntile_ref[e]

        def row_tile(i):
            r0 = pl.multiple_of(seg + i * MOE_TILE, MOE_TILE)
            words = [xs_ref[pl.ds(c * mp + r0, MOE_TILE), :] for c in range(4)]
            xb = jnp.concatenate(
                [pltpu.bitcast(w & jnp.uint32(0xFFFF0000), F32).astype(BF16) for w in words]
                + [pltpu.bitcast(w << 16, F32).astype(BF16) for w in words], axis=1)
            hid = _silu(_dot(xb, wg_ref[...])) * _dot(xb, wu_ref[...])
            y = _dot(hid.astype(BF16), wd_ref[...])
            for j in range(8):
                ys_ref[pl.ds(j * mp + r0, MOE_TILE), :] = y[:, j * 128:(j + 1) * 128]

        def tile_pair(i, carry):
            row_tile(2 * i)
            row_tile(jnp.minimum(2 * i + 1, n_tiles - 1))
            return carry

        lax.fori_loop(0, (n_tiles + 1) // 2, tile_pair, 0)

    @pl.when(s >= k + N_EXPERTS)
    def _():
        sub = s - (k + N_EXPERTS)

        def combine(tt, carry):
            for u in range(unroll):
                tl = tt * unroll + u
                tg = sub * ts + tl
                r = (ys_ref[pl.ds(pos1_ref[tg], 8, stride=mp), :] * gt1_ref[tg]
                     + ys_ref[pl.ds(pos2_ref[tg], 8, stride=mp), :] * gt2_ref[tg])
                og_ref[pl.ds(tl, 8, stride=sg), :] = r
            return carry

        lax.fori_loop(0, ts // unroll, combine, 0)
        ffn = jnp.concatenate([og_ref[j * sg:j * sg + ts, :] for j in range(8)], axis=1)
        o_ref[...] = _layer_norm(ALPHA * x_ref[...] + ffn, g_ref[...], b_ref[...])


def _moe(x2, cnt_blk, pk1, pk2, gt1, gt2, l, wg, wu, wd, g, b, nblk, ts):
    t = x2.shape[0]
    k = nblk // ts
    steps = 2 * k + N_EXPERTS
    mp = 2 * nblk + N_EXPERTS * MOE_TILE + 8
    smem = lambda: pl.BlockSpec((nblk,), lambda i, s: (i,), memory_space=pltpu.SMEM)
    lay = lambda a: pl.BlockSpec((None,) + a.shape[1:], lambda i, s: (l,) + (0,) * (a.ndim - 1))
    exp = lambda a: pl.BlockSpec((None, None) + a.shape[2:],
                                 lambda i, s: (l, jnp.clip(s - k, 0, N_EXPERTS - 1), 0, 0))
    x_map = lambda i, s: (i * k + jnp.where(s < k, s, jnp.where(s < k + N_EXPERTS, k - 1, s - k - N_EXPERTS)), 0)
    o_map = lambda i, s: (i * k + jnp.maximum(s - k - N_EXPERTS, 0), 0)
    return pl.pallas_call(
        functools.partial(_moe_kernel, ts=ts, k=k, mp=mp),
        grid=(t // nblk, steps),
        in_specs=[pl.BlockSpec((None, 1, N_EXPERTS), lambda i, s: (i, 0, 0), memory_space=pltpu.SMEM),
                  smem(), smem(), smem(), smem(),
                  pl.BlockSpec((ts, D_MODEL), x_map), exp(wg), exp(wu), exp(wd), lay(g), lay(b)],
        out_specs=pl.BlockSpec((ts, D_MODEL), o_map),
        out_shape=jax.ShapeDtypeStruct((t, D_MODEL), F32),
        scratch_shapes=[pltpu.VMEM((4 * (ts + 8), 128), jnp.uint32), pltpu.VMEM((4 * mp, 128), jnp.uint32),
                        pltpu.VMEM((8 * mp, 128), F32), pltpu.VMEM((8 * (ts + 8), 128), F32),
                        pltpu.SMEM((N_EXPERTS,), jnp.int32), pltpu.SMEM((N_EXPERTS,), jnp.int32),
                        pltpu.SMEM((nblk,), jnp.int32), pltpu.SMEM((nblk,), jnp.int32)],
        compiler_params=_cparams("arbitrary", "arbitrary"),
        name="moe_experts",
    )(cnt_blk, pk1, pk2, gt1, gt2, x2, wg, wu, wd, g, b)


def _tile(n, pref):
    t = min(n, pref)
    assert n % t == 0, (n, t)
    return t


def kernel(x, mem, positions, w_in, w_out, sgu_ln_g, sgu_ln_b, sgu_w, sgu_b, mla_q_norm_g, mla_w_uq, mla_kv_norm_g, mla_w_ukv, xa_wq, xa_wk, xa_wv, xa_wo, ln_mix_g, ln_mix_b, ln_xa_g, ln_xa_b, ln_moe_g, ln_moe_b, router_w, router_bias, expert_w_gate, expert_w_up, expert_w_down):
    batch, seq, _ = x.shape
    depth = w_in.shape[0]
    t = batch * seq
    assert seq % CHUNK == 0

    w_in_p = jnp.concatenate(
        [w_in[:, :, :1920], jnp.zeros((depth, D_MODEL, 64), F32), w_in[:, :, 1920:1952],
         jnp.zeros((depth, D_MODEL, 32), F32)], axis=2).astype(BF16)
    w_uq_p = jnp.pad(mla_w_uq.reshape(depth, MLA_Q_RANK, MLA_HEADS, MLA_NOPE + MLA_ROPE),
                     ((0, 0), (0, 0), (0, 0), (0, HEAD_PAD - MLA_NOPE - MLA_ROPE))
                     ).reshape(depth, MLA_Q_RANK, MLA_HEADS * HEAD_PAD).astype(BF16)
    ukv = mla_w_ukv.reshape(depth, MLA_KV_RANK, MLA_HEADS, MLA_NOPE + MLA_V)
    w_uk_p = jnp.pad(ukv[..., :MLA_NOPE], ((0, 0), (0, 0), (0, 0), (0, HEAD_PAD - MLA_NOPE))
                     ).reshape(depth, MLA_KV_RANK, MLA_HEADS * HEAD_PAD)
    w_uv_p = jnp.pad(ukv[..., MLA_NOPE:], ((0, 0), (0, 0), (0, 0), (0, HEAD_PAD - MLA_V))
                     ).reshape(depth, MLA_KV_RANK, MLA_HEADS * HEAD_PAD)
    w_ukv_p = jnp.concatenate([w_uk_p, w_uv_p], axis=2).astype(BF16)
    w_out_b = w_out.astype(BF16)
    wq_b, wk_b, wv_b, wo_b = (a.astype(BF16) for a in (xa_wq, xa_wk, xa_wv, xa_wo))
    wg_b, wu_b, wd_b = (a.astype(BF16) for a in (expert_w_gate, expert_w_up, expert_w_down))
    vec = lambda a: a.reshape(depth, 1, a.shape[-1])
    sgu_w_cat = jnp.transpose(sgu_w, (0, 2, 1, 3)).reshape(depth, CHUNK, SGU_GROUPS * CHUNK)
    sgu_b_lane = jnp.repeat(jnp.transpose(sgu_b, (0, 2, 1)), SGU_W // SGU_GROUPS, axis=2)
    rw_t = router_w.T
    rb_col = router_bias.reshape(N_EXPERTS, 1)
    consts = _retention_consts()

    x2d = x.reshape(t, D_MODEL)
    mem2d = mem.reshape(batch * MEM_LEN, D_MODEL)
    cr, sr, cm, sm = _rope_tables(positions, _tile(t, 512))

    tm_proj = _tile(t, 512)
    tb = _tile(seq, 512)
    tq = _tile(seq, 1024)
    tm_mix = _tile(seq, 256)
    nblk = _tile(seq, 2048)
    ts_moe = _tile(nblk, 512)
    tiles_per_block = nblk // tm_mix
    tri = (jnp.arange(tm_mix)[:, None] <= jnp.arange(tm_mix)[None, :]).astype(BF16)
    for l in range(depth):
        rs, q, k, v = _projections(x2d, l, w_in_p, w_uq_p, w_ukv_p, vec(sgu_ln_g), vec(sgu_ln_b),
                                   vec(mla_q_norm_g), vec(mla_kv_norm_g), cr, sr, cm, sm, tm_proj)
        retsgu = _retention_sgu(rs, l, consts, sgu_w_cat, sgu_b_lane, batch, seq, tb)
        attn = _flash_attention(q, k, v, batch, seq, tq, 2)
        km, vm = _memory_kv(mem2d, l, wk_b, wv_b, batch)
        x2, pk1, pk2, gt1, gt2, cnts = _mix_xa(
            x2d, retsgu, attn, km, vm, l, w_out_b, vec(ln_mix_g), vec(ln_mix_b), wq_b, wo_b,
            vec(ln_xa_g), vec(ln_xa_b), rw_t, rb_col, tri, batch, seq, tm_mix, nblk)
        cnt_blk = cnts[tiles_per_block - 1::tiles_per_block, :, 0].astype(jnp.int32).reshape(-1, 1, N_EXPERTS)
        x2d = _moe(x2, cnt_blk, pk1.reshape(t), pk2.reshape(t), gt1.reshape(t), gt2.reshape(t), l,
                   wg_b, wu_b, wd_b, vec(ln_moe_g), vec(ln_moe_b), nblk, ts_moe)
    return x2d.reshape(batch, seq, D_MODEL)
```

```python
import functools
import math

import jax
import jax.numpy as jnp
from jax import lax
from jax.experimental import pallas as pl
from jax.experimental.pallas import tpu as pltpu

F32 = jnp.float32
BF16 = jnp.bfloat16

D_MODEL = 1024
DEPTH = 4
MEM_LEN = 256
ROPE_THETA = 10000.0

RET_HEADS = 4
RET_DK = 64
RET_W = 256
CHUNK = 128

SGU_GROUPS = 4
SGU_W = 256

MLA_HEADS = 8
MLA_Q_RANK = 256
MLA_KV_RANK = 128
MLA_NOPE = 64
MLA_ROPE = 32
MLA_V = 64
MLA_W = MLA_HEADS * MLA_V
HEAD_PAD = 128

XA_HEADS = 4
XA_DIM = 128
XA_W = XA_HEADS * XA_DIM

N_EXPERTS = 16
N_GROUPS = 4
EXPERTS_PER_GROUP = 4
D_EXPERT = 256

ALPHA = (2 * DEPTH) ** 0.25
LN_EPS = 1e-5
IN_PAD = 2048
NEG_BIG = -1e30
MOE_TILE = 128
TILES_PER_TRIP = 3
RANK_BITS = 16
RANK_RADIX = 1 << RANK_BITS

VMEM_LIMIT = 56 * 1024 * 1024


def _cparams(*sem):
    return pltpu.CompilerParams(dimension_semantics=sem, vmem_limit_bytes=VMEM_LIMIT)


def _dot(a, b):
    return jnp.dot(a, b, preferred_element_type=F32)


def _dot_nt(a, b):
    return lax.dot_general(a, b, (((1,), (1,)), ((), ())), preferred_element_type=F32)


def _dot_tn(a, b):
    return lax.dot_general(a, b, (((0,), (0,)), ((), ())), preferred_element_type=F32)


def _layer_norm(z, g, b):
    mu = jnp.mean(z, axis=-1, keepdims=True)
    zc = z - mu
    var = jnp.mean(zc * zc, axis=-1, keepdims=True)
    return zc * lax.rsqrt(var + LN_EPS) * g + b


def _rms_norm(z, g):
    ms = jnp.mean(z * z, axis=-1, keepdims=True)
    return z * lax.rsqrt(ms + LN_EPS) * g


def _silu(z):
    return z / (1.0 + jnp.exp(-z))


def _rope(x, cos, sin_signed, half):
    w = x.shape[-1]
    lane = lax.broadcasted_iota(jnp.int32, x.shape, 1)
    rot = jnp.where((lane & half) == 0, pltpu.roll(x, w - half, 1), pltpu.roll(x, half, 1))
    return x * cos + rot * sin_signed


def _split_bf16(x):
    hi = x.astype(BF16)
    lo = (x - hi.astype(F32)).astype(BF16)
    return hi, lo


def _tables_kernel(pos_ref, invf_ref, cr_ref, sr_ref, cm_ref, sm_ref):
    ang = pos_ref[...].astype(F32) * invf_ref[...]
    c = jnp.cos(ang)
    s = jnp.sin(ang)
    lane = lax.broadcasted_iota(jnp.int32, c.shape, 1)

    def tile32(v):
        v0 = jnp.where(lane < 32, v, 0.0)
        v1 = v0 + pltpu.roll(v0, 32, 1)
        return v1 + pltpu.roll(v1, 64, 1)

    ct = tile32(c)
    st = tile32(s) * jnp.where((lane & 32) == 0, -1.0, 1.0)
    cr_ref[...] = jnp.concatenate([ct, ct], axis=1)
    sr_ref[...] = jnp.concatenate([st, st], axis=1)
    in_src = (lane >= 32) & (lane < 48)
    cmv = jnp.where(in_src, c, 0.0)
    smv = jnp.where(in_src, s, 0.0)
    in_dst = (lane >= 64) & (lane < 96)
    cm_ref[...] = jnp.where(in_dst, pltpu.roll(cmv, 32, 1) + pltpu.roll(cmv, 48, 1), 1.0)
    sm_ref[...] = jnp.where(in_dst, pltpu.roll(smv, 48, 1) - pltpu.roll(smv, 32, 1), 0.0)


def _rope_tables(positions, tm):
    t = positions.size
    pos = positions.reshape(t, 1)
    fr = ROPE_THETA ** (-jnp.arange(0, RET_DK, 2, dtype=F32) / RET_DK)
    fm = ROPE_THETA ** (-jnp.arange(0, MLA_ROPE, 2, dtype=F32) / MLA_ROPE)
    invf = jnp.concatenate([fr, fm, jnp.zeros((128 - 48,), F32)]).reshape(1, 128)
    return pl.pallas_call(
        _tables_kernel,
        grid=(t // tm,),
        in_specs=[pl.BlockSpec((tm, 1), lambda i: (i, 0)),
                  pl.BlockSpec((1, 128), lambda i: (0, 0))],
        out_specs=[pl.BlockSpec((tm, 256), lambda i: (i, 0)),
                   pl.BlockSpec((tm, 256), lambda i: (i, 0)),
                   pl.BlockSpec((tm, 128), lambda i: (i, 0)),
                   pl.BlockSpec((tm, 128), lambda i: (i, 0))],
        out_shape=[jax.ShapeDtypeStruct((t, 256), F32), jax.ShapeDtypeStruct((t, 256), F32),
                   jax.ShapeDtypeStruct((t, 128), F32), jax.ShapeDtypeStruct((t, 128), F32)],
        compiler_params=_cparams("parallel"),
        name="rope_tables",
    )(pos, invf)


def _proj_kernel(x_ref, win_ref, wuq_ref, wukv_ref, lng_ref, lnb_ref, qg_ref, kvg_ref,
                 cr_ref, sr_ref, cm_ref, sm_ref, rs_ref, q_ref, k_ref, v_ref, *, q_scale):
    h = _dot(x_ref[...].astype(BF16), win_ref[...])
    cr = cr_ref[...]
    sr = sr_ref[...]
    rs_ref[:, 0:256] = _rope(h[:, 0:256], cr, sr, 32).astype(BF16)
    rs_ref[:, 256:512] = (_rope(h[:, 256:512], cr, sr, 32) * (RET_DK ** -0.5)).astype(BF16)
    rs_ref[:, 512:768] = h[:, 512:768].astype(BF16)
    rs_ref[:, 768:1024] = _silu(h[:, 768:1024]).astype(BF16)
    rs_ref[:, 1024:1280] = h[:, 1024:1280].astype(BF16)
    rs_ref[:, 1280:1536] = _layer_norm(h[:, 1280:1536], lng_ref[...], lnb_ref[...]).astype(BF16)

    cm = cm_ref[...]
    sm = sm_ref[...]
    cm8 = jnp.concatenate([cm] * MLA_HEADS, axis=1)
    sm8 = jnp.concatenate([sm] * MLA_HEADS, axis=1)
    cq = _rms_norm(h[:, 1536:1792], qg_ref[...]).astype(BF16)
    q = _rope(_dot(cq, wuq_ref[...]), cm8, sm8, 16)
    q_ref[...] = (q * q_scale).astype(BF16)
    ckv = _rms_norm(h[:, 1792:1920], kvg_ref[...]).astype(BF16)
    kv = _dot(ckv, wukv_ref[...])
    kr = _rope(h[:, 1920:2048], cm, sm, 16)
    k_ref[...] = (kv[:, 0:1024] + jnp.concatenate([kr] * MLA_HEADS, axis=1)).astype(BF16)
    lane = lax.broadcasted_iota(jnp.int32, (kv.shape[0], MLA_HEADS * HEAD_PAD), 1)
    ones_lane = jnp.where((lane & (HEAD_PAD - 1)) == MLA_V, 1.0, 0.0)
    v_ref[...] = (kv[:, 1024:2048] + ones_lane).astype(BF16)


def _projections(x2d, l, w_in, w_uq, w_ukv, sgu_g, sgu_b, q_g, kv_g, cr, sr, cm, sm, tm):
    t = x2d.shape[0]
    row = lambda w: pl.BlockSpec((tm, w), lambda i: (i, 0))
    lay = lambda a: pl.BlockSpec((None,) + a.shape[1:], lambda i: (l,) + (0,) * (a.ndim - 1))
    q_scale = (MLA_NOPE + MLA_ROPE) ** -0.5 * math.log2(math.e)
    return pl.pallas_call(
        functools.partial(_proj_kernel, q_scale=q_scale),
        grid=(t // tm,),
        in_specs=[row(D_MODEL), lay(w_in), lay(w_uq), lay(w_ukv), lay(sgu_g), lay(sgu_b),
                  lay(q_g), lay(kv_g), row(256), row(256), row(128), row(128)],
        out_specs=[row(1536), row(1024), row(1024), row(1024)],
        out_shape=[jax.ShapeDtypeStruct((t, 1536), BF16)] + [jax.ShapeDtypeStruct((t, 1024), BF16)] * 3,
        compiler_params=_cparams("parallel"),
        name="projections",
    )(x2d, w_in, w_uq, w_ukv, sgu_g, sgu_b, q_g, kv_g, cr, sr, cm, sm)


def _retsgu_kernel(rs_ref, dmat_ref, qdec_ref, kdec_ref, cdec_ref, gavg_ref, sw_ref, sb_ref,
                   out_ref, state_ref, *, n_chunks):
    @pl.when(pl.program_id(1) == 0)
    def _():
        state_ref[...] = jnp.zeros_like(state_ref)

    lane = lax.broadcasted_iota(jnp.int32, (CHUNK, RET_W), 1)
    head_of_lane = lane // RET_DK
    hmask = [head_of_lane == h for h in range(RET_HEADS)]
    row_i = lax.broadcasted_iota(jnp.int32, (CHUNK, SGU_GROUPS * CHUNK), 0)
    col_i = lax.broadcasted_iota(jnp.int32, (CHUNK, SGU_GROUPS * CHUNK), 1)
    sw = jnp.where(row_i >= (col_i & (CHUNK - 1)), sw_ref[...], 0.0).astype(BF16)
    blk = (lax.broadcasted_iota(jnp.int32, (RET_W, RET_W), 0) // RET_DK
           == lax.broadcasted_iota(jnp.int32, (RET_W, RET_W), 1) // RET_DK)
    gavg = gavg_ref[...]
    zero = jnp.zeros((), BF16)

    def group_mean(y):
        hi, lo = _split_bf16(y)
        return _dot(jnp.concatenate([hi, lo], axis=1), gavg)

    def chunk(c, carry):
        r0 = pl.multiple_of(c * CHUNK, CHUNK)
        rq = rs_ref[pl.ds(r0, CHUNK), 0:256]
        rk = rs_ref[pl.ds(r0, CHUNK), 256:512]
        rv = rs_ref[pl.ds(r0, CHUNK), 512:768]
        gate = rs_ref[pl.ds(r0, CHUNK), 768:1024].astype(F32)
        su = rs_ref[pl.ds(r0, CHUNK), 1024:1280].astype(F32)
        sv = rs_ref[pl.ds(r0, CHUNK), 1280:1536]

        q_heads = jnp.concatenate([jnp.where(m, rq, zero) for m in hmask], axis=0)
        scores = _dot_nt(q_heads, rk) * dmat_ref[...]
        p_cat = jnp.concatenate(
            [scores[h * CHUNK:(h + 1) * CHUNK, :] for h in range(RET_HEADS)], axis=1).astype(BF16)
        v_heads = jnp.concatenate([jnp.where(m, rv, zero) for m in hmask], axis=0)
        intra = _dot(p_cat, v_heads)
        state = state_ref[...]
        cross = _dot((rq.astype(F32) * qdec_ref[...]).astype(BF16), state.astype(BF16))
        kv = _dot_tn((rk.astype(F32) * kdec_ref[...]).astype(BF16), rv)
        state_ref[...] = state * cdec_ref[...] + jnp.where(blk, kv, 0.0)

        y = intra + cross
        yc = y - group_mean(y)
        var = group_mean(yc * yc)
        ret = gate * (yc * lax.rsqrt(var + LN_EPS))

        sv_groups = jnp.concatenate([jnp.where(m, sv, zero) for m in hmask], axis=0)
        mixed = _dot(sw, sv_groups) + sb_ref[...]
        out_ref[pl.ds(r0, CHUNK), 0:256] = ret.astype(BF16)
        out_ref[pl.ds(r0, CHUNK), 256:512] = (su * mixed).astype(BF16)
        return carry

    lax.fori_loop(0, n_chunks, chunk, 0)


def _retention_sgu(rs, l, consts, sgu_w_cat, sgu_b_lane, batch, seq, tb):
    dmat, qdec, kdec, cdec, gavg = consts
    n_blocks = seq // tb
    full = lambda a: pl.BlockSpec(a.shape, lambda b, i: (0,) * a.ndim)
    lay = lambda a: pl.BlockSpec((None,) + a.shape[1:], lambda b, i: (l,) + (0,) * (a.ndim - 1))
    return pl.pallas_call(
        functools.partial(_retsgu_kernel, n_chunks=tb // CHUNK),
        grid=(batch, n_blocks),
        in_specs=[pl.BlockSpec((tb, 1536), lambda b, i: (b * n_blocks + i, 0)),
                  full(dmat), full(qdec), full(kdec), full(cdec), full(gavg),
                  lay(sgu_w_cat), lay(sgu_b_lane)],
        out_specs=pl.BlockSpec((tb, 512), lambda b, i: (b * n_blocks + i, 0)),
        out_shape=jax.ShapeDtypeStruct((batch * seq, 512), BF16),
        scratch_shapes=[pltpu.VMEM((RET_W, RET_W), F32)],
        compiler_params=_cparams("parallel", "arbitrary"),
        name="retention_sgu",
    )(rs, dmat, qdec, kdec, cdec, gavg, sgu_w_cat, sgu_b_lane)


def _retention_consts():
    h = jnp.arange(RET_HEADS, dtype=F32)
    log_gamma = jnp.log1p(-(2.0 ** (-5.0 - h)))
    pos = jnp.arange(CHUNK, dtype=F32)
    diff = pos[:, None] - pos[None, :]
    intra = jnp.where(diff >= 0, jnp.exp(log_gamma[:, None, None] * jnp.maximum(diff, 0.0)), 0.0)
    dmat = intra.reshape(RET_HEADS * CHUNK, CHUNK)
    inner = jnp.exp(log_gamma[None, :] * (CHUNK - 1 - pos)[:, None])
    query = jnp.exp(log_gamma[None, :] * (pos + 1)[:, None])
    kdec = jnp.repeat(inner, RET_DK, axis=1)
    qdec = jnp.repeat(query, RET_DK, axis=1)
    chunk_decay = jnp.repeat(jnp.exp(log_gamma * CHUNK), RET_DK)
    blk = jnp.arange(RET_W)[:, None] // RET_DK == jnp.arange(RET_W)[None, :] // RET_DK
    cdec = jnp.where(blk, chunk_decay[:, None], 0.0)
    gavg = jnp.where(blk, 1.0 / RET_DK, 0.0).astype(BF16)
    return dmat, qdec, kdec, cdec, jnp.concatenate([gavg, gavg], axis=0)


def _flash_kernel(q_ref, k_ref, v_ref, o_ref, sa_ref, sb_ref, m_ref, acc_ref, *, tq, hps):
    i = pl.program_id(2)
    tk = tq // 2
    heads = range(hps)
    cols = [slice(hh * HEAD_PAD, (hh + 1) * HEAD_PAD) for hh in heads]

    def scores(j, hh, rows=slice(None)):
        r0 = pl.multiple_of(j * tk, tk)
        return _dot_nt(q_ref[rows, cols[hh]], k_ref[pl.ds(r0, tk), cols[hh]])

    def update(j, hh, s, rows=slice(None)):
        r0 = pl.multiple_of(j * tk, tk)
        m = m_ref[hh, rows, :]
        s_max = s[:, 0:HEAD_PAD]
        for c in range(1, tk // HEAD_PAD):
            s_max = jnp.maximum(s_max, s[:, c * HEAD_PAD:(c + 1) * HEAD_PAD])
        m_new = jnp.maximum(m, jnp.max(s_max, axis=-1, keepdims=True))
        p = jnp.exp2(s - jnp.concatenate([m_new] * (tk // HEAD_PAD), axis=1)).astype(BF16)
        acc_ref[hh, rows, :] = (jnp.exp2(m - m_new) * acc_ref[hh, rows, :]
                                + _dot(p, v_ref[pl.ds(r0, tk), cols[hh]]))
        m_ref[hh, rows, :] = m_new

    m_ref[...] = jnp.full(m_ref.shape, NEG_BIG, F32)
    acc_ref[...] = jnp.zeros(acc_ref.shape, F32)
    for hh in heads:
        sa_ref[hh] = scores(0, hh)

    def pair(t, carry):
        for hh in heads:
            sb_ref[hh] = scores(2 * t + 1, hh)
            update(2 * t, hh, sa_ref[hh])
        for hh in heads:
            sa_ref[hh] = scores(2 * t + 2, hh)
            update(2 * t + 1, hh, sb_ref[hh])
        return carry

    lax.fori_loop(0, i, pair, 0)
    low = slice(tk, tq)
    visible = (lax.broadcasted_iota(jnp.int32, (tq, tk), 1) <= lax.broadcasted_iota(jnp.int32, (tq, tk), 0))
    visible_low = (lax.broadcasted_iota(jnp.int32, (tk, tk), 1) <= lax.broadcasted_iota(jnp.int32, (tk, tk), 0))
    for hh in heads:
        sb_ref[hh, low, :] = scores(2 * i + 1, hh, low)
        update(2 * i, hh, jnp.where(visible, sa_ref[hh], NEG_BIG))
    for hh in heads:
        update(2 * i + 1, hh, jnp.where(visible_low, sb_ref[hh, low, :], NEG_BIG), low)
    lane = lax.broadcasted_iota(jnp.int32, (tq, HEAD_PAD), 1)
    for pr in range(hps // 2):
        o = []
        for hh in (2 * pr, 2 * pr + 1):
            acc = acc_ref[hh]
            o.append(acc / acc[:, MLA_V:MLA_V + 1])
        o_ref[:, pr * HEAD_PAD:(pr + 1) * HEAD_PAD] = jnp.where(
            lane < MLA_V, o[0], pltpu.roll(o[1], MLA_V, 1)).astype(BF16)


def _flash_attention(q, k, v, batch, seq, tq, hps):
    nq = seq // tq
    return pl.pallas_call(
        functools.partial(_flash_kernel, tq=tq, hps=hps),
        grid=(batch, MLA_HEADS // hps, nq),
        in_specs=[pl.BlockSpec((tq, hps * HEAD_PAD), lambda b, p, i: (b * nq + i, p)),
                  pl.BlockSpec((seq, hps * HEAD_PAD), lambda b, p, i: (b, p)),
                  pl.BlockSpec((seq, hps * HEAD_PAD), lambda b, p, i: (b, p))],
        out_specs=pl.BlockSpec((tq, hps * MLA_V), lambda b, p, i: (b * nq + i, p)),
        out_shape=jax.ShapeDtypeStruct((batch * seq, MLA_W), BF16),
        scratch_shapes=[pltpu.VMEM((hps, tq, tq // 2), F32), pltpu.VMEM((hps, tq, tq // 2), F32),
                        pltpu.VMEM((hps, tq, HEAD_PAD), F32), pltpu.VMEM((hps, tq, HEAD_PAD), F32)],
        compiler_params=_cparams("parallel", "parallel", "arbitrary"),
        name="flash_attention",
    )(q, k, v)


def _memkv_kernel(mem_ref, wk_ref, wv_ref, k_ref, v_ref):
    m = mem_ref[...].astype(BF16)
    k_ref[...] = _dot(m, wk_ref[...]).astype(BF16)
    v_ref[...] = _dot(m, wv_ref[...]).astype(BF16)


def _memory_kv(mem2d, l, wk, wv, batch):
    lay = lambda a: pl.BlockSpec((None,) + a.shape[1:], lambda b: (l,) + (0,) * (a.ndim - 1))
    blk = pl.BlockSpec((MEM_LEN, XA_W), lambda b: (b, 0))
    return pl.pallas_call(
        _memkv_kernel,
        grid=(batch,),
        in_specs=[pl.BlockSpec((MEM_LEN, D_MODEL), lambda b: (b, 0)), lay(wk), lay(wv)],
        out_specs=[blk, blk],
        out_shape=[jax.ShapeDtypeStruct((batch * MEM_LEN, XA_W), BF16)] * 2,
        compiler_params=_cparams("parallel"),
        name="memory_kv",
    )(mem2d, wk, wv)


def _route_rows(scores, biased):
    s = [scores[e:e + 1, :] for e in range(N_EXPERTS)]
    b = [biased[e:e + 1, :] for e in range(N_EXPERTS)]
    group_scores = []
    for g in range(N_GROUPS):
        b0, b1, b2, b3 = b[4 * g:4 * g + 4]
        hi01, lo01 = jnp.maximum(b0, b1), jnp.minimum(b0, b1)
        hi23, lo23 = jnp.maximum(b2, b3), jnp.minimum(b2, b3)
        top1 = jnp.maximum(hi01, hi23)
        top2 = jnp.maximum(jnp.minimum(hi01, hi23), jnp.maximum(lo01, lo23))
        group_scores.append(top1 + top2)
    best = group_scores[0]
    sel = jnp.zeros_like(best, dtype=jnp.int32)
    for g in range(1, N_GROUPS):
        upd = group_scores[g] > best
        sel = jnp.where(upd, g, sel)
        best = jnp.where(upd, group_scores[g], best)

    def pick(rows, j):
        out = rows[j]
        for g in range(1, N_GROUPS):
            out = jnp.where(sel == g, rows[4 * g + j], out)
        return out

    ib = [pick(b, j) for j in range(EXPERTS_PER_GROUP)]
    isc = [pick(s, j) for j in range(EXPERTS_PER_GROUP)]

    def argmax4(vals):
        bv, bi = vals[0], jnp.zeros_like(sel)
        for j in range(1, EXPERTS_PER_GROUP):
            upd = vals[j] > bv
            bi = jnp.where(upd, j, bi)
            bv = jnp.where(upd, vals[j], bv)
        return bi

    i1 = argmax4(ib)
    i2 = argmax4([jnp.where(i1 == j, -jnp.inf, ib[j]) for j in range(EXPERTS_PER_GROUP)])

    def take(vals, idx):
        out = vals[0]
        for j in range(1, EXPERTS_PER_GROUP):
            out = jnp.where(idx == j, vals[j], out)
        return out

    g1, g2 = take(isc, i1), take(isc, i2)
    den = g1 + g2
    g1, g2 = g1 / den, g2 / den
    e1 = sel * EXPERTS_PER_GROUP + i1
    e2 = sel * EXPERTS_PER_GROUP + i2
    return e1, e2, g1, g2


def _mix_xa_kernel(x_ref, rs_ref, at_ref, wout_ref, g1_ref, b1_ref, wq_ref, km_ref, vm_ref, wo_ref,
                   g2_ref, b2_ref, rw_ref, rb_ref, tri_ref, x2_ref, pos1_ref, pos2_ref, gt1_ref, gt2_ref,
                   cnt_ref, base_ref, pk_ref, *, tm, tiles_per_block):
    @pl.when(pl.program_id(1) % tiles_per_block == 0)
    def _():
        base_ref[...] = jnp.zeros_like(base_ref)

    w_hi, w_lo = _split_bf16(rw_ref[...])

    def token_rows(rows):
        mix = _dot(jnp.concatenate([rs_ref[rows, :], at_ref[rows, :]], axis=1), wout_ref[...])
        x1 = _layer_norm(ALPHA * x_ref[rows, :] + mix, g1_ref[...], b1_ref[...])
        q = (_dot(x1.astype(BF16), wq_ref[...]) * (XA_DIM ** -0.5 * math.log2(math.e))).astype(BF16)
        heads = []
        for h in range(XA_HEADS):
            sl = slice(h * XA_DIM, (h + 1) * XA_DIM)
            s = _dot_nt(q[:, sl], km_ref[:, sl])
            p = jnp.exp2(s - jnp.max(s, axis=-1, keepdims=True))
            o = _dot(p.astype(BF16), vm_ref[:, sl])
            heads.append(o / jnp.sum(p, axis=-1, keepdims=True))
        xa = _dot(jnp.concatenate(heads, axis=1).astype(BF16), wo_ref[...])
        x2 = _layer_norm(ALPHA * x1 + xa, g2_ref[...], b2_ref[...])
        x2_ref[rows, :] = x2
        x_hi, x_lo = _split_bf16(x2)
        return _dot_nt(w_hi, x_hi) + (_dot_nt(w_hi, x_lo) + _dot_nt(w_lo, x_hi))

    logits = token_rows(slice(None))
    scores = 1.0 / (1.0 + jnp.exp(-logits))
    e1, e2, gate1, gate2 = _route_rows(scores, scores + rb_ref[...])
    expert = lax.broadcasted_iota(jnp.int32, (N_EXPERTS, tm), 0)
    hit1 = expert == e1
    hit2 = expert == e2
    cnt = jnp.where(hit1, 1.0, jnp.where(hit2, 1.0, 0.0))
    base = base_ref[...]
    before = _dot(cnt.astype(BF16), tri_ref[...]) - cnt + base[:, 0:1]
    rank1 = jnp.sum(jnp.where(hit1, before, 0.0), axis=0, keepdims=True).astype(jnp.int32)
    rank2 = jnp.sum(jnp.where(hit2, before, 0.0), axis=0, keepdims=True).astype(jnp.int32)
    base = base + jnp.sum(cnt, axis=1, keepdims=True)
    base_ref[...] = base
    cnt_ref[...] = base
    gt1_ref[...] = gate1
    gt2_ref[...] = gate2
    ti = pl.program_id(1) % tiles_per_block
    pk_ref[ti] = jnp.concatenate([e1 * RANK_RADIX + rank1, e2 * RANK_RADIX + rank2], axis=0)

    @pl.when(ti == tiles_per_block - 1)
    def _():
        starts = []
        start = jnp.zeros((1, 128), F32)
        for e in range(N_EXPERTS):
            starts.append(start.astype(jnp.int32)[:, 0:1])
            tiles = jnp.floor((base[e:e + 1, :] + (MOE_TILE - 1)) * (1.0 / MOE_TILE))
            start = start + tiles * MOE_TILE
        for tj in range(tiles_per_block):
            pk = pk_ref[tj]
            e_of = pk >> RANK_BITS
            slot = pk & (RANK_RADIX - 1)
            for e in range(N_EXPERTS):
                slot = slot + jnp.where(e_of == e, starts[e], 0)
            pos1_ref[:, tj * tm:(tj + 1) * tm] = slot[0:1, :]
            pos2_ref[:, tj * tm:(tj + 1) * tm] = slot[1:2, :]


def _mix_xa(x2d, rs, at, km, vm, l, w_out, g1, b1, wq, wo, g2, b2, rw_t, rb_col, tri, batch, seq, tm, nblk):
    t = x2d.shape[0]
    nb = seq // tm
    n_tiles = t // tm
    tpb = nblk // tm
    row = lambda w: pl.BlockSpec((tm, w), lambda b, i: (b * nb + i, 0))
    lay = lambda a: pl.BlockSpec((None,) + a.shape[1:], lambda b, i: (l,) + (0,) * (a.ndim - 1))
    full = lambda a: pl.BlockSpec(a.shape, lambda b, i: (0,) * a.ndim)
    memb = pl.BlockSpec((MEM_LEN, XA_W), lambda b, i: (b, 0))
    tok = pl.BlockSpec((None, 1, tm), lambda b, i: (b * nb + i, 0, 0))
    blk = pl.BlockSpec((None, 1, nblk), lambda b, i: ((b * nb + i) // tpb, 0, 0))
    tok_f = jax.ShapeDtypeStruct((n_tiles, 1, tm), F32)
    blk_i = jax.ShapeDtypeStruct((t // nblk, 1, nblk), jnp.int32)
    return pl.pallas_call(
        functools.partial(_mix_xa_kernel, tm=tm, tiles_per_block=tpb),
        grid=(batch, nb),
        in_specs=[row(D_MODEL), row(512), row(512), lay(w_out), lay(g1), lay(b1), lay(wq), memb, memb,
                  lay(wo), lay(g2), lay(b2), full(rw_t), full(rb_col), full(tri)],
        out_specs=[row(D_MODEL), blk, blk, tok, tok,
                   pl.BlockSpec((None, N_EXPERTS, 128), lambda b, i: ((b * nb + i) // tpb, 0, 0))],
        out_shape=[jax.ShapeDtypeStruct((t, D_MODEL), F32), blk_i, blk_i, tok_f, tok_f,
                   jax.ShapeDtypeStruct((t // nblk, N_EXPERTS, 128), F32)],
        scratch_shapes=[pltpu.VMEM((N_EXPERTS, 128), F32), pltpu.VMEM((tpb, 2, tm), jnp.int32)],
        compiler_params=_cparams("parallel", "arbitrary"),
        name="mix_xattn_router",
    )(x2d, rs, at, w_out, g1, b1, wq, km, vm, wo, g2, b2, rw_t, rb_col, tri)


def _moe_kernel(cnt_ref, pos1_ref, pos2_ref, gt1_ref, gt2_ref, x_ref, wg_ref, wu_ref, wd_ref, g_ref, b_ref,
                o_ref, xg_ref, xs_ref, ys_ref, og_ref, off_ref, ntile_ref, *, ts, k, mp):
    s = pl.program_id(1)
    sg = ts + 8
    unroll = 8

    @pl.when(s == 0)
    def _():
        xs_ref[...] = jnp.zeros_like(xs_ref)
        start = jnp.int32(0)
        for e in range(N_EXPERTS):
            tiles = (cnt_ref[0, e] + (MOE_TILE - 1)) // MOE_TILE
            off_ref[e] = start
            ntile_ref[e] = tiles
            start = start + tiles * MOE_TILE

    @pl.when(s < k)
    def _():
        x = x_ref[...]
        half = D_MODEL // 2
        hi = pltpu.bitcast(x[:, :half].astype(BF16).astype(F32), jnp.uint32)
        lo = pltpu.bitcast(x[:, half:].astype(BF16).astype(F32), jnp.uint32)
        words = hi | (lo >> 16)
        for c in range(4):
            xg_ref[c * sg:c * sg + ts, :] = words[:, c * 128:(c + 1) * 128]

        def dispatch(tt, carry):
            for u in range(unroll):
                tl = tt * unroll + u
                tg = s * ts + tl
                row = xg_ref[pl.ds(tl, 4, stride=sg), :]
                xs_ref[pl.ds(pos1_ref[tg], 4, stride=mp), :] = row
                xs_ref[pl.ds(pos2_ref[tg], 4, stride=mp), :] = row
            return carry

        lax.fori_loop(0, ts // unroll, dispatch, 0)

    @pl.when((s >= k) & (s < k + N_EXPERTS))
    def _():
        e = s - k
        seg = off_ref[e]
        n_tiles = ntile_ref[e]

        def row_tile(i):
            r0 = pl.multiple_of(seg + i * MOE_TILE, MOE_TILE)
            words = [xs_ref[pl.ds(c * mp + r0, MOE_TILE), :] for c in range(4)]
            xb = jnp.concatenate(
                [pltpu.bitcast(w & jnp.uint32(0xFFFF0000), F32).astype(BF16) for w in words]
                + [pltpu.bitcast(w << 16, F32).astype(BF16) for w in words], axis=1)
            hid = _silu(_dot(xb, wg_ref[...])) * _dot(xb, wu_ref[...])
            y = _dot(hid.astype(BF16), wd_ref[...])
            for j in range(8):
                ys_ref[pl.ds(j * mp + r0, MOE_TILE), :] = y[:, j * 128:(j + 1) * 128]

        def tile_group(i, carry):
            for u in range(TILES_PER_TRIP):
                row_tile(jnp.minimum(TILES_PER_TRIP * i + u, n_tiles - 1))
            return carry

        lax.fori_loop(0, (n_tiles + TILES_PER_TRIP - 1) // TILES_PER_TRIP, tile_group, 0)

    @pl.when(s >= k + N_EXPERTS)
    def _():
        sub = s - (k + N_EXPERTS)

        def combine(tt, carry):
            for u in range(unroll):
                tl = tt * unroll + u
                tg = sub * ts + tl
                r = (ys_ref[pl.ds(pos1_ref[tg], 8, stride=mp), :] * gt1_ref[tg]
                     + ys_ref[pl.ds(pos2_ref[tg], 8, stride=mp), :] * gt2_ref[tg])
                og_ref[pl.ds(tl, 8, stride=sg), :] = r
            return carry

        lax.fori_loop(0, ts // unroll, combine, 0)
        ffn = jnp.concatenate([og_ref[j * sg:j * sg + ts, :] for j in range(8)], axis=1)
        o_ref[...] = _layer_norm(ALPHA * x_ref[...] + ffn, g_ref[...], b_ref[...])


def _moe(x2, cnt_blk, pos1, pos2, gt1, gt2, l, wg, wu, wd, g, b, nblk, ts):
    t = x2.shape[0]
    k = nblk // ts
    steps = 2 * k + N_EXPERTS
    mp = 2 * nblk + N_EXPERTS * MOE_TILE + 8
    smem = lambda: pl.BlockSpec((nblk,), lambda i, s: (i,), memory_space=pltpu.SMEM)
    lay = lambda a: pl.BlockSpec((None,) + a.shape[1:], lambda i, s: (l,) + (0,) * (a.ndim - 1))
    exp = lambda a: pl.BlockSpec((None, None) + a.shape[2:],
                                 lambda i, s: (l, jnp.clip(s - k, 0, N_EXPERTS - 1), 0, 0))
    x_map = lambda i, s: (i * k + jnp.where(s < k, s, jnp.where(s < k + N_EXPERTS, k - 1, s - k - N_EXPERTS)), 0)
    o_map = lambda i, s: (i * k + jnp.maximum(s - k - N_EXPERTS, 0), 0)
    return pl.pallas_call(
        functools.partial(_moe_kernel, ts=ts, k=k, mp=mp),
        grid=(t // nblk, steps),
        in_specs=[pl.BlockSpec((None, 1, N_EXPERTS), lambda i, s: (i, 0, 0), memory_space=pltpu.SMEM),
                  smem(), smem(), smem(), smem(),
                  pl.BlockSpec((ts, D_MODEL), x_map), exp(wg), exp(wu), exp(wd), lay(g), lay(b)],
        out_specs=pl.BlockSpec((ts, D_MODEL), o_map),
        out_shape=jax.ShapeDtypeStruct((t, D_MODEL), F32),
        scratch_shapes=[pltpu.VMEM((4 * (ts + 8), 128), jnp.uint32), pltpu.VMEM((4 * mp, 128), jnp.uint32),
                        pltpu.VMEM((8 * mp, 128), F32), pltpu.VMEM((8 * (ts + 8), 128), F32),
                        pltpu.SMEM((N_EXPERTS,), jnp.int32), pltpu.SMEM((N_EXPERTS,), jnp.int32)],
        compiler_params=_cparams("arbitrary", "arbitrary"),
        name="moe_experts",
    )(cnt_blk, pos1, pos2, gt1, gt2, x2, wg, wu, wd, g, b)


def _tile(n, pref):
    t = min(n, pref)
    assert n % t == 0, (n, t)
    return t


def kernel(x, mem, positions, w_in, w_out, sgu_ln_g, sgu_ln_b, sgu_w, sgu_b, mla_q_norm_g, mla_w_uq, mla_kv_norm_g, mla_w_ukv, xa_wq, xa_wk, xa_wv, xa_wo, ln_mix_g, ln_mix_b, ln_xa_g, ln_xa_b, ln_moe_g, ln_moe_b, router_w, router_bias, expert_w_gate, expert_w_up, expert_w_down):
    batch, seq, _ = x.shape
    depth = w_in.shape[0]
    t = batch * seq
    assert seq % CHUNK == 0

    w_in_p = jnp.concatenate(
        [w_in[:, :, :1920], jnp.zeros((depth, D_MODEL, 64), F32), w_in[:, :, 1920:1952],
         jnp.zeros((depth, D_MODEL, 32), F32)], axis=2).astype(BF16)
    w_uq_p = jnp.pad(mla_w_uq.reshape(depth, MLA_Q_RANK, MLA_HEADS, MLA_NOPE + MLA_ROPE),
                     ((0, 0), (0, 0), (0, 0), (0, HEAD_PAD - MLA_NOPE - MLA_ROPE))
                     ).reshape(depth, MLA_Q_RANK, MLA_HEADS * HEAD_PAD).astype(BF16)
    ukv = mla_w_ukv.reshape(depth, MLA_KV_RANK, MLA_HEADS, MLA_NOPE + MLA_V)
    w_uk_p = jnp.pad(ukv[..., :MLA_NOPE], ((0, 0), (0, 0), (0, 0), (0, HEAD_PAD - MLA_NOPE))
                     ).reshape(depth, MLA_KV_RANK, MLA_HEADS * HEAD_PAD)
    w_uv_p = jnp.pad(ukv[..., MLA_NOPE:], ((0, 0), (0, 0), (0, 0), (0, HEAD_PAD - MLA_V))
                     ).reshape(depth, MLA_KV_RANK, MLA_HEADS * HEAD_PAD)
    w_ukv_p = jnp.concatenate([w_uk_p, w_uv_p], axis=2).astype(BF16)
    w_out_b = w_out.astype(BF16)
    wq_b, wk_b, wv_b, wo_b = (a.astype(BF16) for a in (xa_wq, xa_wk, xa_wv, xa_wo))
    wg_b, wu_b, wd_b = (a.astype(BF16) for a in (expert_w_gate, expert_w_up, expert_w_down))
    vec = lambda a: a.reshape(depth, 1, a.shape[-1])
    sgu_w_cat = jnp.transpose(sgu_w, (0, 2, 1, 3)).reshape(depth, CHUNK, SGU_GROUPS * CHUNK)
    sgu_b_lane = jnp.repeat(jnp.transpose(sgu_b, (0, 2, 1)), SGU_W // SGU_GROUPS, axis=2)
    rw_t = router_w.T
    rb_col = router_bias.reshape(N_EXPERTS, 1)
    consts = _retention_consts()

    x2d = x.reshape(t, D_MODEL)
    mem2d = mem.reshape(batch * MEM_LEN, D_MODEL)
    cr, sr, cm, sm = _rope_tables(positions, _tile(t, 512))

    tm_proj = _tile(t, 512)
    tb = _tile(seq, 512)
    tq = _tile(seq, 1024)
    tm_mix = _tile(seq, 1024)
    nblk = _tile(seq, 2048)
    ts_moe = _tile(nblk, 512)
    tri = (jnp.arange(tm_mix)[:, None] <= jnp.arange(tm_mix)[None, :]).astype(BF16)
    for l in range(depth):
        rs, q, k, v = _projections(x2d, l, w_in_p, w_uq_p, w_ukv_p, vec(sgu_ln_g), vec(sgu_ln_b),
                                   vec(mla_q_norm_g), vec(mla_kv_norm_g), cr, sr, cm, sm, tm_proj)
        retsgu = _retention_sgu(rs, l, consts, sgu_w_cat, sgu_b_lane, batch, seq, tb)
        attn = _flash_attention(q, k, v, batch, seq, tq, 2)
        km, vm = _memory_kv(mem2d, l, wk_b, wv_b, batch)
        x2, pos1, pos2, gt1, gt2, cnts = _mix_xa(
            x2d, retsgu, attn, km, vm, l, w_out_b, vec(ln_mix_g), vec(ln_mix_b), wq_b, wo_b,
            vec(ln_xa_g), vec(ln_xa_b), rw_t, rb_col, tri, batch, seq, tm_mix, nblk)
        cnt_blk = cnts[:, :, 0].astype(jnp.int32).reshape(-1, 1, N_EXPERTS)
        x2d = _moe(x2, cnt_blk, pos1.reshape(t), pos2.reshape(t), gt1.reshape(t), gt2.reshape(t), l,
                   wg_b, wu_b, wd_b, vec(ln_moe_g), vec(ln_moe_b), nblk, ts_moe)
    return x2d.reshape(batch, seq, D_MODEL)
```

```python
import functools
import math

import jax
import jax.numpy as jnp
from jax import lax
from jax.experimental import pallas as pl
from jax.experimental.pallas import tpu as pltpu

F32 = jnp.float32
BF16 = jnp.bfloat16

D_MODEL = 1024
DEPTH = 4
MEM_LEN = 256
ROPE_THETA = 10000.0

RET_HEADS = 4
RET_DK = 64
RET_W = 256
CHUNK = 128

SGU_GROUPS = 4
SGU_W = 256

MLA_HEADS = 8
MLA_Q_RANK = 256
MLA_KV_RANK = 128
MLA_NOPE = 64
MLA_ROPE = 32
MLA_V = 64
MLA_W = MLA_HEADS * MLA_V
HEAD_PAD = 128

XA_HEADS = 4
XA_DIM = 128
XA_W = XA_HEADS * XA_DIM

N_EXPERTS = 16
N_GROUPS = 4
EXPERTS_PER_GROUP = 4
D_EXPERT = 256

ALPHA = (2 * DEPTH) ** 0.25
LN_EPS = 1e-5
IN_PAD = 2048
NEG_BIG = -1e30
MOE_TILE = 128
TILES_PER_TRIP = 3
RANK_BITS = 16
RANK_RADIX = 1 << RANK_BITS

VMEM_LIMIT = 56 * 1024 * 1024


def _cparams(*sem):
    return pltpu.CompilerParams(dimension_semantics=sem, vmem_limit_bytes=VMEM_LIMIT)


def _dot(a, b):
    return jnp.dot(a, b, preferred_element_type=F32)


def _dot_nt(a, b):
    return lax.dot_general(a, b, (((1,), (1,)), ((), ())), preferred_element_type=F32)


def _dot_tn(a, b):
    return lax.dot_general(a, b, (((0,), (0,)), ((), ())), preferred_element_type=F32)


def _layer_norm(z, g, b):
    mu = jnp.mean(z, axis=-1, keepdims=True)
    zc = z - mu
    var = jnp.mean(zc * zc, axis=-1, keepdims=True)
    return zc * lax.rsqrt(var + LN_EPS) * g + b


def _rms_norm(z, g):
    ms = jnp.mean(z * z, axis=-1, keepdims=True)
    return z * lax.rsqrt(ms + LN_EPS) * g


def _silu(z):
    return z / (1.0 + jnp.exp(-z))


def _rope(x, cos, sin_signed, half):
    w = x.shape[-1]
    lane = lax.broadcasted_iota(jnp.int32, x.shape, 1)
    rot = jnp.where((lane & half) == 0, pltpu.roll(x, w - half, 1), pltpu.roll(x, half, 1))
    return x * cos + rot * sin_signed


def _split_bf16(x):
    hi = x.astype(BF16)
    lo = (x - hi.astype(F32)).astype(BF16)
    return hi, lo


def _tables_kernel(pos_ref, invf_ref, cr_ref, sr_ref, cm_ref, sm_ref):
    ang = pos_ref[...].astype(F32) * invf_ref[...]
    c = jnp.cos(ang)
    s = jnp.sin(ang)
    lane = lax.broadcasted_iota(jnp.int32, c.shape, 1)

    def tile32(v):
        v0 = jnp.where(lane < 32, v, 0.0)
        v1 = v0 + pltpu.roll(v0, 32, 1)
        return v1 + pltpu.roll(v1, 64, 1)

    ct = tile32(c)
    st = tile32(s) * jnp.where((lane & 32) == 0, -1.0, 1.0)
    cr_ref[...] = jnp.concatenate([ct, ct], axis=1)
    sr_ref[...] = jnp.concatenate([st, st], axis=1)
    in_src = (lane >= 32) & (lane < 48)
    cmv = jnp.where(in_src, c, 0.0)
    smv = jnp.where(in_src, s, 0.0)
    in_dst = (lane >= 64) & (lane < 96)
    cm_ref[...] = jnp.where(in_dst, pltpu.roll(cmv, 32, 1) + pltpu.roll(cmv, 48, 1), 1.0)
    sm_ref[...] = jnp.where(in_dst, pltpu.roll(smv, 48, 1) - pltpu.roll(smv, 32, 1), 0.0)


def _rope_tables(positions, tm):
    t = positions.size
    pos = positions.reshape(t, 1)
    fr = ROPE_THETA ** (-jnp.arange(0, RET_DK, 2, dtype=F32) / RET_DK)
    fm = ROPE_THETA ** (-jnp.arange(0, MLA_ROPE, 2, dtype=F32) / MLA_ROPE)
    invf = jnp.concatenate([fr, fm, jnp.zeros((128 - 48,), F32)]).reshape(1, 128)
    return pl.pallas_call(
        _tables_kernel,
        grid=(t // tm,),
        in_specs=[pl.BlockSpec((tm, 1), lambda i: (i, 0)),
                  pl.BlockSpec((1, 128), lambda i: (0, 0))],
        out_specs=[pl.BlockSpec((tm, 256), lambda i: (i, 0)),
                   pl.BlockSpec((tm, 256), lambda i: (i, 0)),
                   pl.BlockSpec((tm, 128), lambda i: (i, 0)),
                   pl.BlockSpec((tm, 128), lambda i: (i, 0))],
        out_shape=[jax.ShapeDtypeStruct((t, 256), F32), jax.ShapeDtypeStruct((t, 256), F32),
                   jax.ShapeDtypeStruct((t, 128), F32), jax.ShapeDtypeStruct((t, 128), F32)],
        compiler_params=_cparams("parallel"),
        name="rope_tables",
    )(pos, invf)


def _proj_kernel(x_ref, win_ref, wuq_ref, wukv_ref, lng_ref, lnb_ref, qg_ref, kvg_ref,
                 cr_ref, sr_ref, cm_ref, sm_ref, rs_ref, q_ref, k_ref, v_ref, *, q_scale):
    h = _dot(x_ref[...].astype(BF16), win_ref[...])
    cr = cr_ref[...]
    sr = sr_ref[...]
    rs_ref[:, 0:256] = _rope(h[:, 0:256], cr, sr, 32).astype(BF16)
    rs_ref[:, 256:512] = (_rope(h[:, 256:512], cr, sr, 32) * (RET_DK ** -0.5)).astype(BF16)
    rs_ref[:, 512:768] = h[:, 512:768].astype(BF16)
    rs_ref[:, 768:1024] = _silu(h[:, 768:1024]).astype(BF16)
    rs_ref[:, 1024:1280] = h[:, 1024:1280].astype(BF16)
    rs_ref[:, 1280:1536] = _layer_norm(h[:, 1280:1536], lng_ref[...], lnb_ref[...]).astype(BF16)

    cm = cm_ref[...]
    sm = sm_ref[...]
    cm8 = jnp.concatenate([cm] * MLA_HEADS, axis=1)
    sm8 = jnp.concatenate([sm] * MLA_HEADS, axis=1)
    cq = _rms_norm(h[:, 1536:1792], qg_ref[...]).astype(BF16)
    q = _rope(_dot(cq, wuq_ref[...]), cm8, sm8, 16)
    q_ref[...] = (q * q_scale).astype(BF16)
    ckv = _rms_norm(h[:, 1792:1920], kvg_ref[...]).astype(BF16)
    kv = _dot(ckv, wukv_ref[...])
    kr = _rope(h[:, 1920:2048], cm, sm, 16)
    k_ref[...] = (kv[:, 0:1024] + jnp.concatenate([kr] * MLA_HEADS, axis=1)).astype(BF16)
    lane = lax.broadcasted_iota(jnp.int32, (kv.shape[0], MLA_HEADS * HEAD_PAD), 1)
    ones_lane = jnp.where((lane & (HEAD_PAD - 1)) == MLA_V, 1.0, 0.0)
    v_ref[...] = (kv[:, 1024:2048] + ones_lane).astype(BF16)


def _projections(x2d, l, w_in, w_uq, w_ukv, sgu_g, sgu_b, q_g, kv_g, cr, sr, cm, sm, tm):
    t = x2d.shape[0]
    row = lambda w: pl.BlockSpec((tm, w), lambda i: (i, 0))
    lay = lambda a: pl.BlockSpec((None,) + a.shape[1:], lambda i: (l,) + (0,) * (a.ndim - 1))
    q_scale = (MLA_NOPE + MLA_ROPE) ** -0.5 * math.log2(math.e)
    return pl.pallas_call(
        functools.partial(_proj_kernel, q_scale=q_scale),
        grid=(t // tm,),
        in_specs=[row(D_MODEL), lay(w_in), lay(w_uq), lay(w_ukv), lay(sgu_g), lay(sgu_b),
                  lay(q_g), lay(kv_g), row(256), row(256), row(128), row(128)],
        out_specs=[row(1536), row(1024), row(1024), row(1024)],
        out_shape=[jax.ShapeDtypeStruct((t, 1536), BF16)] + [jax.ShapeDtypeStruct((t, 1024), BF16)] * 3,
        compiler_params=_cparams("parallel"),
        name="projections",
    )(x2d, w_in, w_uq, w_ukv, sgu_g, sgu_b, q_g, kv_g, cr, sr, cm, sm)


def _retsgu_kernel(rs_ref, dmat_ref, qdec_ref, kdec_ref, cdec_ref, gavg_ref, sw_ref, sb_ref,
                   out_ref, state_ref, *, n_chunks):
    @pl.when(pl.program_id(1) == 0)
    def _():
        state_ref[...] = jnp.zeros_like(state_ref)

    lane = lax.broadcasted_iota(jnp.int32, (CHUNK, RET_W), 1)
    head_of_lane = lane // RET_DK
    hmask = [head_of_lane == h for h in range(RET_HEADS)]
    row_i = lax.broadcasted_iota(jnp.int32, (CHUNK, SGU_GROUPS * CHUNK), 0)
    col_i = lax.broadcasted_iota(jnp.int32, (CHUNK, SGU_GROUPS * CHUNK), 1)
    sw = jnp.where(row_i >= (col_i & (CHUNK - 1)), sw_ref[...], 0.0).astype(BF16)
    blk = (lax.broadcasted_iota(jnp.int32, (RET_W, RET_W), 0) // RET_DK
           == lax.broadcasted_iota(jnp.int32, (RET_W, RET_W), 1) // RET_DK)
    gavg = gavg_ref[...]
    zero = jnp.zeros((), BF16)

    def group_mean(y):
        hi, lo = _split_bf16(y)
        return _dot(jnp.concatenate([hi, lo], axis=1), gavg)

    def chunk(c, state):
        r0 = c * CHUNK
        rq = rs_ref[pl.ds(r0, CHUNK), 0:256]
        rk = rs_ref[pl.ds(r0, CHUNK), 256:512]
        rv = rs_ref[pl.ds(r0, CHUNK), 512:768]
        gate = rs_ref[pl.ds(r0, CHUNK), 768:1024].astype(F32)
        su = rs_ref[pl.ds(r0, CHUNK), 1024:1280].astype(F32)
        sv = rs_ref[pl.ds(r0, CHUNK), 1280:1536]

        q_heads = jnp.concatenate([jnp.where(m, rq, zero) for m in hmask], axis=0)
        scores = _dot_nt(q_heads, rk) * dmat_ref[...]
        p_cat = jnp.concatenate(
            [scores[h * CHUNK:(h + 1) * CHUNK, :] for h in range(RET_HEADS)], axis=1).astype(BF16)
        v_heads = jnp.concatenate([jnp.where(m, rv, zero) for m in hmask], axis=0)
        intra = _dot(p_cat, v_heads)
        cross = _dot((rq.astype(F32) * qdec_ref[...]).astype(BF16), state.astype(BF16))
        kv = _dot_tn((rk.astype(F32) * kdec_ref[...]).astype(BF16), rv)
        state = state * cdec_ref[...] + jnp.where(blk, kv, 0.0)

        y = intra + cross
        yc = y - group_mean(y)
        var = group_mean(yc * yc)
        ret = gate * (yc * lax.rsqrt(var + LN_EPS))

        sv_groups = jnp.concatenate([jnp.where(m, sv, zero) for m in hmask], axis=0)
        mixed = _dot(sw, sv_groups) + sb_ref[...]
        out_ref[pl.ds(r0, CHUNK), 0:256] = ret.astype(BF16)
        out_ref[pl.ds(r0, CHUNK), 256:512] = (su * mixed).astype(BF16)
        return state

    state = state_ref[...]
    for c in range(n_chunks):
        state = chunk(c, state)
    state_ref[...] = state


def _retention_sgu(rs, l, consts, sgu_w_cat, sgu_b_lane, batch, seq, tb):
    dmat, qdec, kdec, cdec, gavg = consts
    n_blocks = seq // tb
    full = lambda a: pl.BlockSpec(a.shape, lambda b, i: (0,) * a.ndim)
    lay = lambda a: pl.BlockSpec((None,) + a.shape[1:], lambda b, i: (l,) + (0,) * (a.ndim - 1))
    return pl.pallas_call(
        functools.partial(_retsgu_kernel, n_chunks=tb // CHUNK),
        grid=(batch, n_blocks),
        in_specs=[pl.BlockSpec((tb, 1536), lambda b, i: (b * n_blocks + i, 0)),
                  full(dmat), full(qdec), full(kdec), full(cdec), full(gavg),
                  lay(sgu_w_cat), lay(sgu_b_lane)],
        out_specs=pl.BlockSpec((tb, 512), lambda b, i: (b * n_blocks + i, 0)),
        out_shape=jax.ShapeDtypeStruct((batch * seq, 512), BF16),
        scratch_shapes=[pltpu.VMEM((RET_W, RET_W), F32)],
        compiler_params=_cparams("parallel", "arbitrary"),
        name="retention_sgu",
    )(rs, dmat, qdec, kdec, cdec, gavg, sgu_w_cat, sgu_b_lane)


def _retention_consts():
    h = jnp.arange(RET_HEADS, dtype=F32)
    log_gamma = jnp.log1p(-(2.0 ** (-5.0 - h)))
    pos = jnp.arange(CHUNK, dtype=F32)
    diff = pos[:, None] - pos[None, :]
    intra = jnp.where(diff >= 0, jnp.exp(log_gamma[:, None, None] * jnp.maximum(diff, 0.0)), 0.0)
    dmat = intra.reshape(RET_HEADS * CHUNK, CHUNK)
    inner = jnp.exp(log_gamma[None, :] * (CHUNK - 1 - pos)[:, None])
    query = jnp.exp(log_gamma[None, :] * (pos + 1)[:, None])
    kdec = jnp.repeat(inner, RET_DK, axis=1)
    qdec = jnp.repeat(query, RET_DK, axis=1)
    chunk_decay = jnp.repeat(jnp.exp(log_gamma * CHUNK), RET_DK)
    blk = jnp.arange(RET_W)[:, None] // RET_DK == jnp.arange(RET_W)[None, :] // RET_DK
    cdec = jnp.where(blk, chunk_decay[:, None], 0.0)
    gavg = jnp.where(blk, 1.0 / RET_DK, 0.0).astype(BF16)
    return dmat, qdec, kdec, cdec, jnp.concatenate([gavg, gavg], axis=0)


def _flash_kernel(q_ref, k_ref, v_ref, o_ref, sa_ref, sb_ref, m_ref, acc_ref, *, tq, hps):
    i = pl.program_id(2)
    tk = tq // 2
    heads = range(hps)
    cols = [slice(hh * HEAD_PAD, (hh + 1) * HEAD_PAD) for hh in heads]

    def scores(j, hh, rows=slice(None)):
        r0 = pl.multiple_of(j * tk, tk)
        return _dot_nt(q_ref[rows, cols[hh]], k_ref[pl.ds(r0, tk), cols[hh]])

    def update(j, hh, s, rows=slice(None)):
        r0 = pl.multiple_of(j * tk, tk)
        m = m_ref[hh, rows, :]
        s_max = s[:, 0:HEAD_PAD]
        for c in range(1, tk // HEAD_PAD):
            s_max = jnp.maximum(s_max, s[:, c * HEAD_PAD:(c + 1) * HEAD_PAD])
        m_new = jnp.maximum(m, jnp.max(s_max, axis=-1, keepdims=True))
        p = jnp.exp2(s - jnp.concatenate([m_new] * (tk // HEAD_PAD), axis=1)).astype(BF16)
        acc_ref[hh, rows, :] = (jnp.exp2(m - m_new) * acc_ref[hh, rows, :]
                                + _dot(p, v_ref[pl.ds(r0, tk), cols[hh]]))
        m_ref[hh, rows, :] = m_new

    m_ref[...] = jnp.full(m_ref.shape, NEG_BIG, F32)
    acc_ref[...] = jnp.zeros(acc_ref.shape, F32)
    for hh in heads:
        sa_ref[hh] = scores(0, hh)

    def pair(t, carry):
        for hh in heads:
            sb_ref[hh] = scores(2 * t + 1, hh)
            update(2 * t, hh, sa_ref[hh])
        for hh in heads:
            sa_ref[hh] = scores(2 * t + 2, hh)
            update(2 * t + 1, hh, sb_ref[hh])
        return carry

    lax.fori_loop(0, i, pair, 0)
    low = slice(tk, tq)
    visible = (lax.broadcasted_iota(jnp.int32, (tq, tk), 1) <= lax.broadcasted_iota(jnp.int32, (tq, tk), 0))
    visible_low = (lax.broadcasted_iota(jnp.int32, (tk, tk), 1) <= lax.broadcasted_iota(jnp.int32, (tk, tk), 0))
    for hh in heads:
        sb_ref[hh, low, :] = scores(2 * i + 1, hh, low)
        update(2 * i, hh, jnp.where(visible, sa_ref[hh], NEG_BIG))
    for hh in heads:
        update(2 * i + 1, hh, jnp.where(visible_low, sb_ref[hh, low, :], NEG_BIG), low)
    lane = lax.broadcasted_iota(jnp.int32, (tq, HEAD_PAD), 1)
    for pr in range(hps // 2):
        o = []
        for hh in (2 * pr, 2 * pr + 1):
            acc = acc_ref[hh]
            o.append(acc / acc[:, MLA_V:MLA_V + 1])
        o_ref[:, pr * HEAD_PAD:(pr + 1) * HEAD_PAD] = jnp.where(
            lane < MLA_V, o[0], pltpu.roll(o[1], MLA_V, 1)).astype(BF16)


def _flash_attention(q, k, v, batch, seq, tq, hps):
    nq = seq // tq
    return pl.pallas_call(
        functools.partial(_flash_kernel, tq=tq, hps=hps),
        grid=(batch, MLA_HEADS // hps, nq),
        in_specs=[pl.BlockSpec((tq, hps * HEAD_PAD), lambda b, p, i: (b * nq + i, p)),
                  pl.BlockSpec((seq, hps * HEAD_PAD), lambda b, p, i: (b, p)),
                  pl.BlockSpec((seq, hps * HEAD_PAD), lambda b, p, i: (b, p))],
        out_specs=pl.BlockSpec((tq, hps * MLA_V), lambda b, p, i: (b * nq + i, p)),
        out_shape=jax.ShapeDtypeStruct((batch * seq, MLA_W), BF16),
        scratch_shapes=[pltpu.VMEM((hps, tq, tq // 2), F32), pltpu.VMEM((hps, tq, tq // 2), F32),
                        pltpu.VMEM((hps, tq, HEAD_PAD), F32), pltpu.VMEM((hps, tq, HEAD_PAD), F32)],
        compiler_params=_cparams("parallel", "parallel", "arbitrary"),
        name="flash_attention",
    )(q, k, v)


def _memkv_kernel(mem_ref, wk_ref, wv_ref, k_ref, v_ref):
    m = mem_ref[...].astype(BF16)
    k_ref[...] = _dot(m, wk_ref[...]).astype(BF16)
    v_ref[...] = _dot(m, wv_ref[...]).astype(BF16)


def _memory_kv(mem2d, l, wk, wv, batch):
    lay = lambda a: pl.BlockSpec((None,) + a.shape[1:], lambda b: (l,) + (0,) * (a.ndim - 1))
    blk = pl.BlockSpec((MEM_LEN, XA_W), lambda b: (b, 0))
    return pl.pallas_call(
        _memkv_kernel,
        grid=(batch,),
        in_specs=[pl.BlockSpec((MEM_LEN, D_MODEL), lambda b: (b, 0)), lay(wk), lay(wv)],
        out_specs=[blk, blk],
        out_shape=[jax.ShapeDtypeStruct((batch * MEM_LEN, XA_W), BF16)] * 2,
        compiler_params=_cparams("parallel"),
        name="memory_kv",
    )(mem2d, wk, wv)


def _route_rows(scores, biased):
    s = [scores[e:e + 1, :] for e in range(N_EXPERTS)]
    b = [biased[e:e + 1, :] for e in range(N_EXPERTS)]
    group_scores = []
    for g in range(N_GROUPS):
        b0, b1, b2, b3 = b[4 * g:4 * g + 4]
        hi01, lo01 = jnp.maximum(b0, b1), jnp.minimum(b0, b1)
        hi23, lo23 = jnp.maximum(b2, b3), jnp.minimum(b2, b3)
        top1 = jnp.maximum(hi01, hi23)
        top2 = jnp.maximum(jnp.minimum(hi01, hi23), jnp.maximum(lo01, lo23))
        group_scores.append(top1 + top2)
    best = group_scores[0]
    sel = jnp.zeros_like(best, dtype=jnp.int32)
    for g in range(1, N_GROUPS):
        upd = group_scores[g] > best
        sel = jnp.where(upd, g, sel)
        best = jnp.where(upd, group_scores[g], best)

    def pick(rows, j):
        out = rows[j]
        for g in range(1, N_GROUPS):
            out = jnp.where(sel == g, rows[4 * g + j], out)
        return out

    ib = [pick(b, j) for j in range(EXPERTS_PER_GROUP)]
    isc = [pick(s, j) for j in range(EXPERTS_PER_GROUP)]

    def argmax4(vals):
        bv, bi = vals[0], jnp.zeros_like(sel)
        for j in range(1, EXPERTS_PER_GROUP):
            upd = vals[j] > bv
            bi = jnp.where(upd, j, bi)
            bv = jnp.where(upd, vals[j], bv)
        return bi

    i1 = argmax4(ib)
    i2 = argmax4([jnp.where(i1 == j, -jnp.inf, ib[j]) for j in range(EXPERTS_PER_GROUP)])

    def take(vals, idx):
        out = vals[0]
        for j in range(1, EXPERTS_PER_GROUP):
            out = jnp.where(idx == j, vals[j], out)
        return out

    g1, g2 = take(isc, i1), take(isc, i2)
    den = g1 + g2
    g1, g2 = g1 / den, g2 / den
    e1 = sel * EXPERTS_PER_GROUP + i1
    e2 = sel * EXPERTS_PER_GROUP + i2
    return e1, e2, g1, g2


def _mix_xa_kernel(x_ref, rs_ref, at_ref, wout_ref, g1_ref, b1_ref, wq_ref, km_ref, vm_ref, wo_ref,
                   g2_ref, b2_ref, rw_ref, rb_ref, tri_ref, x2_ref, pos1_ref, pos2_ref, gt1_ref, gt2_ref,
                   cnt_ref, base_ref, pk_ref, *, tm, tiles_per_block):
    @pl.when(pl.program_id(1) % tiles_per_block == 0)
    def _():
        base_ref[...] = jnp.zeros_like(base_ref)

    w_hi, w_lo = _split_bf16(rw_ref[...])

    def token_rows(rows):
        mix = _dot(jnp.concatenate([rs_ref[rows, :], at_ref[rows, :]], axis=1), wout_ref[...])
        x1 = _layer_norm(ALPHA * x_ref[rows, :] + mix, g1_ref[...], b1_ref[...])
        q = (_dot(x1.astype(BF16), wq_ref[...]) * (XA_DIM ** -0.5 * math.log2(math.e))).astype(BF16)
        heads = []
        for h in range(XA_HEADS):
            sl = slice(h * XA_DIM, (h + 1) * XA_DIM)
            s = _dot_nt(q[:, sl], km_ref[:, sl])
            p = jnp.exp2(s - jnp.max(s, axis=-1, keepdims=True))
            o = _dot(p.astype(BF16), vm_ref[:, sl])
            heads.append(o / jnp.sum(p, axis=-1, keepdims=True))
        xa = _dot(jnp.concatenate(heads, axis=1).astype(BF16), wo_ref[...])
        x2 = _layer_norm(ALPHA * x1 + xa, g2_ref[...], b2_ref[...])
        x2_ref[rows, :] = x2
        x_hi, x_lo = _split_bf16(x2)
        return _dot_nt(w_hi, x_hi) + (_dot_nt(w_hi, x_lo) + _dot_nt(w_lo, x_hi))

    logits = token_rows(slice(None))
    scores = 1.0 / (1.0 + jnp.exp(-logits))
    e1, e2, gate1, gate2 = _route_rows(scores, scores + rb_ref[...])
    expert = lax.broadcasted_iota(jnp.int32, (N_EXPERTS, tm), 0)
    hit1 = expert == e1
    hit2 = expert == e2
    cnt = jnp.where(hit1, 1.0, jnp.where(hit2, 1.0, 0.0))
    base = base_ref[...]
    before = _dot(cnt.astype(BF16), tri_ref[...]) - cnt + base[:, 0:1]
    rank1 = jnp.sum(jnp.where(hit1, before, 0.0), axis=0, keepdims=True).astype(jnp.int32)
    rank2 = jnp.sum(jnp.where(hit2, before, 0.0), axis=0, keepdims=True).astype(jnp.int32)
    base = base + jnp.sum(cnt, axis=1, keepdims=True)
    base_ref[...] = base
    cnt_ref[...] = base
    gt1_ref[...] = gate1
    gt2_ref[...] = gate2
    ti = pl.program_id(1) % tiles_per_block
    pk_ref[ti] = jnp.concatenate([e1 * RANK_RADIX + rank1, e2 * RANK_RADIX + rank2], axis=0)

    @pl.when(ti == tiles_per_block - 1)
    def _():
        starts = []
        start = jnp.zeros((1, 128), F32)
        for e in range(N_EXPERTS):
            starts.append(start.astype(jnp.int32)[:, 0:1])
            tiles = jnp.floor((base[e:e + 1, :] + (MOE_TILE - 1)) * (1.0 / MOE_TILE))
            start = start + tiles * MOE_TILE
        for tj in range(tiles_per_block):
            pk = pk_ref[tj]
            e_of = pk >> RANK_BITS
            slot = pk & (RANK_RADIX - 1)
            for e in range(N_EXPERTS):
                slot = slot + jnp.where(e_of == e, starts[e], 0)
            pos1_ref[:, tj * tm:(tj + 1) * tm] = slot[0:1, :]
            pos2_ref[:, tj * tm:(tj + 1) * tm] = slot[1:2, :]


def _mix_xa(x2d, rs, at, km, vm, l, w_out, g1, b1, wq, wo, g2, b2, rw_t, rb_col, tri, batch, seq, tm, nblk):
    t = x2d.shape[0]
    nb = seq // tm
    n_tiles = t // tm
    tpb = nblk // tm
    row = lambda w: pl.BlockSpec((tm, w), lambda b, i: (b * nb + i, 0))
    lay = lambda a: pl.BlockSpec((None,) + a.shape[1:], lambda b, i: (l,) + (0,) * (a.ndim - 1))
    full = lambda a: pl.BlockSpec(a.shape, lambda b, i: (0,) * a.ndim)
    memb = pl.BlockSpec((MEM_LEN, XA_W), lambda b, i: (b, 0))
    tok = pl.BlockSpec((None, 1, tm), lambda b, i: (b * nb + i, 0, 0))
    blk = pl.BlockSpec((None, 1, nblk), lambda b, i: ((b * nb + i) // tpb, 0, 0))
    tok_f = jax.ShapeDtypeStruct((n_tiles, 1, tm), F32)
    blk_i = jax.ShapeDtypeStruct((t // nblk, 1, nblk), jnp.int32)
    return pl.pallas_call(
        functools.partial(_mix_xa_kernel, tm=tm, tiles_per_block=tpb),
        grid=(batch, nb),
        in_specs=[row(D_MODEL), row(512), row(512), lay(w_out), lay(g1), lay(b1), lay(wq), memb, memb,
                  lay(wo), lay(g2), lay(b2), full(rw_t), full(rb_col), full(tri)],
        out_specs=[row(D_MODEL), blk, blk, tok, tok,
                   pl.BlockSpec((None, N_EXPERTS, 128), lambda b, i: ((b * nb + i) // tpb, 0, 0))],
        out_shape=[jax.ShapeDtypeStruct((t, D_MODEL), F32), blk_i, blk_i, tok_f, tok_f,
                   jax.ShapeDtypeStruct((t // nblk, N_EXPERTS, 128), F32)],
        scratch_shapes=[pltpu.VMEM((N_EXPERTS, 128), F32), pltpu.VMEM((tpb, 2, tm), jnp.int32)],
        compiler_params=_cparams("parallel", "arbitrary"),
        name="mix_xattn_router",
    )(x2d, rs, at, w_out, g1, b1, wq, km, vm, wo, g2, b2, rw_t, rb_col, tri)


def _moe_kernel(cnt_ref, pos1_ref, pos2_ref, gt1_ref, gt2_ref, x_ref, wg_ref, wu_ref, wd_ref, g_ref, b_ref,
                o_ref, xg_ref, xs_ref, ys_ref, og_ref, off_ref, ntile_ref, *, ts, k, mp):
    s = pl.program_id(1)
    sg = ts + 8
    unroll = 8

    @pl.when(s == 0)
    def _():
        xs_ref[...] = jnp.zeros_like(xs_ref)
        start = jnp.int32(0)
        for e in range(N_EXPERTS):
            tiles = (cnt_ref[0, e] + (MOE_TILE - 1)) // MOE_TILE
            off_ref[e] = start
            ntile_ref[e] = tiles
            start = start + tiles * MOE_TILE

    @pl.when(s < k)
    def _():
        x = x_ref[...]
        half = D_MODEL // 2
        hi = pltpu.bitcast(x[:, :half].astype(BF16).astype(F32), jnp.uint32)
        lo = pltpu.bitcast(x[:, half:].astype(BF16).astype(F32), jnp.uint32)
        words = hi | (lo >> 16)
        for c in range(4):
            xg_ref[c * sg:c * sg + ts, :] = words[:, c * 128:(c + 1) * 128]

        def dispatch(tt, carry):
            for u in range(unroll):
                tl = tt * unroll + u
                tg = s * ts + tl
                row = xg_ref[pl.ds(tl, 4, stride=sg), :]
                xs_ref[pl.ds(pos1_ref[tg], 4, stride=mp), :] = row
                xs_ref[pl.ds(pos2_ref[tg], 4, stride=mp), :] = row
            return carry

        lax.fori_loop(0, ts // unroll, dispatch, 0)

    @pl.when((s >= k) & (s < k + N_EXPERTS))
    def _():
        e = s - k
        seg = off_ref[e]
        n_tiles = ntile_ref[e]

        def row_tile(i):
            r0 = pl.multiple_of(seg + i * MOE_TILE, MOE_TILE)
            words = [xs_ref[pl.ds(c * mp + r0, MOE_TILE), :] for c in range(4)]
            xb = jnp.concatenate(
                [pltpu.bitcast(w & jnp.uint32(0xFFFF0000), F32).astype(BF16) for w in words]
                + [pltpu.bitcast(w << 16, F32).astype(BF16) for w in words], axis=1)
            hid = _silu(_dot(xb, wg_ref[...])) * _dot(xb, wu_ref[...])
            y = _dot(hid.astype(BF16), wd_ref[...])
            for j in range(8):
                ys_ref[pl.ds(j * mp + r0, MOE_TILE), :] = y[:, j * 128:(j + 1) * 128]

        def tile_group(i, carry):
            for u in range(TILES_PER_TRIP):
                row_tile(jnp.minimum(TILES_PER_TRIP * i + u, n_tiles - 1))
            return carry

        lax.fori_loop(0, (n_tiles + TILES_PER_TRIP - 1) // TILES_PER_TRIP, tile_group, 0)

    @pl.when(s >= k + N_EXPERTS)
    def _():
        sub = s - (k + N_EXPERTS)

        def combine(tt, carry):
            for u in range(unroll):
                tl = tt * unroll + u
                tg = sub * ts + tl
                r = (ys_ref[pl.ds(pos1_ref[tg], 8, stride=mp), :] * gt1_ref[tg]
                     + ys_ref[pl.ds(pos2_ref[tg], 8, stride=mp), :] * gt2_ref[tg])
                og_ref[pl.ds(tl, 8, stride=sg), :] = r
            return carry

        lax.fori_loop(0, ts // unroll, combine, 0)
        ffn = jnp.concatenate([og_ref[j * sg:j * sg + ts, :] for j in range(8)], axis=1)
        o_ref[...] = _layer_norm(ALPHA * x_ref[...] + ffn, g_ref[...], b_ref[...])


def _moe(x2, cnt_blk, pos1, pos2, gt1, gt2, l, wg, wu, wd, g, b, nblk, ts):
    t = x2.shape[0]
    k = nblk // ts
    steps = 2 * k + N_EXPERTS
    mp = 2 * nblk + N_EXPERTS * MOE_TILE + 8
    smem = lambda: pl.BlockSpec((nblk,), lambda i, s: (i,), memory_space=pltpu.SMEM)
    lay = lambda a: pl.BlockSpec((None,) + a.shape[1:], lambda i, s: (l,) + (0,) * (a.ndim - 1))
    exp = lambda a: pl.BlockSpec((None, None) + a.shape[2:],
                                 lambda i, s: (l, jnp.clip(s - k, 0, N_EXPERTS - 1), 0, 0))
    x_map = lambda i, s: (i * k + jnp.where(s < k, s, jnp.where(s < k + N_EXPERTS, k - 1, s - k - N_EXPERTS)), 0)
    o_map = lambda i, s: (i * k + jnp.maximum(s - k - N_EXPERTS, 0), 0)
    return pl.pallas_call(
        functools.partial(_moe_kernel, ts=ts, k=k, mp=mp),
        grid=(t // nblk, steps),
        in_specs=[pl.BlockSpec((None, 1, N_EXPERTS), lambda i, s: (i, 0, 0), memory_space=pltpu.SMEM),
                  smem(), smem(), smem(), smem(),
                  pl.BlockSpec((ts, D_MODEL), x_map), exp(wg), exp(wu), exp(wd), lay(g), lay(b)],
        out_specs=pl.BlockSpec((ts, D_MODEL), o_map),
        out_shape=jax.ShapeDtypeStruct((t, D_MODEL), F32),
        scratch_shapes=[pltpu.VMEM((4 * (ts + 8), 128), jnp.uint32), pltpu.VMEM((4 * mp, 128), jnp.uint32),
                        pltpu.VMEM((8 * mp, 128), F32), pltpu.VMEM((8 * (ts + 8), 128), F32),
                        pltpu.SMEM((N_EXPERTS,), jnp.int32), pltpu.SMEM((N_EXPERTS,), jnp.int32)],
        compiler_params=_cparams("arbitrary", "arbitrary"),
        name="moe_experts",
    )(cnt_blk, pos1, pos2, gt1, gt2, x2, wg, wu, wd, g, b)


def _tile(n, pref):
    t = min(n, pref)
    assert n % t == 0, (n, t)
    return t


def kernel(x, mem, positions, w_in, w_out, sgu_ln_g, sgu_ln_b, sgu_w, sgu_b, mla_q_norm_g, mla_w_uq, mla_kv_norm_g, mla_w_ukv, xa_wq, xa_wk, xa_wv, xa_wo, ln_mix_g, ln_mix_b, ln_xa_g, ln_xa_b, ln_moe_g, ln_moe_b, router_w, router_bias, expert_w_gate, expert_w_up, expert_w_down):
    batch, seq, _ = x.shape
    depth = w_in.shape[0]
    t = batch * seq
    assert seq % CHUNK == 0

    w_in_p = jnp.concatenate(
        [w_in[:, :, :1920], jnp.zeros((depth, D_MODEL, 64), F32), w_in[:, :, 1920:1952],
         jnp.zeros((depth, D_MODEL, 32), F32)], axis=2).astype(BF16)
    w_uq_p = jnp.pad(mla_w_uq.reshape(depth, MLA_Q_RANK, MLA_HEADS, MLA_NOPE + MLA_ROPE),
                     ((0, 0), (0, 0), (0, 0), (0, HEAD_PAD - MLA_NOPE - MLA_ROPE))
                     ).reshape(depth, MLA_Q_RANK, MLA_HEADS * HEAD_PAD).astype(BF16)
    ukv = mla_w_ukv.reshape(depth, MLA_KV_RANK, MLA_HEADS, MLA_NOPE + MLA_V)
    w_uk_p = jnp.pad(ukv[..., :MLA_NOPE], ((0, 0), (0, 0), (0, 0), (0, HEAD_PAD - MLA_NOPE))
                     ).reshape(depth, MLA_KV_RANK, MLA_HEADS * HEAD_PAD)
    w_uv_p = jnp.pad(ukv[..., MLA_NOPE:], ((0, 0), (0, 0), (0, 0), (0, HEAD_PAD - MLA_V))
                     ).reshape(depth, MLA_KV_RANK, MLA_HEADS * HEAD_PAD)
    w_ukv_p = jnp.concatenate([w_uk_p, w_uv_p], axis=2).astype(BF16)
    w_out_b = w_out.astype(BF16)
    wq_b, wk_b, wv_b, wo_b = (a.astype(BF16) for a in (xa_wq, xa_wk, xa_wv, xa_wo))
    wg_b, wu_b, wd_b = (a.astype(BF16) for a in (expert_w_gate, expert_w_up, expert_w_down))
    vec = lambda a: a.reshape(depth, 1, a.shape[-1])
    sgu_w_cat = jnp.transpose(sgu_w, (0, 2, 1, 3)).reshape(depth, CHUNK, SGU_GROUPS * CHUNK)
    sgu_b_lane = jnp.repeat(jnp.transpose(sgu_b, (0, 2, 1)), SGU_W // SGU_GROUPS, axis=2)
    rw_t = router_w.T
    rb_col = router_bias.reshape(N_EXPERTS, 1)
    consts = _retention_consts()

    x2d = x.reshape(t, D_MODEL)
    mem2d = mem.reshape(batch * MEM_LEN, D_MODEL)
    cr, sr, cm, sm = _rope_tables(positions, _tile(t, 512))

    tm_proj = _tile(t, 1024)
    tb = _tile(seq, 512)
    tq = _tile(seq, 1024)
    tm_mix = _tile(seq, 1024)
    nblk = _tile(seq, 2048)
    ts_moe = _tile(nblk, 512)
    tri = (jnp.arange(tm_mix)[:, None] <= jnp.arange(tm_mix)[None, :]).astype(BF16)
    for l in range(depth):
        rs, q, k, v = _projections(x2d, l, w_in_p, w_uq_p, w_ukv_p, vec(sgu_ln_g), vec(sgu_ln_b),
                                   vec(mla_q_norm_g), vec(mla_kv_norm_g), cr, sr, cm, sm, tm_proj)
        retsgu = _retention_sgu(rs, l, consts, sgu_w_cat, sgu_b_lane, batch, seq, tb)
        attn = _flash_attention(q, k, v, batch, seq, tq, 4)
        km, vm = _memory_kv(mem2d, l, wk_b, wv_b, batch)
        x2, pos1, pos2, gt1, gt2, cnts = _mix_xa(
            x2d, retsgu, attn, km, vm, l, w_out_b, vec(ln_mix_g), vec(ln_mix_b), wq_b, wo_b,
            vec(ln_xa_g), vec(ln_xa_b), rw_t, rb_col, tri, batch, seq, tm_mix, nblk)
        cnt_blk = cnts[:, :, 0].astype(jnp.int32).reshape(-1, 1, N_EXPERTS)
        x2d = _moe(x2, cnt_blk, pos1.reshape(t), pos2.reshape(t), gt1.reshape(t), gt2.reshape(t), l,
                   wg_b, wu_b, wd_b, vec(ln_moe_g), vec(ln_moe_b), nblk, ts_moe)
    return x2d.reshape(batch, seq, D_MODEL)
```

```python
import functools
import math

import jax
import jax.numpy as jnp
from jax import lax
from jax.experimental import pallas as pl
from jax.experimental.pallas import tpu as pltpu

F32 = jnp.float32
BF16 = jnp.bfloat16

D_MODEL = 1024
DEPTH = 4
MEM_LEN = 256
ROPE_THETA = 10000.0

RET_HEADS = 4
RET_DK = 64
RET_W = 256
CHUNK = 128

SGU_GROUPS = 4
SGU_W = 256

MLA_HEADS = 8
MLA_Q_RANK = 256
MLA_KV_RANK = 128
MLA_NOPE = 64
MLA_ROPE = 32
MLA_V = 64
MLA_W = MLA_HEADS * MLA_V
HEAD_PAD = 128

XA_HEADS = 4
XA_DIM = 128
XA_W = XA_HEADS * XA_DIM

N_EXPERTS = 16
N_GROUPS = 4
EXPERTS_PER_GROUP = 4
D_EXPERT = 256

ALPHA = (2 * DEPTH) ** 0.25
LN_EPS = 1e-5
IN_PAD = 2048
NEG_BIG = -1e30
MOE_TILE = 128
RANK_BITS = 16
RANK_RADIX = 1 << RANK_BITS

VMEM_LIMIT = 56 * 1024 * 1024


def _cparams(*sem):
    return pltpu.CompilerParams(dimension_semantics=sem, vmem_limit_bytes=VMEM_LIMIT)


def _dot(a, b):
    return jnp.dot(a, b, preferred_element_type=F32)


def _dot_nt(a, b):
    return lax.dot_general(a, b, (((1,), (1,)), ((), ())), preferred_element_type=F32)


def _dot_tn(a, b):
    return lax.dot_general(a, b, (((0,), (0,)), ((), ())), preferred_element_type=F32)


def _layer_norm(z, g, b):
    mu = jnp.mean(z, axis=-1, keepdims=True)
    zc = z - mu
    var = jnp.mean(zc * zc, axis=-1, keepdims=True)
    return zc * lax.rsqrt(var + LN_EPS) * g + b


def _rms_norm(z, g):
    ms = jnp.mean(z * z, axis=-1, keepdims=True)
    return z * lax.rsqrt(ms + LN_EPS) * g


def _silu(z):
    return z / (1.0 + jnp.exp(-z))


def _rope(x, cos, sin_signed, half):
    w = x.shape[-1]
    lane = lax.broadcasted_iota(jnp.int32, x.shape, 1)
    rot = jnp.where((lane & half) == 0, pltpu.roll(x, w - half, 1), pltpu.roll(x, half, 1))
    return x * cos + rot * sin_signed


def _split_bf16(x):
    hi = x.astype(BF16)
    lo = (x - hi.astype(F32)).astype(BF16)
    return hi, lo


def _tables_kernel(pos_ref, invf_ref, cr_ref, sr_ref, cm_ref, sm_ref):
    ang = pos_ref[...].astype(F32) * invf_ref[...]
    c = jnp.cos(ang)
    s = jnp.sin(ang)
    lane = lax.broadcasted_iota(jnp.int32, c.shape, 1)

    def tile32(v):
        v0 = jnp.where(lane < 32, v, 0.0)
        v1 = v0 + pltpu.roll(v0, 32, 1)
        return v1 + pltpu.roll(v1, 64, 1)

    ct = tile32(c)
    st = tile32(s) * jnp.where((lane & 32) == 0, -1.0, 1.0)
    cr_ref[...] = jnp.concatenate([ct, ct], axis=1)
    sr_ref[...] = jnp.concatenate([st, st], axis=1)
    in_src = (lane >= 32) & (lane < 48)
    cmv = jnp.where(in_src, c, 0.0)
    smv = jnp.where(in_src, s, 0.0)
    in_dst = (lane >= 64) & (lane < 96)
    cm_ref[...] = jnp.where(in_dst, pltpu.roll(cmv, 32, 1) + pltpu.roll(cmv, 48, 1), 1.0)
    sm_ref[...] = jnp.where(in_dst, pltpu.roll(smv, 48, 1) - pltpu.roll(smv, 32, 1), 0.0)


def _rope_tables(positions, tm):
    t = positions.size
    pos = positions.reshape(t, 1)
    fr = ROPE_THETA ** (-jnp.arange(0, RET_DK, 2, dtype=F32) / RET_DK)
    fm = ROPE_THETA ** (-jnp.arange(0, MLA_ROPE, 2, dtype=F32) / MLA_ROPE)
    invf = jnp.concatenate([fr, fm, jnp.zeros((128 - 48,), F32)]).reshape(1, 128)
    return pl.pallas_call(
        _tables_kernel,
        grid=(t // tm,),
        in_specs=[pl.BlockSpec((tm, 1), lambda i: (i, 0)),
                  pl.BlockSpec((1, 128), lambda i: (0, 0))],
        out_specs=[pl.BlockSpec((tm, 256), lambda i: (i, 0)),
                   pl.BlockSpec((tm, 256), lambda i: (i, 0)),
                   pl.BlockSpec((tm, 128), lambda i: (i, 0)),
                   pl.BlockSpec((tm, 128), lambda i: (i, 0))],
        out_shape=[jax.ShapeDtypeStruct((t, 256), F32), jax.ShapeDtypeStruct((t, 256), F32),
                   jax.ShapeDtypeStruct((t, 128), F32), jax.ShapeDtypeStruct((t, 128), F32)],
        compiler_params=_cparams("parallel"),
        name="rope_tables",
    )(pos, invf)


def _proj_kernel(x_ref, win_ref, wuq_ref, wukv_ref, lng_ref, lnb_ref, qg_ref, kvg_ref,
                 cr_ref, sr_ref, cm_ref, sm_ref, rs_ref, q_ref, k_ref, v_ref, *, q_scale):
    h = _dot(x_ref[...].astype(BF16), win_ref[...])
    cr = cr_ref[...]
    sr = sr_ref[...]
    rs_ref[:, 0:256] = _rope(h[:, 0:256], cr, sr, 32).astype(BF16)
    rs_ref[:, 256:512] = (_rope(h[:, 256:512], cr, sr, 32) * (RET_DK ** -0.5)).astype(BF16)
    rs_ref[:, 512:768] = h[:, 512:768].astype(BF16)
    rs_ref[:, 768:1024] = _silu(h[:, 768:1024]).astype(BF16)
    rs_ref[:, 1024:1280] = h[:, 1024:1280].astype(BF16)
    rs_ref[:, 1280:1536] = _layer_norm(h[:, 1280:1536], lng_ref[...], lnb_ref[...]).astype(BF16)

    cm = cm_ref[...]
    sm = sm_ref[...]
    cm8 = jnp.concatenate([cm] * MLA_HEADS, axis=1)
    sm8 = jnp.concatenate([sm] * MLA_HEADS, axis=1)
    cq = _rms_norm(h[:, 1536:1792], qg_ref[...]).astype(BF16)
    q = _rope(_dot(cq, wuq_ref[...]), cm8, sm8, 16)
    q_ref[...] = (q * q_scale).astype(BF16)
    ckv = _rms_norm(h[:, 1792:1920], kvg_ref[...]).astype(BF16)
    kv = _dot(ckv, wukv_ref[...])
    kr = _rope(h[:, 1920:2048], cm, sm, 16)
    k_ref[...] = (kv[:, 0:1024] + jnp.concatenate([kr] * MLA_HEADS, axis=1)).astype(BF16)
    lane = lax.broadcasted_iota(jnp.int32, (kv.shape[0], MLA_HEADS * HEAD_PAD), 1)
    ones_lane = jnp.where((lane & (HEAD_PAD - 1)) == MLA_V, 1.0, 0.0)
    v_ref[...] = (kv[:, 1024:2048] + ones_lane).astype(BF16)


def _projections(x2d, l, w_in, w_uq, w_ukv, sgu_g, sgu_b, q_g, kv_g, cr, sr, cm, sm, tm):
    t = x2d.shape[0]
    row = lambda w: pl.BlockSpec((tm, w), lambda i: (i, 0))
    lay = lambda a: pl.BlockSpec((None,) + a.shape[1:], lambda i: (l,) + (0,) * (a.ndim - 1))
    q_scale = (MLA_NOPE + MLA_ROPE) ** -0.5 * math.log2(math.e)
    return pl.pallas_call(
        functools.partial(_proj_kernel, q_scale=q_scale),
        grid=(t // tm,),
        in_specs=[row(D_MODEL), lay(w_in), lay(w_uq), lay(w_ukv), lay(sgu_g), lay(sgu_b),
                  lay(q_g), lay(kv_g), row(256), row(256), row(128), row(128)],
        out_specs=[row(1536), row(1024), row(1024), row(1024)],
        out_shape=[jax.ShapeDtypeStruct((t, 1536), BF16)] + [jax.ShapeDtypeStruct((t, 1024), BF16)] * 3,
        compiler_params=_cparams("parallel"),
        name="projections",
    )(x2d, w_in, w_uq, w_ukv, sgu_g, sgu_b, q_g, kv_g, cr, sr, cm, sm)


def _retsgu_kernel(rs_ref, dmat_ref, qdec_ref, kdec_ref, cdec_ref, gavg_ref, sw_ref, sb_ref,
                   out_ref, state_ref, *, n_chunks):
    @pl.when(pl.program_id(1) == 0)
    def _():
        state_ref[...] = jnp.zeros_like(state_ref)

    lane = lax.broadcasted_iota(jnp.int32, (CHUNK, RET_W), 1)
    head_of_lane = lane // RET_DK
    hmask = [head_of_lane == h for h in range(RET_HEADS)]
    row_i = lax.broadcasted_iota(jnp.int32, (CHUNK, SGU_GROUPS * CHUNK), 0)
    col_i = lax.broadcasted_iota(jnp.int32, (CHUNK, SGU_GROUPS * CHUNK), 1)
    sw = jnp.where(row_i >= (col_i & (CHUNK - 1)), sw_ref[...], 0.0).astype(BF16)
    blk = (lax.broadcasted_iota(jnp.int32, (RET_W, RET_W), 0) // RET_DK
           == lax.broadcasted_iota(jnp.int32, (RET_W, RET_W), 1) // RET_DK)
    gavg = gavg_ref[...]
    zero = jnp.zeros((), BF16)

    def group_mean(y):
        hi, lo = _split_bf16(y)
        return _dot(jnp.concatenate([hi, lo], axis=1), gavg)

    def chunk(c, state):
        r0 = c * CHUNK
        rq = rs_ref[pl.ds(r0, CHUNK), 0:256]
        rk = rs_ref[pl.ds(r0, CHUNK), 256:512]
        rv = rs_ref[pl.ds(r0, CHUNK), 512:768]
        gate = rs_ref[pl.ds(r0, CHUNK), 768:1024].astype(F32)
        su = rs_ref[pl.ds(r0, CHUNK), 1024:1280].astype(F32)
        sv = rs_ref[pl.ds(r0, CHUNK), 1280:1536]

        q_heads = jnp.concatenate([jnp.where(m, rq, zero) for m in hmask], axis=0)
        scores = _dot_nt(q_heads, rk) * dmat_ref[...]
        p_cat = jnp.concatenate(
            [scores[h * CHUNK:(h + 1) * CHUNK, :] for h in range(RET_HEADS)], axis=1).astype(BF16)
        v_heads = jnp.concatenate([jnp.where(m, rv, zero) for m in hmask], axis=0)
        intra = _dot(p_cat, v_heads)
        cross = _dot((rq.astype(F32) * qdec_ref[...]).astype(BF16), state.astype(BF16))
        kv = _dot_tn((rk.astype(F32) * kdec_ref[...]).astype(BF16), rv)
        state = state * cdec_ref[...] + jnp.where(blk, kv, 0.0)

        y = intra + cross
        yc = y - group_mean(y)
        var = group_mean(yc * yc)
        ret = gate * (yc * lax.rsqrt(var + LN_EPS))

        sv_groups = jnp.concatenate([jnp.where(m, sv, zero) for m in hmask], axis=0)
        mixed = _dot(sw, sv_groups) + sb_ref[...]
        out_ref[pl.ds(r0, CHUNK), 0:256] = ret.astype(BF16)
        out_ref[pl.ds(r0, CHUNK), 256:512] = (su * mixed).astype(BF16)
        return state

    state = state_ref[...]
    for c in range(n_chunks):
        state = chunk(c, state)
    state_ref[...] = state


def _retention_sgu(rs, l, consts, sgu_w_cat, sgu_b_lane, batch, seq, tb):
    dmat, qdec, kdec, cdec, gavg = consts
    n_blocks = seq // tb
    full = lambda a: pl.BlockSpec(a.shape, lambda b, i: (0,) * a.ndim)
    lay = lambda a: pl.BlockSpec((None,) + a.shape[1:], lambda b, i: (l,) + (0,) * (a.ndim - 1))
    return pl.pallas_call(
        functools.partial(_retsgu_kernel, n_chunks=tb // CHUNK),
        grid=(batch, n_blocks),
        in_specs=[pl.BlockSpec((tb, 1536), lambda b, i: (b * n_blocks + i, 0)),
                  full(dmat), full(qdec), full(kdec), full(cdec), full(gavg),
                  lay(sgu_w_cat), lay(sgu_b_lane)],
        out_specs=pl.BlockSpec((tb, 512), lambda b, i: (b * n_blocks + i, 0)),
        out_shape=jax.ShapeDtypeStruct((batch * seq, 512), BF16),
        scratch_shapes=[pltpu.VMEM((RET_W, RET_W), F32)],
        compiler_params=_cparams("parallel", "arbitrary"),
        name="retention_sgu",
    )(rs, dmat, qdec, kdec, cdec, gavg, sgu_w_cat, sgu_b_lane)


def _retention_consts():
    h = jnp.arange(RET_HEADS, dtype=F32)
    log_gamma = jnp.log1p(-(2.0 ** (-5.0 - h)))
    pos = jnp.arange(CHUNK, dtype=F32)
    diff = pos[:, None] - pos[None, :]
    intra = jnp.where(diff >= 0, jnp.exp(log_gamma[:, None, None] * jnp.maximum(diff, 0.0)), 0.0)
    dmat = intra.reshape(RET_HEADS * CHUNK, CHUNK)
    inner = jnp.exp(log_gamma[None, :] * (CHUNK - 1 - pos)[:, None])
    query = jnp.exp(log_gamma[None, :] * (pos + 1)[:, None])
    kdec = jnp.repeat(inner, RET_DK, axis=1)
    qdec = jnp.repeat(query, RET_DK, axis=1)
    chunk_decay = jnp.repeat(jnp.exp(log_gamma * CHUNK), RET_DK)
    blk = jnp.arange(RET_W)[:, None] // RET_DK == jnp.arange(RET_W)[None, :] // RET_DK
    cdec = jnp.where(blk, chunk_decay[:, None], 0.0)
    gavg = jnp.where(blk, 1.0 / RET_DK, 0.0).astype(BF16)
    return dmat, qdec, kdec, cdec, jnp.concatenate([gavg, gavg], axis=0)


def _flash_kernel(q_ref, k_ref, v_ref, o_ref, sa_ref, sb_ref, m_ref, acc_ref, *, tq, hps):
    i = pl.program_id(2)
    tk = tq // 2
    heads = range(hps)
    cols = [slice(hh * HEAD_PAD, (hh + 1) * HEAD_PAD) for hh in heads]

    def scores(j, hh, rows=slice(None)):
        r0 = pl.multiple_of(j * tk, tk)
        return _dot_nt(q_ref[rows, cols[hh]], k_ref[pl.ds(r0, tk), cols[hh]])

    def update(j, hh, s, rows=slice(None)):
        r0 = pl.multiple_of(j * tk, tk)
        m = m_ref[hh, rows, :]
        s_max = s[:, 0:HEAD_PAD]
        for c in range(1, tk // HEAD_PAD):
            s_max = jnp.maximum(s_max, s[:, c * HEAD_PAD:(c + 1) * HEAD_PAD])
        m_new = jnp.maximum(m, jnp.max(s_max, axis=-1, keepdims=True))
        p = jnp.exp2(s - jnp.concatenate([m_new] * (tk // HEAD_PAD), axis=1)).astype(BF16)
        acc_ref[hh, rows, :] = (jnp.exp2(m - m_new) * acc_ref[hh, rows, :]
                                + _dot(p, v_ref[pl.ds(r0, tk), cols[hh]]))
        m_ref[hh, rows, :] = m_new

    m_ref[...] = jnp.full(m_ref.shape, NEG_BIG, F32)
    acc_ref[...] = jnp.zeros(acc_ref.shape, F32)
    for hh in heads:
        sa_ref[hh] = scores(0, hh)

    def pair(t, carry):
        for hh in heads:
            sb_ref[hh] = scores(2 * t + 1, hh)
            update(2 * t, hh, sa_ref[hh])
        for hh in heads:
            sa_ref[hh] = scores(2 * t + 2, hh)
            update(2 * t + 1, hh, sb_ref[hh])
        return carry

    lax.fori_loop(0, i, pair, 0)
    low = slice(tk, tq)
    visible = (lax.broadcasted_iota(jnp.int32, (tq, tk), 1) <= lax.broadcasted_iota(jnp.int32, (tq, tk), 0))
    visible_low = (lax.broadcasted_iota(jnp.int32, (tk, tk), 1) <= lax.broadcasted_iota(jnp.int32, (tk, tk), 0))
    for hh in heads:
        sb_ref[hh, low, :] = scores(2 * i + 1, hh, low)
        update(2 * i, hh, jnp.where(visible, sa_ref[hh], NEG_BIG))
    for hh in heads:
        update(2 * i + 1, hh, jnp.where(visible_low, sb_ref[hh, low, :], NEG_BIG), low)
    lane = lax.broadcasted_iota(jnp.int32, (tq, HEAD_PAD), 1)
    for pr in range(hps // 2):
        o = []
        for hh in (2 * pr, 2 * pr + 1):
            acc = acc_ref[hh]
            o.append(acc / acc[:, MLA_V:MLA_V + 1])
        o_ref[:, pr * HEAD_PAD:(pr + 1) * HEAD_PAD] = jnp.where(
            lane < MLA_V, o[0], pltpu.roll(o[1], MLA_V, 1)).astype(BF16)


def _flash_attention(q, k, v, batch, seq, tq, hps):
    nq = seq // tq
    return pl.pallas_call(
        functools.partial(_flash_kernel, tq=tq, hps=hps),
        grid=(batch, MLA_HEADS // hps, nq),
        in_specs=[pl.BlockSpec((tq, hps * HEAD_PAD), lambda b, p, i: (b * nq + i, p)),
                  pl.BlockSpec((seq, hps * HEAD_PAD), lambda b, p, i: (b, p)),
                  pl.BlockSpec((seq, hps * HEAD_PAD), lambda b, p, i: (b, p))],
        out_specs=pl.BlockSpec((tq, hps * MLA_V), lambda b, p, i: (b * nq + i, p)),
        out_shape=jax.ShapeDtypeStruct((batch * seq, MLA_W), BF16),
        scratch_shapes=[pltpu.VMEM((hps, tq, tq // 2), F32), pltpu.VMEM((hps, tq, tq // 2), F32),
                        pltpu.VMEM((hps, tq, HEAD_PAD), F32), pltpu.VMEM((hps, tq, HEAD_PAD), F32)],
        compiler_params=_cparams("parallel", "parallel", "arbitrary"),
        name="flash_attention",
    )(q, k, v)


def _memkv_kernel(mem_ref, wk_ref, wv_ref, k_ref, v_ref):
    m = mem_ref[...].astype(BF16)
    k_ref[...] = _dot(m, wk_ref[...]).astype(BF16)
    v_ref[...] = _dot(m, wv_ref[...]).astype(BF16)


def _memory_kv(mem2d, l, wk, wv, batch):
    lay = lambda a: pl.BlockSpec((None,) + a.shape[1:], lambda b: (l,) + (0,) * (a.ndim - 1))
    blk = pl.BlockSpec((MEM_LEN, XA_W), lambda b: (b, 0))
    return pl.pallas_call(
        _memkv_kernel,
        grid=(batch,),
        in_specs=[pl.BlockSpec((MEM_LEN, D_MODEL), lambda b: (b, 0)), lay(wk), lay(wv)],
        out_specs=[blk, blk],
        out_shape=[jax.ShapeDtypeStruct((batch * MEM_LEN, XA_W), BF16)] * 2,
        compiler_params=_cparams("parallel"),
        name="memory_kv",
    )(mem2d, wk, wv)


def _route_rows(scores, biased):
    s = [scores[e:e + 1, :] for e in range(N_EXPERTS)]
    b = [biased[e:e + 1, :] for e in range(N_EXPERTS)]
    group_scores = []
    for g in range(N_GROUPS):
        b0, b1, b2, b3 = b[4 * g:4 * g + 4]
        hi01, lo01 = jnp.maximum(b0, b1), jnp.minimum(b0, b1)
        hi23, lo23 = jnp.maximum(b2, b3), jnp.minimum(b2, b3)
        top1 = jnp.maximum(hi01, hi23)
        top2 = jnp.maximum(jnp.minimum(hi01, hi23), jnp.maximum(lo01, lo23))
        group_scores.append(top1 + top2)
    best = group_scores[0]
    sel = jnp.zeros_like(best, dtype=jnp.int32)
    for g in range(1, N_GROUPS):
        upd = group_scores[g] > best
        sel = jnp.where(upd, g, sel)
        best = jnp.where(upd, group_scores[g], best)

    def pick(rows, j):
        out = rows[j]
        for g in range(1, N_GROUPS):
            out = jnp.where(sel == g, rows[4 * g + j], out)
        return out

    ib = [pick(b, j) for j in range(EXPERTS_PER_GROUP)]
    isc = [pick(s, j) for j in range(EXPERTS_PER_GROUP)]

    def argmax4(vals):
        bv, bi = vals[0], jnp.zeros_like(sel)
        for j in range(1, EXPERTS_PER_GROUP):
            upd = vals[j] > bv
            bi = jnp.where(upd, j, bi)
            bv = jnp.where(upd, vals[j], bv)
        return bi

    i1 = argmax4(ib)
    i2 = argmax4([jnp.where(i1 == j, -jnp.inf, ib[j]) for j in range(EXPERTS_PER_GROUP)])

    def take(vals, idx):
        out = vals[0]
        for j in range(1, EXPERTS_PER_GROUP):
            out = jnp.where(idx == j, vals[j], out)
        return out

    g1, g2 = take(isc, i1), take(isc, i2)
    den = g1 + g2
    g1, g2 = g1 / den, g2 / den
    return sel, i1, i2, g1, g2


def _mix_xa_kernel(x_ref, rs_ref, at_ref, wout_ref, g1_ref, b1_ref, wq_ref, km_ref, vm_ref, wo_ref,
                   g2_ref, b2_ref, rw_ref, rb_ref, tri_ref, x2_ref, pos_ref, gate_rows_ref,
                   cnt_ref, base_ref, pk_ref, *, tm, tiles_per_block):
    @pl.when(pl.program_id(1) % tiles_per_block == 0)
    def _():
        base_ref[...] = jnp.zeros_like(base_ref)

    w_hi, w_lo = _split_bf16(rw_ref[...])

    def token_rows(rows):
        mix = _dot(jnp.concatenate([rs_ref[rows, :], at_ref[rows, :]], axis=1), wout_ref[...])
        x1 = _layer_norm(ALPHA * x_ref[rows, :] + mix, g1_ref[...], b1_ref[...])
        q = (_dot(x1.astype(BF16), wq_ref[...]) * (XA_DIM ** -0.5 * math.log2(math.e))).astype(BF16)
        heads = []
        for h in range(XA_HEADS):
            sl = slice(h * XA_DIM, (h + 1) * XA_DIM)
            s = _dot_nt(q[:, sl], km_ref[:, sl])
            p = jnp.exp2(s - jnp.max(s, axis=-1, keepdims=True))
            o = _dot(p.astype(BF16), vm_ref[:, sl])
            heads.append(o / jnp.sum(p, axis=-1, keepdims=True))
        xa = _dot(jnp.concatenate(heads, axis=1).astype(BF16), wo_ref[...])
        x2 = _layer_norm(ALPHA * x1 + xa, g2_ref[...], b2_ref[...])
        x2_ref[rows, :] = x2
        x_hi, x_lo = _split_bf16(x2)
        return _dot_nt(w_hi, x_hi) + (_dot_nt(w_hi, x_lo) + _dot_nt(w_lo, x_hi))

    logits = token_rows(slice(None))
    scores = 1.0 / (1.0 + jnp.exp(-logits))
    sel, i1, i2, gate1, gate2 = _route_rows(scores, scores + rb_ref[...])
    in_group = [jnp.where(i1 == j, gate1, 0.0) + jnp.where(i2 == j, gate2, 0.0)
                for j in range(EXPERTS_PER_GROUP)]
    gate_rows_ref[...] = jnp.concatenate(in_group + [jnp.zeros((128 - EXPERTS_PER_GROUP, tm), F32)], axis=0).T
    hit = lax.broadcasted_iota(jnp.int32, (N_EXPERTS, tm), 0) == sel
    cnt = jnp.where(hit, 1.0, 0.0)
    base = base_ref[...]
    before = _dot(cnt.astype(BF16), tri_ref[...]) - cnt + base[:, 0:1]
    rank = jnp.sum(jnp.where(hit, before, 0.0), axis=0, keepdims=True).astype(jnp.int32)
    base = base + jnp.sum(cnt, axis=1, keepdims=True)
    base_ref[...] = base
    cnt_ref[...] = base
    ti = pl.program_id(1) % tiles_per_block
    pk_ref[ti] = sel * RANK_RADIX + rank

    @pl.when(ti == tiles_per_block - 1)
    def _():
        starts = []
        start = jnp.zeros((1, 128), F32)
        for g in range(N_GROUPS):
            starts.append(start.astype(jnp.int32)[:, 0:1])
            tiles = jnp.floor((base[g:g + 1, :] + (MOE_TILE - 1)) * (1.0 / MOE_TILE))
            start = start + tiles * MOE_TILE
        for tj in range(tiles_per_block):
            pk = pk_ref[tj]
            g_of = pk >> RANK_BITS
            slot = pk & (RANK_RADIX - 1)
            for g in range(N_GROUPS):
                slot = slot + jnp.where(g_of == g, starts[g], 0)
            pos_ref[:, tj * tm:(tj + 1) * tm] = slot


def _mix_xa(x2d, rs, at, km, vm, l, w_out, g1, b1, wq, wo, g2, b2, rw_t, rb_col, tri, batch, seq, tm, nblk):
    t = x2d.shape[0]
    nb = seq // tm
    n_tiles = t // tm
    tpb = nblk // tm
    row = lambda w: pl.BlockSpec((tm, w), lambda b, i: (b * nb + i, 0))
    lay = lambda a: pl.BlockSpec((None,) + a.shape[1:], lambda b, i: (l,) + (0,) * (a.ndim - 1))
    full = lambda a: pl.BlockSpec(a.shape, lambda b, i: (0,) * a.ndim)
    memb = pl.BlockSpec((MEM_LEN, XA_W), lambda b, i: (b, 0))
    blk = pl.BlockSpec((None, 1, nblk), lambda b, i: ((b * nb + i) // tpb, 0, 0))
    blk_i = jax.ShapeDtypeStruct((t // nblk, 1, nblk), jnp.int32)
    return pl.pallas_call(
        functools.partial(_mix_xa_kernel, tm=tm, tiles_per_block=tpb),
        grid=(batch, nb),
        in_specs=[row(D_MODEL), row(512), row(512), lay(w_out), lay(g1), lay(b1), lay(wq), memb, memb,
                  lay(wo), lay(g2), lay(b2), full(rw_t), full(rb_col), full(tri)],
        out_specs=[row(D_MODEL), blk, row(128),
                   pl.BlockSpec((None, N_EXPERTS, 128), lambda b, i: ((b * nb + i) // tpb, 0, 0))],
        out_shape=[jax.ShapeDtypeStruct((t, D_MODEL), F32), blk_i, jax.ShapeDtypeStruct((t, 128), F32),
                   jax.ShapeDtypeStruct((t // nblk, N_EXPERTS, 128), F32)],
        scratch_shapes=[pltpu.VMEM((N_EXPERTS, 128), F32), pltpu.VMEM((tpb, 1, tm), jnp.int32)],
        compiler_params=_cparams("parallel", "arbitrary"),
        name="mix_xattn_router",
    )(x2d, rs, at, w_out, g1, b1, wq, km, vm, wo, g2, b2, rw_t, rb_col, tri)


def _moe_kernel(cnt_ref, pos_ref, x_ref, gates_ref, wgu_ref, wd_ref, g_ref, b_ref,
                o_ref, xg_ref, xs_ref, ys_ref, og_ref, off_ref, ntile_ref, *, ts, k, mp):
    s = pl.program_id(1)
    sg = ts + 8
    unroll = 8

    @pl.when(s == 0)
    def _():
        xs_ref[...] = jnp.zeros_like(xs_ref)
        xg_ref[...] = jnp.zeros_like(xg_ref)
        start = jnp.int32(0)
        for g in range(N_GROUPS):
            tiles = (cnt_ref[0, g] + (MOE_TILE - 1)) // MOE_TILE
            off_ref[g] = start
            ntile_ref[g] = tiles
            start = start + tiles * MOE_TILE

    @pl.when(s < k)
    def _():
        x = x_ref[...]
        half = D_MODEL // 2
        hi = pltpu.bitcast(x[:, :half].astype(BF16).astype(F32), jnp.uint32)
        lo = pltpu.bitcast(x[:, half:].astype(BF16).astype(F32), jnp.uint32)
        words = hi | (lo >> 16)
        for c in range(4):
            xg_ref[c * sg:c * sg + ts, :] = words[:, c * 128:(c + 1) * 128]
        xg_ref[4 * sg:4 * sg + ts, :] = pltpu.bitcast(gates_ref[...], jnp.uint32)

        def dispatch(tt, carry):
            for u in range(unroll):
                tl = tt * unroll + u
                xs_ref[pl.ds(pos_ref[s * ts + tl], 8, stride=mp), :] = xg_ref[pl.ds(tl, 8, stride=sg), :]
            return carry

        lax.fori_loop(0, ts // unroll, dispatch, 0)

    @pl.when((s >= k) & (s < k + N_GROUPS))
    def _():
        g = s - k
        seg = off_ref[g]

        def row_tile(i):
            r0 = pl.multiple_of(seg + i * MOE_TILE, MOE_TILE)
            words = [xs_ref[pl.ds(c * mp + r0, MOE_TILE), :] for c in range(4)]
            gates = pltpu.bitcast(xs_ref[pl.ds(4 * mp + r0, MOE_TILE), :], F32)
            xb = jnp.concatenate(
                [pltpu.bitcast(w & jnp.uint32(0xFFFF0000), F32).astype(BF16) for w in words]
                + [pltpu.bitcast(w << 16, F32).astype(BF16) for w in words], axis=1)
            gu = _dot(xb, wgu_ref[...])
            width = EXPERTS_PER_GROUP * D_EXPERT
            weight = jnp.concatenate([jnp.broadcast_to(gates[:, e:e + 1], (MOE_TILE, D_EXPERT))
                                      for e in range(EXPERTS_PER_GROUP)], axis=1)
            hid = _silu(gu[:, :width]) * gu[:, width:] * weight
            y = _dot(hid.astype(BF16), wd_ref[...])
            for j in range(8):
                ys_ref[pl.ds(j * mp + r0, MOE_TILE), :] = y[:, j * 128:(j + 1) * 128]

        def tile_pair(i, carry):
            row_tile(2 * i)
            row_tile(2 * i + 1)
            return carry

        def tile_last(i, carry):
            row_tile(n_tiles - 1)
            return carry

        n_tiles = ntile_ref[g]
        lax.fori_loop(0, n_tiles // 2, tile_pair, 0)
        lax.fori_loop(0, n_tiles % 2, tile_last, 0)

    @pl.when(s >= k + N_GROUPS)
    def _():
        sub = s - (k + N_GROUPS)

        def combine(tt, carry):
            for u in range(unroll):
                tl = tt * unroll + u
                og_ref[pl.ds(tl, 8, stride=sg), :] = ys_ref[pl.ds(pos_ref[sub * ts + tl], 8, stride=mp), :]
            return carry

        lax.fori_loop(0, ts // unroll, combine, 0)
        ffn = jnp.concatenate([og_ref[j * sg:j * sg + ts, :] for j in range(8)], axis=1)
        o_ref[...] = _layer_norm(ALPHA * x_ref[...] + ffn, g_ref[...], b_ref[...])


def _moe(x2, cnt_blk, pos, gate_rows, l, wgu, wd, g, b, nblk, ts):
    t = x2.shape[0]
    k = nblk // ts
    steps = 2 * k + N_GROUPS
    mp = nblk + N_GROUPS * MOE_TILE + 8
    lay = lambda a: pl.BlockSpec((None,) + a.shape[1:], lambda i, s: (l,) + (0,) * (a.ndim - 1))
    grp = lambda a: pl.BlockSpec((None, None) + a.shape[2:],
                                 lambda i, s: (l, jnp.clip(s - k, 0, N_GROUPS - 1), 0, 0))
    x_map = lambda i, s: (i * k + jnp.where(s < k, s, jnp.where(s < k + N_GROUPS, k - 1, s - k - N_GROUPS)), 0)
    g_map = lambda i, s: (i * k + jnp.minimum(s, k - 1), 0)
    o_map = lambda i, s: (i * k + jnp.maximum(s - k - N_GROUPS, 0), 0)
    return pl.pallas_call(
        functools.partial(_moe_kernel, ts=ts, k=k, mp=mp),
        grid=(t // nblk, steps),
        in_specs=[pl.BlockSpec((None, 1, N_EXPERTS), lambda i, s: (i, 0, 0), memory_space=pltpu.SMEM),
                  pl.BlockSpec((nblk,), lambda i, s: (i,), memory_space=pltpu.SMEM),
                  pl.BlockSpec((ts, D_MODEL), x_map), pl.BlockSpec((ts, 128), g_map),
                  grp(wgu), grp(wd), lay(g), lay(b)],
        out_specs=pl.BlockSpec((ts, D_MODEL), o_map),
        out_shape=jax.ShapeDtypeStruct((t, D_MODEL), F32),
        scratch_shapes=[pltpu.VMEM((8 * (ts + 8), 128), jnp.uint32), pltpu.VMEM((8 * mp, 128), jnp.uint32),
                        pltpu.VMEM((8 * mp, 128), F32), pltpu.VMEM((8 * (ts + 8), 128), F32),
                        pltpu.SMEM((N_GROUPS,), jnp.int32), pltpu.SMEM((N_GROUPS,), jnp.int32)],
        compiler_params=_cparams("arbitrary", "arbitrary"),
        name="moe_experts",
    )(cnt_blk, pos, x2, gate_rows, wgu, wd, g, b)


def _tile(n, pref):
    t = min(n, pref)
    assert n % t == 0, (n, t)
    return t


def kernel(x, mem, positions, w_in, w_out, sgu_ln_g, sgu_ln_b, sgu_w, sgu_b, mla_q_norm_g, mla_w_uq, mla_kv_norm_g, mla_w_ukv, xa_wq, xa_wk, xa_wv, xa_wo, ln_mix_g, ln_mix_b, ln_xa_g, ln_xa_b, ln_moe_g, ln_moe_b, router_w, router_bias, expert_w_gate, expert_w_up, expert_w_down):
    batch, seq, _ = x.shape
    depth = w_in.shape[0]
    t = batch * seq
    assert seq % CHUNK == 0

    w_in_p = jnp.concatenate(
        [w_in[:, :, :1920], jnp.zeros((depth, D_MODEL, 64), F32), w_in[:, :, 1920:1952],
         jnp.zeros((depth, D_MODEL, 32), F32)], axis=2).astype(BF16)
    w_uq_p = jnp.pad(mla_w_uq.reshape(depth, MLA_Q_RANK, MLA_HEADS, MLA_NOPE + MLA_ROPE),
                     ((0, 0), (0, 0), (0, 0), (0, HEAD_PAD - MLA_NOPE - MLA_ROPE))
                     ).reshape(depth, MLA_Q_RANK, MLA_HEADS * HEAD_PAD).astype(BF16)
    ukv = mla_w_ukv.reshape(depth, MLA_KV_RANK, MLA_HEADS, MLA_NOPE + MLA_V)
    w_uk_p = jnp.pad(ukv[..., :MLA_NOPE], ((0, 0), (0, 0), (0, 0), (0, HEAD_PAD - MLA_NOPE))
                     ).reshape(depth, MLA_KV_RANK, MLA_HEADS * HEAD_PAD)
    w_uv_p = jnp.pad(ukv[..., MLA_NOPE:], ((0, 0), (0, 0), (0, 0), (0, HEAD_PAD - MLA_V))
                     ).reshape(depth, MLA_KV_RANK, MLA_HEADS * HEAD_PAD)
    w_ukv_p = jnp.concatenate([w_uk_p, w_uv_p], axis=2).astype(BF16)
    w_out_b = w_out.astype(BF16)
    wq_b, wk_b, wv_b, wo_b = (a.astype(BF16) for a in (xa_wq, xa_wk, xa_wv, xa_wo))
    def group_cols(w):
        return jnp.transpose(w.reshape(depth, N_GROUPS, EXPERTS_PER_GROUP, D_MODEL, D_EXPERT),
                             (0, 1, 3, 2, 4)).reshape(depth, N_GROUPS, D_MODEL, EXPERTS_PER_GROUP * D_EXPERT)

    wgu_b = jnp.concatenate([group_cols(expert_w_gate), group_cols(expert_w_up)], axis=-1).astype(BF16)
    wd_b = expert_w_down.reshape(depth, N_GROUPS, EXPERTS_PER_GROUP * D_EXPERT, D_MODEL).astype(BF16)
    vec = lambda a: a.reshape(depth, 1, a.shape[-1])
    sgu_w_cat = jnp.transpose(sgu_w, (0, 2, 1, 3)).reshape(depth, CHUNK, SGU_GROUPS * CHUNK)
    sgu_b_lane = jnp.repeat(jnp.transpose(sgu_b, (0, 2, 1)), SGU_W // SGU_GROUPS, axis=2)
    rw_t = router_w.T
    rb_col = router_bias.reshape(N_EXPERTS, 1)
    consts = _retention_consts()

    x2d = x.reshape(t, D_MODEL)
    mem2d = mem.reshape(batch * MEM_LEN, D_MODEL)
    cr, sr, cm, sm = _rope_tables(positions, _tile(t, 512))

    tm_proj = _tile(t, 1024)
    tb = _tile(seq, 512)
    tq = _tile(seq, 1024)
    tm_mix = _tile(seq, 1024)
    nblk = _tile(seq, 2048)
    ts_moe = _tile(nblk, 512)
    tri = (jnp.arange(tm_mix)[:, None] <= jnp.arange(tm_mix)[None, :]).astype(BF16)
    for l in range(depth):
        rs, q, k, v = _projections(x2d, l, w_in_p, w_uq_p, w_ukv_p, vec(sgu_ln_g), vec(sgu_ln_b),
                                   vec(mla_q_norm_g), vec(mla_kv_norm_g), cr, sr, cm, sm, tm_proj)
        retsgu = _retention_sgu(rs, l, consts, sgu_w_cat, sgu_b_lane, batch, seq, tb)
        attn = _flash_attention(q, k, v, batch, seq, tq, 4)
        km, vm = _memory_kv(mem2d, l, wk_b, wv_b, batch)
        x2, pos, gate_rows, cnts = _mix_xa(
            x2d, retsgu, attn, km, vm, l, w_out_b, vec(ln_mix_g), vec(ln_mix_b), wq_b, wo_b,
            vec(ln_xa_g), vec(ln_xa_b), rw_t, rb_col, tri, batch, seq, tm_mix, nblk)
        cnt_blk = cnts[:, :, 0].astype(jnp.int32).reshape(-1, 1, N_EXPERTS)
        x2d = _moe(x2, cnt_blk, pos.reshape(t), gate_rows, l,
                   wgu_b, wd_b, vec(ln_moe_g), vec(ln_moe_b), nblk, ts_moe)
    return x2d.reshape(batch, seq, D_MODEL)
```

```python
import functools
import math

import jax
import jax.numpy as jnp
from jax import lax
from jax.experimental import pallas as pl
from jax.experimental.pallas import tpu as pltpu

F32 = jnp.float32
BF16 = jnp.bfloat16

D_MODEL = 1024
DEPTH = 4
MEM_LEN = 256
ROPE_THETA = 10000.0

RET_HEADS = 4
RET_DK = 64
RET_W = 256
CHUNK = 128

SGU_GROUPS = 4
SGU_W = 256

MLA_HEADS = 8
MLA_Q_RANK = 256
MLA_KV_RANK = 128
MLA_NOPE = 64
MLA_ROPE = 32
MLA_V = 64
MLA_W = MLA_HEADS * MLA_V
HEAD_PAD = 128

XA_HEADS = 4
XA_DIM = 128
XA_W = XA_HEADS * XA_DIM

N_EXPERTS = 16
N_GROUPS = 4
EXPERTS_PER_GROUP = 4
D_EXPERT = 256

ALPHA = (2 * DEPTH) ** 0.25
LN_EPS = 1e-5
IN_PAD = 2048
NEG_BIG = -1e30
MOE_TILE = 128
RANK_BITS = 16
RANK_RADIX = 1 << RANK_BITS

VMEM_LIMIT = 56 * 1024 * 1024


def _cparams(*sem):
    return pltpu.CompilerParams(dimension_semantics=sem, vmem_limit_bytes=VMEM_LIMIT)


def _dot(a, b):
    return jnp.dot(a, b, preferred_element_type=F32)


def _dot_nt(a, b):
    return lax.dot_general(a, b, (((1,), (1,)), ((), ())), preferred_element_type=F32)


def _dot_tn(a, b):
    return lax.dot_general(a, b, (((0,), (0,)), ((), ())), preferred_element_type=F32)


def _layer_norm(z, g, b):
    mu = jnp.mean(z, axis=-1, keepdims=True)
    zc = z - mu
    var = jnp.mean(zc * zc, axis=-1, keepdims=True)
    return zc * lax.rsqrt(var + LN_EPS) * g + b


def _rms_norm(z, g):
    ms = jnp.mean(z * z, axis=-1, keepdims=True)
    return z * lax.rsqrt(ms + LN_EPS) * g


def _silu(z):
    return z / (1.0 + jnp.exp(-z))


def _rope(x, cos, sin_signed, half):
    w = x.shape[-1]
    lane = lax.broadcasted_iota(jnp.int32, x.shape, 1)
    rot = jnp.where((lane & half) == 0, pltpu.roll(x, w - half, 1), pltpu.roll(x, half, 1))
    return x * cos + rot * sin_signed


def _split_bf16(x):
    hi = x.astype(BF16)
    lo = (x - hi.astype(F32)).astype(BF16)
    return hi, lo


def _tables_kernel(pos_ref, invf_ref, cr_ref, sr_ref, cm_ref, sm_ref):
    ang = pos_ref[...].astype(F32) * invf_ref[...]
    c = jnp.cos(ang)
    s = jnp.sin(ang)
    lane = lax.broadcasted_iota(jnp.int32, c.shape, 1)

    def tile32(v):
        v0 = jnp.where(lane < 32, v, 0.0)
        v1 = v0 + pltpu.roll(v0, 32, 1)
        return v1 + pltpu.roll(v1, 64, 1)

    ct = tile32(c)
    st = tile32(s) * jnp.where((lane & 32) == 0, -1.0, 1.0)
    cr_ref[...] = jnp.concatenate([ct, ct], axis=1)
    sr_ref[...] = jnp.concatenate([st, st], axis=1)
    in_src = (lane >= 32) & (lane < 48)
    cmv = jnp.where(in_src, c, 0.0)
    smv = jnp.where(in_src, s, 0.0)
    in_dst = (lane >= 64) & (lane < 96)
    cm_ref[...] = jnp.where(in_dst, pltpu.roll(cmv, 32, 1) + pltpu.roll(cmv, 48, 1), 1.0)
    sm_ref[...] = jnp.where(in_dst, pltpu.roll(smv, 48, 1) - pltpu.roll(smv, 32, 1), 0.0)


def _rope_tables(positions, tm):
    t = positions.size
    pos = positions.reshape(t, 1)
    fr = ROPE_THETA ** (-jnp.arange(0, RET_DK, 2, dtype=F32) / RET_DK)
    fm = ROPE_THETA ** (-jnp.arange(0, MLA_ROPE, 2, dtype=F32) / MLA_ROPE)
    invf = jnp.concatenate([fr, fm, jnp.zeros((128 - 48,), F32)]).reshape(1, 128)
    return pl.pallas_call(
        _tables_kernel,
        grid=(t // tm,),
        in_specs=[pl.BlockSpec((tm, 1), lambda i: (i, 0)),
                  pl.BlockSpec((1, 128), lambda i: (0, 0))],
        out_specs=[pl.BlockSpec((tm, 256), lambda i: (i, 0)),
                   pl.BlockSpec((tm, 256), lambda i: (i, 0)),
                   pl.BlockSpec((tm, 128), lambda i: (i, 0)),
                   pl.BlockSpec((tm, 128), lambda i: (i, 0))],
        out_shape=[jax.ShapeDtypeStruct((t, 256), F32), jax.ShapeDtypeStruct((t, 256), F32),
                   jax.ShapeDtypeStruct((t, 128), F32), jax.ShapeDtypeStruct((t, 128), F32)],
        compiler_params=_cparams("parallel"),
        name="rope_tables",
    )(pos, invf)


def _proj_kernel(x_ref, win_ref, wuq_ref, wukv_ref, lng_ref, lnb_ref, qg_ref, kvg_ref,
                 cr_ref, sr_ref, cm_ref, sm_ref, rs_ref, q_ref, k_ref, v_ref, *, q_scale):
    h = _dot(x_ref[...].astype(BF16), win_ref[...])
    cr = cr_ref[...]
    sr = sr_ref[...]
    rs_ref[:, 0:256] = _rope(h[:, 0:256], cr, sr, 32).astype(BF16)
    rs_ref[:, 256:512] = (_rope(h[:, 256:512], cr, sr, 32) * (RET_DK ** -0.5)).astype(BF16)
    rs_ref[:, 512:768] = h[:, 512:768].astype(BF16)
    rs_ref[:, 768:1024] = _silu(h[:, 768:1024]).astype(BF16)
    rs_ref[:, 1024:1280] = h[:, 1024:1280].astype(BF16)
    rs_ref[:, 1280:1536] = _layer_norm(h[:, 1280:1536], lng_ref[...], lnb_ref[...]).astype(BF16)

    cm = cm_ref[...]
    sm = sm_ref[...]
    cm8 = jnp.concatenate([cm] * MLA_HEADS, axis=1)
    sm8 = jnp.concatenate([sm] * MLA_HEADS, axis=1)
    cq = _rms_norm(h[:, 1536:1792], qg_ref[...]).astype(BF16)
    q = _rope(_dot(cq, wuq_ref[...]), cm8, sm8, 16)
    q_ref[...] = (q * q_scale).astype(BF16)
    ckv = _rms_norm(h[:, 1792:1920], kvg_ref[...]).astype(BF16)
    kv = _dot(ckv, wukv_ref[...])
    kr = _rope(h[:, 1920:2048], cm, sm, 16)
    k_ref[...] = (kv[:, 0:1024] + jnp.concatenate([kr] * MLA_HEADS, axis=1)).astype(BF16)
    lane = lax.broadcasted_iota(jnp.int32, (kv.shape[0], MLA_HEADS * HEAD_PAD), 1)
    ones_lane = jnp.where((lane & (HEAD_PAD - 1)) == MLA_V, 1.0, 0.0)
    v_ref[...] = (kv[:, 1024:2048] + ones_lane).astype(BF16)


def _projections(x2d, l, w_in, w_uq, w_ukv, sgu_g, sgu_b, q_g, kv_g, cr, sr, cm, sm, tm):
    t = x2d.shape[0]
    row = lambda w: pl.BlockSpec((tm, w), lambda i: (i, 0))
    lay = lambda a: pl.BlockSpec((None,) + a.shape[1:], lambda i: (l,) + (0,) * (a.ndim - 1))
    q_scale = (MLA_NOPE + MLA_ROPE) ** -0.5 * math.log2(math.e)
    return pl.pallas_call(
        functools.partial(_proj_kernel, q_scale=q_scale),
        grid=(t // tm,),
        in_specs=[row(D_MODEL), lay(w_in), lay(w_uq), lay(w_ukv), lay(sgu_g), lay(sgu_b),
                  lay(q_g), lay(kv_g), row(256), row(256), row(128), row(128)],
        out_specs=[row(1536), row(1024), row(1024), row(1024)],
        out_shape=[jax.ShapeDtypeStruct((t, 1536), BF16)] + [jax.ShapeDtypeStruct((t, 1024), BF16)] * 3,
        compiler_params=_cparams("parallel"),
        name="projections",
    )(x2d, w_in, w_uq, w_ukv, sgu_g, sgu_b, q_g, kv_g, cr, sr, cm, sm)


def _retsgu_kernel(rs_ref, dmat_ref, qdec_ref, kdec_ref, cdec_ref, gavg_ref, sw_ref, sb_ref,
                   out_ref, state_ref, *, n_chunks):
    @pl.when(pl.program_id(1) == 0)
    def _():
        state_ref[...] = jnp.zeros_like(state_ref)

    lane = lax.broadcasted_iota(jnp.int32, (CHUNK, RET_W), 1)
    head_of_lane = lane // RET_DK
    hmask = [head_of_lane == h for h in range(RET_HEADS)]
    row_i = lax.broadcasted_iota(jnp.int32, (CHUNK, SGU_GROUPS * CHUNK), 0)
    col_i = lax.broadcasted_iota(jnp.int32, (CHUNK, SGU_GROUPS * CHUNK), 1)
    sw = jnp.where(row_i >= (col_i & (CHUNK - 1)), sw_ref[...], 0.0).astype(BF16)
    blk = (lax.broadcasted_iota(jnp.int32, (RET_W, RET_W), 0) // RET_DK
           == lax.broadcasted_iota(jnp.int32, (RET_W, RET_W), 1) // RET_DK)
    gavg = gavg_ref[...]
    zero = jnp.zeros((), BF16)

    def group_mean(y):
        hi, lo = _split_bf16(y)
        return _dot(jnp.concatenate([hi, lo], axis=1), gavg)

    def chunk(c, state):
        r0 = c * CHUNK
        rq = rs_ref[pl.ds(r0, CHUNK), 0:256]
        rk = rs_ref[pl.ds(r0, CHUNK), 256:512]
        rv = rs_ref[pl.ds(r0, CHUNK), 512:768]
        gate = rs_ref[pl.ds(r0, CHUNK), 768:1024].astype(F32)
        su = rs_ref[pl.ds(r0, CHUNK), 1024:1280].astype(F32)
        sv = rs_ref[pl.ds(r0, CHUNK), 1280:1536]

        q_heads = jnp.concatenate([jnp.where(m, rq, zero) for m in hmask], axis=0)
        scores = _dot_nt(q_heads, rk) * dmat_ref[...]
        p_cat = jnp.concatenate(
            [scores[h * CHUNK:(h + 1) * CHUNK, :] for h in range(RET_HEADS)], axis=1).astype(BF16)
        v_heads = jnp.concatenate([jnp.where(m, rv, zero) for m in hmask], axis=0)
        intra = _dot(p_cat, v_heads)
        cross = _dot((rq.astype(F32) * qdec_ref[...]).astype(BF16), state.astype(BF16))
        kv = _dot_tn((rk.astype(F32) * kdec_ref[...]).astype(BF16), rv)
        state = state * cdec_ref[...] + jnp.where(blk, kv, 0.0)

        y = intra + cross
        yc = y - group_mean(y)
        var = group_mean(yc * yc)
        ret = gate * (yc * lax.rsqrt(var + LN_EPS))

        sv_groups = jnp.concatenate([jnp.where(m, sv, zero) for m in hmask], axis=0)
        mixed = _dot(sw, sv_groups) + sb_ref[...]
        out_ref[pl.ds(r0, CHUNK), 0:256] = ret.astype(BF16)
        out_ref[pl.ds(r0, CHUNK), 256:512] = (su * mixed).astype(BF16)
        return state

    state = state_ref[...]
    for c in range(n_chunks):
        state = chunk(c, state)
    state_ref[...] = state


def _retention_sgu(rs, l, consts, sgu_w_cat, sgu_b_lane, batch, seq, tb):
    dmat, qdec, kdec, cdec, gavg = consts
    n_blocks = seq // tb
    full = lambda a: pl.BlockSpec(a.shape, lambda b, i: (0,) * a.ndim)
    lay = lambda a: pl.BlockSpec((None,) + a.shape[1:], lambda b, i: (l,) + (0,) * (a.ndim - 1))
    return pl.pallas_call(
        functools.partial(_retsgu_kernel, n_chunks=tb // CHUNK),
        grid=(batch, n_blocks),
        in_specs=[pl.BlockSpec((tb, 1536), lambda b, i: (b * n_blocks + i, 0)),
                  full(dmat), full(qdec), full(kdec), full(cdec), full(gavg),
                  lay(sgu_w_cat), lay(sgu_b_lane)],
        out_specs=pl.BlockSpec((tb, 512), lambda b, i: (b * n_blocks + i, 0)),
        out_shape=jax.ShapeDtypeStruct((batch * seq, 512), BF16),
        scratch_shapes=[pltpu.VMEM((RET_W, RET_W), F32)],
        compiler_params=_cparams("parallel", "arbitrary"),
        name="retention_sgu",
    )(rs, dmat, qdec, kdec, cdec, gavg, sgu_w_cat, sgu_b_lane)


def _retention_consts():
    h = jnp.arange(RET_HEADS, dtype=F32)
    log_gamma = jnp.log1p(-(2.0 ** (-5.0 - h)))
    pos = jnp.arange(CHUNK, dtype=F32)
    diff = pos[:, None] - pos[None, :]
    intra = jnp.where(diff >= 0, jnp.exp(log_gamma[:, None, None] * jnp.maximum(diff, 0.0)), 0.0)
    dmat = intra.reshape(RET_HEADS * CHUNK, CHUNK)
    inner = jnp.exp(log_gamma[None, :] * (CHUNK - 1 - pos)[:, None])
    query = jnp.exp(log_gamma[None, :] * (pos + 1)[:, None])
    kdec = jnp.repeat(inner, RET_DK, axis=1)
    qdec = jnp.repeat(query, RET_DK, axis=1)
    chunk_decay = jnp.repeat(jnp.exp(log_gamma * CHUNK), RET_DK)
    blk = jnp.arange(RET_W)[:, None] // RET_DK == jnp.arange(RET_W)[None, :] // RET_DK
    cdec = jnp.where(blk, chunk_decay[:, None], 0.0)
    gavg = jnp.where(blk, 1.0 / RET_DK, 0.0).astype(BF16)
    return dmat, qdec, kdec, cdec, jnp.concatenate([gavg, gavg], axis=0)


def _flash_kernel(q_ref, k_ref, v_ref, o_ref, sa_ref, sb_ref, m_ref, acc_ref, *, tq, hps):
    i = pl.program_id(2)
    tk = tq // 2
    heads = range(hps)
    cols = [slice(hh * HEAD_PAD, (hh + 1) * HEAD_PAD) for hh in heads]

    def scores(j, hh, rows=slice(None)):
        r0 = pl.multiple_of(j * tk, tk)
        return _dot_nt(q_ref[rows, cols[hh]], k_ref[pl.ds(r0, tk), cols[hh]])

    def update(j, hh, s, rows=slice(None)):
        r0 = pl.multiple_of(j * tk, tk)
        m = m_ref[hh, rows, :]
        s_max = s[:, 0:HEAD_PAD]
        for c in range(1, tk // HEAD_PAD):
            s_max = jnp.maximum(s_max, s[:, c * HEAD_PAD:(c + 1) * HEAD_PAD])
        m_new = jnp.maximum(m, jnp.max(s_max, axis=-1, keepdims=True))
        p = jnp.exp2(s - jnp.concatenate([m_new] * (tk // HEAD_PAD), axis=1)).astype(BF16)
        acc_ref[hh, rows, :] = (jnp.exp2(m - m_new) * acc_ref[hh, rows, :]
                                + _dot(p, v_ref[pl.ds(r0, tk), cols[hh]]))
        m_ref[hh, rows, :] = m_new

    m_ref[...] = jnp.full(m_ref.shape, NEG_BIG, F32)
    acc_ref[...] = jnp.zeros(acc_ref.shape, F32)
    for hh in heads:
        sa_ref[hh] = scores(0, hh)

    def pair(t, carry):
        for hh in heads:
            sb_ref[hh] = scores(2 * t + 1, hh)
            update(2 * t, hh, sa_ref[hh])
        for hh in heads:
            sa_ref[hh] = scores(2 * t + 2, hh)
            update(2 * t + 1, hh, sb_ref[hh])
        return carry

    lax.fori_loop(0, i, pair, 0)
    low = slice(tk, tq)
    visible = (lax.broadcasted_iota(jnp.int32, (tq, tk), 1) <= lax.broadcasted_iota(jnp.int32, (tq, tk), 0))
    visible_low = (lax.broadcasted_iota(jnp.int32, (tk, tk), 1) <= lax.broadcasted_iota(jnp.int32, (tk, tk), 0))
    for hh in heads:
        sb_ref[hh, low, :] = scores(2 * i + 1, hh, low)
        update(2 * i, hh, jnp.where(visible, sa_ref[hh], NEG_BIG))
    for hh in heads:
        update(2 * i + 1, hh, jnp.where(visible_low, sb_ref[hh, low, :], NEG_BIG), low)
    lane = lax.broadcasted_iota(jnp.int32, (tq, HEAD_PAD), 1)
    for pr in range(hps // 2):
        o = []
        for hh in (2 * pr, 2 * pr + 1):
            acc = acc_ref[hh]
            o.append(acc / acc[:, MLA_V:MLA_V + 1])
        o_ref[:, pr * HEAD_PAD:(pr + 1) * HEAD_PAD] = jnp.where(
            lane < MLA_V, o[0], pltpu.roll(o[1], MLA_V, 1)).astype(BF16)


def _flash_attention(q, k, v, batch, seq, tq, hps):
    nq = seq // tq
    return pl.pallas_call(
        functools.partial(_flash_kernel, tq=tq, hps=hps),
        grid=(batch, MLA_HEADS // hps, nq),
        in_specs=[pl.BlockSpec((tq, hps * HEAD_PAD), lambda b, p, i: (b * nq + i, p)),
                  pl.BlockSpec((seq, hps * HEAD_PAD), lambda b, p, i: (b, p)),
                  pl.BlockSpec((seq, hps * HEAD_PAD), lambda b, p, i: (b, p))],
        out_specs=pl.BlockSpec((tq, hps * MLA_V), lambda b, p, i: (b * nq + i, p)),
        out_shape=jax.ShapeDtypeStruct((batch * seq, MLA_W), BF16),
        scratch_shapes=[pltpu.VMEM((hps, tq, tq // 2), F32), pltpu.VMEM((hps, tq, tq // 2), F32),
                        pltpu.VMEM((hps, tq, HEAD_PAD), F32), pltpu.VMEM((hps, tq, HEAD_PAD), F32)],
        compiler_params=_cparams("parallel", "parallel", "arbitrary"),
        name="flash_attention",
    )(q, k, v)


def _memkv_kernel(mem_ref, wk_ref, wv_ref, k_ref, v_ref):
    m = mem_ref[...].astype(BF16)
    k_ref[...] = _dot(m, wk_ref[...]).astype(BF16)
    v_ref[...] = _dot(m, wv_ref[...]).astype(BF16)


def _memory_kv(mem2d, l, wk, wv, batch):
    lay = lambda a: pl.BlockSpec((None,) + a.shape[1:], lambda b: (l,) + (0,) * (a.ndim - 1))
    blk = pl.BlockSpec((MEM_LEN, XA_W), lambda b: (b, 0))
    return pl.pallas_call(
        _memkv_kernel,
        grid=(batch,),
        in_specs=[pl.BlockSpec((MEM_LEN, D_MODEL), lambda b: (b, 0)), lay(wk), lay(wv)],
        out_specs=[blk, blk],
        out_shape=[jax.ShapeDtypeStruct((batch * MEM_LEN, XA_W), BF16)] * 2,
        compiler_params=_cparams("parallel"),
        name="memory_kv",
    )(mem2d, wk, wv)


def _route_rows(scores, biased):
    s = [scores[e:e + 1, :] for e in range(N_EXPERTS)]
    b = [biased[e:e + 1, :] for e in range(N_EXPERTS)]
    group_scores = []
    for g in range(N_GROUPS):
        b0, b1, b2, b3 = b[4 * g:4 * g + 4]
        hi01, lo01 = jnp.maximum(b0, b1), jnp.minimum(b0, b1)
        hi23, lo23 = jnp.maximum(b2, b3), jnp.minimum(b2, b3)
        top1 = jnp.maximum(hi01, hi23)
        top2 = jnp.maximum(jnp.minimum(hi01, hi23), jnp.maximum(lo01, lo23))
        group_scores.append(top1 + top2)
    best = group_scores[0]
    sel = jnp.zeros_like(best, dtype=jnp.int32)
    for g in range(1, N_GROUPS):
        upd = group_scores[g] > best
        sel = jnp.where(upd, g, sel)
        best = jnp.where(upd, group_scores[g], best)

    def pick(rows, j):
        out = rows[j]
        for g in range(1, N_GROUPS):
            out = jnp.where(sel == g, rows[4 * g + j], out)
        return out

    ib = [pick(b, j) for j in range(EXPERTS_PER_GROUP)]
    isc = [pick(s, j) for j in range(EXPERTS_PER_GROUP)]

    def argmax4(vals):
        bv, bi = vals[0], jnp.zeros_like(sel)
        for j in range(1, EXPERTS_PER_GROUP):
            upd = vals[j] > bv
            bi = jnp.where(upd, j, bi)
            bv = jnp.where(upd, vals[j], bv)
        return bi

    i1 = argmax4(ib)
    i2 = argmax4([jnp.where(i1 == j, -jnp.inf, ib[j]) for j in range(EXPERTS_PER_GROUP)])

    def take(vals, idx):
        out = vals[0]
        for j in range(1, EXPERTS_PER_GROUP):
            out = jnp.where(idx == j, vals[j], out)
        return out

    g1, g2 = take(isc, i1), take(isc, i2)
    den = g1 + g2
    g1, g2 = g1 / den, g2 / den
    return sel, i1, i2, g1, g2


def _mix_xa_kernel(x_ref, rs_ref, at_ref, wout_ref, g1_ref, b1_ref, wq_ref, km_ref, vm_ref, wo_ref,
                   g2_ref, b2_ref, rw_ref, rb_ref, tri_ref, x2_ref, pos_ref, gate_rows_ref,
                   cnt_ref, base_ref, pk_ref, *, tm, tiles_per_block):
    @pl.when(pl.program_id(1) % tiles_per_block == 0)
    def _():
        base_ref[...] = jnp.zeros_like(base_ref)

    w_hi, w_lo = _split_bf16(rw_ref[...])

    def front(rows):
        mix = _dot(jnp.concatenate([rs_ref[rows, :], at_ref[rows, :]], axis=1), wout_ref[...])
        x1 = _layer_norm(ALPHA * x_ref[rows, :] + mix, g1_ref[...], b1_ref[...])
        q = (_dot(x1.astype(BF16), wq_ref[...]) * (XA_DIM ** -0.5 * math.log2(math.e))).astype(BF16)
        heads = []
        for h in range(XA_HEADS):
            sl = slice(h * XA_DIM, (h + 1) * XA_DIM)
            s = _dot_nt(q[:, sl], km_ref[:, sl])
            p = jnp.exp2(s - jnp.max(s, axis=-1, keepdims=True))
            o = _dot(p.astype(BF16), vm_ref[:, sl])
            heads.append(o / jnp.sum(p, axis=-1, keepdims=True))
        xa = _dot(jnp.concatenate(heads, axis=1).astype(BF16), wo_ref[...])
        return ALPHA * x1 + xa

    def back(rows, z):
        x2 = _layer_norm(z, g2_ref[...], b2_ref[...])
        x2_ref[rows, :] = x2
        x_hi, x_lo = _split_bf16(x2)
        return _dot_nt(w_hi, x_hi) + (_dot_nt(w_hi, x_lo) + _dot_nt(w_lo, x_hi))

    logits = back(slice(None), front(slice(None)))
    scores = 1.0 / (1.0 + jnp.exp(-logits))
    sel, i1, i2, gate1, gate2 = _route_rows(scores, scores + rb_ref[...])
    in_group = [jnp.where(i1 == j, gate1, 0.0) + jnp.where(i2 == j, gate2, 0.0)
                for j in range(EXPERTS_PER_GROUP)]
    gate_rows_ref[...] = jnp.concatenate(in_group + [jnp.zeros((128 - EXPERTS_PER_GROUP, tm), F32)], axis=0).T
    hit = lax.broadcasted_iota(jnp.int32, (N_EXPERTS, tm), 0) == sel
    cnt = jnp.where(hit, 1.0, 0.0)
    base = base_ref[...]
    before = _dot(cnt.astype(BF16), tri_ref[...]) - cnt + base[:, 0:1]
    rank = jnp.sum(jnp.where(hit, before, 0.0), axis=0, keepdims=True).astype(jnp.int32)
    base = base + jnp.sum(cnt, axis=1, keepdims=True)
    base_ref[...] = base
    cnt_ref[...] = base
    ti = pl.program_id(1) % tiles_per_block
    pk_ref[ti] = sel * RANK_RADIX + rank

    @pl.when(ti == tiles_per_block - 1)
    def _():
        starts = []
        start = jnp.zeros((1, 128), F32)
        for g in range(N_GROUPS):
            starts.append(start.astype(jnp.int32)[:, 0:1])
            tiles = jnp.floor((base[g:g + 1, :] + (MOE_TILE - 1)) * (1.0 / MOE_TILE))
            start = start + tiles * MOE_TILE
        for tj in range(tiles_per_block):
            pk = pk_ref[tj]
            g_of = pk >> RANK_BITS
            slot = pk & (RANK_RADIX - 1)
            for g in range(N_GROUPS):
                slot = slot + jnp.where(g_of == g, starts[g], 0)
            pos_ref[:, tj * tm:(tj + 1) * tm] = slot


def _mix_xa(x2d, rs, at, km, vm, l, w_out, g1, b1, wq, wo, g2, b2, rw_t, rb_col, tri, batch, seq, tm, nblk):
    t = x2d.shape[0]
    nb = seq // tm
    n_tiles = t // tm
    tpb = nblk // tm
    row = lambda w: pl.BlockSpec((tm, w), lambda b, i: (b * nb + i, 0))
    lay = lambda a: pl.BlockSpec((None,) + a.shape[1:], lambda b, i: (l,) + (0,) * (a.ndim - 1))
    full = lambda a: pl.BlockSpec(a.shape, lambda b, i: (0,) * a.ndim)
    memb = pl.BlockSpec((MEM_LEN, XA_W), lambda b, i: (b, 0))
    blk = pl.BlockSpec((None, 1, nblk), lambda b, i: ((b * nb + i) // tpb, 0, 0))
    blk_i = jax.ShapeDtypeStruct((t // nblk, 1, nblk), jnp.int32)
    return pl.pallas_call(
        functools.partial(_mix_xa_kernel, tm=tm, tiles_per_block=tpb),
        grid=(batch, nb),
        in_specs=[row(D_MODEL), row(512), row(512), lay(w_out), lay(g1), lay(b1), lay(wq), memb, memb,
                  lay(wo), lay(g2), lay(b2), full(rw_t), full(rb_col), full(tri)],
        out_specs=[row(D_MODEL), blk, row(128),
                   pl.BlockSpec((None, N_EXPERTS, 128), lambda b, i: ((b * nb + i) // tpb, 0, 0))],
        out_shape=[jax.ShapeDtypeStruct((t, D_MODEL), F32), blk_i, jax.ShapeDtypeStruct((t, 128), F32),
                   jax.ShapeDtypeStruct((t // nblk, N_EXPERTS, 128), F32)],
        scratch_shapes=[pltpu.VMEM((N_EXPERTS, 128), F32), pltpu.VMEM((tpb, 1, tm), jnp.int32)],
        compiler_params=_cparams("parallel", "arbitrary"),
        name="mix_xattn_router",
    )(x2d, rs, at, w_out, g1, b1, wq, km, vm, wo, g2, b2, rw_t, rb_col, tri)


def _moe_kernel(cnt_ref, pos_ref, x_ref, gates_ref, wg_ref, wu_ref, wd_ref, g_ref, b_ref,
                o_ref, xg_ref, xs_ref, ys_ref, og_ref, off_ref, ntile_ref, *, ts, k, mp):
    s = pl.program_id(1)
    sg = ts + 8
    unroll = 8

    @pl.when((s == 0) & (pl.program_id(0) == 0))
    def _():
        xs_ref[...] = jnp.zeros_like(xs_ref)
        xg_ref[...] = jnp.zeros_like(xg_ref)

    @pl.when(s == 0)
    def _():
        start = jnp.int32(0)
        for g in range(N_GROUPS):
            tiles = (cnt_ref[0, g] + (MOE_TILE - 1)) // MOE_TILE
            off_ref[g] = start
            ntile_ref[g] = tiles
            start = start + tiles * MOE_TILE

    @pl.when(s < k)
    def _():
        x = x_ref[...]
        half = D_MODEL // 2
        hi = pltpu.bitcast(x[:, :half].astype(BF16).astype(F32), jnp.uint32)
        lo = pltpu.bitcast(x[:, half:].astype(BF16).astype(F32), jnp.uint32)
        words = hi | (lo >> 16)
        for c in range(4):
            xg_ref[c * sg:c * sg + ts, :] = words[:, c * 128:(c + 1) * 128]
        xg_ref[4 * sg:4 * sg + ts, :] = pltpu.bitcast(gates_ref[...], jnp.uint32)

        def dispatch(tt, carry):
            for u in range(unroll):
                tl = tt * unroll + u
                xs_ref[pl.ds(pos_ref[s * ts + tl], 8, stride=mp), :] = xg_ref[pl.ds(tl, 8, stride=sg), :]
            return carry

        lax.fori_loop(0, ts // unroll, dispatch, 0)

    @pl.when((s >= k) & (s < k + N_GROUPS))
    def _():
        g = s - k
        seg = off_ref[g]

        def row_tile(i):
            r0 = pl.multiple_of(seg + i * MOE_TILE, MOE_TILE)
            words = [xs_ref[pl.ds(c * mp + r0, MOE_TILE), :] for c in range(4)]
            gates = pltpu.bitcast(xs_ref[pl.ds(4 * mp + r0, MOE_TILE), :], F32)
            xb = jnp.concatenate(
                [pltpu.bitcast(w & jnp.uint32(0xFFFF0000), F32).astype(BF16) for w in words]
                + [pltpu.bitcast(w << 16, F32).astype(BF16) for w in words], axis=1)
            weight = jnp.concatenate([jnp.broadcast_to(gates[:, e:e + 1], (MOE_TILE, D_EXPERT))
                                      for e in range(EXPERTS_PER_GROUP)], axis=1)
            hid = _silu(_dot(xb, wg_ref[...])) * _dot(xb, wu_ref[...]) * weight
            y = _dot(hid.astype(BF16), wd_ref[...])
            for j in range(8):
                ys_ref[pl.ds(j * mp + r0, MOE_TILE), :] = y[:, j * 128:(j + 1) * 128]

        def tile_pair(i, carry):
            row_tile(2 * i)
            row_tile(2 * i + 1)
            return carry

        def tile_last(i, carry):
            row_tile(n_tiles - 1)
            return carry

        n_tiles = ntile_ref[g]
        lax.fori_loop(0, n_tiles // 2, tile_pair, 0)
        lax.fori_loop(0, n_tiles % 2, tile_last, 0)

    @pl.when(s >= k + N_GROUPS)
    def _():
        sub = s - (k + N_GROUPS)

        def combine(tt, carry):
            for u in range(unroll):
                tl = tt * unroll + u
                og_ref[pl.ds(tl, 8, stride=sg), :] = ys_ref[pl.ds(pos_ref[sub * ts + tl], 8, stride=mp), :]
            return carry

        lax.fori_loop(0, ts // unroll, combine, 0)
        ffn = jnp.concatenate([og_ref[j * sg:j * sg + ts, :] for j in range(8)], axis=1)
        o_ref[...] = _layer_norm(ALPHA * x_ref[...] + ffn, g_ref[...], b_ref[...])


def _moe(x2, cnt_blk, pos, gate_rows, l, wg, wu, wd, g, b, nblk, ts):
    t = x2.shape[0]
    k = nblk // ts
    steps = 2 * k + N_GROUPS
    mp = nblk + N_GROUPS * MOE_TILE + 8
    lay = lambda a: pl.BlockSpec((None,) + a.shape[1:], lambda i, s: (l,) + (0,) * (a.ndim - 1))
    grp = lambda a: pl.BlockSpec((None, None) + a.shape[2:],
                                 lambda i, s: (l, jnp.clip(s - k, 0, N_GROUPS - 1), 0, 0))
    x_map = lambda i, s: (i * k + jnp.where(s < k, s, jnp.where(s < k + N_GROUPS, k - 1, s - k - N_GROUPS)), 0)
    g_map = lambda i, s: (i * k + jnp.minimum(s, k - 1), 0)
    o_map = lambda i, s: (i * k + jnp.maximum(s - k - N_GROUPS, 0), 0)
    return pl.pallas_call(
        functools.partial(_moe_kernel, ts=ts, k=k, mp=mp),
        grid=(t // nblk, steps),
        in_specs=[pl.BlockSpec((None, 1, N_EXPERTS), lambda i, s: (i, 0, 0), memory_space=pltpu.SMEM),
                  pl.BlockSpec((nblk,), lambda i, s: (i,), memory_space=pltpu.SMEM),
                  pl.BlockSpec((ts, D_MODEL), x_map), pl.BlockSpec((ts, 128), g_map),
                  grp(wg), grp(wu), grp(wd), lay(g), lay(b)],
        out_specs=pl.BlockSpec((ts, D_MODEL), o_map),
        out_shape=jax.ShapeDtypeStruct((t, D_MODEL), F32),
        scratch_shapes=[pltpu.VMEM((8 * (ts + 8), 128), jnp.uint32), pltpu.VMEM((8 * mp, 128), jnp.uint32),
                        pltpu.VMEM((8 * mp, 128), F32), pltpu.VMEM((8 * (ts + 8), 128), F32),
                        pltpu.SMEM((N_GROUPS,), jnp.int32), pltpu.SMEM((N_GROUPS,), jnp.int32)],
        compiler_params=_cparams("arbitrary", "arbitrary"),
        name="moe_experts",
    )(cnt_blk, pos, x2, gate_rows, wg, wu, wd, g, b)


def _tile(n, pref):
    t = min(n, pref)
    assert n % t == 0, (n, t)
    return t


def kernel(x, mem, positions, w_in, w_out, sgu_ln_g, sgu_ln_b, sgu_w, sgu_b, mla_q_norm_g, mla_w_uq, mla_kv_norm_g, mla_w_ukv, xa_wq, xa_wk, xa_wv, xa_wo, ln_mix_g, ln_mix_b, ln_xa_g, ln_xa_b, ln_moe_g, ln_moe_b, router_w, router_bias, expert_w_gate, expert_w_up, expert_w_down):
    batch, seq, _ = x.shape
    depth = w_in.shape[0]
    t = batch * seq
    assert seq % CHUNK == 0

    w_in_b = w_in.astype(BF16)
    w_in_p = jnp.concatenate(
        [w_in_b[:, :, :1920], jnp.zeros((depth, D_MODEL, 64), BF16), w_in_b[:, :, 1920:1952],
         jnp.zeros((depth, D_MODEL, 32), BF16)], axis=2)
    w_uq_p = jnp.pad(mla_w_uq.astype(BF16).reshape(depth, MLA_Q_RANK, MLA_HEADS, MLA_NOPE + MLA_ROPE),
                     ((0, 0), (0, 0), (0, 0), (0, HEAD_PAD - MLA_NOPE - MLA_ROPE))
                     ).reshape(depth, MLA_Q_RANK, MLA_HEADS * HEAD_PAD)
    ukv = mla_w_ukv.astype(BF16).reshape(depth, MLA_KV_RANK, MLA_HEADS, MLA_NOPE + MLA_V)
    w_uk_p = jnp.pad(ukv[..., :MLA_NOPE], ((0, 0), (0, 0), (0, 0), (0, HEAD_PAD - MLA_NOPE))
                     ).reshape(depth, MLA_KV_RANK, MLA_HEADS * HEAD_PAD)
    w_uv_p = jnp.pad(ukv[..., MLA_NOPE:], ((0, 0), (0, 0), (0, 0), (0, HEAD_PAD - MLA_V))
                     ).reshape(depth, MLA_KV_RANK, MLA_HEADS * HEAD_PAD)
    w_ukv_p = jnp.concatenate([w_uk_p, w_uv_p], axis=2)
    w_out_b = w_out.astype(BF16)
    wq_b, wk_b, wv_b, wo_b = (a.astype(BF16) for a in (xa_wq, xa_wk, xa_wv, xa_wo))
    def group_cols(w):
        return jnp.transpose(w.reshape(depth, N_GROUPS, EXPERTS_PER_GROUP, D_MODEL, D_EXPERT),
                             (0, 1, 3, 2, 4)).reshape(depth, N_GROUPS, D_MODEL, EXPERTS_PER_GROUP * D_EXPERT)

    wg_b = group_cols(expert_w_gate).astype(BF16)
    wu_b = group_cols(expert_w_up).astype(BF16)
    wd_b = expert_w_down.reshape(depth, N_GROUPS, EXPERTS_PER_GROUP * D_EXPERT, D_MODEL).astype(BF16)
    vec = lambda a: a.reshape(depth, 1, a.shape[-1])
    sgu_w_cat = jnp.transpose(sgu_w, (0, 2, 1, 3)).reshape(depth, CHUNK, SGU_GROUPS * CHUNK)
    sgu_b_lane = jnp.repeat(jnp.transpose(sgu_b, (0, 2, 1)), SGU_W // SGU_GROUPS, axis=2)
    rw_t = router_w.T
    rb_col = router_bias.reshape(N_EXPERTS, 1)
    consts = _retention_consts()

    x2d = x.reshape(t, D_MODEL)
    mem2d = mem.reshape(batch * MEM_LEN, D_MODEL)
    cr, sr, cm, sm = _rope_tables(positions, _tile(t, 512))

    tm_proj = _tile(t, 1024)
    tb = _tile(seq, 1024)
    tq = _tile(seq, 1024)
    tm_mix = _tile(seq, 1024)
    nblk = _tile(seq, 2048)
    ts_moe = _tile(nblk, 512)
    tri = (jnp.arange(tm_mix)[:, None] <= jnp.arange(tm_mix)[None, :]).astype(BF16)
    for l in range(depth):
        rs, q, k, v = _projections(x2d, l, w_in_p, w_uq_p, w_ukv_p, vec(sgu_ln_g), vec(sgu_ln_b),
                                   vec(mla_q_norm_g), vec(mla_kv_norm_g), cr, sr, cm, sm, tm_proj)
        retsgu = _retention_sgu(rs, l, consts, sgu_w_cat, sgu_b_lane, batch, seq, tb)
        attn = _flash_attention(q, k, v, batch, seq, tq, 4)
        km, vm = _memory_kv(mem2d, l, wk_b, wv_b, batch)
        x2, pos, gate_rows, cnts = _mix_xa(
            x2d, retsgu, attn, km, vm, l, w_out_b, vec(ln_mix_g), vec(ln_mix_b), wq_b, wo_b,
            vec(ln_xa_g), vec(ln_xa_b), rw_t, rb_col, tri, batch, seq, tm_mix, nblk)
        cnt_blk = cnts[:, :, 0].astype(jnp.int32).reshape(-1, 1, N_EXPERTS)
        x2d = _moe(x2, cnt_blk, pos.reshape(t), gate_rows, l,
                   wg_b, wu_b, wd_b, vec(ln_moe_g), vec(ln_moe_b), nblk, ts_moe)
    return x2d.reshape(batch, seq, D_MODEL)
```

```python
import functools
import math

import jax
import jax.numpy as jnp
from jax import lax
from jax.experimental import pallas as pl
from jax.experimental.pallas import tpu as pltpu

F32 = jnp.float32
BF16 = jnp.bfloat16

D_MODEL = 1024
DEPTH = 4
MEM_LEN = 256
ROPE_THETA = 10000.0

RET_HEADS = 4
RET_DK = 64
RET_W = 256
CHUNK = 128

SGU_GROUPS = 4
SGU_W = 256

MLA_HEADS = 8
MLA_Q_RANK = 256
MLA_KV_RANK = 128
MLA_NOPE = 64
MLA_ROPE = 32
MLA_V = 64
MLA_W = MLA_HEADS * MLA_V
HEAD_PAD = 128

XA_HEADS = 4
XA_DIM = 128
XA_W = XA_HEADS * XA_DIM

N_EXPERTS = 16
N_GROUPS = 4
EXPERTS_PER_GROUP = 4
D_EXPERT = 256

ALPHA = (2 * DEPTH) ** 0.25
LN_EPS = 1e-5
IN_PAD = 2048
NEG_BIG = -1e30
MOE_TILE = 128
RANK_BITS = 16
RANK_RADIX = 1 << RANK_BITS

VMEM_LIMIT = 56 * 1024 * 1024


def _cparams(*sem):
    return pltpu.CompilerParams(dimension_semantics=sem, vmem_limit_bytes=VMEM_LIMIT)


def _dot(a, b):
    return jnp.dot(a, b, preferred_element_type=F32)


def _dot_nt(a, b):
    return lax.dot_general(a, b, (((1,), (1,)), ((), ())), preferred_element_type=F32)


def _dot_tn(a, b):
    return lax.dot_general(a, b, (((0,), (0,)), ((), ())), preferred_element_type=F32)


def _layer_norm(z, g, b):
    mu = jnp.mean(z, axis=-1, keepdims=True)
    zc = z - mu
    var = jnp.mean(zc * zc, axis=-1, keepdims=True)
    return zc * lax.rsqrt(var + LN_EPS) * g + b


def _rms_norm(z, g):
    ms = jnp.mean(z * z, axis=-1, keepdims=True)
    return z * lax.rsqrt(ms + LN_EPS) * g


def _silu(z):
    return z / (1.0 + jnp.exp(-z))


def _rope(x, cos, sin_signed, half):
    w = x.shape[-1]
    lane = lax.broadcasted_iota(jnp.int32, x.shape, 1)
    rot = jnp.where((lane & half) == 0, pltpu.roll(x, w - half, 1), pltpu.roll(x, half, 1))
    return x * cos + rot * sin_signed


def _split_bf16(x):
    hi = x.astype(BF16)
    lo = (x - hi.astype(F32)).astype(BF16)
    return hi, lo


def _tables_kernel(pos_ref, invf_ref, cr_ref, sr_ref, cm_ref, sm_ref):
    ang = pos_ref[...].astype(F32) * invf_ref[...]
    c = jnp.cos(ang)
    s = jnp.sin(ang)
    lane = lax.broadcasted_iota(jnp.int32, c.shape, 1)

    def tile32(v):
        v0 = jnp.where(lane < 32, v, 0.0)
        v1 = v0 + pltpu.roll(v0, 32, 1)
        return v1 + pltpu.roll(v1, 64, 1)

    ct = tile32(c)
    st = tile32(s) * jnp.where((lane & 32) == 0, -1.0, 1.0)
    cr_ref[...] = jnp.concatenate([ct, ct], axis=1)
    sr_ref[...] = jnp.concatenate([st, st], axis=1)
    in_src = (lane >= 32) & (lane < 48)
    cmv = jnp.where(in_src, c, 0.0)
    smv = jnp.where(in_src, s, 0.0)
    in_dst = (lane >= 64) & (lane < 96)
    cm_ref[...] = jnp.where(in_dst, pltpu.roll(cmv, 32, 1) + pltpu.roll(cmv, 48, 1), 1.0)
    sm_ref[...] = jnp.where(in_dst, pltpu.roll(smv, 48, 1) - pltpu.roll(smv, 32, 1), 0.0)


def _rope_tables(positions, tm):
    t = positions.size
    pos = positions.reshape(t, 1)
    fr = ROPE_THETA ** (-jnp.arange(0, RET_DK, 2, dtype=F32) / RET_DK)
    fm = ROPE_THETA ** (-jnp.arange(0, MLA_ROPE, 2, dtype=F32) / MLA_ROPE)
    invf = jnp.concatenate([fr, fm, jnp.zeros((128 - 48,), F32)]).reshape(1, 128)
    return pl.pallas_call(
        _tables_kernel,
        grid=(t // tm,),
        in_specs=[pl.BlockSpec((tm, 1), lambda i: (i, 0)),
                  pl.BlockSpec((1, 128), lambda i: (0, 0))],
        out_specs=[pl.BlockSpec((tm, 256), lambda i: (i, 0)),
                   pl.BlockSpec((tm, 256), lambda i: (i, 0)),
                   pl.BlockSpec((tm, 128), lambda i: (i, 0)),
                   pl.BlockSpec((tm, 128), lambda i: (i, 0))],
        out_shape=[jax.ShapeDtypeStruct((t, 256), F32), jax.ShapeDtypeStruct((t, 256), F32),
                   jax.ShapeDtypeStruct((t, 128), F32), jax.ShapeDtypeStruct((t, 128), F32)],
        compiler_params=_cparams("parallel"),
        name="rope_tables",
    )(pos, invf)


def _proj_kernel(x_ref, win_ref, wuq_ref, wukv_ref, lng_ref, lnb_ref, qg_ref, kvg_ref,
                 cr_ref, sr_ref, cm_ref, sm_ref, rs_ref, q_ref, k_ref, v_ref, *, q_scale):
    h = _dot(x_ref[...].astype(BF16), win_ref[...])
    cr = cr_ref[...]
    sr = sr_ref[...]
    rs_ref[:, 0:256] = _rope(h[:, 0:256], cr, sr, 32).astype(BF16)
    rs_ref[:, 256:512] = (_rope(h[:, 256:512], cr, sr, 32) * (RET_DK ** -0.5)).astype(BF16)
    rs_ref[:, 512:768] = h[:, 512:768].astype(BF16)
    rs_ref[:, 768:1024] = _silu(h[:, 768:1024]).astype(BF16)
    rs_ref[:, 1024:1280] = h[:, 1024:1280].astype(BF16)
    rs_ref[:, 1280:1536] = _layer_norm(h[:, 1280:1536], lng_ref[...], lnb_ref[...]).astype(BF16)

    cm = cm_ref[...]
    sm = sm_ref[...]
    cm8 = jnp.concatenate([cm] * MLA_HEADS, axis=1)
    sm8 = jnp.concatenate([sm] * MLA_HEADS, axis=1)
    cq = _rms_norm(h[:, 1536:1792], qg_ref[...]).astype(BF16)
    q = _rope(_dot(cq, wuq_ref[...]), cm8, sm8, 16)
    q_ref[...] = (q * q_scale).astype(BF16)
    ckv = _rms_norm(h[:, 1792:1920], kvg_ref[...]).astype(BF16)
    kv = _dot(ckv, wukv_ref[...])
    kr = _rope(h[:, 1920:2048], cm, sm, 16)
    k_ref[...] = (kv[:, 0:1024] + jnp.concatenate([kr] * MLA_HEADS, axis=1)).astype(BF16)
    lane = lax.broadcasted_iota(jnp.int32, (kv.shape[0], MLA_HEADS * HEAD_PAD), 1)
    ones_lane = jnp.where((lane & (HEAD_PAD - 1)) == MLA_V, 1.0, 0.0)
    v_ref[...] = (kv[:, 1024:2048] + ones_lane).astype(BF16)


def _projections(x2d, l, w_in, w_uq, w_ukv, sgu_g, sgu_b, q_g, kv_g, cr, sr, cm, sm, tm):
    t = x2d.shape[0]
    row = lambda w: pl.BlockSpec((tm, w), lambda i: (i, 0))
    lay = lambda a: pl.BlockSpec((None,) + a.shape[1:], lambda i: (l,) + (0,) * (a.ndim - 1))
    q_scale = (MLA_NOPE + MLA_ROPE) ** -0.5 * math.log2(math.e)
    return pl.pallas_call(
        functools.partial(_proj_kernel, q_scale=q_scale),
        grid=(t // tm,),
        in_specs=[row(D_MODEL), lay(w_in), lay(w_uq), lay(w_ukv), lay(sgu_g), lay(sgu_b),
                  lay(q_g), lay(kv_g), row(256), row(256), row(128), row(128)],
        out_specs=[row(1536), row(1024), row(1024), row(1024)],
        out_shape=[jax.ShapeDtypeStruct((t, 1536), BF16)] + [jax.ShapeDtypeStruct((t, 1024), BF16)] * 3,
        compiler_params=_cparams("parallel"),
        name="projections",
    )(x2d, w_in, w_uq, w_ukv, sgu_g, sgu_b, q_g, kv_g, cr, sr, cm, sm)


def _retsgu_kernel(rs_ref, dmat_ref, qdec_ref, kdec_ref, cdec_ref, gavg_ref, sw_ref, sb_ref,
                   out_ref, state_ref, *, n_chunks):
    @pl.when(pl.program_id(1) == 0)
    def _():
        state_ref[...] = jnp.zeros_like(state_ref)

    lane = lax.broadcasted_iota(jnp.int32, (CHUNK, RET_W), 1)
    head_of_lane = lane // RET_DK
    hmask = [head_of_lane == h for h in range(RET_HEADS)]
    row_i = lax.broadcasted_iota(jnp.int32, (CHUNK, SGU_GROUPS * CHUNK), 0)
    col_i = lax.broadcasted_iota(jnp.int32, (CHUNK, SGU_GROUPS * CHUNK), 1)
    sw = jnp.where(row_i >= (col_i & (CHUNK - 1)), sw_ref[...], 0.0).astype(BF16)
    blk = (lax.broadcasted_iota(jnp.int32, (RET_W, RET_W), 0) // RET_DK
           == lax.broadcasted_iota(jnp.int32, (RET_W, RET_W), 1) // RET_DK)
    gavg = gavg_ref[...]
    zero = jnp.zeros((), BF16)

    def group_mean(y):
        hi, lo = _split_bf16(y)
        return _dot(jnp.concatenate([hi, lo], axis=1), gavg)

    def chunk(c, state):
        r0 = c * CHUNK
        rq = rs_ref[pl.ds(r0, CHUNK), 0:256]
        rk = rs_ref[pl.ds(r0, CHUNK), 256:512]
        rv = rs_ref[pl.ds(r0, CHUNK), 512:768]
        gate = rs_ref[pl.ds(r0, CHUNK), 768:1024].astype(F32)
        su = rs_ref[pl.ds(r0, CHUNK), 1024:1280].astype(F32)
        sv = rs_ref[pl.ds(r0, CHUNK), 1280:1536]

        q_heads = jnp.concatenate([jnp.where(m, rq, zero) for m in hmask], axis=0)
        scores = _dot_nt(q_heads, rk) * dmat_ref[...]
        p_cat = jnp.concatenate(
            [scores[h * CHUNK:(h + 1) * CHUNK, :] for h in range(RET_HEADS)], axis=1).astype(BF16)
        v_heads = jnp.concatenate([jnp.where(m, rv, zero) for m in hmask], axis=0)
        intra = _dot(p_cat, v_heads)
        cross = _dot((rq.astype(F32) * qdec_ref[...]).astype(BF16), state.astype(BF16))
        kv = _dot_tn((rk.astype(F32) * kdec_ref[...]).astype(BF16), rv)
        state = state * cdec_ref[...] + jnp.where(blk, kv, 0.0)

        y = intra + cross
        yc = y - group_mean(y)
        var = group_mean(yc * yc)
        ret = gate * (yc * lax.rsqrt(var + LN_EPS))

        sv_groups = jnp.concatenate([jnp.where(m, sv, zero) for m in hmask], axis=0)
        mixed = _dot(sw, sv_groups) + sb_ref[...]
        out_ref[pl.ds(r0, CHUNK), 0:256] = ret.astype(BF16)
        out_ref[pl.ds(r0, CHUNK), 256:512] = (su * mixed).astype(BF16)
        return state

    state = state_ref[...]
    for c in range(n_chunks):
        state = chunk(c, state)
    state_ref[...] = state


def _retention_sgu(rs, l, consts, sgu_w_cat, sgu_b_lane, batch, seq, tb):
    dmat, qdec, kdec, cdec, gavg = consts
    n_blocks = seq // tb
    full = lambda a: pl.BlockSpec(a.shape, lambda b, i: (0,) * a.ndim)
    lay = lambda a: pl.BlockSpec((None,) + a.shape[1:], lambda b, i: (l,) + (0,) * (a.ndim - 1))
    return pl.pallas_call(
        functools.partial(_retsgu_kernel, n_chunks=tb // CHUNK),
        grid=(batch, n_blocks),
        in_specs=[pl.BlockSpec((tb, 1536), lambda b, i: (b * n_blocks + i, 0)),
                  full(dmat), full(qdec), full(kdec), full(cdec), full(gavg),
                  lay(sgu_w_cat), lay(sgu_b_lane)],
        out_specs=pl.BlockSpec((tb, 512), lambda b, i: (b * n_blocks + i, 0)),
        out_shape=jax.ShapeDtypeStruct((batch * seq, 512), BF16),
        scratch_shapes=[pltpu.VMEM((RET_W, RET_W), F32)],
        compiler_params=_cparams("parallel", "arbitrary"),
        name="retention_sgu",
    )(rs, dmat, qdec, kdec, cdec, gavg, sgu_w_cat, sgu_b_lane)


def _retention_consts():
    h = jnp.arange(RET_HEADS, dtype=F32)
    log_gamma = jnp.log1p(-(2.0 ** (-5.0 - h)))
    pos = jnp.arange(CHUNK, dtype=F32)
    diff = pos[:, None] - pos[None, :]
    intra = jnp.where(diff >= 0, jnp.exp(log_gamma[:, None, None] * jnp.maximum(diff, 0.0)), 0.0)
    dmat = intra.reshape(RET_HEADS * CHUNK, CHUNK)
    inner = jnp.exp(log_gamma[None, :] * (CHUNK - 1 - pos)[:, None])
    query = jnp.exp(log_gamma[None, :] * (pos + 1)[:, None])
    kdec = jnp.repeat(inner, RET_DK, axis=1)
    qdec = jnp.repeat(query, RET_DK, axis=1)
    chunk_decay = jnp.repeat(jnp.exp(log_gamma * CHUNK), RET_DK)
    blk = jnp.arange(RET_W)[:, None] // RET_DK == jnp.arange(RET_W)[None, :] // RET_DK
    cdec = jnp.where(blk, chunk_decay[:, None], 0.0)
    gavg = jnp.where(blk, 1.0 / RET_DK, 0.0).astype(BF16)
    return dmat, qdec, kdec, cdec, jnp.concatenate([gavg, gavg], axis=0)


def _flash_kernel(q_ref, k_ref, v_ref, o_ref, sa_ref, sb_ref, m_ref, acc_ref, *, tq, hps):
    i = pl.program_id(2)
    tk = tq // 2
    heads = range(hps)
    cols = [slice(hh * HEAD_PAD, (hh + 1) * HEAD_PAD) for hh in heads]

    def scores(j, hh, rows=slice(None)):
        r0 = pl.multiple_of(j * tk, tk)
        return _dot_nt(q_ref[rows, cols[hh]], k_ref[pl.ds(r0, tk), cols[hh]])

    def update(j, hh, s, rows=slice(None)):
        r0 = pl.multiple_of(j * tk, tk)
        m = m_ref[hh, rows, :]
        s_max = s[:, 0:HEAD_PAD]
        for c in range(1, tk // HEAD_PAD):
            s_max = jnp.maximum(s_max, s[:, c * HEAD_PAD:(c + 1) * HEAD_PAD])
        m_new = jnp.maximum(m, jnp.max(s_max, axis=-1, keepdims=True))
        p = jnp.exp2(s - jnp.concatenate([m_new] * (tk // HEAD_PAD), axis=1)).astype(BF16)
        acc_ref[hh, rows, :] = (jnp.exp2(m - m_new) * acc_ref[hh, rows, :]
                                + _dot(p, v_ref[pl.ds(r0, tk), cols[hh]]))
        m_ref[hh, rows, :] = m_new

    m_ref[...] = jnp.full(m_ref.shape, NEG_BIG, F32)
    acc_ref[...] = jnp.zeros(acc_ref.shape, F32)
    for hh in heads:
        sa_ref[hh] = scores(0, hh)

    def pair(t, carry):
        for hh in heads:
            sb_ref[hh] = scores(2 * t + 1, hh)
            update(2 * t, hh, sa_ref[hh])
        for hh in heads:
            sa_ref[hh] = scores(2 * t + 2, hh)
            update(2 * t + 1, hh, sb_ref[hh])
        return carry

    lax.fori_loop(0, i, pair, 0)
    low = slice(tk, tq)
    visible = (lax.broadcasted_iota(jnp.int32, (tq, tk), 1) <= lax.broadcasted_iota(jnp.int32, (tq, tk), 0))
    visible_low = (lax.broadcasted_iota(jnp.int32, (tk, tk), 1) <= lax.broadcasted_iota(jnp.int32, (tk, tk), 0))
    for hh in heads:
        sb_ref[hh, low, :] = scores(2 * i + 1, hh, low)
        update(2 * i, hh, jnp.where(visible, sa_ref[hh], NEG_BIG))
    for hh in heads:
        update(2 * i + 1, hh, jnp.where(visible_low, sb_ref[hh, low, :], NEG_BIG), low)
    lane = lax.broadcasted_iota(jnp.int32, (tq, HEAD_PAD), 1)
    for pr in range(hps // 2):
        o = []
        for hh in (2 * pr, 2 * pr + 1):
            acc = acc_ref[hh]
            o.append(acc / acc[:, MLA_V:MLA_V + 1])
        o_ref[:, pr * HEAD_PAD:(pr + 1) * HEAD_PAD] = jnp.where(
            lane < MLA_V, o[0], pltpu.roll(o[1], MLA_V, 1)).astype(BF16)


def _flash_attention(q, k, v, batch, seq, tq, hps):
    nq = seq // tq
    return pl.pallas_call(
        functools.partial(_flash_kernel, tq=tq, hps=hps),
        grid=(batch, MLA_HEADS // hps, nq),
        in_specs=[pl.BlockSpec((tq, hps * HEAD_PAD), lambda b, p, i: (b * nq + i, p)),
                  pl.BlockSpec((seq, hps * HEAD_PAD), lambda b, p, i: (b, p)),
                  pl.BlockSpec((seq, hps * HEAD_PAD), lambda b, p, i: (b, p))],
        out_specs=pl.BlockSpec((tq, hps * MLA_V), lambda b, p, i: (b * nq + i, p)),
        out_shape=jax.ShapeDtypeStruct((batch * seq, MLA_W), BF16),
        scratch_shapes=[pltpu.VMEM((hps, tq, tq // 2), F32), pltpu.VMEM((hps, tq, tq // 2), F32),
                        pltpu.VMEM((hps, tq, HEAD_PAD), F32), pltpu.VMEM((hps, tq, HEAD_PAD), F32)],
        compiler_params=_cparams("parallel", "parallel", "arbitrary"),
        name="flash_attention",
    )(q, k, v)


def _memkv_kernel(mem_ref, wk_ref, wv_ref, k_ref, v_ref):
    m = mem_ref[...].astype(BF16)
    k_ref[...] = _dot(m, wk_ref[...]).astype(BF16)
    v_ref[...] = _dot(m, wv_ref[...]).astype(BF16)


def _memory_kv(mem2d, l, wk, wv, batch):
    lay = lambda a: pl.BlockSpec((None,) + a.shape[1:], lambda b: (l,) + (0,) * (a.ndim - 1))
    blk = pl.BlockSpec((MEM_LEN, XA_W), lambda b: (b, 0))
    return pl.pallas_call(
        _memkv_kernel,
        grid=(batch,),
        in_specs=[pl.BlockSpec((MEM_LEN, D_MODEL), lambda b: (b, 0)), lay(wk), lay(wv)],
        out_specs=[blk, blk],
        out_shape=[jax.ShapeDtypeStruct((batch * MEM_LEN, XA_W), BF16)] * 2,
        compiler_params=_cparams("parallel"),
        name="memory_kv",
    )(mem2d, wk, wv)


def _route_rows(scores, biased):
    s = [scores[e:e + 1, :] for e in range(N_EXPERTS)]
    b = [biased[e:e + 1, :] for e in range(N_EXPERTS)]
    group_scores = []
    for g in range(N_GROUPS):
        b0, b1, b2, b3 = b[4 * g:4 * g + 4]
        hi01, lo01 = jnp.maximum(b0, b1), jnp.minimum(b0, b1)
        hi23, lo23 = jnp.maximum(b2, b3), jnp.minimum(b2, b3)
        top1 = jnp.maximum(hi01, hi23)
        top2 = jnp.maximum(jnp.minimum(hi01, hi23), jnp.maximum(lo01, lo23))
        group_scores.append(top1 + top2)
    best = group_scores[0]
    sel = jnp.zeros_like(best, dtype=jnp.int32)
    for g in range(1, N_GROUPS):
        upd = group_scores[g] > best
        sel = jnp.where(upd, g, sel)
        best = jnp.where(upd, group_scores[g], best)

    def pick(rows, j):
        out = rows[j]
        for g in range(1, N_GROUPS):
            out = jnp.where(sel == g, rows[4 * g + j], out)
        return out

    ib = [pick(b, j) for j in range(EXPERTS_PER_GROUP)]
    isc = [pick(s, j) for j in range(EXPERTS_PER_GROUP)]

    def argmax4(vals):
        bv, bi = vals[0], jnp.zeros_like(sel)
        for j in range(1, EXPERTS_PER_GROUP):
            upd = vals[j] > bv
            bi = jnp.where(upd, j, bi)
            bv = jnp.where(upd, vals[j], bv)
        return bi

    i1 = argmax4(ib)
    i2 = argmax4([jnp.where(i1 == j, -jnp.inf, ib[j]) for j in range(EXPERTS_PER_GROUP)])

    def take(vals, idx):
        out = vals[0]
        for j in range(1, EXPERTS_PER_GROUP):
            out = jnp.where(idx == j, vals[j], out)
        return out

    g1, g2 = take(isc, i1), take(isc, i2)
    den = g1 + g2
    g1, g2 = g1 / den, g2 / den
    return sel, i1, i2, g1, g2


def _mix_xa_kernel(x_ref, rs_ref, at_ref, wout_ref, g1_ref, b1_ref, wq_ref, km_ref, vm_ref, wo_ref,
                   g2_ref, b2_ref, rw_ref, rb_ref, tri_ref, x2_ref, pos_ref, gate_rows_ref,
                   cnt_ref, base_ref, pk_ref, *, tm, tiles_per_block):
    @pl.when(pl.program_id(1) % tiles_per_block == 0)
    def _():
        base_ref[...] = jnp.zeros_like(base_ref)

    w_hi, w_lo = _split_bf16(rw_ref[...])

    def front(rows):
        mix = _dot(jnp.concatenate([rs_ref[rows, :], at_ref[rows, :]], axis=1), wout_ref[...])
        x1 = _layer_norm(ALPHA * x_ref[rows, :] + mix, g1_ref[...], b1_ref[...])
        q = (_dot(x1.astype(BF16), wq_ref[...]) * (XA_DIM ** -0.5 * math.log2(math.e))).astype(BF16)
        heads = []
        for h in range(XA_HEADS):
            sl = slice(h * XA_DIM, (h + 1) * XA_DIM)
            s = _dot_nt(q[:, sl], km_ref[:, sl])
            p = jnp.exp2(s - jnp.max(s, axis=-1, keepdims=True))
            o = _dot(p.astype(BF16), vm_ref[:, sl])
            heads.append(o / jnp.sum(p, axis=-1, keepdims=True))
        xa = _dot(jnp.concatenate(heads, axis=1).astype(BF16), wo_ref[...])
        return ALPHA * x1 + xa

    def back(rows, z):
        x2 = _layer_norm(z, g2_ref[...], b2_ref[...])
        x2_ref[rows, :] = x2
        x_hi, x_lo = _split_bf16(x2)
        return _dot_nt(w_hi, x_hi) + (_dot_nt(w_hi, x_lo) + _dot_nt(w_lo, x_hi))

    logits = back(slice(None), front(slice(None)))
    scores = 1.0 / (1.0 + jnp.exp(-logits))
    sel, i1, i2, gate1, gate2 = _route_rows(scores, scores + rb_ref[...])
    in_group = [jnp.where(i1 == j, gate1, 0.0) + jnp.where(i2 == j, gate2, 0.0)
                for j in range(EXPERTS_PER_GROUP)]
    gate_rows_ref[...] = jnp.concatenate(in_group + [jnp.zeros((128 - EXPERTS_PER_GROUP, tm), F32)], axis=0).T
    hit = lax.broadcasted_iota(jnp.int32, (N_EXPERTS, tm), 0) == sel
    cnt = jnp.where(hit, 1.0, 0.0)
    base = base_ref[...]
    before = _dot(cnt.astype(BF16), tri_ref[...]) - cnt + base[:, 0:1]
    rank = jnp.sum(jnp.where(hit, before, 0.0), axis=0, keepdims=True).astype(jnp.int32)
    base = base + jnp.sum(cnt, axis=1, keepdims=True)
    base_ref[...] = base
    cnt_ref[...] = base
    ti = pl.program_id(1) % tiles_per_block
    pk_ref[ti] = sel * RANK_RADIX + rank

    @pl.when(ti == tiles_per_block - 1)
    def _():
        starts = []
        start = jnp.zeros((1, 128), F32)
        for g in range(N_GROUPS):
            starts.append(start.astype(jnp.int32)[:, 0:1])
            tiles = jnp.floor((base[g:g + 1, :] + (MOE_TILE - 1)) * (1.0 / MOE_TILE))
            start = start + tiles * MOE_TILE
        for tj in range(tiles_per_block):
            pk = pk_ref[tj]
            g_of = pk >> RANK_BITS
            slot = pk & (RANK_RADIX - 1)
            for g in range(N_GROUPS):
                slot = slot + jnp.where(g_of == g, starts[g], 0)
            pos_ref[:, tj * tm:(tj + 1) * tm] = slot


def _mix_xa(x2d, rs, at, km, vm, l, w_out, g1, b1, wq, wo, g2, b2, rw_t, rb_col, tri, batch, seq, tm, nblk):
    t = x2d.shape[0]
    nb = seq // tm
    n_tiles = t // tm
    tpb = nblk // tm
    row = lambda w: pl.BlockSpec((tm, w), lambda b, i: (b * nb + i, 0))
    lay = lambda a: pl.BlockSpec((None,) + a.shape[1:], lambda b, i: (l,) + (0,) * (a.ndim - 1))
    full = lambda a: pl.BlockSpec(a.shape, lambda b, i: (0,) * a.ndim)
    memb = pl.BlockSpec((MEM_LEN, XA_W), lambda b, i: (b, 0))
    blk = pl.BlockSpec((None, 1, nblk), lambda b, i: ((b * nb + i) // tpb, 0, 0))
    blk_i = jax.ShapeDtypeStruct((t // nblk, 1, nblk), jnp.int32)
    return pl.pallas_call(
        functools.partial(_mix_xa_kernel, tm=tm, tiles_per_block=tpb),
        grid=(batch, nb),
        in_specs=[row(D_MODEL), row(512), row(512), lay(w_out), lay(g1), lay(b1), lay(wq), memb, memb,
                  lay(wo), lay(g2), lay(b2), full(rw_t), full(rb_col), full(tri)],
        out_specs=[row(D_MODEL), blk, row(128),
                   pl.BlockSpec((None, N_EXPERTS, 128), lambda b, i: ((b * nb + i) // tpb, 0, 0))],
        out_shape=[jax.ShapeDtypeStruct((t, D_MODEL), F32), blk_i, jax.ShapeDtypeStruct((t, 128), F32),
                   jax.ShapeDtypeStruct((t // nblk, N_EXPERTS, 128), F32)],
        scratch_shapes=[pltpu.VMEM((N_EXPERTS, 128), F32), pltpu.VMEM((tpb, 1, tm), jnp.int32)],
        compiler_params=_cparams("parallel", "arbitrary"),
        name="mix_xattn_router",
    )(x2d, rs, at, w_out, g1, b1, wq, km, vm, wo, g2, b2, rw_t, rb_col, tri)


def _moe_kernel(cnt_ref, pos_ref, x_ref, gates_ref, wg_ref, wu_ref, wd_ref, g_ref, b_ref,
                o_ref, xg_ref, xs_ref, ys_ref, og_ref, off_ref, ntile_ref, wgc_ref, wuc_ref, *, ts, k, mp):
    s = pl.program_id(1)
    sg = ts + 8
    unroll = 8

    @pl.when((s == 0) & (pl.program_id(0) == 0))
    def _():
        xs_ref[...] = jnp.zeros_like(xs_ref)
        xg_ref[...] = jnp.zeros_like(xg_ref)

    @pl.when(s == 0)
    def _():
        start = jnp.int32(0)
        for g in range(N_GROUPS):
            tiles = (cnt_ref[0, g] + (MOE_TILE - 1)) // MOE_TILE
            off_ref[g] = start
            ntile_ref[g] = tiles
            start = start + tiles * MOE_TILE

    @pl.when(s < k)
    def _():
        x = x_ref[...]
        half = D_MODEL // 2
        hi = pltpu.bitcast(x[:, :half].astype(BF16).astype(F32), jnp.uint32)
        lo = pltpu.bitcast(x[:, half:].astype(BF16).astype(F32), jnp.uint32)
        words = hi | (lo >> 16)
        for c in range(4):
            xg_ref[c * sg:c * sg + ts, :] = words[:, c * 128:(c + 1) * 128]
        xg_ref[4 * sg:4 * sg + ts, :] = pltpu.bitcast(gates_ref[...], jnp.uint32)

        def dispatch(tt, carry):
            for u in range(unroll):
                tl = tt * unroll + u
                xs_ref[pl.ds(pos_ref[s * ts + tl], 8, stride=mp), :] = xg_ref[pl.ds(tl, 8, stride=sg), :]
            return carry

        lax.fori_loop(0, ts // unroll, dispatch, 0)

    @pl.when((s >= k) & (s < k + N_GROUPS))
    def _():
        g = s - k
        seg = off_ref[g]
        for e in range(EXPERTS_PER_GROUP):
            wgc_ref[:, e * D_EXPERT:(e + 1) * D_EXPERT] = wg_ref[e]
            wuc_ref[:, e * D_EXPERT:(e + 1) * D_EXPERT] = wu_ref[e]

        def row_tile(i):
            r0 = pl.multiple_of(seg + i * MOE_TILE, MOE_TILE)
            words = [xs_ref[pl.ds(c * mp + r0, MOE_TILE), :] for c in range(4)]
            gates = pltpu.bitcast(xs_ref[pl.ds(4 * mp + r0, MOE_TILE), :], F32)
            xb = jnp.concatenate(
                [pltpu.bitcast(w & jnp.uint32(0xFFFF0000), F32).astype(BF16) for w in words]
                + [pltpu.bitcast(w << 16, F32).astype(BF16) for w in words], axis=1)
            weight = jnp.concatenate([jnp.broadcast_to(gates[:, e:e + 1], (MOE_TILE, D_EXPERT))
                                      for e in range(EXPERTS_PER_GROUP)], axis=1)
            hid = _silu(_dot(xb, wgc_ref[...])) * _dot(xb, wuc_ref[...]) * weight
            y = _dot(hid.astype(BF16), wd_ref[...])
            for j in range(8):
                ys_ref[pl.ds(j * mp + r0, MOE_TILE), :] = y[:, j * 128:(j + 1) * 128]

        def tile_pair(i, carry):
            row_tile(2 * i)
            row_tile(2 * i + 1)
            return carry

        def tile_last(i, carry):
            row_tile(n_tiles - 1)
            return carry

        n_tiles = ntile_ref[g]
        lax.fori_loop(0, n_tiles // 2, tile_pair, 0)
        lax.fori_loop(0, n_tiles % 2, tile_last, 0)

    @pl.when(s >= k + N_GROUPS)
    def _():
        sub = s - (k + N_GROUPS)

        def combine(tt, carry):
            for u in range(unroll):
                tl = tt * unroll + u
                og_ref[pl.ds(tl, 8, stride=sg), :] = ys_ref[pl.ds(pos_ref[sub * ts + tl], 8, stride=mp), :]
            return carry

        lax.fori_loop(0, ts // unroll, combine, 0)
        ffn = jnp.concatenate([og_ref[j * sg:j * sg + ts, :] for j in range(8)], axis=1)
        o_ref[...] = _layer_norm(ALPHA * x_ref[...] + ffn, g_ref[...], b_ref[...])


def _moe(x2, cnt_blk, pos, gate_rows, l, wg, wu, wd, g, b, nblk, ts):
    t = x2.shape[0]
    k = nblk // ts
    steps = 2 * k + N_GROUPS
    mp = nblk + N_GROUPS * MOE_TILE + 8
    lay = lambda a: pl.BlockSpec((None,) + a.shape[1:], lambda i, s: (l,) + (0,) * (a.ndim - 1))
    grp = lambda a: pl.BlockSpec((None, None) + a.shape[2:],
                                 lambda i, s: (l, jnp.clip(s - k, 0, N_GROUPS - 1), 0, 0))
    grp4 = lambda a: pl.BlockSpec((None, EXPERTS_PER_GROUP) + a.shape[2:],
                                  lambda i, s: (l, jnp.clip(s - k, 0, N_GROUPS - 1), 0, 0))
    x_map = lambda i, s: (i * k + jnp.where(s < k, s, jnp.where(s < k + N_GROUPS, k - 1, s - k - N_GROUPS)), 0)
    g_map = lambda i, s: (i * k + jnp.minimum(s, k - 1), 0)
    o_map = lambda i, s: (i * k + jnp.maximum(s - k - N_GROUPS, 0), 0)
    return pl.pallas_call(
        functools.partial(_moe_kernel, ts=ts, k=k, mp=mp),
        grid=(t // nblk, steps),
        in_specs=[pl.BlockSpec((None, 1, N_EXPERTS), lambda i, s: (i, 0, 0), memory_space=pltpu.SMEM),
                  pl.BlockSpec((nblk,), lambda i, s: (i,), memory_space=pltpu.SMEM),
                  pl.BlockSpec((ts, D_MODEL), x_map), pl.BlockSpec((ts, 128), g_map),
                  grp4(wg), grp4(wu), grp(wd), lay(g), lay(b)],
        out_specs=pl.BlockSpec((ts, D_MODEL), o_map),
        out_shape=jax.ShapeDtypeStruct((t, D_MODEL), F32),
        scratch_shapes=[pltpu.VMEM((8 * (ts + 8), 128), jnp.uint32), pltpu.VMEM((8 * mp, 128), jnp.uint32),
                        pltpu.VMEM((8 * mp, 128), F32), pltpu.VMEM((8 * (ts + 8), 128), F32),
                        pltpu.SMEM((N_GROUPS,), jnp.int32), pltpu.SMEM((N_GROUPS,), jnp.int32),
                        pltpu.VMEM((D_MODEL, EXPERTS_PER_GROUP * D_EXPERT), BF16),
                        pltpu.VMEM((D_MODEL, EXPERTS_PER_GROUP * D_EXPERT), BF16)],
        compiler_params=_cparams("arbitrary", "arbitrary"),
        name="moe_experts",
    )(cnt_blk, pos, x2, gate_rows, wg, wu, wd, g, b)


def _tile(n, pref):
    t = min(n, pref)
    assert n % t == 0, (n, t)
    return t


def kernel(x, mem, positions, w_in, w_out, sgu_ln_g, sgu_ln_b, sgu_w, sgu_b, mla_q_norm_g, mla_w_uq, mla_kv_norm_g, mla_w_ukv, xa_wq, xa_wk, xa_wv, xa_wo, ln_mix_g, ln_mix_b, ln_xa_g, ln_xa_b, ln_moe_g, ln_moe_b, router_w, router_bias, expert_w_gate, expert_w_up, expert_w_down):
    batch, seq, _ = x.shape
    depth = w_in.shape[0]
    t = batch * seq
    assert seq % CHUNK == 0

    w_in_b = w_in.astype(BF16)
    w_in_p = jnp.concatenate(
        [w_in_b[:, :, :1920], jnp.zeros((depth, D_MODEL, 64), BF16), w_in_b[:, :, 1920:1952],
         jnp.zeros((depth, D_MODEL, 32), BF16)], axis=2)
    w_uq_p = jnp.pad(mla_w_uq.astype(BF16).reshape(depth, MLA_Q_RANK, MLA_HEADS, MLA_NOPE + MLA_ROPE),
                     ((0, 0), (0, 0), (0, 0), (0, HEAD_PAD - MLA_NOPE - MLA_ROPE))
                     ).reshape(depth, MLA_Q_RANK, MLA_HEADS * HEAD_PAD)
    ukv = mla_w_ukv.astype(BF16).reshape(depth, MLA_KV_RANK, MLA_HEADS, MLA_NOPE + MLA_V)
    w_uk_p = jnp.pad(ukv[..., :MLA_NOPE], ((0, 0), (0, 0), (0, 0), (0, HEAD_PAD - MLA_NOPE))
                     ).reshape(depth, MLA_KV_RANK, MLA_HEADS * HEAD_PAD)
    w_uv_p = jnp.pad(ukv[..., MLA_NOPE:], ((0, 0), (0, 0), (0, 0), (0, HEAD_PAD - MLA_V))
                     ).reshape(depth, MLA_KV_RANK, MLA_HEADS * HEAD_PAD)
    w_ukv_p = jnp.concatenate([w_uk_p, w_uv_p], axis=2)
    w_out_b = w_out.astype(BF16)
    wq_b, wk_b, wv_b, wo_b = (a.astype(BF16) for a in (xa_wq, xa_wk, xa_wv, xa_wo))
    wg_b = expert_w_gate.astype(BF16)
    wu_b = expert_w_up.astype(BF16)
    wd_b = expert_w_down.reshape(depth, N_GROUPS, EXPERTS_PER_GROUP * D_EXPERT, D_MODEL).astype(BF16)
    vec = lambda a: a.reshape(depth, 1, a.shape[-1])
    sgu_w_cat = jnp.transpose(sgu_w, (0, 2, 1, 3)).reshape(depth, CHUNK, SGU_GROUPS * CHUNK)
    sgu_b_lane = jnp.repeat(jnp.transpose(sgu_b, (0, 2, 1)), SGU_W // SGU_GROUPS, axis=2)
    rw_t = router_w.T
    rb_col = router_bias.reshape(N_EXPERTS, 1)
    consts = _retention_consts()

    x2d = x.reshape(t, D_MODEL)
    mem2d = mem.reshape(batch * MEM_LEN, D_MODEL)
    cr, sr, cm, sm = _rope_tables(positions, _tile(t, 512))

    tm_proj = _tile(t, 1024)
    tb = _tile(seq, 1024)
    tq = _tile(seq, 1024)
    tm_mix = _tile(seq, 1024)
    nblk = _tile(seq, 2048)
    ts_moe = _tile(nblk, 512)
    tri = (jnp.arange(tm_mix)[:, None] <= jnp.arange(tm_mix)[None, :]).astype(BF16)
    for l in range(depth):
        rs, q, k, v = _projections(x2d, l, w_in_p, w_uq_p, w_ukv_p, vec(sgu_ln_g), vec(sgu_ln_b),
                                   vec(mla_q_norm_g), vec(mla_kv_norm_g), cr, sr, cm, sm, tm_proj)
        retsgu = _retention_sgu(rs, l, consts, sgu_w_cat, sgu_b_lane, batch, seq, tb)
        attn = _flash_attention(q, k, v, batch, seq, tq, 4)
        km, vm = _memory_kv(mem2d, l, wk_b, wv_b, batch)
        x2, pos, gate_rows, cnts = _mix_xa(
            x2d, retsgu, attn, km, vm, l, w_out_b, vec(ln_mix_g), vec(ln_mix_b), wq_b, wo_b,
            vec(ln_xa_g), vec(ln_xa_b), rw_t, rb_col, tri, batch, seq, tm_mix, nblk)
        cnt_blk = cnts[:, :, 0].astype(jnp.int32).reshape(-1, 1, N_EXPERTS)
        x2d = _moe(x2, cnt_blk, pos.reshape(t), gate_rows, l,
                   wg_b, wu_b, wd_b, vec(ln_moe_g), vec(ln_moe_b), nblk, ts_moe)
    return x2d.reshape(batch, seq, D_MODEL)
```

```python
import functools
import math

import jax
import jax.numpy as jnp
from jax import lax
from jax.experimental import pallas as pl
from jax.experimental.pallas import tpu as pltpu

F32 = jnp.float32
BF16 = jnp.bfloat16

D_MODEL = 1024
DEPTH = 4
MEM_LEN = 256
ROPE_THETA = 10000.0

RET_HEADS = 4
RET_DK = 64
RET_W = 256
CHUNK = 128

SGU_GROUPS = 4
SGU_W = 256

MLA_HEADS = 8
MLA_Q_RANK = 256
MLA_KV_RANK = 128
MLA_NOPE = 64
MLA_ROPE = 32
MLA_V = 64
MLA_W = MLA_HEADS * MLA_V
HEAD_PAD = 128

XA_HEADS = 4
XA_DIM = 128
XA_W = XA_HEADS * XA_DIM

N_EXPERTS = 16
N_GROUPS = 4
EXPERTS_PER_GROUP = 4
D_EXPERT = 256

ALPHA = (2 * DEPTH) ** 0.25
LN_EPS = 1e-5
IN_PAD = 2048
NEG_BIG = -1e30
MOE_TILE = 128
RANK_BITS = 16
RANK_RADIX = 1 << RANK_BITS

VMEM_LIMIT = 56 * 1024 * 1024


def _cparams(*sem):
    return pltpu.CompilerParams(dimension_semantics=sem, vmem_limit_bytes=VMEM_LIMIT)


def _dot(a, b):
    return jnp.dot(a, b, preferred_element_type=F32)


def _dot_nt(a, b):
    return lax.dot_general(a, b, (((1,), (1,)), ((), ())), preferred_element_type=F32)


def _dot_tn(a, b):
    return lax.dot_general(a, b, (((0,), (0,)), ((), ())), preferred_element_type=F32)


def _layer_norm(z, g, b):
    mu = jnp.mean(z, axis=-1, keepdims=True)
    zc = z - mu
    var = jnp.mean(zc * zc, axis=-1, keepdims=True)
    return zc * lax.rsqrt(var + LN_EPS) * g + b


def _rms_norm(z, g):
    ms = jnp.mean(z * z, axis=-1, keepdims=True)
    return z * lax.rsqrt(ms + LN_EPS) * g


def _silu(z):
    return z / (1.0 + jnp.exp(-z))


def _rope(x, cos, sin_signed, half):
    w = x.shape[-1]
    lane = lax.broadcasted_iota(jnp.int32, x.shape, 1)
    rot = jnp.where((lane & half) == 0, pltpu.roll(x, w - half, 1), pltpu.roll(x, half, 1))
    return x * cos + rot * sin_signed


def _split_bf16(x):
    hi = x.astype(BF16)
    lo = (x - hi.astype(F32)).astype(BF16)
    return hi, lo


def _tables_kernel(pos_a_ref, pos_b_ref, invf_ref, cr_ref, sr_ref, cm_ref, sm_ref):
    lane = lax.broadcasted_iota(jnp.int32, (pos_a_ref.shape[0], 128), 1)
    pos = jnp.where(lane < 64, pos_a_ref[...].astype(F32), pos_b_ref[...].astype(F32))
    ang = pos * invf_ref[...]
    c_both = jnp.cos(ang)
    s_both = jnp.sin(ang)

    def tile32(v):
        v0 = jnp.where(lane < 32, v, 0.0)
        v1 = v0 + pltpu.roll(v0, 32, 1)
        return v1 + pltpu.roll(v1, 64, 1)

    for half in range(2):
        c = c_both if half == 0 else pltpu.roll(c_both, 64, 1)
        s = s_both if half == 0 else pltpu.roll(s_both, 64, 1)
        ct = tile32(c)
        st = tile32(s) * jnp.where((lane & 32) == 0, -1.0, 1.0)
        cr_ref[half] = jnp.concatenate([ct, ct], axis=1)
        sr_ref[half] = jnp.concatenate([st, st], axis=1)
        in_src = (lane >= 32) & (lane < 48)
        cmv = jnp.where(in_src, c, 0.0)
        smv = jnp.where(in_src, s, 0.0)
        in_dst = (lane >= 64) & (lane < 96)
        cm_ref[half] = jnp.where(in_dst, pltpu.roll(cmv, 32, 1) + pltpu.roll(cmv, 48, 1), 1.0)
        sm_ref[half] = jnp.where(in_dst, pltpu.roll(smv, 48, 1) - pltpu.roll(smv, 32, 1), 0.0)


def _rope_tables(positions, tm):
    t = positions.size
    half_t = t // 2
    n = half_t // tm
    pos = positions.reshape(t, 1)
    fr = ROPE_THETA ** (-jnp.arange(0, RET_DK, 2, dtype=F32) / RET_DK)
    fm = ROPE_THETA ** (-jnp.arange(0, MLA_ROPE, 2, dtype=F32) / MLA_ROPE)
    one = jnp.concatenate([fr, fm, jnp.zeros((64 - 48,), F32)])
    invf = jnp.concatenate([one, one]).reshape(1, 128)
    out = lambda w: pl.BlockSpec((2, tm, w), lambda i: (0, i, 0))
    tables = pl.pallas_call(
        _tables_kernel,
        grid=(n,),
        in_specs=[pl.BlockSpec((tm, 1), lambda i: (i, 0)),
                  pl.BlockSpec((tm, 1), lambda i: (i + n, 0)),
                  pl.BlockSpec((1, 128), lambda i: (0, 0))],
        out_specs=[out(256), out(256), out(128), out(128)],
        out_shape=[jax.ShapeDtypeStruct((2, half_t, 256), F32), jax.ShapeDtypeStruct((2, half_t, 256), F32),
                   jax.ShapeDtypeStruct((2, half_t, 128), F32), jax.ShapeDtypeStruct((2, half_t, 128), F32)],
        compiler_params=_cparams("parallel"),
        name="rope_tables",
    )(pos, pos, invf)
    return [a.reshape(t, a.shape[-1]) for a in tables]


def _proj_kernel(x_ref, win_ref, wuq_ref, wukv_ref, lng_ref, lnb_ref, qg_ref, kvg_ref,
                 cr_ref, sr_ref, cm_ref, sm_ref, rs_ref, q_ref, k_ref, v_ref, *, q_scale):
    h = _dot(x_ref[...].astype(BF16), win_ref[...])
    cr = cr_ref[...]
    sr = sr_ref[...]
    rs_ref[:, 0:256] = _rope(h[:, 0:256], cr, sr, 32).astype(BF16)
    rs_ref[:, 256:512] = (_rope(h[:, 256:512], cr, sr, 32) * (RET_DK ** -0.5)).astype(BF16)
    rs_ref[:, 512:768] = h[:, 512:768].astype(BF16)
    rs_ref[:, 768:1024] = _silu(h[:, 768:1024]).astype(BF16)
    rs_ref[:, 1024:1280] = h[:, 1024:1280].astype(BF16)
    rs_ref[:, 1280:1536] = _layer_norm(h[:, 1280:1536], lng_ref[...], lnb_ref[...]).astype(BF16)

    cm = cm_ref[...]
    sm = sm_ref[...]
    cm8 = jnp.concatenate([cm] * MLA_HEADS, axis=1)
    sm8 = jnp.concatenate([sm] * MLA_HEADS, axis=1)
    cq = _rms_norm(h[:, 1536:1792], qg_ref[...]).astype(BF16)
    q = _rope(_dot(cq, wuq_ref[...]), cm8, sm8, 16)
    q_ref[...] = (q * q_scale).astype(BF16)
    ckv = _rms_norm(h[:, 1792:1920], kvg_ref[...]).astype(BF16)
    kv = _dot(ckv, wukv_ref[...])
    kr = _rope(h[:, 1920:2048], cm, sm, 16)
    k_ref[...] = (kv[:, 0:1024] + jnp.concatenate([kr] * MLA_HEADS, axis=1)).astype(BF16)
    lane = lax.broadcasted_iota(jnp.int32, (kv.shape[0], MLA_HEADS * HEAD_PAD), 1)
    ones_lane = jnp.where((lane & (HEAD_PAD - 1)) == MLA_V, 1.0, 0.0)
    v_ref[...] = (kv[:, 1024:2048] + ones_lane).astype(BF16)


def _projections(x2d, l, w_in, w_uq, w_ukv, sgu_g, sgu_b, q_g, kv_g, cr, sr, cm, sm, tm):
    t = x2d.shape[0]
    row = lambda w: pl.BlockSpec((tm, w), lambda i: (i, 0))
    lay = lambda a: pl.BlockSpec((None,) + a.shape[1:], lambda i: (l,) + (0,) * (a.ndim - 1))
    q_scale = (MLA_NOPE + MLA_ROPE) ** -0.5 * math.log2(math.e)
    return pl.pallas_call(
        functools.partial(_proj_kernel, q_scale=q_scale),
        grid=(t // tm,),
        in_specs=[row(D_MODEL), lay(w_in), lay(w_uq), lay(w_ukv), lay(sgu_g), lay(sgu_b),
                  lay(q_g), lay(kv_g), row(256), row(256), row(128), row(128)],
        out_specs=[row(1536), row(1024), row(1024), row(1024)],
        out_shape=[jax.ShapeDtypeStruct((t, 1536), BF16)] + [jax.ShapeDtypeStruct((t, 1024), BF16)] * 3,
        compiler_params=_cparams("parallel"),
        name="projections",
    )(x2d, w_in, w_uq, w_ukv, sgu_g, sgu_b, q_g, kv_g, cr, sr, cm, sm)


def _retsgu_kernel(rs_ref, dmat_ref, qdec_ref, kdec_ref, cdec_ref, gavg_ref, sw_ref, sb_ref,
                   out_ref, state_ref, *, n_chunks):
    @pl.when(pl.program_id(1) == 0)
    def _():
        state_ref[...] = jnp.zeros_like(state_ref)

    lane = lax.broadcasted_iota(jnp.int32, (CHUNK, RET_W), 1)
    head_of_lane = lane // RET_DK
    hmask = [head_of_lane == h for h in range(RET_HEADS)]
    row_i = lax.broadcasted_iota(jnp.int32, (CHUNK, SGU_GROUPS * CHUNK), 0)
    col_i = lax.broadcasted_iota(jnp.int32, (CHUNK, SGU_GROUPS * CHUNK), 1)
    sw = jnp.where(row_i >= (col_i & (CHUNK - 1)), sw_ref[...], 0.0).astype(BF16)
    blk = (lax.broadcasted_iota(jnp.int32, (RET_W, RET_W), 0) // RET_DK
           == lax.broadcasted_iota(jnp.int32, (RET_W, RET_W), 1) // RET_DK)
    gavg = gavg_ref[...]
    zero = jnp.zeros((), BF16)

    def group_mean(y):
        hi, lo = _split_bf16(y)
        return _dot(jnp.concatenate([hi, lo], axis=1), gavg)

    def chunk(c, state):
        r0 = c * CHUNK
        rq = rs_ref[pl.ds(r0, CHUNK), 0:256]
        rk = rs_ref[pl.ds(r0, CHUNK), 256:512]
        rv = rs_ref[pl.ds(r0, CHUNK), 512:768]
        gate = rs_ref[pl.ds(r0, CHUNK), 768:1024].astype(F32)
        su = rs_ref[pl.ds(r0, CHUNK), 1024:1280].astype(F32)
        sv = rs_ref[pl.ds(r0, CHUNK), 1280:1536]

        q_heads = jnp.concatenate([jnp.where(m, rq, zero) for m in hmask], axis=0)
        scores = _dot_nt(q_heads, rk) * dmat_ref[...]
        p_cat = jnp.concatenate(
            [scores[h * CHUNK:(h + 1) * CHUNK, :] for h in range(RET_HEADS)], axis=1).astype(BF16)
        v_heads = jnp.concatenate([jnp.where(m, rv, zero) for m in hmask], axis=0)
        intra = _dot(p_cat, v_heads)
        cross = _dot((rq.astype(F32) * qdec_ref[...]).astype(BF16), state.astype(BF16))
        kv = _dot_tn((rk.astype(F32) * kdec_ref[...]).astype(BF16), rv)
        state = state * cdec_ref[...] + jnp.where(blk, kv, 0.0)

        y = intra + cross
        yc = y - group_mean(y)
        var = group_mean(yc * yc)
        ret = gate * (yc * lax.rsqrt(var + LN_EPS))

        sv_groups = jnp.concatenate([jnp.where(m, sv, zero) for m in hmask], axis=0)
        mixed = _dot(sw, sv_groups) + sb_ref[...]
        out_ref[pl.ds(r0, CHUNK), 0:256] = ret.astype(BF16)
        out_ref[pl.ds(r0, CHUNK), 256:512] = (su * mixed).astype(BF16)
        return state

    state = state_ref[...]
    for c in range(n_chunks):
        state = chunk(c, state)
    state_ref[...] = state


def _retention_sgu(rs, l, consts, sgu_w_cat, sgu_b_lane, batch, seq, tb):
    dmat, qdec, kdec, cdec, gavg = consts
    n_blocks = seq // tb
    full = lambda a: pl.BlockSpec(a.shape, lambda b, i: (0,) * a.ndim)
    lay = lambda a: pl.BlockSpec((None,) + a.shape[1:], lambda b, i: (l,) + (0,) * (a.ndim - 1))
    return pl.pallas_call(
        functools.partial(_retsgu_kernel, n_chunks=tb // CHUNK),
        grid=(batch, n_blocks),
        in_specs=[pl.BlockSpec((tb, 1536), lambda b, i: (b * n_blocks + i, 0)),
                  full(dmat), full(qdec), full(kdec), full(cdec), full(gavg),
                  lay(sgu_w_cat), lay(sgu_b_lane)],
        out_specs=pl.BlockSpec((tb, 512), lambda b, i: (b * n_blocks + i, 0)),
        out_shape=jax.ShapeDtypeStruct((batch * seq, 512), BF16),
        scratch_shapes=[pltpu.VMEM((RET_W, RET_W), F32)],
        compiler_params=_cparams("parallel", "arbitrary"),
        name="retention_sgu",
    )(rs, dmat, qdec, kdec, cdec, gavg, sgu_w_cat, sgu_b_lane)


def _retention_consts():
    h = jnp.arange(RET_HEADS, dtype=F32)
    log_gamma = jnp.log1p(-(2.0 ** (-5.0 - h)))
    pos = jnp.arange(CHUNK, dtype=F32)
    diff = pos[:, None] - pos[None, :]
    intra = jnp.where(diff >= 0, jnp.exp(log_gamma[:, None, None] * jnp.maximum(diff, 0.0)), 0.0)
    dmat = intra.reshape(RET_HEADS * CHUNK, CHUNK)
    inner = jnp.exp(log_gamma[None, :] * (CHUNK - 1 - pos)[:, None])
    query = jnp.exp(log_gamma[None, :] * (pos + 1)[:, None])
    kdec = jnp.repeat(inner, RET_DK, axis=1)
    qdec = jnp.repeat(query, RET_DK, axis=1)
    chunk_decay = jnp.repeat(jnp.exp(log_gamma * CHUNK), RET_DK)
    blk = jnp.arange(RET_W)[:, None] // RET_DK == jnp.arange(RET_W)[None, :] // RET_DK
    cdec = jnp.where(blk, chunk_decay[:, None], 0.0)
    gavg = jnp.where(blk, 1.0 / RET_DK, 0.0).astype(BF16)
    return dmat, qdec, kdec, cdec, jnp.concatenate([gavg, gavg], axis=0)


def _flash_kernel(q_ref, k_ref, v_ref, o_ref, sa_ref, sb_ref, m_ref, acc_ref, *, tq, hps):
    i = pl.program_id(2)
    tk = tq // 2
    heads = range(hps)
    cols = [slice(hh * HEAD_PAD, (hh + 1) * HEAD_PAD) for hh in heads]

    def scores(j, hh, rows=slice(None)):
        r0 = pl.multiple_of(j * tk, tk)
        return _dot_nt(q_ref[rows, cols[hh]], k_ref[pl.ds(r0, tk), cols[hh]])

    def update(j, hh, s, rows=slice(None)):
        r0 = pl.multiple_of(j * tk, tk)
        m = m_ref[hh, rows, :]
        s_max = s[:, 0:HEAD_PAD]
        for c in range(1, tk // HEAD_PAD):
            s_max = jnp.maximum(s_max, s[:, c * HEAD_PAD:(c + 1) * HEAD_PAD])
        m_new = jnp.maximum(m, jnp.max(s_max, axis=-1, keepdims=True))
        p = jnp.exp2(s - jnp.concatenate([m_new] * (tk // HEAD_PAD), axis=1)).astype(BF16)
        acc_ref[hh, rows, :] = (jnp.exp2(m - m_new) * acc_ref[hh, rows, :]
                                + _dot(p, v_ref[pl.ds(r0, tk), cols[hh]]))
        m_ref[hh, rows, :] = m_new

    m_ref[...] = jnp.full(m_ref.shape, NEG_BIG, F32)
    acc_ref[...] = jnp.zeros(acc_ref.shape, F32)
    for hh in heads:
        sa_ref[hh] = scores(0, hh)

    def pair(t, carry):
        for hh in heads:
            sb_ref[hh] = scores(2 * t + 1, hh)
            update(2 * t, hh, sa_ref[hh])
        for hh in heads:
            sa_ref[hh] = scores(2 * t + 2, hh)
            update(2 * t + 1, hh, sb_ref[hh])
        return carry

    lax.fori_loop(0, i, pair, 0)
    low = slice(tk, tq)
    visible = (lax.broadcasted_iota(jnp.int32, (tq, tk), 1) <= lax.broadcasted_iota(jnp.int32, (tq, tk), 0))
    visible_low = (lax.broadcasted_iota(jnp.int32, (tk, tk), 1) <= lax.broadcasted_iota(jnp.int32, (tk, tk), 0))
    for hh in heads:
        sb_ref[hh, low, :] = scores(2 * i + 1, hh, low)
        update(2 * i, hh, jnp.where(visible, sa_ref[hh], NEG_BIG))
    for hh in heads:
        update(2 * i + 1, hh, jnp.where(visible_low, sb_ref[hh, low, :], NEG_BIG), low)
    lane = lax.broadcasted_iota(jnp.int32, (tq, HEAD_PAD), 1)
    for pr in range(hps // 2):
        o = []
        for hh in (2 * pr, 2 * pr + 1):
            acc = acc_ref[hh]
            o.append(acc / acc[:, MLA_V:MLA_V + 1])
        o_ref[:, pr * HEAD_PAD:(pr + 1) * HEAD_PAD] = jnp.where(
            lane < MLA_V, o[0], pltpu.roll(o[1], MLA_V, 1)).astype(BF16)


def _flash_attention(q, k, v, batch, seq, tq, hps):
    nq = seq // tq
    return pl.pallas_call(
        functools.partial(_flash_kernel, tq=tq, hps=hps),
        grid=(batch, MLA_HEADS // hps, nq),
        in_specs=[pl.BlockSpec((tq, hps * HEAD_PAD), lambda b, p, i: (b * nq + i, p)),
                  pl.BlockSpec((seq, hps * HEAD_PAD), lambda b, p, i: (b, p)),
                  pl.BlockSpec((seq, hps * HEAD_PAD), lambda b, p, i: (b, p))],
        out_specs=pl.BlockSpec((tq, hps * MLA_V), lambda b, p, i: (b * nq + i, p)),
        out_shape=jax.ShapeDtypeStruct((batch * seq, MLA_W), BF16),
        scratch_shapes=[pltpu.VMEM((hps, tq, tq // 2), F32), pltpu.VMEM((hps, tq, tq // 2), F32),
                        pltpu.VMEM((hps, tq, HEAD_PAD), F32), pltpu.VMEM((hps, tq, HEAD_PAD), F32)],
        compiler_params=_cparams("parallel", "parallel", "arbitrary"),
        name="flash_attention",
    )(q, k, v)


def _memkv_kernel(mem_ref, wk_ref, wv_ref, k_ref, v_ref):
    m = mem_ref[...].astype(BF16)
    k_ref[...] = _dot(m, wk_ref[...]).astype(BF16)
    v_ref[...] = _dot(m, wv_ref[...]).astype(BF16)


def _memory_kv(mem2d, l, wk, wv, batch):
    lay = lambda a: pl.BlockSpec((None,) + a.shape[1:], lambda b: (l,) + (0,) * (a.ndim - 1))
    blk = pl.BlockSpec((MEM_LEN, XA_W), lambda b: (b, 0))
    return pl.pallas_call(
        _memkv_kernel,
        grid=(batch,),
        in_specs=[pl.BlockSpec((MEM_LEN, D_MODEL), lambda b: (b, 0)), lay(wk), lay(wv)],
        out_specs=[blk, blk],
        out_shape=[jax.ShapeDtypeStruct((batch * MEM_LEN, XA_W), BF16)] * 2,
        compiler_params=_cparams("parallel"),
        name="memory_kv",
    )(mem2d, wk, wv)


def _route_rows(scores, biased):
    s = [scores[e:e + 1, :] for e in range(N_EXPERTS)]
    b = [biased[e:e + 1, :] for e in range(N_EXPERTS)]
    group_scores = []
    for g in range(N_GROUPS):
        b0, b1, b2, b3 = b[4 * g:4 * g + 4]
        hi01, lo01 = jnp.maximum(b0, b1), jnp.minimum(b0, b1)
        hi23, lo23 = jnp.maximum(b2, b3), jnp.minimum(b2, b3)
        top1 = jnp.maximum(hi01, hi23)
        top2 = jnp.maximum(jnp.minimum(hi01, hi23), jnp.maximum(lo01, lo23))
        group_scores.append(top1 + top2)
    best = group_scores[0]
    sel = jnp.zeros_like(best, dtype=jnp.int32)
    for g in range(1, N_GROUPS):
        upd = group_scores[g] > best
        sel = jnp.where(upd, g, sel)
        best = jnp.where(upd, group_scores[g], best)

    def pick(rows, j):
        out = rows[j]
        for g in range(1, N_GROUPS):
            out = jnp.where(sel == g, rows[4 * g + j], out)
        return out

    ib = [pick(b, j) for j in range(EXPERTS_PER_GROUP)]
    isc = [pick(s, j) for j in range(EXPERTS_PER_GROUP)]

    def argmax4(vals):
        bv, bi = vals[0], jnp.zeros_like(sel)
        for j in range(1, EXPERTS_PER_GROUP):
            upd = vals[j] > bv
            bi = jnp.where(upd, j, bi)
            bv = jnp.where(upd, vals[j], bv)
        return bi

    i1 = argmax4(ib)
    i2 = argmax4([jnp.where(i1 == j, -jnp.inf, ib[j]) for j in range(EXPERTS_PER_GROUP)])

    def take(vals, idx):
        out = vals[0]
        for j in range(1, EXPERTS_PER_GROUP):
            out = jnp.where(idx == j, vals[j], out)
        return out

    g1, g2 = take(isc, i1), take(isc, i2)
    den = g1 + g2
    g1, g2 = g1 / den, g2 / den
    return sel, i1, i2, g1, g2


def _mix_xa_kernel(x_ref, rs_ref, at_ref, wout_ref, g1_ref, b1_ref, wq_ref, km_ref, vm_ref, wo_ref,
                   g2_ref, b2_ref, rw_ref, rb_ref, tri_ref, x2_ref, pos_ref, gate_rows_ref,
                   cnt_ref, base_ref, pk_ref, *, tm, tiles_per_block):
    @pl.when(pl.program_id(1) % tiles_per_block == 0)
    def _():
        base_ref[...] = jnp.zeros_like(base_ref)

    w_hi, w_lo = _split_bf16(rw_ref[...])

    def front(rows):
        mix = _dot(jnp.concatenate([rs_ref[rows, :], at_ref[rows, :]], axis=1), wout_ref[...])
        x1 = _layer_norm(ALPHA * x_ref[rows, :] + mix, g1_ref[...], b1_ref[...])
        q = (_dot(x1.astype(BF16), wq_ref[...]) * (XA_DIM ** -0.5 * math.log2(math.e))).astype(BF16)
        heads = []
        for h in range(XA_HEADS):
            sl = slice(h * XA_DIM, (h + 1) * XA_DIM)
            s = _dot_nt(q[:, sl], km_ref[:, sl])
            p = jnp.exp2(s - jnp.max(s, axis=-1, keepdims=True))
            o = _dot(p.astype(BF16), vm_ref[:, sl])
            heads.append(o / jnp.sum(p, axis=-1, keepdims=True))
        xa = _dot(jnp.concatenate(heads, axis=1).astype(BF16), wo_ref[...])
        return ALPHA * x1 + xa

    def back(rows, z):
        x2 = _layer_norm(z, g2_ref[...], b2_ref[...])
        x2_ref[rows, :] = x2
        x_hi, x_lo = _split_bf16(x2)
        return _dot_nt(w_hi, x_hi) + (_dot_nt(w_hi, x_lo) + _dot_nt(w_lo, x_hi))

    logits = back(slice(None), front(slice(None)))
    scores = 1.0 / (1.0 + jnp.exp(-logits))
    sel, i1, i2, gate1, gate2 = _route_rows(scores, scores + rb_ref[...])
    in_group = [jnp.where(i1 == j, gate1, 0.0) + jnp.where(i2 == j, gate2, 0.0)
                for j in range(EXPERTS_PER_GROUP)]
    gate_rows_ref[...] = jnp.concatenate(in_group + [jnp.zeros((128 - EXPERTS_PER_GROUP, tm), F32)], axis=0).T
    hit = lax.broadcasted_iota(jnp.int32, (N_EXPERTS, tm), 0) == sel
    cnt = jnp.where(hit, 1.0, 0.0)
    base = base_ref[...]
    before = _dot(cnt.astype(BF16), tri_ref[...]) - cnt + base[:, 0:1]
    rank = jnp.sum(jnp.where(hit, before, 0.0), axis=0, keepdims=True).astype(jnp.int32)
    base = base + jnp.sum(cnt, axis=1, keepdims=True)
    base_ref[...] = base
    cnt_ref[...] = base
    ti = pl.program_id(1) % tiles_per_block
    pk_ref[ti] = sel * RANK_RADIX + rank

    @pl.when(ti == tiles_per_block - 1)
    def _():
        starts = []
        start = jnp.zeros((1, 128), F32)
        for g in range(N_GROUPS):
            starts.append(start.astype(jnp.int32)[:, 0:1])
            tiles = jnp.floor((base[g:g + 1, :] + (MOE_TILE - 1)) * (1.0 / MOE_TILE))
            start = start + tiles * MOE_TILE
        for tj in range(tiles_per_block):
            pk = pk_ref[tj]
            g_of = pk >> RANK_BITS
            slot = pk & (RANK_RADIX - 1)
            for g in range(N_GROUPS):
                slot = slot + jnp.where(g_of == g, starts[g], 0)
            pos_ref[:, tj * tm:(tj + 1) * tm] = slot


def _mix_xa(x2d, rs, at, km, vm, l, w_out, g1, b1, wq, wo, g2, b2, rw_t, rb_col, tri, batch, seq, tm, nblk):
    t = x2d.shape[0]
    nb = seq // tm
    n_tiles = t // tm
    tpb = nblk // tm
    row = lambda w: pl.BlockSpec((tm, w), lambda b, i: (b * nb + i, 0))
    lay = lambda a: pl.BlockSpec((None,) + a.shape[1:], lambda b, i: (l,) + (0,) * (a.ndim - 1))
    full = lambda a: pl.BlockSpec(a.shape, lambda b, i: (0,) * a.ndim)
    memb = pl.BlockSpec((MEM_LEN, XA_W), lambda b, i: (b, 0))
    blk = pl.BlockSpec((None, 1, nblk), lambda b, i: ((b * nb + i) // tpb, 0, 0))
    blk_i = jax.ShapeDtypeStruct((t // nblk, 1, nblk), jnp.int32)
    return pl.pallas_call(
        functools.partial(_mix_xa_kernel, tm=tm, tiles_per_block=tpb),
        grid=(batch, nb),
        in_specs=[row(D_MODEL), row(512), row(512), lay(w_out), lay(g1), lay(b1), lay(wq), memb, memb,
                  lay(wo), lay(g2), lay(b2), full(rw_t), full(rb_col), full(tri)],
        out_specs=[row(D_MODEL), blk, row(128),
                   pl.BlockSpec((None, N_EXPERTS, 128), lambda b, i: ((b * nb + i) // tpb, 0, 0))],
        out_shape=[jax.ShapeDtypeStruct((t, D_MODEL), F32), blk_i, jax.ShapeDtypeStruct((t, 128), F32),
                   jax.ShapeDtypeStruct((t // nblk, N_EXPERTS, 128), F32)],
        scratch_shapes=[pltpu.VMEM((N_EXPERTS, 128), F32), pltpu.VMEM((tpb, 1, tm), jnp.int32)],
        compiler_params=_cparams("parallel", "arbitrary"),
        name="mix_xattn_router",
    )(x2d, rs, at, w_out, g1, b1, wq, km, vm, wo, g2, b2, rw_t, rb_col, tri)


def _group_weights_kernel(w_ref, o_ref):
    for e in range(EXPERTS_PER_GROUP):
        o_ref[:, e * D_EXPERT:(e + 1) * D_EXPERT] = w_ref[e].astype(BF16)


def _group_weights(w):
    depth = w.shape[0]
    return pl.pallas_call(
        _group_weights_kernel,
        grid=(depth, N_GROUPS),
        in_specs=[pl.BlockSpec((None, EXPERTS_PER_GROUP, D_MODEL, D_EXPERT), lambda l, g: (l, g, 0, 0))],
        out_specs=pl.BlockSpec((None, None, D_MODEL, EXPERTS_PER_GROUP * D_EXPERT), lambda l, g: (l, g, 0, 0)),
        out_shape=jax.ShapeDtypeStruct((depth, N_GROUPS, D_MODEL, EXPERTS_PER_GROUP * D_EXPERT), BF16),
        compiler_params=_cparams("parallel", "parallel"),
        name="group_weights",
    )(w)


def _moe_kernel(cnt_ref, pos_ref, x_ref, gates_ref, wg_ref, wu_ref, wd_ref, g_ref, b_ref,
                o_ref, xg_ref, xs_ref, ys_ref, og_ref, off_ref, ntile_ref, *, ts, k, mp):
    s = pl.program_id(1)
    sg = ts + 8
    unroll = 8

    @pl.when((s == 0) & (pl.program_id(0) == 0))
    def _():
        xs_ref[...] = jnp.zeros_like(xs_ref)
        xg_ref[...] = jnp.zeros_like(xg_ref)

    @pl.when(s == 0)
    def _():
        start = jnp.int32(0)
        for g in range(N_GROUPS):
            tiles = (cnt_ref[0, g] + (MOE_TILE - 1)) // MOE_TILE
            off_ref[g] = start
            ntile_ref[g] = tiles
            start = start + tiles * MOE_TILE

    @pl.when(s < k)
    def _():
        x = x_ref[...]
        half = D_MODEL // 2
        hi = pltpu.bitcast(x[:, :half].astype(BF16).astype(F32), jnp.uint32)
        lo = pltpu.bitcast(x[:, half:].astype(BF16).astype(F32), jnp.uint32)
        words = hi | (lo >> 16)
        for c in range(4):
            xg_ref[c * sg:c * sg + ts, :] = words[:, c * 128:(c + 1) * 128]
        xg_ref[4 * sg:4 * sg + ts, :] = pltpu.bitcast(gates_ref[...], jnp.uint32)

        def dispatch(tt, carry):
            for u in range(unroll):
                tl = tt * unroll + u
                xs_ref[pl.ds(pos_ref[s * ts + tl], 8, stride=mp), :] = xg_ref[pl.ds(tl, 8, stride=sg), :]
            return carry

        lax.fori_loop(0, ts // unroll, dispatch, 0)

    @pl.when((s >= k) & (s < k + N_GROUPS))
    def _():
        g = s - k
        seg = off_ref[g]

        def row_tile(i):
            r0 = pl.multiple_of(seg + i * MOE_TILE, MOE_TILE)
            words = [xs_ref[pl.ds(c * mp + r0, MOE_TILE), :] for c in range(4)]
            gates = pltpu.bitcast(xs_ref[pl.ds(4 * mp + r0, MOE_TILE), :], F32)
            xb = jnp.concatenate(
                [pltpu.bitcast(w & jnp.uint32(0xFFFF0000), F32).astype(BF16) for w in words]
                + [pltpu.bitcast(w << 16, F32).astype(BF16) for w in words], axis=1)
            weight = jnp.concatenate([jnp.broadcast_to(gates[:, e:e + 1], (MOE_TILE, D_EXPERT))
                                      for e in range(EXPERTS_PER_GROUP)], axis=1)
            hid = _silu(_dot(xb, wg_ref[...])) * _dot(xb, wu_ref[...]) * weight
            y = _dot(hid.astype(BF16), wd_ref[...])
            for j in range(8):
                ys_ref[pl.ds(j * mp + r0, MOE_TILE), :] = y[:, j * 128:(j + 1) * 128]

        def tile_pair(i, carry):
            row_tile(2 * i)
            row_tile(2 * i + 1)
            return carry

        def tile_last(i, carry):
            row_tile(n_tiles - 1)
            return carry

        n_tiles = ntile_ref[g]
        lax.fori_loop(0, n_tiles // 2, tile_pair, 0)
        lax.fori_loop(0, n_tiles % 2, tile_last, 0)

    @pl.when(s >= k + N_GROUPS)
    def _():
        sub = s - (k + N_GROUPS)

        def combine(tt, carry):
            for u in range(unroll):
                tl = tt * unroll + u
                og_ref[pl.ds(tl, 8, stride=sg), :] = ys_ref[pl.ds(pos_ref[sub * ts + tl], 8, stride=mp), :]
            return carry

        lax.fori_loop(0, ts // unroll, combine, 0)
        ffn = jnp.concatenate([og_ref[j * sg:j * sg + ts, :] for j in range(8)], axis=1)
        o_ref[...] = _layer_norm(ALPHA * x_ref[...] + ffn, g_ref[...], b_ref[...])


def _moe(x2, cnt_blk, pos, gate_rows, l, wg, wu, wd, g, b, nblk, ts):
    t = x2.shape[0]
    k = nblk // ts
    steps = 2 * k + N_GROUPS
    mp = nblk + N_GROUPS * MOE_TILE + 8
    lay = lambda a: pl.BlockSpec((None,) + a.shape[1:], lambda i, s: (l,) + (0,) * (a.ndim - 1))
    grp = lambda a: pl.BlockSpec((None, None) + a.shape[2:],
                                 lambda i, s: (l, jnp.clip(s - k, 0, N_GROUPS - 1), 0, 0))
    x_map = lambda i, s: (i * k + jnp.where(s < k, s, jnp.where(s < k + N_GROUPS, k - 1, s - k - N_GROUPS)), 0)
    g_map = lambda i, s: (i * k + jnp.minimum(s, k - 1), 0)
    o_map = lambda i, s: (i * k + jnp.maximum(s - k - N_GROUPS, 0), 0)
    return pl.pallas_call(
        functools.partial(_moe_kernel, ts=ts, k=k, mp=mp),
        grid=(t // nblk, steps),
        in_specs=[pl.BlockSpec((None, 1, N_EXPERTS), lambda i, s: (i, 0, 0), memory_space=pltpu.SMEM),
                  pl.BlockSpec((nblk,), lambda i, s: (i,), memory_space=pltpu.SMEM),
                  pl.BlockSpec((ts, D_MODEL), x_map), pl.BlockSpec((ts, 128), g_map),
                  grp(wg), grp(wu), grp(wd), lay(g), lay(b)],
        out_specs=pl.BlockSpec((ts, D_MODEL), o_map),
        out_shape=jax.ShapeDtypeStruct((t, D_MODEL), F32),
        scratch_shapes=[pltpu.VMEM((8 * (ts + 8), 128), jnp.uint32), pltpu.VMEM((8 * mp, 128), jnp.uint32),
                        pltpu.VMEM((8 * mp, 128), F32), pltpu.VMEM((8 * (ts + 8), 128), F32),
                        pltpu.SMEM((N_GROUPS,), jnp.int32), pltpu.SMEM((N_GROUPS,), jnp.int32)],
        compiler_params=_cparams("arbitrary", "arbitrary"),
        name="moe_experts",
    )(cnt_blk, pos, x2, gate_rows, wg, wu, wd, g, b)


def _tile(n, pref):
    t = min(n, pref)
    assert n % t == 0, (n, t)
    return t


def kernel(x, mem, positions, w_in, w_out, sgu_ln_g, sgu_ln_b, sgu_w, sgu_b, mla_q_norm_g, mla_w_uq, mla_kv_norm_g, mla_w_ukv, xa_wq, xa_wk, xa_wv, xa_wo, ln_mix_g, ln_mix_b, ln_xa_g, ln_xa_b, ln_moe_g, ln_moe_b, router_w, router_bias, expert_w_gate, expert_w_up, expert_w_down):
    batch, seq, _ = x.shape
    depth = w_in.shape[0]
    t = batch * seq
    assert seq % CHUNK == 0

    w_in_b = w_in.astype(BF16)
    w_in_p = jnp.concatenate(
        [w_in_b[:, :, :1920], jnp.zeros((depth, D_MODEL, 64), BF16), w_in_b[:, :, 1920:1952],
         jnp.zeros((depth, D_MODEL, 32), BF16)], axis=2)
    w_uq_p = jnp.pad(mla_w_uq.astype(BF16).reshape(depth, MLA_Q_RANK, MLA_HEADS, MLA_NOPE + MLA_ROPE),
                     ((0, 0), (0, 0), (0, 0), (0, HEAD_PAD - MLA_NOPE - MLA_ROPE))
                     ).reshape(depth, MLA_Q_RANK, MLA_HEADS * HEAD_PAD)
    ukv = mla_w_ukv.astype(BF16).reshape(depth, MLA_KV_RANK, MLA_HEADS, MLA_NOPE + MLA_V)
    w_uk_p = jnp.pad(ukv[..., :MLA_NOPE], ((0, 0), (0, 0), (0, 0), (0, HEAD_PAD - MLA_NOPE))
                     ).reshape(depth, MLA_KV_RANK, MLA_HEADS * HEAD_PAD)
    w_uv_p = jnp.pad(ukv[..., MLA_NOPE:], ((0, 0), (0, 0), (0, 0), (0, HEAD_PAD - MLA_V))
                     ).reshape(depth, MLA_KV_RANK, MLA_HEADS * HEAD_PAD)
    w_ukv_p = jnp.concatenate([w_uk_p, w_uv_p], axis=2)
    w_out_b = w_out.astype(BF16)
    wq_b, wk_b, wv_b, wo_b = (a.astype(BF16) for a in (xa_wq, xa_wk, xa_wv, xa_wo))
    wg_b = _group_weights(expert_w_gate)
    wu_b = _group_weights(expert_w_up)
    wd_b = expert_w_down.reshape(depth, N_GROUPS, EXPERTS_PER_GROUP * D_EXPERT, D_MODEL).astype(BF16)
    vec = lambda a: a.reshape(depth, 1, a.shape[-1])
    sgu_w_cat = jnp.transpose(sgu_w, (0, 2, 1, 3)).reshape(depth, CHUNK, SGU_GROUPS * CHUNK)
    sgu_b_lane = jnp.repeat(jnp.transpose(sgu_b, (0, 2, 1)), SGU_W // SGU_GROUPS, axis=2)
    rw_t = router_w.T
    rb_col = router_bias.reshape(N_EXPERTS, 1)
    consts = _retention_consts()

    x2d = x.reshape(t, D_MODEL)
    mem2d = mem.reshape(batch * MEM_LEN, D_MODEL)
    cr, sr, cm, sm = _rope_tables(positions, _tile(t // 2, 512))

    tm_proj = _tile(t, 1024)
    tb = _tile(seq, 1024)
    tq = _tile(seq, 1024)
    tm_mix = _tile(seq, 1024)
    nblk = _tile(seq, 2048)
    ts_moe = _tile(nblk, 512)
    tri = (jnp.arange(tm_mix)[:, None] <= jnp.arange(tm_mix)[None, :]).astype(BF16)
    for l in range(depth):
        rs, q, k, v = _projections(x2d, l, w_in_p, w_uq_p, w_ukv_p, vec(sgu_ln_g), vec(sgu_ln_b),
                                   vec(mla_q_norm_g), vec(mla_kv_norm_g), cr, sr, cm, sm, tm_proj)
        retsgu = _retention_sgu(rs, l, consts, sgu_w_cat, sgu_b_lane, batch, seq, tb)
        attn = _flash_attention(q, k, v, batch, seq, tq, 4)
        km, vm = _memory_kv(mem2d, l, wk_b, wv_b, batch)
        x2, pos, gate_rows, cnts = _mix_xa(
            x2d, retsgu, attn, km, vm, l, w_out_b, vec(ln_mix_g), vec(ln_mix_b), wq_b, wo_b,
            vec(ln_xa_g), vec(ln_xa_b), rw_t, rb_col, tri, batch, seq, tm_mix, nblk)
        cnt_blk = cnts[:, :, 0].astype(jnp.int32).reshape(-1, 1, N_EXPERTS)
        x2d = _moe(x2, cnt_blk, pos.reshape(t), gate_rows, l,
                   wg_b, wu_b, wd_b, vec(ln_moe_g), vec(ln_moe_b), nblk, ts_moe)
    return x2d.reshape(batch, seq, D_MODEL)
```

```python
import functools
import math

import jax
import jax.numpy as jnp
from jax import lax
from jax.experimental import pallas as pl
from jax.experimental.pallas import tpu as pltpu

F32 = jnp.float32
BF16 = jnp.bfloat16

D_MODEL = 1024
DEPTH = 4
MEM_LEN = 256
ROPE_THETA = 10000.0

RET_HEADS = 4
RET_DK = 64
RET_W = 256
CHUNK = 128

SGU_GROUPS = 4
SGU_W = 256

MLA_HEADS = 8
MLA_Q_RANK = 256
MLA_KV_RANK = 128
MLA_NOPE = 64
MLA_ROPE = 32
MLA_V = 64
MLA_W = MLA_HEADS * MLA_V
HEAD_PAD = 128

XA_HEADS = 4
XA_DIM = 128
XA_W = XA_HEADS * XA_DIM

N_EXPERTS = 16
N_GROUPS = 4
EXPERTS_PER_GROUP = 4
D_EXPERT = 256

ALPHA = (2 * DEPTH) ** 0.25
LN_EPS = 1e-5
IN_PAD = 2048
NEG_BIG = -1e30
PROJ_CHUNK = 256
MOE_TILE = 128
RANK_BITS = 16
RANK_RADIX = 1 << RANK_BITS

VMEM_LIMIT = 56 * 1024 * 1024


def _cparams(*sem):
    return pltpu.CompilerParams(dimension_semantics=sem, vmem_limit_bytes=VMEM_LIMIT)


def _dot(a, b):
    return jnp.dot(a, b, preferred_element_type=F32)


def _dot_nt(a, b):
    return lax.dot_general(a, b, (((1,), (1,)), ((), ())), preferred_element_type=F32)


def _dot_tn(a, b):
    return lax.dot_general(a, b, (((0,), (0,)), ((), ())), preferred_element_type=F32)


def _layer_norm(z, g, b):
    mu = jnp.mean(z, axis=-1, keepdims=True)
    zc = z - mu
    var = jnp.mean(zc * zc, axis=-1, keepdims=True)
    return zc * lax.rsqrt(var + LN_EPS) * g + b


def _rms_norm(z, g):
    ms = jnp.mean(z * z, axis=-1, keepdims=True)
    return z * lax.rsqrt(ms + LN_EPS) * g


def _silu(z):
    return z / (1.0 + jnp.exp(-z))


def _rope(x, cos, sin_signed, half):
    w = x.shape[-1]
    lane = lax.broadcasted_iota(jnp.int32, x.shape, 1)
    rot = jnp.where((lane & half) == 0, pltpu.roll(x, w - half, 1), pltpu.roll(x, half, 1))
    return x * cos + rot * sin_signed


def _split_bf16(x):
    hi = x.astype(BF16)
    lo = (x - hi.astype(F32)).astype(BF16)
    return hi, lo


def _tables_kernel(pos_a_ref, pos_b_ref, invf_ref, cr_ref, sr_ref, cm_ref, sm_ref):
    lane = lax.broadcasted_iota(jnp.int32, (pos_a_ref.shape[0], 128), 1)
    pos = jnp.where(lane < 64, pos_a_ref[...].astype(F32), pos_b_ref[...].astype(F32))
    ang = pos * invf_ref[...]
    c_both = jnp.cos(ang)
    s_both = jnp.sin(ang)

    def tile32(v):
        v0 = jnp.where(lane < 32, v, 0.0)
        v1 = v0 + pltpu.roll(v0, 32, 1)
        return v1 + pltpu.roll(v1, 64, 1)

    for half in range(2):
        c = c_both if half == 0 else pltpu.roll(c_both, 64, 1)
        s = s_both if half == 0 else pltpu.roll(s_both, 64, 1)
        ct = tile32(c)
        st = tile32(s) * jnp.where((lane & 32) == 0, -1.0, 1.0)
        cr_ref[half] = jnp.concatenate([ct, ct], axis=1)
        sr_ref[half] = jnp.concatenate([st, st], axis=1)
        in_src = (lane >= 32) & (lane < 48)
        cmv = jnp.where(in_src, c, 0.0)
        smv = jnp.where(in_src, s, 0.0)
        in_dst = (lane >= 64) & (lane < 96)
        cm_ref[half] = jnp.where(in_dst, pltpu.roll(cmv, 32, 1) + pltpu.roll(cmv, 48, 1), 1.0)
        sm_ref[half] = jnp.where(in_dst, pltpu.roll(smv, 48, 1) - pltpu.roll(smv, 32, 1), 0.0)


def _rope_tables(positions, tm):
    t = positions.size
    half_t = t // 2
    n = half_t // tm
    pos = positions.reshape(t, 1)
    fr = ROPE_THETA ** (-jnp.arange(0, RET_DK, 2, dtype=F32) / RET_DK)
    fm = ROPE_THETA ** (-jnp.arange(0, MLA_ROPE, 2, dtype=F32) / MLA_ROPE)
    one = jnp.concatenate([fr, fm, jnp.zeros((64 - 48,), F32)])
    invf = jnp.concatenate([one, one]).reshape(1, 128)
    out = lambda w: pl.BlockSpec((2, tm, w), lambda i: (0, i, 0))
    tables = pl.pallas_call(
        _tables_kernel,
        grid=(n,),
        in_specs=[pl.BlockSpec((tm, 1), lambda i: (i, 0)),
                  pl.BlockSpec((tm, 1), lambda i: (i + n, 0)),
                  pl.BlockSpec((1, 128), lambda i: (0, 0))],
        out_specs=[out(256), out(256), out(128), out(128)],
        out_shape=[jax.ShapeDtypeStruct((2, half_t, 256), F32), jax.ShapeDtypeStruct((2, half_t, 256), F32),
                   jax.ShapeDtypeStruct((2, half_t, 128), F32), jax.ShapeDtypeStruct((2, half_t, 128), F32)],
        compiler_params=_cparams("parallel"),
        name="rope_tables",
    )(pos, pos, invf)
    return [a.reshape(t, a.shape[-1]) for a in tables]


def _proj_kernel(x_ref, win_ref, wuq_ref, wukv_ref, lng_ref, lnb_ref, qg_ref, kvg_ref,
                 cr_ref, sr_ref, cm_ref, sm_ref, rs_ref, q_ref, k_ref, v_ref, *, q_scale):
    tm = x_ref.shape[0]
    n_chunks = max(tm // PROJ_CHUNK, 1)
    chunk = tm // n_chunks
    lane = lax.broadcasted_iota(jnp.int32, (chunk, MLA_HEADS * HEAD_PAD), 1)
    ones_lane = jnp.where((lane & (HEAD_PAD - 1)) == MLA_V, 1.0, 0.0)

    def in_proj(c):
        rows = slice(c * chunk, (c + 1) * chunk)
        return _dot(x_ref[rows, :].astype(BF16), win_ref[...])

    def finish(c, h):
        rows = slice(c * chunk, (c + 1) * chunk)
        cm = cm_ref[rows, :]
        sm = sm_ref[rows, :]
        cq = _rms_norm(h[:, 1536:1792], qg_ref[...]).astype(BF16)
        q = _dot(cq, wuq_ref[...])
        ckv = _rms_norm(h[:, 1792:1920], kvg_ref[...]).astype(BF16)
        kv = _dot(ckv, wukv_ref[...])
        cr = cr_ref[rows, :]
        sr = sr_ref[rows, :]
        rs_ref[rows, 0:256] = _rope(h[:, 0:256], cr, sr, 32).astype(BF16)
        rs_ref[rows, 256:512] = (_rope(h[:, 256:512], cr, sr, 32) * (RET_DK ** -0.5)).astype(BF16)
        rs_ref[rows, 512:768] = h[:, 512:768].astype(BF16)
        rs_ref[rows, 768:1024] = _silu(h[:, 768:1024]).astype(BF16)
        rs_ref[rows, 1024:1280] = h[:, 1024:1280].astype(BF16)
        rs_ref[rows, 1280:1536] = _layer_norm(h[:, 1280:1536], lng_ref[...], lnb_ref[...]).astype(BF16)
        cm8 = jnp.concatenate([cm] * MLA_HEADS, axis=1)
        sm8 = jnp.concatenate([sm] * MLA_HEADS, axis=1)
        q_ref[rows, :] = (_rope(q, cm8, sm8, 16) * q_scale).astype(BF16)
        kr = _rope(h[:, 1920:2048], cm, sm, 16)
        k_ref[rows, :] = (kv[:, 0:1024] + jnp.concatenate([kr] * MLA_HEADS, axis=1)).astype(BF16)
        v_ref[rows, :] = (kv[:, 1024:2048] + ones_lane).astype(BF16)

    h_prev = in_proj(0)
    for c in range(1, n_chunks):
        h_next = in_proj(c)
        finish(c - 1, h_prev)
        h_prev = h_next
    finish(n_chunks - 1, h_prev)


def _projections(x2d, l, w_in, w_uq, w_ukv, sgu_g, sgu_b, q_g, kv_g, cr, sr, cm, sm, tm):
    t = x2d.shape[0]
    row = lambda w: pl.BlockSpec((tm, w), lambda i: (i, 0))
    lay = lambda a: pl.BlockSpec((None,) + a.shape[1:], lambda i: (l,) + (0,) * (a.ndim - 1))
    q_scale = (MLA_NOPE + MLA_ROPE) ** -0.5 * math.log2(math.e)
    return pl.pallas_call(
        functools.partial(_proj_kernel, q_scale=q_scale),
        grid=(t // tm,),
        in_specs=[row(D_MODEL), lay(w_in), lay(w_uq), lay(w_ukv), lay(sgu_g), lay(sgu_b),
                  lay(q_g), lay(kv_g), row(256), row(256), row(128), row(128)],
        out_specs=[row(1536), row(1024), row(1024), row(1024)],
        out_shape=[jax.ShapeDtypeStruct((t, 1536), BF16)] + [jax.ShapeDtypeStruct((t, 1024), BF16)] * 3,
        compiler_params=_cparams("parallel"),
        name="projections",
    )(x2d, w_in, w_uq, w_ukv, sgu_g, sgu_b, q_g, kv_g, cr, sr, cm, sm)


def _retsgu_kernel(rs_ref, dmat_ref, qdec_ref, kdec_ref, cdec_ref, gavg_ref, sw_ref, sb_ref,
                   out_ref, state_ref, *, n_chunks):
    @pl.when(pl.program_id(1) == 0)
    def _():
        state_ref[...] = jnp.zeros_like(state_ref)

    lane = lax.broadcasted_iota(jnp.int32, (CHUNK, RET_W), 1)
    head_of_lane = lane // RET_DK
    hmask = [head_of_lane == h for h in range(RET_HEADS)]
    row_i = lax.broadcasted_iota(jnp.int32, (CHUNK, SGU_GROUPS * CHUNK), 0)
    col_i = lax.broadcasted_iota(jnp.int32, (CHUNK, SGU_GROUPS * CHUNK), 1)
    sw = jnp.where(row_i >= (col_i & (CHUNK - 1)), sw_ref[...], 0.0).astype(BF16)
    blk = (lax.broadcasted_iota(jnp.int32, (RET_W, RET_W), 0) // RET_DK
           == lax.broadcasted_iota(jnp.int32, (RET_W, RET_W), 1) // RET_DK)
    gavg = gavg_ref[...]
    zero = jnp.zeros((), BF16)

    def group_mean(y):
        hi, lo = _split_bf16(y)
        return _dot(jnp.concatenate([hi, lo], axis=1), gavg)

    cs = range(n_chunks)
    rows = [slice(c * CHUNK, (c + 1) * CHUNK) for c in cs]
    rq = [rs_ref[r, 0:256] for r in rows]
    rk = [rs_ref[r, 256:512] for r in rows]
    rv = [rs_ref[r, 512:768] for r in rows]
    heads_of = lambda a: jnp.concatenate([jnp.where(m, a, zero) for m in hmask], axis=0)

    scores = [_dot_nt(heads_of(rq[c]), rk[c]) * dmat_ref[...] for c in cs]
    kv = [_dot_tn((rk[c].astype(F32) * kdec_ref[...]).astype(BF16), rv[c]) for c in cs]
    mixed = [_dot(sw, heads_of(rs_ref[rows[c], 1280:1536])) + sb_ref[...] for c in cs]
    intra = [_dot(jnp.concatenate([scores[c][h * CHUNK:(h + 1) * CHUNK, :] for h in range(RET_HEADS)],
                                  axis=1).astype(BF16), heads_of(rv[c])) for c in cs]
    states = []
    state = state_ref[...]
    for c in cs:
        states.append(state.astype(BF16))
        state = state * cdec_ref[...] + jnp.where(blk, kv[c], 0.0)
    state_ref[...] = state
    y = jnp.concatenate([intra[c] + _dot((rq[c].astype(F32) * qdec_ref[...]).astype(BF16), states[c])
                         for c in cs], axis=0)
    yc = y - group_mean(y)
    var = group_mean(yc * yc)
    out_ref[:, 0:256] = (rs_ref[:, 768:1024].astype(F32) * (yc * lax.rsqrt(var + LN_EPS))).astype(BF16)
    out_ref[:, 256:512] = (rs_ref[:, 1024:1280].astype(F32) * jnp.concatenate(mixed, axis=0)).astype(BF16)


def _retention_sgu(rs, l, consts, sgu_w_cat, sgu_b_lane, batch, seq, tb):
    dmat, qdec, kdec, cdec, gavg = consts
    n_blocks = seq // tb
    full = lambda a: pl.BlockSpec(a.shape, lambda b, i: (0,) * a.ndim)
    lay = lambda a: pl.BlockSpec((None,) + a.shape[1:], lambda b, i: (l,) + (0,) * (a.ndim - 1))
    return pl.pallas_call(
        functools.partial(_retsgu_kernel, n_chunks=tb // CHUNK),
        grid=(batch, n_blocks),
        in_specs=[pl.BlockSpec((tb, 1536), lambda b, i: (b * n_blocks + i, 0)),
                  full(dmat), full(qdec), full(kdec), full(cdec), full(gavg),
                  lay(sgu_w_cat), lay(sgu_b_lane)],
        out_specs=pl.BlockSpec((tb, 512), lambda b, i: (b * n_blocks + i, 0)),
        out_shape=jax.ShapeDtypeStruct((batch * seq, 512), BF16),
        scratch_shapes=[pltpu.VMEM((RET_W, RET_W), F32)],
        compiler_params=_cparams("parallel", "arbitrary"),
        name="retention_sgu",
    )(rs, dmat, qdec, kdec, cdec, gavg, sgu_w_cat, sgu_b_lane)


def _retention_consts():
    h = jnp.arange(RET_HEADS, dtype=F32)
    log_gamma = jnp.log1p(-(2.0 ** (-5.0 - h)))
    pos = jnp.arange(CHUNK, dtype=F32)
    diff = pos[:, None] - pos[None, :]
    intra = jnp.where(diff >= 0, jnp.exp(log_gamma[:, None, None] * jnp.maximum(diff, 0.0)), 0.0)
    dmat = intra.reshape(RET_HEADS * CHUNK, CHUNK)
    inner = jnp.exp(log_gamma[None, :] * (CHUNK - 1 - pos)[:, None])
    query = jnp.exp(log_gamma[None, :] * (pos + 1)[:, None])
    kdec = jnp.repeat(inner, RET_DK, axis=1)
    qdec = jnp.repeat(query, RET_DK, axis=1)
    chunk_decay = jnp.repeat(jnp.exp(log_gamma * CHUNK), RET_DK)
    blk = jnp.arange(RET_W)[:, None] // RET_DK == jnp.arange(RET_W)[None, :] // RET_DK
    cdec = jnp.where(blk, chunk_decay[:, None], 0.0)
    gavg = jnp.where(blk, 1.0 / RET_DK, 0.0).astype(BF16)
    return dmat, qdec, kdec, cdec, jnp.concatenate([gavg, gavg], axis=0)


def _flash_kernel(q_ref, k_ref, v_ref, o_ref, sa_ref, sb_ref, m_ref, acc_ref, *, tq, hps):
    i = pl.program_id(2)
    tk = tq // 2
    heads = range(hps)
    cols = [slice(hh * HEAD_PAD, (hh + 1) * HEAD_PAD) for hh in heads]

    def scores(j, hh, rows=slice(None)):
        r0 = pl.multiple_of(j * tk, tk)
        return _dot_nt(q_ref[rows, cols[hh]], k_ref[pl.ds(r0, tk), cols[hh]])

    def update(j, hh, s, rows=slice(None)):
        r0 = pl.multiple_of(j * tk, tk)
        m = m_ref[hh, rows, :]
        s_max = s[:, 0:HEAD_PAD]
        for c in range(1, tk // HEAD_PAD):
            s_max = jnp.maximum(s_max, s[:, c * HEAD_PAD:(c + 1) * HEAD_PAD])
        m_new = jnp.maximum(m, jnp.max(s_max, axis=-1, keepdims=True))
        p = jnp.exp2(s - jnp.concatenate([m_new] * (tk // HEAD_PAD), axis=1)).astype(BF16)
        acc_ref[hh, rows, :] = (jnp.exp2(m - m_new) * acc_ref[hh, rows, :]
                                + _dot(p, v_ref[pl.ds(r0, tk), cols[hh]]))
        m_ref[hh, rows, :] = m_new

    m_ref[...] = jnp.full(m_ref.shape, NEG_BIG, F32)
    acc_ref[...] = jnp.zeros(acc_ref.shape, F32)
    for hh in heads:
        sa_ref[hh] = scores(0, hh)

    def pair(t, carry):
        for hh in heads:
            sb_ref[hh] = scores(2 * t + 1, hh)
        for hh in heads:
            update(2 * t, hh, sa_ref[hh])
        for hh in heads:
            sa_ref[hh] = scores(2 * t + 2, hh)
        for hh in heads:
            update(2 * t + 1, hh, sb_ref[hh])
        return carry

    lax.fori_loop(0, i, pair, 0)
    low = slice(tk, tq)
    visible = (lax.broadcasted_iota(jnp.int32, (tq, tk), 1) <= lax.broadcasted_iota(jnp.int32, (tq, tk), 0))
    visible_low = (lax.broadcasted_iota(jnp.int32, (tk, tk), 1) <= lax.broadcasted_iota(jnp.int32, (tk, tk), 0))
    for hh in heads:
        sb_ref[hh, low, :] = scores(2 * i + 1, hh, low)
        update(2 * i, hh, jnp.where(visible, sa_ref[hh], NEG_BIG))
    for hh in heads:
        update(2 * i + 1, hh, jnp.where(visible_low, sb_ref[hh, low, :], NEG_BIG), low)
    lane = lax.broadcasted_iota(jnp.int32, (tq, HEAD_PAD), 1)
    for pr in range(hps // 2):
        o = []
        for hh in (2 * pr, 2 * pr + 1):
            acc = acc_ref[hh]
            o.append(acc / acc[:, MLA_V:MLA_V + 1])
        o_ref[:, pr * HEAD_PAD:(pr + 1) * HEAD_PAD] = jnp.where(
            lane < MLA_V, o[0], pltpu.roll(o[1], MLA_V, 1)).astype(BF16)


def _flash_attention(q, k, v, batch, seq, tq, hps):
    nq = seq // tq
    return pl.pallas_call(
        functools.partial(_flash_kernel, tq=tq, hps=hps),
        grid=(batch, MLA_HEADS // hps, nq),
        in_specs=[pl.BlockSpec((tq, hps * HEAD_PAD), lambda b, p, i: (b * nq + i, p)),
                  pl.BlockSpec((seq, hps * HEAD_PAD), lambda b, p, i: (b, p)),
                  pl.BlockSpec((seq, hps * HEAD_PAD), lambda b, p, i: (b, p))],
        out_specs=pl.BlockSpec((tq, hps * MLA_V), lambda b, p, i: (b * nq + i, p)),
        out_shape=jax.ShapeDtypeStruct((batch * seq, MLA_W), BF16),
        scratch_shapes=[pltpu.VMEM((hps, tq, tq // 2), F32), pltpu.VMEM((hps, tq, tq // 2), F32),
                        pltpu.VMEM((hps, tq, HEAD_PAD), F32), pltpu.VMEM((hps, tq, HEAD_PAD), F32)],
        compiler_params=_cparams("parallel", "parallel", "arbitrary"),
        name="flash_attention",
    )(q, k, v)


def _memkv_kernel(mem_ref, wk_ref, wv_ref, k_ref, v_ref):
    m = mem_ref[...].astype(BF16)
    k_ref[...] = _dot(m, wk_ref[...]).astype(BF16)
    v_ref[...] = _dot(m, wv_ref[...]).astype(BF16)


def _memory_kv(mem2d, l, wk, wv, batch):
    lay = lambda a: pl.BlockSpec((None,) + a.shape[1:], lambda b: (l,) + (0,) * (a.ndim - 1))
    blk = pl.BlockSpec((MEM_LEN, XA_W), lambda b: (b, 0))
    return pl.pallas_call(
        _memkv_kernel,
        grid=(batch,),
        in_specs=[pl.BlockSpec((MEM_LEN, D_MODEL), lambda b: (b, 0)), lay(wk), lay(wv)],
        out_specs=[blk, blk],
        out_shape=[jax.ShapeDtypeStruct((batch * MEM_LEN, XA_W), BF16)] * 2,
        compiler_params=_cparams("parallel"),
        name="memory_kv",
    )(mem2d, wk, wv)


def _route_rows(scores, biased):
    s = [scores[e:e + 1, :] for e in range(N_EXPERTS)]
    b = [biased[e:e + 1, :] for e in range(N_EXPERTS)]
    group_scores = []
    for g in range(N_GROUPS):
        b0, b1, b2, b3 = b[4 * g:4 * g + 4]
        hi01, lo01 = jnp.maximum(b0, b1), jnp.minimum(b0, b1)
        hi23, lo23 = jnp.maximum(b2, b3), jnp.minimum(b2, b3)
        top1 = jnp.maximum(hi01, hi23)
        top2 = jnp.maximum(jnp.minimum(hi01, hi23), jnp.maximum(lo01, lo23))
        group_scores.append(top1 + top2)
    best = group_scores[0]
    sel = jnp.zeros_like(best, dtype=jnp.int32)
    for g in range(1, N_GROUPS):
        upd = group_scores[g] > best
        sel = jnp.where(upd, g, sel)
        best = jnp.where(upd, group_scores[g], best)

    def pick(rows, j):
        out = rows[j]
        for g in range(1, N_GROUPS):
            out = jnp.where(sel == g, rows[4 * g + j], out)
        return out

    ib = [pick(b, j) for j in range(EXPERTS_PER_GROUP)]
    isc = [pick(s, j) for j in range(EXPERTS_PER_GROUP)]

    def argmax4(vals):
        bv, bi = vals[0], jnp.zeros_like(sel)
        for j in range(1, EXPERTS_PER_GROUP):
            upd = vals[j] > bv
            bi = jnp.where(upd, j, bi)
            bv = jnp.where(upd, vals[j], bv)
        return bi

    i1 = argmax4(ib)
    i2 = argmax4([jnp.where(i1 == j, -jnp.inf, ib[j]) for j in range(EXPERTS_PER_GROUP)])

    def take(vals, idx):
        out = vals[0]
        for j in range(1, EXPERTS_PER_GROUP):
            out = jnp.where(idx == j, vals[j], out)
        return out

    g1, g2 = take(isc, i1), take(isc, i2)
    den = g1 + g2
    g1, g2 = g1 / den, g2 / den
    return sel, i1, i2, g1, g2


def _mix_xa_kernel(x_ref, rs_ref, at_ref, wout_ref, g1_ref, b1_ref, wq_ref, km_ref, vm_ref, wo_ref,
                   g2_ref, b2_ref, rw_ref, rb_ref, tri_ref, x2_ref, pos_ref, gate_rows_ref,
                   cnt_ref, base_ref, pk_ref, *, tm, tiles_per_block):
    @pl.when(pl.program_id(1) % tiles_per_block == 0)
    def _():
        base_ref[...] = jnp.zeros_like(base_ref)

    w_hi, w_lo = _split_bf16(rw_ref[...])

    def out_proj(rows):
        return _dot(jnp.concatenate([rs_ref[rows, :], at_ref[rows, :]], axis=1), wout_ref[...])

    def middle(rows, mix):
        x1 = _layer_norm(ALPHA * x_ref[rows, :] + mix, g1_ref[...], b1_ref[...])
        q = (_dot(x1.astype(BF16), wq_ref[...]) * (XA_DIM ** -0.5 * math.log2(math.e))).astype(BF16)
        cols = [slice(h * XA_DIM, (h + 1) * XA_DIM) for h in range(XA_HEADS)]
        s = [_dot_nt(q[:, sl], km_ref[:, sl]) for sl in cols]
        p = [jnp.exp2(sh - jnp.max(sh, axis=-1, keepdims=True)) for sh in s]
        o = [_dot(ph.astype(BF16), vm_ref[:, sl]) for ph, sl in zip(p, cols)]
        heads = [oh / jnp.sum(ph, axis=-1, keepdims=True) for oh, ph in zip(o, p)]
        xa = _dot(jnp.concatenate(heads, axis=1).astype(BF16), wo_ref[...])
        return ALPHA * x1 + xa

    def back(rows, z):
        x2 = _layer_norm(z, g2_ref[...], b2_ref[...])
        x2_ref[rows, :] = x2
        x_hi, x_lo = _split_bf16(x2)
        return _dot_nt(w_hi, x_hi) + (_dot_nt(w_hi, x_lo) + _dot_nt(w_lo, x_hi))

    whole = slice(None)
    logits = back(whole, middle(whole, out_proj(whole)))
    scores = 1.0 / (1.0 + jnp.exp(-logits))
    sel, i1, i2, gate1, gate2 = _route_rows(scores, scores + rb_ref[...])
    in_group = [jnp.where(i1 == j, gate1, 0.0) + jnp.where(i2 == j, gate2, 0.0)
                for j in range(EXPERTS_PER_GROUP)]
    gate_rows_ref[...] = jnp.concatenate(in_group + [jnp.zeros((128 - EXPERTS_PER_GROUP, tm), F32)], axis=0).T
    hit = lax.broadcasted_iota(jnp.int32, (N_EXPERTS, tm), 0) == sel
    cnt = jnp.where(hit, 1.0, 0.0)
    base = base_ref[...]
    before = _dot(cnt.astype(BF16), tri_ref[...]) - cnt + base[:, 0:1]
    rank = jnp.sum(jnp.where(hit, before, 0.0), axis=0, keepdims=True).astype(jnp.int32)
    base = base + jnp.sum(cnt, axis=1, keepdims=True)
    base_ref[...] = base
    cnt_ref[...] = base
    ti = pl.program_id(1) % tiles_per_block
    pk_ref[ti] = sel * RANK_RADIX + rank

    @pl.when(ti == tiles_per_block - 1)
    def _():
        starts = []
        start = jnp.zeros((1, 128), F32)
        for g in range(N_GROUPS):
            starts.append(start.astype(jnp.int32)[:, 0:1])
            tiles = jnp.floor((base[g:g + 1, :] + (MOE_TILE - 1)) * (1.0 / MOE_TILE))
            start = start + tiles * MOE_TILE
        for tj in range(tiles_per_block):
            pk = pk_ref[tj]
            g_of = pk >> RANK_BITS
            slot = pk & (RANK_RADIX - 1)
            for g in range(N_GROUPS):
                slot = slot + jnp.where(g_of == g, starts[g], 0)
            pos_ref[:, tj * tm:(tj + 1) * tm] = slot


def _mix_xa(x2d, rs, at, km, vm, l, w_out, g1, b1, wq, wo, g2, b2, rw_t, rb_col, tri, batch, seq, tm, nblk):
    t = x2d.shape[0]
    nb = seq // tm
    n_tiles = t // tm
    tpb = nblk // tm
    row = lambda w: pl.BlockSpec((tm, w), lambda b, i: (b * nb + i, 0))
    lay = lambda a: pl.BlockSpec((None,) + a.shape[1:], lambda b, i: (l,) + (0,) * (a.ndim - 1))
    full = lambda a: pl.BlockSpec(a.shape, lambda b, i: (0,) * a.ndim)
    memb = pl.BlockSpec((MEM_LEN, XA_W), lambda b, i: (b, 0))
    blk = pl.BlockSpec((None, 1, nblk), lambda b, i: ((b * nb + i) // tpb, 0, 0))
    blk_i = jax.ShapeDtypeStruct((t // nblk, 1, nblk), jnp.int32)
    return pl.pallas_call(
        functools.partial(_mix_xa_kernel, tm=tm, tiles_per_block=tpb),
        grid=(batch, nb),
        in_specs=[row(D_MODEL), row(512), row(512), lay(w_out), lay(g1), lay(b1), lay(wq), memb, memb,
                  lay(wo), lay(g2), lay(b2), full(rw_t), full(rb_col), full(tri)],
        out_specs=[row(D_MODEL), blk, row(128),
                   pl.BlockSpec((None, N_EXPERTS, 128), lambda b, i: ((b * nb + i) // tpb, 0, 0))],
        out_shape=[jax.ShapeDtypeStruct((t, D_MODEL), F32), blk_i, jax.ShapeDtypeStruct((t, 128), F32),
                   jax.ShapeDtypeStruct((t // nblk, N_EXPERTS, 128), F32)],
        scratch_shapes=[pltpu.VMEM((N_EXPERTS, 128), F32), pltpu.VMEM((tpb, 1, tm), jnp.int32)],
        compiler_params=_cparams("parallel", "arbitrary"),
        name="mix_xattn_router",
    )(x2d, rs, at, w_out, g1, b1, wq, km, vm, wo, g2, b2, rw_t, rb_col, tri)


def _group_weights_kernel(w_ref, o_ref):
    for e in range(EXPERTS_PER_GROUP):
        o_ref[:, e * D_EXPERT:(e + 1) * D_EXPERT] = w_ref[e].astype(BF16)


def _group_weights(w):
    depth = w.shape[0]
    return pl.pallas_call(
        _group_weights_kernel,
        grid=(depth, N_GROUPS),
        in_specs=[pl.BlockSpec((None, EXPERTS_PER_GROUP, D_MODEL, D_EXPERT), lambda l, g: (l, g, 0, 0))],
        out_specs=pl.BlockSpec((None, None, D_MODEL, EXPERTS_PER_GROUP * D_EXPERT), lambda l, g: (l, g, 0, 0)),
        out_shape=jax.ShapeDtypeStruct((depth, N_GROUPS, D_MODEL, EXPERTS_PER_GROUP * D_EXPERT), BF16),
        compiler_params=_cparams("parallel", "parallel"),
        name="group_weights",
    )(w)


def _moe_kernel(cnt_ref, pos_ref, x_ref, gates_ref, wg_ref, wu_ref, wd_ref, g_ref, b_ref,
                o_ref, xg_ref, xs_ref, ys_ref, og_ref, off_ref, ntile_ref, *, ts, k, mp):
    s = pl.program_id(1)
    sg = ts + 8
    unroll = 8

    @pl.when((s == 0) & (pl.program_id(0) == 0))
    def _():
        xs_ref[...] = jnp.zeros_like(xs_ref)
        xg_ref[...] = jnp.zeros_like(xg_ref)

    @pl.when(s == 0)
    def _():
        start = jnp.int32(0)
        for g in range(N_GROUPS):
            tiles = (cnt_ref[0, g] + (MOE_TILE - 1)) // MOE_TILE
            off_ref[g] = start
            ntile_ref[g] = tiles
            start = start + tiles * MOE_TILE

    @pl.when(s < k)
    def _():
        x = x_ref[...]
        half = D_MODEL // 2
        hi = pltpu.bitcast(x[:, :half].astype(BF16).astype(F32), jnp.uint32)
        lo = pltpu.bitcast(x[:, half:].astype(BF16).astype(F32), jnp.uint32)
        words = hi | (lo >> 16)
        for c in range(4):
            xg_ref[c * sg:c * sg + ts, :] = words[:, c * 128:(c + 1) * 128]
        xg_ref[4 * sg:4 * sg + ts, :] = pltpu.bitcast(gates_ref[...], jnp.uint32)

        def dispatch(tt, carry):
            for u in range(unroll):
                tl = tt * unroll + u
                xs_ref[pl.ds(pos_ref[s * ts + tl], 8, stride=mp), :] = xg_ref[pl.ds(tl, 8, stride=sg), :]
            return carry

        lax.fori_loop(0, ts // unroll, dispatch, 0)

    @pl.when((s >= k) & (s < k + N_GROUPS))
    def _():
        g = s - k
        seg = off_ref[g]

        def row_tiles(tiles):
            r0 = [pl.multiple_of(seg + i * MOE_TILE, MOE_TILE) for i in tiles]
            xb, weight = [], []
            for r in r0:
                words = [xs_ref[pl.ds(c * mp + r, MOE_TILE), :] for c in range(4)]
                gates = pltpu.bitcast(xs_ref[pl.ds(4 * mp + r, MOE_TILE), :], F32)
                xb.append(jnp.concatenate(
                    [pltpu.bitcast(w & jnp.uint32(0xFFFF0000), F32).astype(BF16) for w in words]
                    + [pltpu.bitcast(w << 16, F32).astype(BF16) for w in words], axis=1))
                weight.append(jnp.concatenate([jnp.broadcast_to(gates[:, e:e + 1], (MOE_TILE, D_EXPERT))
                                               for e in range(EXPERTS_PER_GROUP)], axis=1))
            gate_act = [_dot(x, wg_ref[...]) for x in xb]
            up_act = [_dot(x, wu_ref[...]) for x in xb]
            hid = [(_silu(a) * u * w).astype(BF16) for a, u, w in zip(gate_act, up_act, weight)]
            y = [_dot(h, wd_ref[...]) for h in hid]
            for r, yt in zip(r0, y):
                for j in range(8):
                    ys_ref[pl.ds(j * mp + r, MOE_TILE), :] = yt[:, j * 128:(j + 1) * 128]

        def tile_pair(i, carry):
            row_tiles((2 * i, 2 * i + 1))
            return carry

        def tile_last(i, carry):
            row_tiles((n_tiles - 1,))
            return carry

        n_tiles = ntile_ref[g]
        lax.fori_loop(0, n_tiles // 2, tile_pair, 0)
        lax.fori_loop(0, n_tiles % 2, tile_last, 0)

    @pl.when(s >= k + N_GROUPS)
    def _():
        sub = s - (k + N_GROUPS)

        def combine(tt, carry):
            for u in range(unroll):
                tl = tt * unroll + u
                og_ref[pl.ds(tl, 8, stride=sg), :] = ys_ref[pl.ds(pos_ref[sub * ts + tl], 8, stride=mp), :]
            return carry

        lax.fori_loop(0, ts // unroll, combine, 0)
        ffn = jnp.concatenate([og_ref[j * sg:j * sg + ts, :] for j in range(8)], axis=1)
        o_ref[...] = _layer_norm(ALPHA * x_ref[...] + ffn, g_ref[...], b_ref[...])


def _moe(x2, cnt_blk, pos, gate_rows, l, wg, wu, wd, g, b, nblk, ts):
    t = x2.shape[0]
    k = nblk // ts
    steps = 2 * k + N_GROUPS
    mp = nblk + N_GROUPS * MOE_TILE + 8
    lay = lambda a: pl.BlockSpec((None,) + a.shape[1:], lambda i, s: (l,) + (0,) * (a.ndim - 1))
    grp = lambda a: pl.BlockSpec((None, None) + a.shape[2:],
                                 lambda i, s: (l, jnp.clip(s - k, 0, N_GROUPS - 1), 0, 0))
    x_map = lambda i, s: (i * k + jnp.where(s < k, s, jnp.where(s < k + N_GROUPS, k - 1, s - k - N_GROUPS)), 0)
    g_map = lambda i, s: (i * k + jnp.minimum(s, k - 1), 0)
    o_map = lambda i, s: (i * k + jnp.maximum(s - k - N_GROUPS, 0), 0)
    return pl.pallas_call(
        functools.partial(_moe_kernel, ts=ts, k=k, mp=mp),
        grid=(t // nblk, steps),
        in_specs=[pl.BlockSpec((None, 1, N_EXPERTS), lambda i, s: (i, 0, 0), memory_space=pltpu.SMEM),
                  pl.BlockSpec((nblk,), lambda i, s: (i,), memory_space=pltpu.SMEM),
                  pl.BlockSpec((ts, D_MODEL), x_map), pl.BlockSpec((ts, 128), g_map),
                  grp(wg), grp(wu), grp(wd), lay(g), lay(b)],
        out_specs=pl.BlockSpec((ts, D_MODEL), o_map),
        out_shape=jax.ShapeDtypeStruct((t, D_MODEL), F32),
        scratch_shapes=[pltpu.VMEM((8 * (ts + 8), 128), jnp.uint32), pltpu.VMEM((8 * mp, 128), jnp.uint32),
                        pltpu.VMEM((8 * mp, 128), F32), pltpu.VMEM((8 * (ts + 8), 128), F32),
                        pltpu.SMEM((N_GROUPS,), jnp.int32), pltpu.SMEM((N_GROUPS,), jnp.int32)],
        compiler_params=_cparams("arbitrary", "arbitrary"),
        name="moe_experts",
    )(cnt_blk, pos, x2, gate_rows, wg, wu, wd, g, b)


def _tile(n, pref):
    t = min(n, pref)
    assert n % t == 0, (n, t)
    return t


def kernel(x, mem, positions, w_in, w_out, sgu_ln_g, sgu_ln_b, sgu_w, sgu_b, mla_q_norm_g, mla_w_uq, mla_kv_norm_g, mla_w_ukv, xa_wq, xa_wk, xa_wv, xa_wo, ln_mix_g, ln_mix_b, ln_xa_g, ln_xa_b, ln_moe_g, ln_moe_b, router_w, router_bias, expert_w_gate, expert_w_up, expert_w_down):
    batch, seq, _ = x.shape
    depth = w_in.shape[0]
    t = batch * seq
    assert seq % CHUNK == 0

    w_in_b = w_in.astype(BF16)
    w_in_p = jnp.concatenate(
        [w_in_b[:, :, :1920], jnp.zeros((depth, D_MODEL, 64), BF16), w_in_b[:, :, 1920:1952],
         jnp.zeros((depth, D_MODEL, 32), BF16)], axis=2)
    w_uq_p = jnp.pad(mla_w_uq.astype(BF16).reshape(depth, MLA_Q_RANK, MLA_HEADS, MLA_NOPE + MLA_ROPE),
                     ((0, 0), (0, 0), (0, 0), (0, HEAD_PAD - MLA_NOPE - MLA_ROPE))
                     ).reshape(depth, MLA_Q_RANK, MLA_HEADS * HEAD_PAD)
    ukv = mla_w_ukv.astype(BF16).reshape(depth, MLA_KV_RANK, MLA_HEADS, MLA_NOPE + MLA_V)
    w_uk_p = jnp.pad(ukv[..., :MLA_NOPE], ((0, 0), (0, 0), (0, 0), (0, HEAD_PAD - MLA_NOPE))
                     ).reshape(depth, MLA_KV_RANK, MLA_HEADS * HEAD_PAD)
    w_uv_p = jnp.pad(ukv[..., MLA_NOPE:], ((0, 0), (0, 0), (0, 0), (0, HEAD_PAD - MLA_V))
                     ).reshape(depth, MLA_KV_RANK, MLA_HEADS * HEAD_PAD)
    w_ukv_p = jnp.concatenate([w_uk_p, w_uv_p], axis=2)
    w_out_b = w_out.astype(BF16)
    wq_b, wk_b, wv_b, wo_b = (a.astype(BF16) for a in (xa_wq, xa_wk, xa_wv, xa_wo))
    wg_b = _group_weights(expert_w_gate)
    wu_b = _group_weights(expert_w_up)
    wd_b = expert_w_down.reshape(depth, N_GROUPS, EXPERTS_PER_GROUP * D_EXPERT, D_MODEL).astype(BF16)
    vec = lambda a: a.reshape(depth, 1, a.shape[-1])
    sgu_w_cat = jnp.transpose(sgu_w, (0, 2, 1, 3)).reshape(depth, CHUNK, SGU_GROUPS * CHUNK)
    sgu_b_lane = jnp.repeat(jnp.transpose(sgu_b, (0, 2, 1)), SGU_W // SGU_GROUPS, axis=2)
    rw_t = router_w.T
    rb_col = router_bias.reshape(N_EXPERTS, 1)
    consts = _retention_consts()

    x2d = x.reshape(t, D_MODEL)
    mem2d = mem.reshape(batch * MEM_LEN, D_MODEL)
    cr, sr, cm, sm = _rope_tables(positions, _tile(t // 2, 512))

    tm_proj = _tile(t, 1024)
    tb = _tile(seq, 1024)
    tq = _tile(seq, 1024)
    tm_mix = _tile(seq, 1024)
    nblk = _tile(seq, 2048)
    ts_moe = _tile(nblk, 512)
    tri = (jnp.arange(tm_mix)[:, None] <= jnp.arange(tm_mix)[None, :]).astype(BF16)
    for l in range(depth):
        rs, q, k, v = _projections(x2d, l, w_in_p, w_uq_p, w_ukv_p, vec(sgu_ln_g), vec(sgu_ln_b),
                                   vec(mla_q_norm_g), vec(mla_kv_norm_g), cr, sr, cm, sm, tm_proj)
        retsgu = _retention_sgu(rs, l, consts, sgu_w_cat, sgu_b_lane, batch, seq, tb)
        attn = _flash_attention(q, k, v, batch, seq, tq, 4)
        km, vm = _memory_kv(mem2d, l, wk_b, wv_b, batch)
        x2, pos, gate_rows, cnts = _mix_xa(
            x2d, retsgu, attn, km, vm, l, w_out_b, vec(ln_mix_g), vec(ln_mix_b), wq_b, wo_b,
            vec(ln_xa_g), vec(ln_xa_b), rw_t, rb_col, tri, batch, seq, tm_mix, nblk)
        cnt_blk = cnts[:, :, 0].astype(jnp.int32).reshape(-1, 1, N_EXPERTS)
        x2d = _moe(x2, cnt_blk, pos.reshape(t), gate_rows, l,
                   wg_b, wu_b, wd_b, vec(ln_moe_g), vec(ln_moe_b), nblk, ts_moe)
    return x2d.reshape(batch, seq, D_MODEL)
```

```python
import functools
import math

import jax
import jax.numpy as jnp
from jax import lax
from jax.experimental import pallas as pl
from jax.experimental.pallas import tpu as pltpu

F32 = jnp.float32
BF16 = jnp.bfloat16

D_MODEL = 1024
DEPTH = 4
MEM_LEN = 256
ROPE_THETA = 10000.0

RET_HEADS = 4
RET_DK = 64
RET_W = 256
CHUNK = 128

SGU_GROUPS = 4
SGU_W = 256

MLA_HEADS = 8
MLA_Q_RANK = 256
MLA_KV_RANK = 128
MLA_NOPE = 64
MLA_ROPE = 32
MLA_V = 64
MLA_W = MLA_HEADS * MLA_V
HEAD_PAD = 128

XA_HEADS = 4
XA_DIM = 128
XA_W = XA_HEADS * XA_DIM

N_EXPERTS = 16
N_GROUPS = 4
EXPERTS_PER_GROUP = 4
D_EXPERT = 256

ALPHA = (2 * DEPTH) ** 0.25
LN_EPS = 1e-5
IN_PAD = 2048
NEG_BIG = -1e30
PROJ_CHUNK = 256
MIX_CHUNK = 256
MOE_TILE = 128
RANK_BITS = 16
RANK_RADIX = 1 << RANK_BITS

VMEM_LIMIT = 56 * 1024 * 1024


def _cparams(*sem):
    return pltpu.CompilerParams(dimension_semantics=sem, vmem_limit_bytes=VMEM_LIMIT)


def _dot(a, b):
    return jnp.dot(a, b, preferred_element_type=F32)


def _dot_nt(a, b):
    return lax.dot_general(a, b, (((1,), (1,)), ((), ())), preferred_element_type=F32)


def _dot_tn(a, b):
    return lax.dot_general(a, b, (((0,), (0,)), ((), ())), preferred_element_type=F32)


def _layer_norm(z, g, b):
    mu = jnp.mean(z, axis=-1, keepdims=True)
    zc = z - mu
    var = jnp.mean(zc * zc, axis=-1, keepdims=True)
    return zc * lax.rsqrt(var + LN_EPS) * g + b


def _rms_norm(z, g):
    ms = jnp.mean(z * z, axis=-1, keepdims=True)
    return z * lax.rsqrt(ms + LN_EPS) * g


def _silu(z):
    return z / (1.0 + jnp.exp(-z))


def _rope(x, cos, sin_signed, half):
    w = x.shape[-1]
    lane = lax.broadcasted_iota(jnp.int32, x.shape, 1)
    rot = jnp.where((lane & half) == 0, pltpu.roll(x, w - half, 1), pltpu.roll(x, half, 1))
    return x * cos + rot * sin_signed


def _split_bf16(x):
    hi = x.astype(BF16)
    lo = (x - hi.astype(F32)).astype(BF16)
    return hi, lo


def _tables_kernel(pos_a_ref, pos_b_ref, invf_ref, cr_ref, sr_ref, cm_ref, sm_ref):
    lane = lax.broadcasted_iota(jnp.int32, (pos_a_ref.shape[0], 128), 1)
    pos = jnp.where(lane < 64, pos_a_ref[...].astype(F32), pos_b_ref[...].astype(F32))
    ang = pos * invf_ref[...]
    c_both = jnp.cos(ang)
    s_both = jnp.sin(ang)

    def tile32(v):
        v0 = jnp.where(lane < 32, v, 0.0)
        v1 = v0 + pltpu.roll(v0, 32, 1)
        return v1 + pltpu.roll(v1, 64, 1)

    for half in range(2):
        c = c_both if half == 0 else pltpu.roll(c_both, 64, 1)
        s = s_both if half == 0 else pltpu.roll(s_both, 64, 1)
        ct = tile32(c)
        st = tile32(s) * jnp.where((lane & 32) == 0, -1.0, 1.0)
        cr_ref[half] = jnp.concatenate([ct, ct], axis=1)
        sr_ref[half] = jnp.concatenate([st, st], axis=1)
        in_src = (lane >= 32) & (lane < 48)
        cmv = jnp.where(in_src, c, 0.0)
        smv = jnp.where(in_src, s, 0.0)
        in_dst = (lane >= 64) & (lane < 96)
        cm_ref[half] = jnp.where(in_dst, pltpu.roll(cmv, 32, 1) + pltpu.roll(cmv, 48, 1), 1.0)
        sm_ref[half] = jnp.where(in_dst, pltpu.roll(smv, 48, 1) - pltpu.roll(smv, 32, 1), 0.0)


def _rope_tables(positions, tm):
    t = positions.size
    half_t = t // 2
    n = half_t // tm
    pos = positions.reshape(t, 1)
    fr = ROPE_THETA ** (-jnp.arange(0, RET_DK, 2, dtype=F32) / RET_DK)
    fm = ROPE_THETA ** (-jnp.arange(0, MLA_ROPE, 2, dtype=F32) / MLA_ROPE)
    one = jnp.concatenate([fr, fm, jnp.zeros((64 - 48,), F32)])
    invf = jnp.concatenate([one, one]).reshape(1, 128)
    out = lambda w: pl.BlockSpec((2, tm, w), lambda i: (0, i, 0))
    tables = pl.pallas_call(
        _tables_kernel,
        grid=(n,),
        in_specs=[pl.BlockSpec((tm, 1), lambda i: (i, 0)),
                  pl.BlockSpec((tm, 1), lambda i: (i + n, 0)),
                  pl.BlockSpec((1, 128), lambda i: (0, 0))],
        out_specs=[out(256), out(256), out(128), out(128)],
        out_shape=[jax.ShapeDtypeStruct((2, half_t, 256), F32), jax.ShapeDtypeStruct((2, half_t, 256), F32),
                   jax.ShapeDtypeStruct((2, half_t, 128), F32), jax.ShapeDtypeStruct((2, half_t, 128), F32)],
        compiler_params=_cparams("parallel"),
        name="rope_tables",
    )(pos, pos, invf)
    return [a.reshape(t, a.shape[-1]) for a in tables]


def _proj_kernel(x_ref, win_ref, wuq_ref, wukv_ref, lng_ref, lnb_ref, qg_ref, kvg_ref,
                 cr_ref, sr_ref, cm_ref, sm_ref, rs_ref, q_ref, k_ref, v_ref, *, q_scale):
    tm = x_ref.shape[0]
    n_chunks = max(tm // PROJ_CHUNK, 1)
    chunk = tm // n_chunks
    lane = lax.broadcasted_iota(jnp.int32, (chunk, MLA_HEADS * HEAD_PAD), 1)
    ones_lane = jnp.where((lane & (HEAD_PAD - 1)) == MLA_V, 1.0, 0.0)

    def in_proj(c):
        rows = slice(c * chunk, (c + 1) * chunk)
        return _dot(x_ref[rows, :].astype(BF16), win_ref[...])

    def finish(c, h):
        rows = slice(c * chunk, (c + 1) * chunk)
        cm = cm_ref[rows, :]
        sm = sm_ref[rows, :]
        cq = _rms_norm(h[:, 1536:1792], qg_ref[...]).astype(BF16)
        q = _dot(cq, wuq_ref[...])
        ckv = _rms_norm(h[:, 1792:1920], kvg_ref[...]).astype(BF16)
        kv = _dot(ckv, wukv_ref[...])
        cr = cr_ref[rows, :]
        sr = sr_ref[rows, :]
        rs_ref[rows, 0:256] = _rope(h[:, 0:256], cr, sr, 32).astype(BF16)
        rs_ref[rows, 256:512] = (_rope(h[:, 256:512], cr, sr, 32) * (RET_DK ** -0.5)).astype(BF16)
        rs_ref[rows, 512:768] = h[:, 512:768].astype(BF16)
        rs_ref[rows, 768:1024] = _silu(h[:, 768:1024]).astype(BF16)
        rs_ref[rows, 1024:1280] = h[:, 1024:1280].astype(BF16)
        rs_ref[rows, 1280:1536] = _layer_norm(h[:, 1280:1536], lng_ref[...], lnb_ref[...]).astype(BF16)
        cm8 = jnp.concatenate([cm] * MLA_HEADS, axis=1)
        sm8 = jnp.concatenate([sm] * MLA_HEADS, axis=1)
        q_ref[rows, :] = (_rope(q, cm8, sm8, 16) * q_scale).astype(BF16)
        kr = _rope(h[:, 1920:2048], cm, sm, 16)
        k_ref[rows, :] = (kv[:, 0:1024] + jnp.concatenate([kr] * MLA_HEADS, axis=1)).astype(BF16)
        v_ref[rows, :] = (kv[:, 1024:2048] + ones_lane).astype(BF16)

    h_prev = in_proj(0)
    for c in range(1, n_chunks):
        h_next = in_proj(c)
        finish(c - 1, h_prev)
        h_prev = h_next
    finish(n_chunks - 1, h_prev)


def _projections(x2d, l, w_in, w_uq, w_ukv, sgu_g, sgu_b, q_g, kv_g, cr, sr, cm, sm, tm):
    t = x2d.shape[0]
    row = lambda w: pl.BlockSpec((tm, w), lambda i: (i, 0))
    lay = lambda a: pl.BlockSpec((None,) + a.shape[1:], lambda i: (l,) + (0,) * (a.ndim - 1))
    q_scale = (MLA_NOPE + MLA_ROPE) ** -0.5 * math.log2(math.e)
    return pl.pallas_call(
        functools.partial(_proj_kernel, q_scale=q_scale),
        grid=(t // tm,),
        in_specs=[row(D_MODEL), lay(w_in), lay(w_uq), lay(w_ukv), lay(sgu_g), lay(sgu_b),
                  lay(q_g), lay(kv_g), row(256), row(256), row(128), row(128)],
        out_specs=[row(1536), row(1024), row(1024), row(1024)],
        out_shape=[jax.ShapeDtypeStruct((t, 1536), BF16)] + [jax.ShapeDtypeStruct((t, 1024), BF16)] * 3,
        compiler_params=_cparams("parallel"),
        name="projections",
    )(x2d, w_in, w_uq, w_ukv, sgu_g, sgu_b, q_g, kv_g, cr, sr, cm, sm)


def _retsgu_kernel(rs_ref, dmat_ref, qdec_ref, kdec_ref, cdec_ref, gavg_ref, sw_ref, sb_ref,
                   out_ref, state_ref, *, n_chunks):
    @pl.when(pl.program_id(1) == 0)
    def _():
        state_ref[...] = jnp.zeros_like(state_ref)

    lane = lax.broadcasted_iota(jnp.int32, (CHUNK, RET_W), 1)
    head_of_lane = lane // RET_DK
    hmask = [head_of_lane == h for h in range(RET_HEADS)]
    row_i = lax.broadcasted_iota(jnp.int32, (CHUNK, SGU_GROUPS * CHUNK), 0)
    col_i = lax.broadcasted_iota(jnp.int32, (CHUNK, SGU_GROUPS * CHUNK), 1)
    sw = jnp.where(row_i >= (col_i & (CHUNK - 1)), sw_ref[...], 0.0).astype(BF16)
    blk = (lax.broadcasted_iota(jnp.int32, (RET_W, RET_W), 0) // RET_DK
           == lax.broadcasted_iota(jnp.int32, (RET_W, RET_W), 1) // RET_DK)
    gavg = gavg_ref[...]
    zero = jnp.zeros((), BF16)

    def group_mean(y):
        hi, lo = _split_bf16(y)
        return _dot(jnp.concatenate([hi, lo], axis=1), gavg)

    cs = range(n_chunks)
    rows = [slice(c * CHUNK, (c + 1) * CHUNK) for c in cs]
    rq = [rs_ref[r, 0:256] for r in rows]
    rk = [rs_ref[r, 256:512] for r in rows]
    rv = [rs_ref[r, 512:768] for r in rows]
    heads_of = lambda a: jnp.concatenate([jnp.where(m, a, zero) for m in hmask], axis=0)

    scores = [_dot_nt(heads_of(rq[c]), rk[c]) * dmat_ref[...] for c in cs]
    kv = [_dot_tn((rk[c].astype(F32) * kdec_ref[...]).astype(BF16), rv[c]) for c in cs]
    mixed = [_dot(sw, heads_of(rs_ref[rows[c], 1280:1536])) + sb_ref[...] for c in cs]
    intra = [_dot(jnp.concatenate([scores[c][h * CHUNK:(h + 1) * CHUNK, :] for h in range(RET_HEADS)],
                                  axis=1).astype(BF16), heads_of(rv[c])) for c in cs]
    states = []
    state = state_ref[...]
    for c in cs:
        states.append(state.astype(BF16))
        state = state * cdec_ref[...] + jnp.where(blk, kv[c], 0.0)
    state_ref[...] = state
    y = jnp.concatenate([intra[c] + _dot((rq[c].astype(F32) * qdec_ref[...]).astype(BF16), states[c])
                         for c in cs], axis=0)
    yc = y - group_mean(y)
    var = group_mean(yc * yc)
    out_ref[:, 0:256] = (rs_ref[:, 768:1024].astype(F32) * (yc * lax.rsqrt(var + LN_EPS))).astype(BF16)
    out_ref[:, 256:512] = (rs_ref[:, 1024:1280].astype(F32) * jnp.concatenate(mixed, axis=0)).astype(BF16)


def _retention_sgu(rs, l, consts, sgu_w_cat, sgu_b_lane, batch, seq, tb):
    dmat, qdec, kdec, cdec, gavg = consts
    n_blocks = seq // tb
    full = lambda a: pl.BlockSpec(a.shape, lambda b, i: (0,) * a.ndim)
    lay = lambda a: pl.BlockSpec((None,) + a.shape[1:], lambda b, i: (l,) + (0,) * (a.ndim - 1))
    return pl.pallas_call(
        functools.partial(_retsgu_kernel, n_chunks=tb // CHUNK),
        grid=(batch, n_blocks),
        in_specs=[pl.BlockSpec((tb, 1536), lambda b, i: (b * n_blocks + i, 0)),
                  full(dmat), full(qdec), full(kdec), full(cdec), full(gavg),
                  lay(sgu_w_cat), lay(sgu_b_lane)],
        out_specs=pl.BlockSpec((tb, 512), lambda b, i: (b * n_blocks + i, 0)),
        out_shape=jax.ShapeDtypeStruct((batch * seq, 512), BF16),
        scratch_shapes=[pltpu.VMEM((RET_W, RET_W), F32)],
        compiler_params=_cparams("parallel", "arbitrary"),
        name="retention_sgu",
    )(rs, dmat, qdec, kdec, cdec, gavg, sgu_w_cat, sgu_b_lane)


def _retention_consts():
    h = jnp.arange(RET_HEADS, dtype=F32)
    log_gamma = jnp.log1p(-(2.0 ** (-5.0 - h)))
    pos = jnp.arange(CHUNK, dtype=F32)
    diff = pos[:, None] - pos[None, :]
    intra = jnp.where(diff >= 0, jnp.exp(log_gamma[:, None, None] * jnp.maximum(diff, 0.0)), 0.0)
    dmat = intra.reshape(RET_HEADS * CHUNK, CHUNK)
    inner = jnp.exp(log_gamma[None, :] * (CHUNK - 1 - pos)[:, None])
    query = jnp.exp(log_gamma[None, :] * (pos + 1)[:, None])
    kdec = jnp.repeat(inner, RET_DK, axis=1)
    qdec = jnp.repeat(query, RET_DK, axis=1)
    chunk_decay = jnp.repeat(jnp.exp(log_gamma * CHUNK), RET_DK)
    blk = jnp.arange(RET_W)[:, None] // RET_DK == jnp.arange(RET_W)[None, :] // RET_DK
    cdec = jnp.where(blk, chunk_decay[:, None], 0.0)
    gavg = jnp.where(blk, 1.0 / RET_DK, 0.0).astype(BF16)
    return dmat, qdec, kdec, cdec, jnp.concatenate([gavg, gavg], axis=0)


def _flash_kernel(q_ref, k_ref, v_ref, o_ref, sa_ref, sb_ref, m_ref, acc_ref, *, tq, hps):
    i = pl.program_id(2)
    tk = tq // 2
    heads = range(hps)
    cols = [slice(hh * HEAD_PAD, (hh + 1) * HEAD_PAD) for hh in heads]

    def scores(j, hh, rows=slice(None)):
        r0 = pl.multiple_of(j * tk, tk)
        return _dot_nt(q_ref[rows, cols[hh]], k_ref[pl.ds(r0, tk), cols[hh]])

    def update(j, hh, s, rows=slice(None)):
        r0 = pl.multiple_of(j * tk, tk)
        m = m_ref[hh, rows, :]
        s_max = s[:, 0:HEAD_PAD]
        for c in range(1, tk // HEAD_PAD):
            s_max = jnp.maximum(s_max, s[:, c * HEAD_PAD:(c + 1) * HEAD_PAD])
        m_new = jnp.maximum(m, jnp.max(s_max, axis=-1, keepdims=True))
        p = jnp.exp2(s - jnp.concatenate([m_new] * (tk // HEAD_PAD), axis=1)).astype(BF16)
        acc_ref[hh, rows, :] = (jnp.exp2(m - m_new) * acc_ref[hh, rows, :]
                                + _dot(p, v_ref[pl.ds(r0, tk), cols[hh]]))
        m_ref[hh, rows, :] = m_new

    m_ref[...] = jnp.full(m_ref.shape, NEG_BIG, F32)
    acc_ref[...] = jnp.zeros(acc_ref.shape, F32)
    for hh in heads:
        sa_ref[hh] = scores(0, hh)

    def pair(t, carry):
        for hh in heads:
            sb_ref[hh] = scores(2 * t + 1, hh)
        for hh in heads:
            update(2 * t, hh, sa_ref[hh])
        for hh in heads:
            sa_ref[hh] = scores(2 * t + 2, hh)
        for hh in heads:
            update(2 * t + 1, hh, sb_ref[hh])
        return carry

    lax.fori_loop(0, i, pair, 0)
    low = slice(tk, tq)
    visible = (lax.broadcasted_iota(jnp.int32, (tq, tk), 1) <= lax.broadcasted_iota(jnp.int32, (tq, tk), 0))
    visible_low = (lax.broadcasted_iota(jnp.int32, (tk, tk), 1) <= lax.broadcasted_iota(jnp.int32, (tk, tk), 0))
    for hh in heads:
        sb_ref[hh, low, :] = scores(2 * i + 1, hh, low)
        update(2 * i, hh, jnp.where(visible, sa_ref[hh], NEG_BIG))
    for hh in heads:
        update(2 * i + 1, hh, jnp.where(visible_low, sb_ref[hh, low, :], NEG_BIG), low)
    lane = lax.broadcasted_iota(jnp.int32, (tq, HEAD_PAD), 1)
    for pr in range(hps // 2):
        o = []
        for hh in (2 * pr, 2 * pr + 1):
            acc = acc_ref[hh]
            o.append(acc / acc[:, MLA_V:MLA_V + 1])
        o_ref[:, pr * HEAD_PAD:(pr + 1) * HEAD_PAD] = jnp.where(
            lane < MLA_V, o[0], pltpu.roll(o[1], MLA_V, 1)).astype(BF16)


def _flash_attention(q, k, v, batch, seq, tq, hps):
    nq = seq // tq
    return pl.pallas_call(
        functools.partial(_flash_kernel, tq=tq, hps=hps),
        grid=(batch, MLA_HEADS // hps, nq),
        in_specs=[pl.BlockSpec((tq, hps * HEAD_PAD), lambda b, p, i: (b * nq + i, p)),
                  pl.BlockSpec((seq, hps * HEAD_PAD), lambda b, p, i: (b, p)),
                  pl.BlockSpec((seq, hps * HEAD_PAD), lambda b, p, i: (b, p))],
        out_specs=pl.BlockSpec((tq, hps * MLA_V), lambda b, p, i: (b * nq + i, p)),
        out_shape=jax.ShapeDtypeStruct((batch * seq, MLA_W), BF16),
        scratch_shapes=[pltpu.VMEM((hps, tq, tq // 2), F32), pltpu.VMEM((hps, tq, tq // 2), F32),
                        pltpu.VMEM((hps, tq, HEAD_PAD), F32), pltpu.VMEM((hps, tq, HEAD_PAD), F32)],
        compiler_params=_cparams("parallel", "parallel", "arbitrary"),
        name="flash_attention",
    )(q, k, v)


def _memkv_kernel(mem_ref, wk_ref, wv_ref, k_ref, v_ref):
    m = mem_ref[...].astype(BF16)
    k_ref[...] = _dot(m, wk_ref[...]).astype(BF16)
    v_ref[...] = _dot(m, wv_ref[...]).astype(BF16)


def _memory_kv(mem2d, l, wk, wv, batch):
    lay = lambda a: pl.BlockSpec((None,) + a.shape[1:], lambda b: (l,) + (0,) * (a.ndim - 1))
    blk = pl.BlockSpec((MEM_LEN, XA_W), lambda b: (b, 0))
    return pl.pallas_call(
        _memkv_kernel,
        grid=(batch,),
        in_specs=[pl.BlockSpec((MEM_LEN, D_MODEL), lambda b: (b, 0)), lay(wk), lay(wv)],
        out_specs=[blk, blk],
        out_shape=[jax.ShapeDtypeStruct((batch * MEM_LEN, XA_W), BF16)] * 2,
        compiler_params=_cparams("parallel"),
        name="memory_kv",
    )(mem2d, wk, wv)


def _route_rows(scores, biased):
    s = [scores[e:e + 1, :] for e in range(N_EXPERTS)]
    b = [biased[e:e + 1, :] for e in range(N_EXPERTS)]
    group_scores = []
    for g in range(N_GROUPS):
        b0, b1, b2, b3 = b[4 * g:4 * g + 4]
        hi01, lo01 = jnp.maximum(b0, b1), jnp.minimum(b0, b1)
        hi23, lo23 = jnp.maximum(b2, b3), jnp.minimum(b2, b3)
        top1 = jnp.maximum(hi01, hi23)
        top2 = jnp.maximum(jnp.minimum(hi01, hi23), jnp.maximum(lo01, lo23))
        group_scores.append(top1 + top2)
    best = group_scores[0]
    sel = jnp.zeros_like(best, dtype=jnp.int32)
    for g in range(1, N_GROUPS):
        upd = group_scores[g] > best
        sel = jnp.where(upd, g, sel)
        best = jnp.where(upd, group_scores[g], best)

    def pick(rows, j):
        out = rows[j]
        for g in range(1, N_GROUPS):
            out = jnp.where(sel == g, rows[4 * g + j], out)
        return out

    ib = [pick(b, j) for j in range(EXPERTS_PER_GROUP)]
    isc = [pick(s, j) for j in range(EXPERTS_PER_GROUP)]

    def argmax4(vals):
        bv, bi = vals[0], jnp.zeros_like(sel)
        for j in range(1, EXPERTS_PER_GROUP):
            upd = vals[j] > bv
            bi = jnp.where(upd, j, bi)
            bv = jnp.where(upd, vals[j], bv)
        return bi

    i1 = argmax4(ib)
    i2 = argmax4([jnp.where(i1 == j, -jnp.inf, ib[j]) for j in range(EXPERTS_PER_GROUP)])

    def take(vals, idx):
        out = vals[0]
        for j in range(1, EXPERTS_PER_GROUP):
            out = jnp.where(idx == j, vals[j], out)
        return out

    g1, g2 = take(isc, i1), take(isc, i2)
    den = g1 + g2
    g1, g2 = g1 / den, g2 / den
    return sel, i1, i2, g1, g2


def _mix_xa_kernel(x_ref, rs_ref, at_ref, wout_ref, g1_ref, b1_ref, wq_ref, km_ref, vm_ref, wo_ref,
                   g2_ref, b2_ref, rw_ref, rb_ref, tri_ref, x2_ref, pos_ref, gate_rows_ref,
                   cnt_ref, base_ref, pk_ref, *, tm, tiles_per_block):
    @pl.when(pl.program_id(1) % tiles_per_block == 0)
    def _():
        base_ref[...] = jnp.zeros_like(base_ref)

    w_hi, w_lo = _split_bf16(rw_ref[...])

    n_chunks = max(tm // MIX_CHUNK, 1)
    chunk = tm // n_chunks
    rows = [slice(c * chunk, (c + 1) * chunk) for c in range(n_chunks)]

    def skewed(matmul, finish):
        out = []
        prev = matmul(0)
        for c in range(n_chunks):
            nxt = matmul(c + 1) if c + 1 < n_chunks else None
            out.append(finish(c, prev))
            prev = nxt
        return out

    x1 = jnp.concatenate(skewed(
        lambda c: _dot(jnp.concatenate([rs_ref[rows[c], :], at_ref[rows[c], :]], axis=1), wout_ref[...]),
        lambda c, mix: _layer_norm(ALPHA * x_ref[rows[c], :] + mix, g1_ref[...], b1_ref[...])), axis=0)
    q = (_dot(x1.astype(BF16), wq_ref[...]) * (XA_DIM ** -0.5 * math.log2(math.e))).astype(BF16)
    cols = [slice(h * XA_DIM, (h + 1) * XA_DIM) for h in range(XA_HEADS)]
    s = [_dot_nt(q[:, sl], km_ref[:, sl]) for sl in cols]
    p = [jnp.exp2(sh - jnp.max(sh, axis=-1, keepdims=True)) for sh in s]
    o = [_dot(ph.astype(BF16), vm_ref[:, sl]) for ph, sl in zip(p, cols)]
    heads = jnp.concatenate([oh / jnp.sum(ph, axis=-1, keepdims=True) for oh, ph in zip(o, p)],
                            axis=1).astype(BF16)

    def norm2_router(c, xa):
        x2 = _layer_norm(ALPHA * x1[rows[c], :] + xa, g2_ref[...], b2_ref[...])
        x2_ref[rows[c], :] = x2
        x_hi, x_lo = _split_bf16(x2)
        return _dot_nt(w_hi, x_hi) + (_dot_nt(w_hi, x_lo) + _dot_nt(w_lo, x_hi))

    logits = jnp.concatenate(skewed(lambda c: _dot(heads[rows[c], :], wo_ref[...]), norm2_router), axis=1)
    scores = 1.0 / (1.0 + jnp.exp(-logits))
    sel, i1, i2, gate1, gate2 = _route_rows(scores, scores + rb_ref[...])
    in_group = [jnp.where(i1 == j, gate1, 0.0) + jnp.where(i2 == j, gate2, 0.0)
                for j in range(EXPERTS_PER_GROUP)]
    gate_rows_ref[...] = jnp.concatenate(in_group + [jnp.zeros((128 - EXPERTS_PER_GROUP, tm), F32)], axis=0).T
    hit = lax.broadcasted_iota(jnp.int32, (N_EXPERTS, tm), 0) == sel
    cnt = jnp.where(hit, 1.0, 0.0)
    base = base_ref[...]
    before = _dot(cnt.astype(BF16), tri_ref[...]) - cnt + base[:, 0:1]
    rank = jnp.sum(jnp.where(hit, before, 0.0), axis=0, keepdims=True).astype(jnp.int32)
    base = base + jnp.sum(cnt, axis=1, keepdims=True)
    base_ref[...] = base
    cnt_ref[...] = base
    ti = pl.program_id(1) % tiles_per_block
    pk_ref[ti] = sel * RANK_RADIX + rank

    @pl.when(ti == tiles_per_block - 1)
    def _():
        starts = []
        start = jnp.zeros((1, 128), F32)
        for g in range(N_GROUPS):
            starts.append(start.astype(jnp.int32)[:, 0:1])
            tiles = jnp.floor((base[g:g + 1, :] + (MOE_TILE - 1)) * (1.0 / MOE_TILE))
            start = start + tiles * MOE_TILE
        for tj in range(tiles_per_block):
            pk = pk_ref[tj]
            g_of = pk >> RANK_BITS
            slot = pk & (RANK_RADIX - 1)
            for g in range(N_GROUPS):
                slot = slot + jnp.where(g_of == g, starts[g], 0)
            pos_ref[:, tj * tm:(tj + 1) * tm] = slot


def _mix_xa(x2d, rs, at, km, vm, l, w_out, g1, b1, wq, wo, g2, b2, rw_t, rb_col, tri, batch, seq, tm, nblk):
    t = x2d.shape[0]
    nb = seq // tm
    n_tiles = t // tm
    tpb = nblk // tm
    row = lambda w: pl.BlockSpec((tm, w), lambda b, i: (b * nb + i, 0))
    lay = lambda a: pl.BlockSpec((None,) + a.shape[1:], lambda b, i: (l,) + (0,) * (a.ndim - 1))
    full = lambda a: pl.BlockSpec(a.shape, lambda b, i: (0,) * a.ndim)
    memb = pl.BlockSpec((MEM_LEN, XA_W), lambda b, i: (b, 0))
    blk = pl.BlockSpec((None, 1, nblk), lambda b, i: ((b * nb + i) // tpb, 0, 0))
    blk_i = jax.ShapeDtypeStruct((t // nblk, 1, nblk), jnp.int32)
    return pl.pallas_call(
        functools.partial(_mix_xa_kernel, tm=tm, tiles_per_block=tpb),
        grid=(batch, nb),
        in_specs=[row(D_MODEL), row(512), row(512), lay(w_out), lay(g1), lay(b1), lay(wq), memb, memb,
                  lay(wo), lay(g2), lay(b2), full(rw_t), full(rb_col), full(tri)],
        out_specs=[row(D_MODEL), blk, row(128),
                   pl.BlockSpec((None, N_EXPERTS, 128), lambda b, i: ((b * nb + i) // tpb, 0, 0))],
        out_shape=[jax.ShapeDtypeStruct((t, D_MODEL), F32), blk_i, jax.ShapeDtypeStruct((t, 128), F32),
                   jax.ShapeDtypeStruct((t // nblk, N_EXPERTS, 128), F32)],
        scratch_shapes=[pltpu.VMEM((N_EXPERTS, 128), F32), pltpu.VMEM((tpb, 1, tm), jnp.int32)],
        compiler_params=_cparams("parallel", "arbitrary"),
        name="mix_xattn_router",
    )(x2d, rs, at, w_out, g1, b1, wq, km, vm, wo, g2, b2, rw_t, rb_col, tri)


def _group_weights_kernel(w_ref, o_ref):
    for e in range(EXPERTS_PER_GROUP):
        o_ref[:, e * D_EXPERT:(e + 1) * D_EXPERT] = w_ref[e].astype(BF16)


def _group_weights(w):
    depth = w.shape[0]
    return pl.pallas_call(
        _group_weights_kernel,
        grid=(depth, N_GROUPS),
        in_specs=[pl.BlockSpec((None, EXPERTS_PER_GROUP, D_MODEL, D_EXPERT), lambda l, g: (l, g, 0, 0))],
        out_specs=pl.BlockSpec((None, None, D_MODEL, EXPERTS_PER_GROUP * D_EXPERT), lambda l, g: (l, g, 0, 0)),
        out_shape=jax.ShapeDtypeStruct((depth, N_GROUPS, D_MODEL, EXPERTS_PER_GROUP * D_EXPERT), BF16),
        compiler_params=_cparams("parallel", "parallel"),
        name="group_weights",
    )(w)


def _moe_kernel(cnt_ref, pos_ref, x_ref, gates_ref, wg_ref, wu_ref, wd_ref, g_ref, b_ref,
                o_ref, xg_ref, xs_ref, ys_ref, og_ref, off_ref, ntile_ref, *, ts, k, mp):
    s = pl.program_id(1)
    sg = ts + 8
    unroll = 8

    @pl.when((s == 0) & (pl.program_id(0) == 0))
    def _():
        xs_ref[...] = jnp.zeros_like(xs_ref)
        xg_ref[...] = jnp.zeros_like(xg_ref)

    @pl.when(s == 0)
    def _():
        start = jnp.int32(0)
        for g in range(N_GROUPS):
            tiles = (cnt_ref[0, g] + (MOE_TILE - 1)) // MOE_TILE
            off_ref[g] = start
            ntile_ref[g] = tiles
            start = start + tiles * MOE_TILE

    @pl.when(s < k)
    def _():
        x = x_ref[...]
        half = D_MODEL // 2
        hi = pltpu.bitcast(x[:, :half].astype(BF16).astype(F32), jnp.uint32)
        lo = pltpu.bitcast(x[:, half:].astype(BF16).astype(F32), jnp.uint32)
        words = hi | (lo >> 16)
        for c in range(4):
            xg_ref[c * sg:c * sg + ts, :] = words[:, c * 128:(c + 1) * 128]
        xg_ref[4 * sg:4 * sg + ts, :] = pltpu.bitcast(gates_ref[...], jnp.uint32)

        def dispatch(tt, carry):
            for u in range(unroll):
                tl = tt * unroll + u
                xs_ref[pl.ds(pos_ref[s * ts + tl], 8, stride=mp), :] = xg_ref[pl.ds(tl, 8, stride=sg), :]
            return carry

        lax.fori_loop(0, ts // unroll, dispatch, 0)

    @pl.when((s >= k) & (s < k + N_GROUPS))
    def _():
        g = s - k
        seg = off_ref[g]

        def row_tiles(tiles):
            r0 = [pl.multiple_of(seg + i * MOE_TILE, MOE_TILE) for i in tiles]
            xb, weight = [], []
            for r in r0:
                words = [xs_ref[pl.ds(c * mp + r, MOE_TILE), :] for c in range(4)]
                gates = pltpu.bitcast(xs_ref[pl.ds(4 * mp + r, MOE_TILE), :], F32)
                xb.append(jnp.concatenate(
                    [pltpu.bitcast(w & jnp.uint32(0xFFFF0000), F32).astype(BF16) for w in words]
                    + [pltpu.bitcast(w << 16, F32).astype(BF16) for w in words], axis=1))
                weight.append(jnp.concatenate([jnp.broadcast_to(gates[:, e:e + 1], (MOE_TILE, D_EXPERT))
                                               for e in range(EXPERTS_PER_GROUP)], axis=1))
            gate_act = [_dot(x, wg_ref[...]) for x in xb]
            up_act = [_dot(x, wu_ref[...]) for x in xb]
            hid = [(_silu(a) * u * w).astype(BF16) for a, u, w in zip(gate_act, up_act, weight)]
            y = [_dot(h, wd_ref[...]) for h in hid]
            for r, yt in zip(r0, y):
                for j in range(8):
                    ys_ref[pl.ds(j * mp + r, MOE_TILE), :] = yt[:, j * 128:(j + 1) * 128]

        def tile_pair(i, carry):
            row_tiles((2 * i, 2 * i + 1))
            return carry

        def tile_last(i, carry):
            row_tiles((n_tiles - 1,))
            return carry

        n_tiles = ntile_ref[g]
        lax.fori_loop(0, n_tiles // 2, tile_pair, 0)
        lax.fori_loop(0, n_tiles % 2, tile_last, 0)

    @pl.when(s >= k + N_GROUPS)
    def _():
        sub = s - (k + N_GROUPS)

        base = sub * ts
        n_chunks = ts // MOE_TILE

        def gather(c):
            for tl in range(c * MOE_TILE, (c + 1) * MOE_TILE):
                og_ref[pl.ds(tl, 8, stride=sg), :] = ys_ref[pl.ds(pos_ref[base + tl], 8, stride=mp), :]

        def norm(c):
            rows = slice(c * MOE_TILE, (c + 1) * MOE_TILE)
            ffn = jnp.concatenate([og_ref[j * sg + c * MOE_TILE:j * sg + (c + 1) * MOE_TILE, :]
                                   for j in range(8)], axis=1)
            o_ref[rows, :] = _layer_norm(ALPHA * x_ref[rows, :] + ffn, g_ref[...], b_ref[...])

        gather(0)
        for c in range(n_chunks):
            if c + 1 < n_chunks:
                gather(c + 1)
            norm(c)


def _moe(x2, cnt_blk, pos, gate_rows, l, wg, wu, wd, g, b, nblk, ts):
    t = x2.shape[0]
    k = nblk // ts
    steps = 2 * k + N_GROUPS
    mp = nblk + N_GROUPS * MOE_TILE + 8
    lay = lambda a: pl.BlockSpec((None,) + a.shape[1:], lambda i, s: (l,) + (0,) * (a.ndim - 1))
    grp = lambda a: pl.BlockSpec((None, None) + a.shape[2:],
                                 lambda i, s: (l, jnp.clip(s - k, 0, N_GROUPS - 1), 0, 0))
    x_map = lambda i, s: (i * k + jnp.where(s < k, s, jnp.where(s < k + N_GROUPS, k - 1, s - k - N_GROUPS)), 0)
    g_map = lambda i, s: (i * k + jnp.minimum(s, k - 1), 0)
    o_map = lambda i, s: (i * k + jnp.maximum(s - k - N_GROUPS, 0), 0)
    return pl.pallas_call(
        functools.partial(_moe_kernel, ts=ts, k=k, mp=mp),
        grid=(t // nblk, steps),
        in_specs=[pl.BlockSpec((None, 1, N_EXPERTS), lambda i, s: (i, 0, 0), memory_space=pltpu.SMEM),
                  pl.BlockSpec((nblk,), lambda i, s: (i,), memory_space=pltpu.SMEM),
                  pl.BlockSpec((ts, D_MODEL), x_map), pl.BlockSpec((ts, 128), g_map),
                  grp(wg), grp(wu), grp(wd), lay(g), lay(b)],
        out_specs=pl.BlockSpec((ts, D_MODEL), o_map),
        out_shape=jax.ShapeDtypeStruct((t, D_MODEL), F32),
        scratch_shapes=[pltpu.VMEM((8 * (ts + 8), 128), jnp.uint32), pltpu.VMEM((8 * mp, 128), jnp.uint32),
                        pltpu.VMEM((8 * mp, 128), F32), pltpu.VMEM((8 * (ts + 8), 128), F32),
                        pltpu.SMEM((N_GROUPS,), jnp.int32), pltpu.SMEM((N_GROUPS,), jnp.int32)],
        compiler_params=_cparams("arbitrary", "arbitrary"),
        name="moe_experts",
    )(cnt_blk, pos, x2, gate_rows, wg, wu, wd, g, b)


def _tile(n, pref):
    t = min(n, pref)
    assert n % t == 0, (n, t)
    return t


def kernel(x, mem, positions, w_in, w_out, sgu_ln_g, sgu_ln_b, sgu_w, sgu_b, mla_q_norm_g, mla_w_uq, mla_kv_norm_g, mla_w_ukv, xa_wq, xa_wk, xa_wv, xa_wo, ln_mix_g, ln_mix_b, ln_xa_g, ln_xa_b, ln_moe_g, ln_moe_b, router_w, router_bias, expert_w_gate, expert_w_up, expert_w_down):
    batch, seq, _ = x.shape
    depth = w_in.shape[0]
    t = batch * seq
    assert seq % CHUNK == 0

    w_in_b = w_in.astype(BF16)
    w_in_p = jnp.concatenate(
        [w_in_b[:, :, :1920], jnp.zeros((depth, D_MODEL, 64), BF16), w_in_b[:, :, 1920:1952],
         jnp.zeros((depth, D_MODEL, 32), BF16)], axis=2)
    w_uq_p = jnp.pad(mla_w_uq.astype(BF16).reshape(depth, MLA_Q_RANK, MLA_HEADS, MLA_NOPE + MLA_ROPE),
                     ((0, 0), (0, 0), (0, 0), (0, HEAD_PAD - MLA_NOPE - MLA_ROPE))
                     ).reshape(depth, MLA_Q_RANK, MLA_HEADS * HEAD_PAD)
    ukv = mla_w_ukv.astype(BF16).reshape(depth, MLA_KV_RANK, MLA_HEADS, MLA_NOPE + MLA_V)
    w_uk_p = jnp.pad(ukv[..., :MLA_NOPE], ((0, 0), (0, 0), (0, 0), (0, HEAD_PAD - MLA_NOPE))
                     ).reshape(depth, MLA_KV_RANK, MLA_HEADS * HEAD_PAD)
    w_uv_p = jnp.pad(ukv[..., MLA_NOPE:], ((0, 0), (0, 0), (0, 0), (0, HEAD_PAD - MLA_V))
                     ).reshape(depth, MLA_KV_RANK, MLA_HEADS * HEAD_PAD)
    w_ukv_p = jnp.concatenate([w_uk_p, w_uv_p], axis=2)
    w_out_b = w_out.astype(BF16)
    wq_b, wk_b, wv_b, wo_b = (a.astype(BF16) for a in (xa_wq, xa_wk, xa_wv, xa_wo))
    wg_b = _group_weights(expert_w_gate)
    wu_b = _group_weights(expert_w_up)
    wd_b = expert_w_down.reshape(depth, N_GROUPS, EXPERTS_PER_GROUP * D_EXPERT, D_MODEL).astype(BF16)
    vec = lambda a: a.reshape(depth, 1, a.shape[-1])
    sgu_w_cat = jnp.transpose(sgu_w, (0, 2, 1, 3)).reshape(depth, CHUNK, SGU_GROUPS * CHUNK)
    sgu_b_lane = jnp.repeat(jnp.transpose(sgu_b, (0, 2, 1)), SGU_W // SGU_GROUPS, axis=2)
    rw_t = router_w.T
    rb_col = router_bias.reshape(N_EXPERTS, 1)
    consts = _retention_consts()

    x2d = x.reshape(t, D_MODEL)
    mem2d = mem.reshape(batch * MEM_LEN, D_MODEL)
    cr, sr, cm, sm = _rope_tables(positions, _tile(t // 2, 512))

    tm_proj = _tile(t, 1024)
    tb = _tile(seq, 1024)
    tq = _tile(seq, 1024)
    tm_mix = _tile(seq, 1024)
    nblk = _tile(seq, 2048)
    ts_moe = _tile(nblk, 512)
    tri = (jnp.arange(tm_mix)[:, None] <= jnp.arange(tm_mix)[None, :]).astype(BF16)
    for l in range(depth):
        rs, q, k, v = _projections(x2d, l, w_in_p, w_uq_p, w_ukv_p, vec(sgu_ln_g), vec(sgu_ln_b),
                                   vec(mla_q_norm_g), vec(mla_kv_norm_g), cr, sr, cm, sm, tm_proj)
        retsgu = _retention_sgu(rs, l, consts, sgu_w_cat, sgu_b_lane, batch, seq, tb)
        attn = _flash_attention(q, k, v, batch, seq, tq, 4)
        km, vm = _memory_kv(mem2d, l, wk_b, wv_b, batch)
        x2, pos, gate_rows, cnts = _mix_xa(
            x2d, retsgu, attn, km, vm, l, w_out_b, vec(ln_mix_g), vec(ln_mix_b), wq_b, wo_b,
            vec(ln_xa_g), vec(ln_xa_b), rw_t, rb_col, tri, batch, seq, tm_mix, nblk)
        cnt_blk = cnts[:, :, 0].astype(jnp.int32).reshape(-1, 1, N_EXPERTS)
        x2d = _moe(x2, cnt_blk, pos.reshape(t), gate_rows, l,
                   wg_b, wu_b, wd_b, vec(ln_moe_g), vec(ln_moe_b), nblk, ts_moe)
    return x2d.reshape(batch, seq, D_MODEL)
```

```python
import functools
import math

import jax
import jax.numpy as jnp
from jax import lax
from jax.experimental import pallas as pl
from jax.experimental.pallas import tpu as pltpu

F32 = jnp.float32
BF16 = jnp.bfloat16

D_MODEL = 1024
DEPTH = 4
MEM_LEN = 256
ROPE_THETA = 10000.0

RET_HEADS = 4
RET_DK = 64
RET_W = 256
CHUNK = 128

SGU_GROUPS = 4
SGU_W = 256

MLA_HEADS = 8
MLA_Q_RANK = 256
MLA_KV_RANK = 128
MLA_NOPE = 64
MLA_ROPE = 32
MLA_V = 64
MLA_W = MLA_HEADS * MLA_V
HEAD_PAD = 128

XA_HEADS = 4
XA_DIM = 128
XA_W = XA_HEADS * XA_DIM

N_EXPERTS = 16
N_GROUPS = 4
EXPERTS_PER_GROUP = 4
D_EXPERT = 256

ALPHA = (2 * DEPTH) ** 0.25
LN_EPS = 1e-5
IN_PAD = 2048
NEG_BIG = -1e30
PROJ_CHUNK = 256
MIX_CHUNK = 256
MOE_TILE = 128
RANK_BITS = 16
RANK_RADIX = 1 << RANK_BITS

VMEM_LIMIT = 56 * 1024 * 1024


def _cparams(*sem):
    return pltpu.CompilerParams(dimension_semantics=sem, vmem_limit_bytes=VMEM_LIMIT)


def _dot(a, b):
    return jnp.dot(a, b, preferred_element_type=F32)


def _dot_nt(a, b):
    return lax.dot_general(a, b, (((1,), (1,)), ((), ())), preferred_element_type=F32)


def _dot_tn(a, b):
    return lax.dot_general(a, b, (((0,), (0,)), ((), ())), preferred_element_type=F32)


def _layer_norm(z, g, b):
    mu = jnp.mean(z, axis=-1, keepdims=True)
    zc = z - mu
    var = jnp.mean(zc * zc, axis=-1, keepdims=True)
    return zc * lax.rsqrt(var + LN_EPS) * g + b


def _rms_norm(z, g):
    ms = jnp.mean(z * z, axis=-1, keepdims=True)
    return z * lax.rsqrt(ms + LN_EPS) * g


def _silu(z):
    return z / (1.0 + jnp.exp(-z))


def _rope(x, cos, sin_signed, half):
    w = x.shape[-1]
    lane = lax.broadcasted_iota(jnp.int32, x.shape, 1)
    rot = jnp.where((lane & half) == 0, pltpu.roll(x, w - half, 1), pltpu.roll(x, half, 1))
    return x * cos + rot * sin_signed


def _split_bf16(x):
    hi = x.astype(BF16)
    lo = (x - hi.astype(F32)).astype(BF16)
    return hi, lo


def _tables_kernel(pos_a_ref, pos_b_ref, invf_ref, cr_ref, sr_ref, cm_ref, sm_ref):
    lane = lax.broadcasted_iota(jnp.int32, (pos_a_ref.shape[0], 128), 1)
    pos = jnp.where(lane < 64, pos_a_ref[...].astype(F32), pos_b_ref[...].astype(F32))
    ang = pos * invf_ref[...]
    c_both = jnp.cos(ang)
    s_both = jnp.sin(ang)

    def tile32(v):
        v0 = jnp.where(lane < 32, v, 0.0)
        v1 = v0 + pltpu.roll(v0, 32, 1)
        return v1 + pltpu.roll(v1, 64, 1)

    for half in range(2):
        c = c_both if half == 0 else pltpu.roll(c_both, 64, 1)
        s = s_both if half == 0 else pltpu.roll(s_both, 64, 1)
        ct = tile32(c)
        st = tile32(s) * jnp.where((lane & 32) == 0, -1.0, 1.0)
        cr_ref[half] = jnp.concatenate([ct, ct], axis=1)
        sr_ref[half] = jnp.concatenate([st, st], axis=1)
        in_src = (lane >= 32) & (lane < 48)
        cmv = jnp.where(in_src, c, 0.0)
        smv = jnp.where(in_src, s, 0.0)
        in_dst = (lane >= 64) & (lane < 96)
        cm_ref[half] = jnp.where(in_dst, pltpu.roll(cmv, 32, 1) + pltpu.roll(cmv, 48, 1), 1.0)
        sm_ref[half] = jnp.where(in_dst, pltpu.roll(smv, 48, 1) - pltpu.roll(smv, 32, 1), 0.0)


def _rope_tables(positions, tm):
    t = positions.size
    half_t = t // 2
    n = half_t // tm
    pos = positions.reshape(t, 1)
    fr = ROPE_THETA ** (-jnp.arange(0, RET_DK, 2, dtype=F32) / RET_DK)
    fm = ROPE_THETA ** (-jnp.arange(0, MLA_ROPE, 2, dtype=F32) / MLA_ROPE)
    one = jnp.concatenate([fr, fm, jnp.zeros((64 - 48,), F32)])
    invf = jnp.concatenate([one, one]).reshape(1, 128)
    out = lambda w: pl.BlockSpec((2, tm, w), lambda i: (0, i, 0))
    tables = pl.pallas_call(
        _tables_kernel,
        grid=(n,),
        in_specs=[pl.BlockSpec((tm, 1), lambda i: (i, 0)),
                  pl.BlockSpec((tm, 1), lambda i: (i + n, 0)),
                  pl.BlockSpec((1, 128), lambda i: (0, 0))],
        out_specs=[out(256), out(256), out(128), out(128)],
        out_shape=[jax.ShapeDtypeStruct((2, half_t, 256), F32), jax.ShapeDtypeStruct((2, half_t, 256), F32),
                   jax.ShapeDtypeStruct((2, half_t, 128), F32), jax.ShapeDtypeStruct((2, half_t, 128), F32)],
        compiler_params=_cparams("parallel"),
        name="rope_tables",
    )(pos, pos, invf)
    return [a.reshape(t, a.shape[-1]) for a in tables]


def _proj_kernel(x_ref, win_ref, wuq_ref, wukv_ref, lng_ref, lnb_ref, qg_ref, kvg_ref,
                 cr_ref, sr_ref, cm_ref, sm_ref, rs_ref, q_ref, k_ref, v_ref, *, q_scale):
    tm = x_ref.shape[0]
    n_chunks = max(tm // PROJ_CHUNK, 1)
    chunk = tm // n_chunks
    lane = lax.broadcasted_iota(jnp.int32, (chunk, MLA_HEADS * HEAD_PAD), 1)
    ones_lane = jnp.where((lane & (HEAD_PAD - 1)) == MLA_V, 1.0, 0.0)

    def in_proj(c):
        rows = slice(c * chunk, (c + 1) * chunk)
        return _dot(x_ref[rows, :].astype(BF16), win_ref[...])

    def finish(c, h):
        rows = slice(c * chunk, (c + 1) * chunk)
        cm = cm_ref[rows, :]
        sm = sm_ref[rows, :]
        cq = _rms_norm(h[:, 1536:1792], qg_ref[...]).astype(BF16)
        q = _dot(cq, wuq_ref[...])
        ckv = _rms_norm(h[:, 1792:1920], kvg_ref[...]).astype(BF16)
        kv = _dot(ckv, wukv_ref[...])
        cr = cr_ref[rows, :]
        sr = sr_ref[rows, :]
        rs_ref[rows, 0:256] = _rope(h[:, 0:256], cr, sr, 32).astype(BF16)
        rs_ref[rows, 256:512] = (_rope(h[:, 256:512], cr, sr, 32) * (RET_DK ** -0.5)).astype(BF16)
        rs_ref[rows, 512:768] = h[:, 512:768].astype(BF16)
        rs_ref[rows, 768:1024] = _silu(h[:, 768:1024]).astype(BF16)
        rs_ref[rows, 1024:1280] = h[:, 1024:1280].astype(BF16)
        rs_ref[rows, 1280:1536] = _layer_norm(h[:, 1280:1536], lng_ref[...], lnb_ref[...]).astype(BF16)
        cm8 = jnp.concatenate([cm] * MLA_HEADS, axis=1)
        sm8 = jnp.concatenate([sm] * MLA_HEADS, axis=1)
        q_ref[rows, :] = (_rope(q, cm8, sm8, 16) * q_scale).astype(BF16)
        kr = _rope(h[:, 1920:2048], cm, sm, 16)
        k_ref[rows, :] = (kv[:, 0:1024] + jnp.concatenate([kr] * MLA_HEADS, axis=1)).astype(BF16)
        v_ref[rows, :] = (kv[:, 1024:2048] + ones_lane).astype(BF16)

    h_prev = in_proj(0)
    for c in range(1, n_chunks):
        h_next = in_proj(c)
        finish(c - 1, h_prev)
        h_prev = h_next
    finish(n_chunks - 1, h_prev)


def _projections(x2d, l, w_in, w_uq, w_ukv, sgu_g, sgu_b, q_g, kv_g, cr, sr, cm, sm, tm):
    t = x2d.shape[0]
    row = lambda w: pl.BlockSpec((tm, w), lambda i: (i, 0))
    lay = lambda a: pl.BlockSpec((None,) + a.shape[1:], lambda i: (l,) + (0,) * (a.ndim - 1))
    q_scale = (MLA_NOPE + MLA_ROPE) ** -0.5 * math.log2(math.e)
    return pl.pallas_call(
        functools.partial(_proj_kernel, q_scale=q_scale),
        grid=(t // tm,),
        in_specs=[row(D_MODEL), lay(w_in), lay(w_uq), lay(w_ukv), lay(sgu_g), lay(sgu_b),
                  lay(q_g), lay(kv_g), row(256), row(256), row(128), row(128)],
        out_specs=[row(1536), row(1024), row(1024), row(1024)],
        out_shape=[jax.ShapeDtypeStruct((t, 1536), BF16)] + [jax.ShapeDtypeStruct((t, 1024), BF16)] * 3,
        compiler_params=_cparams("parallel"),
        name="projections",
    )(x2d, w_in, w_uq, w_ukv, sgu_g, sgu_b, q_g, kv_g, cr, sr, cm, sm)


def _retsgu_kernel(rs_ref, dmat_ref, qdec_ref, kdec_ref, cdec_ref, gavg_ref, sw_ref, sb_ref,
                   out_ref, state_ref, *, n_chunks):
    @pl.when(pl.program_id(1) == 0)
    def _():
        state_ref[...] = jnp.zeros_like(state_ref)

    lane = lax.broadcasted_iota(jnp.int32, (CHUNK, RET_W), 1)
    head_of_lane = lane // RET_DK
    hmask = [head_of_lane == h for h in range(RET_HEADS)]
    row_i = lax.broadcasted_iota(jnp.int32, (CHUNK, SGU_GROUPS * CHUNK), 0)
    col_i = lax.broadcasted_iota(jnp.int32, (CHUNK, SGU_GROUPS * CHUNK), 1)
    sw = jnp.where(row_i >= (col_i & (CHUNK - 1)), sw_ref[...], 0.0).astype(BF16)
    blk = (lax.broadcasted_iota(jnp.int32, (RET_W, RET_W), 0) // RET_DK
           == lax.broadcasted_iota(jnp.int32, (RET_W, RET_W), 1) // RET_DK)
    gavg = gavg_ref[...]
    zero = jnp.zeros((), BF16)

    def group_mean(y):
        hi, lo = _split_bf16(y)
        return _dot(jnp.concatenate([hi, lo], axis=1), gavg)

    cs = range(n_chunks)
    rows = [slice(c * CHUNK, (c + 1) * CHUNK) for c in cs]
    rq = [rs_ref[r, 0:256] for r in rows]
    rk = [rs_ref[r, 256:512] for r in rows]
    rv = [rs_ref[r, 512:768] for r in rows]
    heads_of = lambda a: jnp.concatenate([jnp.where(m, a, zero) for m in hmask], axis=0)

    scores = [_dot_nt(heads_of(rq[c]), rk[c]) * dmat_ref[...] for c in cs]
    kv = [_dot_tn((rk[c].astype(F32) * kdec_ref[...]).astype(BF16), rv[c]) for c in cs]
    mixed = [_dot(sw, heads_of(rs_ref[rows[c], 1280:1536])) + sb_ref[...] for c in cs]
    intra = [_dot(jnp.concatenate([scores[c][h * CHUNK:(h + 1) * CHUNK, :] for h in range(RET_HEADS)],
                                  axis=1).astype(BF16), heads_of(rv[c])) for c in cs]
    states = []
    state = state_ref[...]
    for c in cs:
        states.append(state.astype(BF16))
        state = state * cdec_ref[...] + jnp.where(blk, kv[c], 0.0)
    state_ref[...] = state
    y = jnp.concatenate([intra[c] + _dot((rq[c].astype(F32) * qdec_ref[...]).astype(BF16), states[c])
                         for c in cs], axis=0)
    yc = y - group_mean(y)
    var = group_mean(yc * yc)
    out_ref[:, 0:256] = (rs_ref[:, 768:1024].astype(F32) * (yc * lax.rsqrt(var + LN_EPS))).astype(BF16)
    out_ref[:, 256:512] = (rs_ref[:, 1024:1280].astype(F32) * jnp.concatenate(mixed, axis=0)).astype(BF16)


def _retention_sgu(rs, l, consts, sgu_w_cat, sgu_b_lane, batch, seq, tb):
    dmat, qdec, kdec, cdec, gavg = consts
    n_blocks = seq // tb
    full = lambda a: pl.BlockSpec(a.shape, lambda b, i: (0,) * a.ndim)
    lay = lambda a: pl.BlockSpec((None,) + a.shape[1:], lambda b, i: (l,) + (0,) * (a.ndim - 1))
    return pl.pallas_call(
        functools.partial(_retsgu_kernel, n_chunks=tb // CHUNK),
        grid=(batch, n_blocks),
        in_specs=[pl.BlockSpec((tb, 1536), lambda b, i: (b * n_blocks + i, 0)),
                  full(dmat), full(qdec), full(kdec), full(cdec), full(gavg),
                  lay(sgu_w_cat), lay(sgu_b_lane)],
        out_specs=pl.BlockSpec((tb, 512), lambda b, i: (b * n_blocks + i, 0)),
        out_shape=jax.ShapeDtypeStruct((batch * seq, 512), BF16),
        scratch_shapes=[pltpu.VMEM((RET_W, RET_W), F32)],
        compiler_params=_cparams("parallel", "arbitrary"),
        name="retention_sgu",
    )(rs, dmat, qdec, kdec, cdec, gavg, sgu_w_cat, sgu_b_lane)


def _retention_consts():
    h = jnp.arange(RET_HEADS, dtype=F32)
    log_gamma = jnp.log1p(-(2.0 ** (-5.0 - h)))
    pos = jnp.arange(CHUNK, dtype=F32)
    diff = pos[:, None] - pos[None, :]
    intra = jnp.where(diff >= 0, jnp.exp(log_gamma[:, None, None] * jnp.maximum(diff, 0.0)), 0.0)
    dmat = intra.reshape(RET_HEADS * CHUNK, CHUNK)
    inner = jnp.exp(log_gamma[None, :] * (CHUNK - 1 - pos)[:, None])
    query = jnp.exp(log_gamma[None, :] * (pos + 1)[:, None])
    kdec = jnp.repeat(inner, RET_DK, axis=1)
    qdec = jnp.repeat(query, RET_DK, axis=1)
    chunk_decay = jnp.repeat(jnp.exp(log_gamma * CHUNK), RET_DK)
    blk = jnp.arange(RET_W)[:, None] // RET_DK == jnp.arange(RET_W)[None, :] // RET_DK
    cdec = jnp.where(blk, chunk_decay[:, None], 0.0)
    gavg = jnp.where(blk, 1.0 / RET_DK, 0.0).astype(BF16)
    return dmat, qdec, kdec, cdec, jnp.concatenate([gavg, gavg], axis=0)


def _flash_kernel(q_ref, k_ref, v_ref, o_ref, sa_ref, sb_ref, m_ref, acc_ref, *, tq, hps):
    i = pl.program_id(2)
    tk = tq // 2
    heads = range(hps)
    cols = [slice(hh * HEAD_PAD, (hh + 1) * HEAD_PAD) for hh in heads]

    def scores(j, hh, rows=slice(None)):
        r0 = pl.multiple_of(j * tk, tk)
        return _dot_nt(q_ref[rows, cols[hh]], k_ref[pl.ds(r0, tk), cols[hh]])

    def update(j, hh, s, rows=slice(None)):
        r0 = pl.multiple_of(j * tk, tk)
        m = m_ref[hh, rows, :]
        s_max = s[:, 0:HEAD_PAD]
        for c in range(1, tk // HEAD_PAD):
            s_max = jnp.maximum(s_max, s[:, c * HEAD_PAD:(c + 1) * HEAD_PAD])
        m_new = jnp.maximum(m, jnp.max(s_max, axis=-1, keepdims=True))
        p = jnp.exp2(s - jnp.concatenate([m_new] * (tk // HEAD_PAD), axis=1)).astype(BF16)
        acc_ref[hh, rows, :] = (jnp.exp2(m - m_new) * acc_ref[hh, rows, :]
                                + _dot(p, v_ref[pl.ds(r0, tk), cols[hh]]))
        m_ref[hh, rows, :] = m_new

    m_ref[...] = jnp.full(m_ref.shape, NEG_BIG, F32)
    acc_ref[...] = jnp.zeros(acc_ref.shape, F32)
    for hh in heads:
        sa_ref[hh] = scores(0, hh)

    def pair(t, carry):
        for hh in heads:
            sb_ref[hh] = scores(2 * t + 1, hh)
        for hh in heads:
            update(2 * t, hh, sa_ref[hh])
        for hh in heads:
            sa_ref[hh] = scores(2 * t + 2, hh)
        for hh in heads:
            update(2 * t + 1, hh, sb_ref[hh])
        return carry

    lax.fori_loop(0, i, pair, 0)
    low = slice(tk, tq)
    visible = (lax.broadcasted_iota(jnp.int32, (tq, tk), 1) <= lax.broadcasted_iota(jnp.int32, (tq, tk), 0))
    visible_low = (lax.broadcasted_iota(jnp.int32, (tk, tk), 1) <= lax.broadcasted_iota(jnp.int32, (tk, tk), 0))
    for hh in heads:
        sb_ref[hh, low, :] = scores(2 * i + 1, hh, low)
        update(2 * i, hh, jnp.where(visible, sa_ref[hh], NEG_BIG))
    for hh in heads:
        update(2 * i + 1, hh, jnp.where(visible_low, sb_ref[hh, low, :], NEG_BIG), low)
    lane = lax.broadcasted_iota(jnp.int32, (tq, HEAD_PAD), 1)
    for pr in range(hps // 2):
        o = []
        for hh in (2 * pr, 2 * pr + 1):
            acc = acc_ref[hh]
            o.append(acc / acc[:, MLA_V:MLA_V + 1])
        o_ref[:, pr * HEAD_PAD:(pr + 1) * HEAD_PAD] = jnp.where(
            lane < MLA_V, o[0], pltpu.roll(o[1], MLA_V, 1)).astype(BF16)


def _flash_attention(q, k, v, batch, seq, tq, hps):
    nq = seq // tq
    return pl.pallas_call(
        functools.partial(_flash_kernel, tq=tq, hps=hps),
        grid=(batch, MLA_HEADS // hps, nq),
        in_specs=[pl.BlockSpec((tq, hps * HEAD_PAD), lambda b, p, i: (b * nq + i, p)),
                  pl.BlockSpec((seq, hps * HEAD_PAD), lambda b, p, i: (b, p)),
                  pl.BlockSpec((seq, hps * HEAD_PAD), lambda b, p, i: (b, p))],
        out_specs=pl.BlockSpec((tq, hps * MLA_V), lambda b, p, i: (b * nq + i, p)),
        out_shape=jax.ShapeDtypeStruct((batch * seq, MLA_W), BF16),
        scratch_shapes=[pltpu.VMEM((hps, tq, tq // 2), F32), pltpu.VMEM((hps, tq, tq // 2), F32),
                        pltpu.VMEM((hps, tq, HEAD_PAD), F32), pltpu.VMEM((hps, tq, HEAD_PAD), F32)],
        compiler_params=_cparams("parallel", "parallel", "arbitrary"),
        name="flash_attention",
    )(q, k, v)


def _memkv_kernel(mem_ref, wk_ref, wv_ref, k_ref, v_ref):
    m = mem_ref[...].astype(BF16)
    k_ref[...] = _dot(m, wk_ref[...]).astype(BF16)
    v_ref[...] = _dot(m, wv_ref[...]).astype(BF16)


def _memory_kv(mem2d, l, wk, wv, batch):
    lay = lambda a: pl.BlockSpec((None,) + a.shape[1:], lambda b: (l,) + (0,) * (a.ndim - 1))
    blk = pl.BlockSpec((MEM_LEN, XA_W), lambda b: (b, 0))
    return pl.pallas_call(
        _memkv_kernel,
        grid=(batch,),
        in_specs=[pl.BlockSpec((MEM_LEN, D_MODEL), lambda b: (b, 0)), lay(wk), lay(wv)],
        out_specs=[blk, blk],
        out_shape=[jax.ShapeDtypeStruct((batch * MEM_LEN, XA_W), BF16)] * 2,
        compiler_params=_cparams("parallel"),
        name="memory_kv",
    )(mem2d, wk, wv)


def _route_rows(scores, biased):
    s = [scores[e:e + 1, :] for e in range(N_EXPERTS)]
    b = [biased[e:e + 1, :] for e in range(N_EXPERTS)]
    group_scores = []
    for g in range(N_GROUPS):
        b0, b1, b2, b3 = b[4 * g:4 * g + 4]
        hi01, lo01 = jnp.maximum(b0, b1), jnp.minimum(b0, b1)
        hi23, lo23 = jnp.maximum(b2, b3), jnp.minimum(b2, b3)
        top1 = jnp.maximum(hi01, hi23)
        top2 = jnp.maximum(jnp.minimum(hi01, hi23), jnp.maximum(lo01, lo23))
        group_scores.append(top1 + top2)
    best = group_scores[0]
    sel = jnp.zeros_like(best, dtype=jnp.int32)
    for g in range(1, N_GROUPS):
        upd = group_scores[g] > best
        sel = jnp.where(upd, g, sel)
        best = jnp.where(upd, group_scores[g], best)

    def pick(rows, j):
        out = rows[j]
        for g in range(1, N_GROUPS):
            out = jnp.where(sel == g, rows[4 * g + j], out)
        return out

    ib = [pick(b, j) for j in range(EXPERTS_PER_GROUP)]
    isc = [pick(s, j) for j in range(EXPERTS_PER_GROUP)]

    def argmax4(vals):
        bv, bi = vals[0], jnp.zeros_like(sel)
        for j in range(1, EXPERTS_PER_GROUP):
            upd = vals[j] > bv
            bi = jnp.where(upd, j, bi)
            bv = jnp.where(upd, vals[j], bv)
        return bi

    i1 = argmax4(ib)
    i2 = argmax4([jnp.where(i1 == j, -jnp.inf, ib[j]) for j in range(EXPERTS_PER_GROUP)])

    def take(vals, idx):
        out = vals[0]
        for j in range(1, EXPERTS_PER_GROUP):
            out = jnp.where(idx == j, vals[j], out)
        return out

    g1, g2 = take(isc, i1), take(isc, i2)
    den = g1 + g2
    g1, g2 = g1 / den, g2 / den
    return sel, i1, i2, g1, g2


def _mix_xa_kernel(x_ref, rs_ref, at_ref, wout_ref, g1_ref, b1_ref, wq_ref, km_ref, vm_ref, wo_ref,
                   g2_ref, b2_ref, rw_ref, rb_ref, tri_ref, x2_ref, pos_ref, gate_rows_ref,
                   cnt_ref, base_ref, pk_ref, *, tm, tiles_per_block):
    @pl.when(pl.program_id(1) % tiles_per_block == 0)
    def _():
        base_ref[...] = jnp.zeros_like(base_ref)

    w_hi, w_lo = _split_bf16(rw_ref[...])

    n_chunks = max(tm // MIX_CHUNK, 1)
    chunk = tm // n_chunks
    rows = [slice(c * chunk, (c + 1) * chunk) for c in range(n_chunks)]

    def skewed(matmul, finish):
        out = []
        prev = matmul(0)
        for c in range(n_chunks):
            nxt = matmul(c + 1) if c + 1 < n_chunks else None
            out.append(finish(c, prev))
            prev = nxt
        return out

    x1 = jnp.concatenate(skewed(
        lambda c: _dot(jnp.concatenate([rs_ref[rows[c], :], at_ref[rows[c], :]], axis=1), wout_ref[...]),
        lambda c, mix: _layer_norm(ALPHA * x_ref[rows[c], :] + mix, g1_ref[...], b1_ref[...])), axis=0)
    q = (_dot(x1.astype(BF16), wq_ref[...]) * (XA_DIM ** -0.5 * math.log2(math.e))).astype(BF16)
    cols = [slice(h * XA_DIM, (h + 1) * XA_DIM) for h in range(XA_HEADS)]
    s = [_dot_nt(q[:, sl], km_ref[:, sl]) for sl in cols]
    p = [jnp.exp2(sh - jnp.max(sh, axis=-1, keepdims=True)) for sh in s]
    o = [_dot(ph.astype(BF16), vm_ref[:, sl]) for ph, sl in zip(p, cols)]
    heads = jnp.concatenate([oh / jnp.sum(ph, axis=-1, keepdims=True) for oh, ph in zip(o, p)],
                            axis=1).astype(BF16)

    def norm2_router(c, xa):
        x2 = _layer_norm(ALPHA * x1[rows[c], :] + xa, g2_ref[...], b2_ref[...])
        x2_ref[rows[c], :] = x2
        x_hi, x_lo = _split_bf16(x2)
        return _dot_nt(w_hi, x_hi) + (_dot_nt(w_hi, x_lo) + _dot_nt(w_lo, x_hi))

    logits = jnp.concatenate(skewed(lambda c: _dot(heads[rows[c], :], wo_ref[...]), norm2_router), axis=1)
    scores = 1.0 / (1.0 + jnp.exp(-logits))
    sel, i1, i2, gate1, gate2 = _route_rows(scores, scores + rb_ref[...])
    in_group = [jnp.where(i1 == j, gate1, 0.0) + jnp.where(i2 == j, gate2, 0.0)
                for j in range(EXPERTS_PER_GROUP)]
    gate_rows_ref[...] = jnp.concatenate(in_group + [jnp.zeros((128 - EXPERTS_PER_GROUP, tm), F32)], axis=0).T
    hit = lax.broadcasted_iota(jnp.int32, (N_EXPERTS, tm), 0) == sel
    cnt = jnp.where(hit, 1.0, 0.0)
    base = base_ref[...]
    before = _dot(cnt.astype(BF16), tri_ref[...]) - cnt + base[:, 0:1]
    rank = jnp.sum(jnp.where(hit, before, 0.0), axis=0, keepdims=True).astype(jnp.int32)
    base = base + jnp.sum(cnt, axis=1, keepdims=True)
    base_ref[...] = base
    cnt_ref[...] = base
    ti = pl.program_id(1) % tiles_per_block
    pk_ref[ti] = sel * RANK_RADIX + rank

    @pl.when(ti == tiles_per_block - 1)
    def _():
        starts = []
        start = jnp.zeros((1, 128), F32)
        for g in range(N_GROUPS):
            starts.append(start.astype(jnp.int32)[:, 0:1])
            tiles = jnp.floor((base[g:g + 1, :] + (MOE_TILE - 1)) * (1.0 / MOE_TILE))
            start = start + tiles * MOE_TILE
        for tj in range(tiles_per_block):
            pk = pk_ref[tj]
            g_of = pk >> RANK_BITS
            slot = pk & (RANK_RADIX - 1)
            for g in range(N_GROUPS):
                slot = slot + jnp.where(g_of == g, starts[g], 0)
            pos_ref[:, tj * tm:(tj + 1) * tm] = slot


def _mix_xa(x2d, rs, at, km, vm, l, w_out, g1, b1, wq, wo, g2, b2, rw_t, rb_col, tri, batch, seq, tm, nblk):
    t = x2d.shape[0]
    nb = seq // tm
    n_tiles = t // tm
    tpb = nblk // tm
    row = lambda w: pl.BlockSpec((tm, w), lambda b, i: (b * nb + i, 0))
    lay = lambda a: pl.BlockSpec((None,) + a.shape[1:], lambda b, i: (l,) + (0,) * (a.ndim - 1))
    full = lambda a: pl.BlockSpec(a.shape, lambda b, i: (0,) * a.ndim)
    memb = pl.BlockSpec((MEM_LEN, XA_W), lambda b, i: (b, 0))
    blk = pl.BlockSpec((None, 1, nblk), lambda b, i: ((b * nb + i) // tpb, 0, 0))
    blk_i = jax.ShapeDtypeStruct((t // nblk, 1, nblk), jnp.int32)
    return pl.pallas_call(
        functools.partial(_mix_xa_kernel, tm=tm, tiles_per_block=tpb),
        grid=(batch, nb),
        in_specs=[row(D_MODEL), row(512), row(512), lay(w_out), lay(g1), lay(b1), lay(wq), memb, memb,
                  lay(wo), lay(g2), lay(b2), full(rw_t), full(rb_col), full(tri)],
        out_specs=[row(D_MODEL), blk, row(128),
                   pl.BlockSpec((None, N_EXPERTS, 128), lambda b, i: ((b * nb + i) // tpb, 0, 0))],
        out_shape=[jax.ShapeDtypeStruct((t, D_MODEL), F32), blk_i, jax.ShapeDtypeStruct((t, 128), F32),
                   jax.ShapeDtypeStruct((t // nblk, N_EXPERTS, 128), F32)],
        scratch_shapes=[pltpu.VMEM((N_EXPERTS, 128), F32), pltpu.VMEM((tpb, 1, tm), jnp.int32)],
        compiler_params=_cparams("parallel", "arbitrary"),
        name="mix_xattn_router",
    )(x2d, rs, at, w_out, g1, b1, wq, km, vm, wo, g2, b2, rw_t, rb_col, tri)


def _pack_bf16_pairs(x):
    k = x.shape[1] // 2
    hi = pltpu.bitcast(x[:, :k].astype(BF16).astype(F32), jnp.uint32)
    lo = pltpu.bitcast(x[:, k:].astype(BF16).astype(F32), jnp.uint32)
    return hi | (lo >> 16)


def _unpack_bf16_pairs(words):
    return jnp.concatenate([pltpu.bitcast(w & jnp.uint32(0xFFFF0000), F32) for w in words]
                           + [pltpu.bitcast(w << 16, F32) for w in words], axis=1)


def _group_weights_kernel(w_ref, o_ref):
    for e in range(EXPERTS_PER_GROUP):
        o_ref[:, e * D_EXPERT:(e + 1) * D_EXPERT] = w_ref[e].astype(BF16)


def _group_weights(w):
    depth = w.shape[0]
    return pl.pallas_call(
        _group_weights_kernel,
        grid=(depth, N_GROUPS),
        in_specs=[pl.BlockSpec((None, EXPERTS_PER_GROUP, D_MODEL, D_EXPERT), lambda l, g: (l, g, 0, 0))],
        out_specs=pl.BlockSpec((None, None, D_MODEL, EXPERTS_PER_GROUP * D_EXPERT), lambda l, g: (l, g, 0, 0)),
        out_shape=jax.ShapeDtypeStruct((depth, N_GROUPS, D_MODEL, EXPERTS_PER_GROUP * D_EXPERT), BF16),
        compiler_params=_cparams("parallel", "parallel"),
        name="group_weights",
    )(w)


def _moe_kernel(cnt_ref, pos_ref, x_ref, gates_ref, wg_ref, wu_ref, wd_ref, g_ref, b_ref,
                o_ref, xg_ref, xs_ref, ys_ref, og_ref, off_ref, ntile_ref, *, ts, k, mp):
    s = pl.program_id(1)
    sg = ts + 8
    unroll = 8

    @pl.when((s == 0) & (pl.program_id(0) == 0))
    def _():
        xs_ref[...] = jnp.zeros_like(xs_ref)
        xg_ref[...] = jnp.zeros_like(xg_ref)

    @pl.when(s == 0)
    def _():
        start = jnp.int32(0)
        for g in range(N_GROUPS):
            tiles = (cnt_ref[0, g] + (MOE_TILE - 1)) // MOE_TILE
            off_ref[g] = start
            ntile_ref[g] = tiles
            start = start + tiles * MOE_TILE

    @pl.when(s < k)
    def _():
        words = _pack_bf16_pairs(x_ref[...])
        for c in range(4):
            xg_ref[c * sg:c * sg + ts, :] = words[:, c * 128:(c + 1) * 128]
        xg_ref[4 * sg:4 * sg + ts, :] = pltpu.bitcast(gates_ref[...], jnp.uint32)

        def dispatch(tt, carry):
            for u in range(unroll):
                tl = tt * unroll + u
                xs_ref[pl.ds(pos_ref[s * ts + tl], 8, stride=mp), :] = xg_ref[pl.ds(tl, 8, stride=sg), :]
            return carry

        lax.fori_loop(0, ts // unroll, dispatch, 0)

    @pl.when((s >= k) & (s < k + N_GROUPS))
    def _():
        g = s - k
        seg = off_ref[g]

        def row_tiles(tiles):
            r0 = [pl.multiple_of(seg + i * MOE_TILE, MOE_TILE) for i in tiles]
            xb, weight = [], []
            for r in r0:
                words = [xs_ref[pl.ds(c * mp + r, MOE_TILE), :] for c in range(4)]
                gates = pltpu.bitcast(xs_ref[pl.ds(4 * mp + r, MOE_TILE), :], F32)
                xb.append(_unpack_bf16_pairs(words).astype(BF16))
                weight.append(jnp.concatenate([jnp.broadcast_to(gates[:, e:e + 1], (MOE_TILE, D_EXPERT))
                                               for e in range(EXPERTS_PER_GROUP)], axis=1))
            gate_act = [_dot(x, wg_ref[...]) for x in xb]
            up_act = [_dot(x, wu_ref[...]) for x in xb]
            hid = [(_silu(a) * u * w).astype(BF16) for a, u, w in zip(gate_act, up_act, weight)]
            y = [_pack_bf16_pairs(_dot(h, wd_ref[...])) for h in hid]
            for r, yt in zip(r0, y):
                for j in range(4):
                    ys_ref[pl.ds(j * mp + r, MOE_TILE), :] = yt[:, j * 128:(j + 1) * 128]

        def tile_pair(i, carry):
            row_tiles((2 * i, 2 * i + 1))
            return carry

        def tile_last(i, carry):
            row_tiles((n_tiles - 1,))
            return carry

        n_tiles = ntile_ref[g]
        lax.fori_loop(0, n_tiles // 2, tile_pair, 0)
        lax.fori_loop(0, n_tiles % 2, tile_last, 0)

    @pl.when(s >= k + N_GROUPS)
    def _():
        sub = s - (k + N_GROUPS)

        base = sub * ts
        n_chunks = ts // MOE_TILE

        def gather(c):
            for tl in range(c * MOE_TILE, (c + 1) * MOE_TILE):
                og_ref[pl.ds(tl, 4, stride=sg), :] = ys_ref[pl.ds(pos_ref[base + tl], 4, stride=mp), :]

        def norm(c):
            rows = slice(c * MOE_TILE, (c + 1) * MOE_TILE)
            ffn = _unpack_bf16_pairs([og_ref[j * sg + c * MOE_TILE:j * sg + (c + 1) * MOE_TILE, :]
                                      for j in range(4)])
            o_ref[rows, :] = _layer_norm(ALPHA * x_ref[rows, :] + ffn, g_ref[...], b_ref[...])

        gather(0)
        for c in range(n_chunks):
            if c + 1 < n_chunks:
                gather(c + 1)
            norm(c)


def _moe(x2, cnt_blk, pos, gate_rows, l, wg, wu, wd, g, b, nblk, ts):
    t = x2.shape[0]
    k = nblk // ts
    steps = 2 * k + N_GROUPS
    mp = nblk + N_GROUPS * MOE_TILE + 8
    lay = lambda a: pl.BlockSpec((None,) + a.shape[1:], lambda i, s: (l,) + (0,) * (a.ndim - 1))
    grp = lambda a: pl.BlockSpec((None, None) + a.shape[2:],
                                 lambda i, s: (l, jnp.clip(s - k, 0, N_GROUPS - 1), 0, 0))
    x_map = lambda i, s: (i * k + jnp.where(s < k, s, jnp.where(s < k + N_GROUPS, k - 1, s - k - N_GROUPS)), 0)
    g_map = lambda i, s: (i * k + jnp.minimum(s, k - 1), 0)
    o_map = lambda i, s: (i * k + jnp.maximum(s - k - N_GROUPS, 0), 0)
    return pl.pallas_call(
        functools.partial(_moe_kernel, ts=ts, k=k, mp=mp),
        grid=(t // nblk, steps),
        in_specs=[pl.BlockSpec((None, 1, N_EXPERTS), lambda i, s: (i, 0, 0), memory_space=pltpu.SMEM),
                  pl.BlockSpec((nblk,), lambda i, s: (i,), memory_space=pltpu.SMEM),
                  pl.BlockSpec((ts, D_MODEL), x_map), pl.BlockSpec((ts, 128), g_map),
                  grp(wg), grp(wu), grp(wd), lay(g), lay(b)],
        out_specs=pl.BlockSpec((ts, D_MODEL), o_map),
        out_shape=jax.ShapeDtypeStruct((t, D_MODEL), F32),
        scratch_shapes=[pltpu.VMEM((8 * (ts + 8), 128), jnp.uint32), pltpu.VMEM((8 * mp, 128), jnp.uint32),
                        pltpu.VMEM((4 * mp, 128), jnp.uint32), pltpu.VMEM((4 * (ts + 8), 128), jnp.uint32),
                        pltpu.SMEM((N_GROUPS,), jnp.int32), pltpu.SMEM((N_GROUPS,), jnp.int32)],
        compiler_params=_cparams("arbitrary", "arbitrary"),
        name="moe_experts",
    )(cnt_blk, pos, x2, gate_rows, wg, wu, wd, g, b)


def _tile(n, pref):
    t = min(n, pref)
    assert n % t == 0, (n, t)
    return t


def kernel(x, mem, positions, w_in, w_out, sgu_ln_g, sgu_ln_b, sgu_w, sgu_b, mla_q_norm_g, mla_w_uq, mla_kv_norm_g, mla_w_ukv, xa_wq, xa_wk, xa_wv, xa_wo, ln_mix_g, ln_mix_b, ln_xa_g, ln_xa_b, ln_moe_g, ln_moe_b, router_w, router_bias, expert_w_gate, expert_w_up, expert_w_down):
    batch, seq, _ = x.shape
    depth = w_in.shape[0]
    t = batch * seq
    assert seq % CHUNK == 0

    w_in_b = w_in.astype(BF16)
    w_in_p = jnp.concatenate(
        [w_in_b[:, :, :1920], jnp.zeros((depth, D_MODEL, 64), BF16), w_in_b[:, :, 1920:1952],
         jnp.zeros((depth, D_MODEL, 32), BF16)], axis=2)
    w_uq_p = jnp.pad(mla_w_uq.astype(BF16).reshape(depth, MLA_Q_RANK, MLA_HEADS, MLA_NOPE + MLA_ROPE),
                     ((0, 0), (0, 0), (0, 0), (0, HEAD_PAD - MLA_NOPE - MLA_ROPE))
                     ).reshape(depth, MLA_Q_RANK, MLA_HEADS * HEAD_PAD)
    ukv = mla_w_ukv.astype(BF16).reshape(depth, MLA_KV_RANK, MLA_HEADS, MLA_NOPE + MLA_V)
    w_uk_p = jnp.pad(ukv[..., :MLA_NOPE], ((0, 0), (0, 0), (0, 0), (0, HEAD_PAD - MLA_NOPE))
                     ).reshape(depth, MLA_KV_RANK, MLA_HEADS * HEAD_PAD)
    w_uv_p = jnp.pad(ukv[..., MLA_NOPE:], ((0, 0), (0, 0), (0, 0), (0, HEAD_PAD - MLA_V))
                     ).reshape(depth, MLA_KV_RANK, MLA_HEADS * HEAD_PAD)
    w_ukv_p = jnp.concatenate([w_uk_p, w_uv_p], axis=2)
    w_out_b = w_out.astype(BF16)
    wq_b, wk_b, wv_b, wo_b = (a.astype(BF16) for a in (xa_wq, xa_wk, xa_wv, xa_wo))
    wg_b = _group_weights(expert_w_gate)
    wu_b = _group_weights(expert_w_up)
    wd_b = expert_w_down.reshape(depth, N_GROUPS, EXPERTS_PER_GROUP * D_EXPERT, D_MODEL).astype(BF16)
    vec = lambda a: a.reshape(depth, 1, a.shape[-1])
    sgu_w_cat = jnp.transpose(sgu_w, (0, 2, 1, 3)).reshape(depth, CHUNK, SGU_GROUPS * CHUNK)
    sgu_b_lane = jnp.repeat(jnp.transpose(sgu_b, (0, 2, 1)), SGU_W // SGU_GROUPS, axis=2)
    rw_t = router_w.T
    rb_col = router_bias.reshape(N_EXPERTS, 1)
    consts = _retention_consts()

    x2d = x.reshape(t, D_MODEL)
    mem2d = mem.reshape(batch * MEM_LEN, D_MODEL)
    cr, sr, cm, sm = _rope_tables(positions, _tile(t // 2, 512))

    tm_proj = _tile(t, 1024)
    tb = _tile(seq, 1024)
    tq = _tile(seq, 1024)
    tm_mix = _tile(seq, 1024)
    nblk = _tile(seq, 4096)
    ts_moe = _tile(nblk, 512)
    tri = (jnp.arange(tm_mix)[:, None] <= jnp.arange(tm_mix)[None, :]).astype(BF16)
    for l in range(depth):
        rs, q, k, v = _projections(x2d, l, w_in_p, w_uq_p, w_ukv_p, vec(sgu_ln_g), vec(sgu_ln_b),
                                   vec(mla_q_norm_g), vec(mla_kv_norm_g), cr, sr, cm, sm, tm_proj)
        retsgu = _retention_sgu(rs, l, consts, sgu_w_cat, sgu_b_lane, batch, seq, tb)
        attn = _flash_attention(q, k, v, batch, seq, tq, 4)
        km, vm = _memory_kv(mem2d, l, wk_b, wv_b, batch)
        x2, pos, gate_rows, cnts = _mix_xa(
            x2d, retsgu, attn, km, vm, l, w_out_b, vec(ln_mix_g), vec(ln_mix_b), wq_b, wo_b,
            vec(ln_xa_g), vec(ln_xa_b), rw_t, rb_col, tri, batch, seq, tm_mix, nblk)
        cnt_blk = cnts[:, :, 0].astype(jnp.int32).reshape(-1, 1, N_EXPERTS)
        x2d = _moe(x2, cnt_blk, pos.reshape(t), gate_rows, l,
                   wg_b, wu_b, wd_b, vec(ln_moe_g), vec(ln_moe_b), nblk, ts_moe)
    return x2d.reshape(batch, seq, D_MODEL)
```

```python
import functools
import math

import jax
import jax.numpy as jnp
from jax import lax
from jax.experimental import pallas as pl
from jax.experimental.pallas import tpu as pltpu

F32 = jnp.float32
BF16 = jnp.bfloat16

D_MODEL = 1024
DEPTH = 4
MEM_LEN = 256
ROPE_THETA = 10000.0

RET_HEADS = 4
RET_DK = 64
RET_W = 256
CHUNK = 128

SGU_GROUPS = 4
SGU_W = 256

MLA_HEADS = 8
MLA_Q_RANK = 256
MLA_KV_RANK = 128
MLA_NOPE = 64
MLA_ROPE = 32
MLA_V = 64
MLA_W = MLA_HEADS * MLA_V
HEAD_PAD = 128

XA_HEADS = 4
XA_DIM = 128
XA_W = XA_HEADS * XA_DIM

N_EXPERTS = 16
N_GROUPS = 4
EXPERTS_PER_GROUP = 4
D_EXPERT = 256

ALPHA = (2 * DEPTH) ** 0.25
LN_EPS = 1e-5
IN_PAD = 2048
NEG_BIG = -1e30
PROJ_CHUNK = 256
MIX_CHUNK = 256
MOE_TILE = 128
RANK_BITS = 16
RANK_RADIX = 1 << RANK_BITS

VMEM_LIMIT = 56 * 1024 * 1024


def _cparams(*sem):
    return pltpu.CompilerParams(dimension_semantics=sem, vmem_limit_bytes=VMEM_LIMIT)


def _dot(a, b):
    return jnp.dot(a, b, preferred_element_type=F32)


def _dot_nt(a, b):
    return lax.dot_general(a, b, (((1,), (1,)), ((), ())), preferred_element_type=F32)


def _dot_tn(a, b):
    return lax.dot_general(a, b, (((0,), (0,)), ((), ())), preferred_element_type=F32)


def _layer_norm(z, g, b):
    mu = jnp.mean(z, axis=-1, keepdims=True)
    zc = z - mu
    var = jnp.mean(zc * zc, axis=-1, keepdims=True)
    return zc * lax.rsqrt(var + LN_EPS) * g + b


def _rms_norm(z, g):
    ms = jnp.mean(z * z, axis=-1, keepdims=True)
    return z * lax.rsqrt(ms + LN_EPS) * g


def _silu(z):
    return z / (1.0 + jnp.exp(-z))


def _rope(x, cos, sin_signed, half):
    w = x.shape[-1]
    lane = lax.broadcasted_iota(jnp.int32, x.shape, 1)
    rot = jnp.where((lane & half) == 0, pltpu.roll(x, w - half, 1), pltpu.roll(x, half, 1))
    return x * cos + rot * sin_signed


def _split_bf16(x):
    hi = x.astype(BF16)
    lo = (x - hi.astype(F32)).astype(BF16)
    return hi, lo


def _tables_kernel(pos_a_ref, pos_b_ref, invf_ref, cr_ref, sr_ref, cm_ref, sm_ref):
    lane = lax.broadcasted_iota(jnp.int32, (pos_a_ref.shape[0], 128), 1)
    pos = jnp.where(lane < 64, pos_a_ref[...].astype(F32), pos_b_ref[...].astype(F32))
    ang = pos * invf_ref[...]
    c_both = jnp.cos(ang)
    s_both = jnp.sin(ang)

    def tile32(v):
        v0 = jnp.where(lane < 32, v, 0.0)
        v1 = v0 + pltpu.roll(v0, 32, 1)
        return v1 + pltpu.roll(v1, 64, 1)

    for half in range(2):
        c = c_both if half == 0 else pltpu.roll(c_both, 64, 1)
        s = s_both if half == 0 else pltpu.roll(s_both, 64, 1)
        ct = tile32(c)
        st = tile32(s) * jnp.where((lane & 32) == 0, -1.0, 1.0)
        cr_ref[half] = jnp.concatenate([ct, ct], axis=1)
        sr_ref[half] = jnp.concatenate([st, st], axis=1)
        in_src = (lane >= 32) & (lane < 48)
        cmv = jnp.where(in_src, c, 0.0)
        smv = jnp.where(in_src, s, 0.0)
        in_dst = (lane >= 64) & (lane < 96)
        cm_ref[half] = jnp.where(in_dst, pltpu.roll(cmv, 32, 1) + pltpu.roll(cmv, 48, 1), 1.0)
        sm_ref[half] = jnp.where(in_dst, pltpu.roll(smv, 48, 1) - pltpu.roll(smv, 32, 1), 0.0)


def _rope_tables(positions, tm):
    t = positions.size
    half_t = t // 2
    n = half_t // tm
    pos = positions.reshape(t, 1)
    fr = ROPE_THETA ** (-jnp.arange(0, RET_DK, 2, dtype=F32) / RET_DK)
    fm = ROPE_THETA ** (-jnp.arange(0, MLA_ROPE, 2, dtype=F32) / MLA_ROPE)
    one = jnp.concatenate([fr, fm, jnp.zeros((64 - 48,), F32)])
    invf = jnp.concatenate([one, one]).reshape(1, 128)
    out = lambda w: pl.BlockSpec((2, tm, w), lambda i: (0, i, 0))
    tables = pl.pallas_call(
        _tables_kernel,
        grid=(n,),
        in_specs=[pl.BlockSpec((tm, 1), lambda i: (i, 0)),
                  pl.BlockSpec((tm, 1), lambda i: (i + n, 0)),
                  pl.BlockSpec((1, 128), lambda i: (0, 0))],
        out_specs=[out(256), out(256), out(128), out(128)],
        out_shape=[jax.ShapeDtypeStruct((2, half_t, 256), F32), jax.ShapeDtypeStruct((2, half_t, 256), F32),
                   jax.ShapeDtypeStruct((2, half_t, 128), F32), jax.ShapeDtypeStruct((2, half_t, 128), F32)],
        compiler_params=_cparams("parallel"),
        name="rope_tables",
    )(pos, pos, invf)
    return [a.reshape(t, a.shape[-1]) for a in tables]


def _proj_kernel(x_ref, win_ref, wuq_ref, wukv_ref, lng_ref, lnb_ref, qg_ref, kvg_ref,
                 cr_ref, sr_ref, cm_ref, sm_ref, rs_ref, q_ref, k_ref, v_ref, *, q_scale):
    tm = x_ref.shape[0]
    n_chunks = max(tm // PROJ_CHUNK, 1)
    chunk = tm // n_chunks
    lane = lax.broadcasted_iota(jnp.int32, (chunk, MLA_HEADS * HEAD_PAD), 1)
    ones_lane = jnp.where((lane & (HEAD_PAD - 1)) == MLA_V, 1.0, 0.0)

    def in_proj(c):
        rows = slice(c * chunk, (c + 1) * chunk)
        return _dot(x_ref[rows, :].astype(BF16), win_ref[...])

    def finish(c, h):
        rows = slice(c * chunk, (c + 1) * chunk)
        cm = cm_ref[rows, :]
        sm = sm_ref[rows, :]
        cq = _rms_norm(h[:, 1536:1792], qg_ref[...]).astype(BF16)
        q = _dot(cq, wuq_ref[...])
        ckv = _rms_norm(h[:, 1792:1920], kvg_ref[...]).astype(BF16)
        kv = _dot(ckv, wukv_ref[...])
        cr = cr_ref[rows, :]
        sr = sr_ref[rows, :]
        rs_ref[rows, 0:256] = _rope(h[:, 0:256], cr, sr, 32).astype(BF16)
        rs_ref[rows, 256:512] = (_rope(h[:, 256:512], cr, sr, 32) * (RET_DK ** -0.5)).astype(BF16)
        rs_ref[rows, 512:768] = h[:, 512:768].astype(BF16)
        rs_ref[rows, 768:1024] = _silu(h[:, 768:1024]).astype(BF16)
        rs_ref[rows, 1024:1280] = h[:, 1024:1280].astype(BF16)
        rs_ref[rows, 1280:1536] = _layer_norm(h[:, 1280:1536], lng_ref[...], lnb_ref[...]).astype(BF16)
        cm8 = jnp.concatenate([cm] * MLA_HEADS, axis=1)
        sm8 = jnp.concatenate([sm] * MLA_HEADS, axis=1)
        q_ref[rows, :] = (_rope(q, cm8, sm8, 16) * q_scale).astype(BF16)
        kr = _rope(h[:, 1920:2048], cm, sm, 16)
        k_ref[rows, :] = (kv[:, 0:1024] + jnp.concatenate([kr] * MLA_HEADS, axis=1)).astype(BF16)
        v_ref[rows, :] = (kv[:, 1024:2048] + ones_lane).astype(BF16)

    h_prev = in_proj(0)
    for c in range(1, n_chunks):
        h_next = in_proj(c)
        finish(c - 1, h_prev)
        h_prev = h_next
    finish(n_chunks - 1, h_prev)


def _projections(x2d, l, w_in, w_uq, w_ukv, sgu_g, sgu_b, q_g, kv_g, cr, sr, cm, sm, tm):
    t = x2d.shape[0]
    row = lambda w: pl.BlockSpec((tm, w), lambda i: (i, 0))
    lay = lambda a: pl.BlockSpec((None,) + a.shape[1:], lambda i: (l,) + (0,) * (a.ndim - 1))
    q_scale = (MLA_NOPE + MLA_ROPE) ** -0.5 * math.log2(math.e)
    return pl.pallas_call(
        functools.partial(_proj_kernel, q_scale=q_scale),
        grid=(t // tm,),
        in_specs=[row(D_MODEL), lay(w_in), lay(w_uq), lay(w_ukv), lay(sgu_g), lay(sgu_b),
                  lay(q_g), lay(kv_g), row(256), row(256), row(128), row(128)],
        out_specs=[row(1536), row(1024), row(1024), row(1024)],
        out_shape=[jax.ShapeDtypeStruct((t, 1536), BF16)] + [jax.ShapeDtypeStruct((t, 1024), BF16)] * 3,
        compiler_params=_cparams("parallel"),
        name="projections",
    )(x2d, w_in, w_uq, w_ukv, sgu_g, sgu_b, q_g, kv_g, cr, sr, cm, sm)


def _retsgu_kernel(rs_ref, dmat_ref, qdec_ref, kdec_ref, cdec_ref, gavg_ref, sw_ref, sb_ref,
                   out_ref, state_ref, *, n_chunks):
    @pl.when(pl.program_id(1) == 0)
    def _():
        state_ref[...] = jnp.zeros_like(state_ref)

    lane = lax.broadcasted_iota(jnp.int32, (CHUNK, RET_W), 1)
    head_of_lane = lane // RET_DK
    hmask = [head_of_lane == h for h in range(RET_HEADS)]
    row_i = lax.broadcasted_iota(jnp.int32, (CHUNK, SGU_GROUPS * CHUNK), 0)
    col_i = lax.broadcasted_iota(jnp.int32, (CHUNK, SGU_GROUPS * CHUNK), 1)
    sw = jnp.where(row_i >= (col_i & (CHUNK - 1)), sw_ref[...], 0.0).astype(BF16)
    blk = (lax.broadcasted_iota(jnp.int32, (RET_W, RET_W), 0) // RET_DK
           == lax.broadcasted_iota(jnp.int32, (RET_W, RET_W), 1) // RET_DK)
    gavg = gavg_ref[...]
    zero = jnp.zeros((), BF16)

    def group_mean(y):
        hi, lo = _split_bf16(y)
        return _dot(jnp.concatenate([hi, lo], axis=1), gavg)

    cs = range(n_chunks)
    rows = [slice(c * CHUNK, (c + 1) * CHUNK) for c in cs]
    rq = [rs_ref[r, 0:256] for r in rows]
    rk = [rs_ref[r, 256:512] for r in rows]
    rv = [rs_ref[r, 512:768] for r in rows]
    heads_of = lambda a: jnp.concatenate([jnp.where(m, a, zero) for m in hmask], axis=0)

    scores = [_dot_nt(heads_of(rq[c]), rk[c]) * dmat_ref[...] for c in cs]
    kv = [_dot_tn((rk[c].astype(F32) * kdec_ref[...]).astype(BF16), rv[c]) for c in cs]
    mixed = [_dot(sw, heads_of(rs_ref[rows[c], 1280:1536])) + sb_ref[...] for c in cs]
    intra = [_dot(jnp.concatenate([scores[c][h * CHUNK:(h + 1) * CHUNK, :] for h in range(RET_HEADS)],
                                  axis=1).astype(BF16), heads_of(rv[c])) for c in cs]
    states = []
    state = state_ref[...]
    for c in cs:
        states.append(state.astype(BF16))
        state = state * cdec_ref[...] + jnp.where(blk, kv[c], 0.0)
    state_ref[...] = state
    y = jnp.concatenate([intra[c] + _dot((rq[c].astype(F32) * qdec_ref[...]).astype(BF16), states[c])
                         for c in cs], axis=0)
    yc = y - group_mean(y)
    var = group_mean(yc * yc)
    out_ref[:, 0:256] = (rs_ref[:, 768:1024].astype(F32) * (yc * lax.rsqrt(var + LN_EPS))).astype(BF16)
    out_ref[:, 256:512] = (rs_ref[:, 1024:1280].astype(F32) * jnp.concatenate(mixed, axis=0)).astype(BF16)


def _retention_sgu(rs, l, consts, sgu_w_cat, sgu_b_lane, batch, seq, tb):
    dmat, qdec, kdec, cdec, gavg = consts
    n_blocks = seq // tb
    full = lambda a: pl.BlockSpec(a.shape, lambda b, i: (0,) * a.ndim)
    lay = lambda a: pl.BlockSpec((None,) + a.shape[1:], lambda b, i: (l,) + (0,) * (a.ndim - 1))
    return pl.pallas_call(
        functools.partial(_retsgu_kernel, n_chunks=tb // CHUNK),
        grid=(batch, n_blocks),
        in_specs=[pl.BlockSpec((tb, 1536), lambda b, i: (b * n_blocks + i, 0)),
                  full(dmat), full(qdec), full(kdec), full(cdec), full(gavg),
                  lay(sgu_w_cat), lay(sgu_b_lane)],
        out_specs=pl.BlockSpec((tb, 512), lambda b, i: (b * n_blocks + i, 0)),
        out_shape=jax.ShapeDtypeStruct((batch * seq, 512), BF16),
        scratch_shapes=[pltpu.VMEM((RET_W, RET_W), F32)],
        compiler_params=_cparams("parallel", "arbitrary"),
        name="retention_sgu",
    )(rs, dmat, qdec, kdec, cdec, gavg, sgu_w_cat, sgu_b_lane)


def _retention_consts():
    h = jnp.arange(RET_HEADS, dtype=F32)
    log_gamma = jnp.log1p(-(2.0 ** (-5.0 - h)))
    pos = jnp.arange(CHUNK, dtype=F32)
    diff = pos[:, None] - pos[None, :]
    intra = jnp.where(diff >= 0, jnp.exp(log_gamma[:, None, None] * jnp.maximum(diff, 0.0)), 0.0)
    dmat = intra.reshape(RET_HEADS * CHUNK, CHUNK)
    inner = jnp.exp(log_gamma[None, :] * (CHUNK - 1 - pos)[:, None])
    query = jnp.exp(log_gamma[None, :] * (pos + 1)[:, None])
    kdec = jnp.repeat(inner, RET_DK, axis=1)
    qdec = jnp.repeat(query, RET_DK, axis=1)
    chunk_decay = jnp.repeat(jnp.exp(log_gamma * CHUNK), RET_DK)
    blk = jnp.arange(RET_W)[:, None] // RET_DK == jnp.arange(RET_W)[None, :] // RET_DK
    cdec = jnp.where(blk, chunk_decay[:, None], 0.0)
    gavg = jnp.where(blk, 1.0 / RET_DK, 0.0).astype(BF16)
    return dmat, qdec, kdec, cdec, jnp.concatenate([gavg, gavg], axis=0)


def _flash_kernel(q_ref, k_ref, v_ref, o_ref, sa_ref, sb_ref, m_ref, acc_ref, *, tq, hps):
    i = pl.program_id(2)
    tk = tq // 2
    heads = range(hps)
    cols = [slice(hh * HEAD_PAD, (hh + 1) * HEAD_PAD) for hh in heads]

    def scores(j, hh, rows=slice(None)):
        r0 = pl.multiple_of(j * tk, tk)
        return _dot_nt(q_ref[rows, cols[hh]], k_ref[pl.ds(r0, tk), cols[hh]])

    def update(j, hh, s, rows=slice(None)):
        r0 = pl.multiple_of(j * tk, tk)
        m = m_ref[hh, rows, :]
        s_max = s[:, 0:HEAD_PAD]
        for c in range(1, tk // HEAD_PAD):
            s_max = jnp.maximum(s_max, s[:, c * HEAD_PAD:(c + 1) * HEAD_PAD])
        m_new = jnp.maximum(m, jnp.max(s_max, axis=-1, keepdims=True))
        p = jnp.exp2(s - jnp.concatenate([m_new] * (tk // HEAD_PAD), axis=1)).astype(BF16)
        acc_ref[hh, rows, :] = (jnp.exp2(m - m_new) * acc_ref[hh, rows, :]
                                + _dot(p, v_ref[pl.ds(r0, tk), cols[hh]]))
        m_ref[hh, rows, :] = m_new

    m_ref[...] = jnp.full(m_ref.shape, NEG_BIG, F32)
    acc_ref[...] = jnp.zeros(acc_ref.shape, F32)
    for hh in heads:
        sa_ref[hh] = scores(0, hh)

    def pair(t, carry):
        for hh in heads:
            sb_ref[hh] = scores(2 * t + 1, hh)
        for hh in heads:
            update(2 * t, hh, sa_ref[hh])
        for hh in heads:
            sa_ref[hh] = scores(2 * t + 2, hh)
        for hh in heads:
            update(2 * t + 1, hh, sb_ref[hh])
        return carry

    lax.fori_loop(0, i, pair, 0)
    low = slice(tk, tq)
    visible = (lax.broadcasted_iota(jnp.int32, (tq, tk), 1) <= lax.broadcasted_iota(jnp.int32, (tq, tk), 0))
    visible_low = (lax.broadcasted_iota(jnp.int32, (tk, tk), 1) <= lax.broadcasted_iota(jnp.int32, (tk, tk), 0))
    for hh in heads:
        sb_ref[hh, low, :] = scores(2 * i + 1, hh, low)
        update(2 * i, hh, jnp.where(visible, sa_ref[hh], NEG_BIG))
    for hh in heads:
        update(2 * i + 1, hh, jnp.where(visible_low, sb_ref[hh, low, :], NEG_BIG), low)
    lane = lax.broadcasted_iota(jnp.int32, (tq, HEAD_PAD), 1)
    for pr in range(hps // 2):
        o = []
        for hh in (2 * pr, 2 * pr + 1):
            acc = acc_ref[hh]
            o.append(acc / acc[:, MLA_V:MLA_V + 1])
        o_ref[:, pr * HEAD_PAD:(pr + 1) * HEAD_PAD] = jnp.where(
            lane < MLA_V, o[0], pltpu.roll(o[1], MLA_V, 1)).astype(BF16)


def _flash_attention(q, k, v, batch, seq, tq, hps):
    nq = seq // tq
    return pl.pallas_call(
        functools.partial(_flash_kernel, tq=tq, hps=hps),
        grid=(batch, MLA_HEADS // hps, nq),
        in_specs=[pl.BlockSpec((tq, hps * HEAD_PAD), lambda b, p, i: (b * nq + i, p)),
                  pl.BlockSpec((seq, hps * HEAD_PAD), lambda b, p, i: (b, p)),
                  pl.BlockSpec((seq, hps * HEAD_PAD), lambda b, p, i: (b, p))],
        out_specs=pl.BlockSpec((tq, hps * MLA_V), lambda b, p, i: (b * nq + i, p)),
        out_shape=jax.ShapeDtypeStruct((batch * seq, MLA_W), BF16),
        scratch_shapes=[pltpu.VMEM((hps, tq, tq // 2), F32), pltpu.VMEM((hps, tq, tq // 2), F32),
                        pltpu.VMEM((hps, tq, HEAD_PAD), F32), pltpu.VMEM((hps, tq, HEAD_PAD), F32)],
        compiler_params=_cparams("parallel", "parallel", "arbitrary"),
        name="flash_attention",
    )(q, k, v)


def _memkv_kernel(mem_ref, wk_ref, wv_ref, k_ref, v_ref):
    m = mem_ref[...].astype(BF16)
    k_ref[...] = _dot(m, wk_ref[...]).astype(BF16)
    v_ref[...] = _dot(m, wv_ref[...]).astype(BF16)


def _memory_kv(mem2d, l, wk, wv, batch):
    lay = lambda a: pl.BlockSpec((None,) + a.shape[1:], lambda b: (l,) + (0,) * (a.ndim - 1))
    blk = pl.BlockSpec((MEM_LEN, XA_W), lambda b: (b, 0))
    return pl.pallas_call(
        _memkv_kernel,
        grid=(batch,),
        in_specs=[pl.BlockSpec((MEM_LEN, D_MODEL), lambda b: (b, 0)), lay(wk), lay(wv)],
        out_specs=[blk, blk],
        out_shape=[jax.ShapeDtypeStruct((batch * MEM_LEN, XA_W), BF16)] * 2,
        compiler_params=_cparams("parallel"),
        name="memory_kv",
    )(mem2d, wk, wv)


def _route_rows(scores, biased):
    s = [scores[e:e + 1, :] for e in range(N_EXPERTS)]
    b = [biased[e:e + 1, :] for e in range(N_EXPERTS)]
    group_scores = []
    for g in range(N_GROUPS):
        b0, b1, b2, b3 = b[4 * g:4 * g + 4]
        hi01, lo01 = jnp.maximum(b0, b1), jnp.minimum(b0, b1)
        hi23, lo23 = jnp.maximum(b2, b3), jnp.minimum(b2, b3)
        top1 = jnp.maximum(hi01, hi23)
        top2 = jnp.maximum(jnp.minimum(hi01, hi23), jnp.maximum(lo01, lo23))
        group_scores.append(top1 + top2)
    best = group_scores[0]
    sel = jnp.zeros_like(best, dtype=jnp.int32)
    for g in range(1, N_GROUPS):
        upd = group_scores[g] > best
        sel = jnp.where(upd, g, sel)
        best = jnp.where(upd, group_scores[g], best)

    def pick(rows, j):
        out = rows[j]
        for g in range(1, N_GROUPS):
            out = jnp.where(sel == g, rows[4 * g + j], out)
        return out

    ib = [pick(b, j) for j in range(EXPERTS_PER_GROUP)]
    isc = [pick(s, j) for j in range(EXPERTS_PER_GROUP)]

    def argmax4(vals):
        bv, bi = vals[0], jnp.zeros_like(sel)
        for j in range(1, EXPERTS_PER_GROUP):
            upd = vals[j] > bv
            bi = jnp.where(upd, j, bi)
            bv = jnp.where(upd, vals[j], bv)
        return bi

    i1 = argmax4(ib)
    i2 = argmax4([jnp.where(i1 == j, -jnp.inf, ib[j]) for j in range(EXPERTS_PER_GROUP)])

    def take(vals, idx):
        out = vals[0]
        for j in range(1, EXPERTS_PER_GROUP):
            out = jnp.where(idx == j, vals[j], out)
        return out

    g1, g2 = take(isc, i1), take(isc, i2)
    den = g1 + g2
    g1, g2 = g1 / den, g2 / den
    return sel, i1, i2, g1, g2


def _mix_xa_kernel(x_ref, rs_ref, at_ref, wout_ref, g1_ref, b1_ref, wq_ref, km_ref, vm_ref, wo_ref,
                   g2_ref, b2_ref, rw_ref, rb_ref, tri_ref, x2_ref, pos_ref, gate_rows_ref,
                   cnt_ref, base_ref, pk_ref, *, tm, tiles_per_block):
    @pl.when(pl.program_id(1) % tiles_per_block == 0)
    def _():
        base_ref[...] = jnp.zeros_like(base_ref)

    w_hi, w_lo = _split_bf16(rw_ref[...])
    w_both = jnp.concatenate([w_hi, w_lo], axis=0)

    n_chunks = max(tm // MIX_CHUNK, 1)
    chunk = tm // n_chunks
    rows = [slice(c * chunk, (c + 1) * chunk) for c in range(n_chunks)]

    def skewed(matmul, finish):
        out = []
        prev = matmul(0)
        for c in range(n_chunks):
            nxt = matmul(c + 1) if c + 1 < n_chunks else None
            out.append(finish(c, prev))
            prev = nxt
        return out

    x1 = jnp.concatenate(skewed(
        lambda c: _dot(jnp.concatenate([rs_ref[rows[c], :], at_ref[rows[c], :]], axis=1), wout_ref[...]),
        lambda c, mix: _layer_norm(ALPHA * x_ref[rows[c], :] + mix, g1_ref[...], b1_ref[...])), axis=0)
    q = (_dot(x1.astype(BF16), wq_ref[...]) * (XA_DIM ** -0.5 * math.log2(math.e))).astype(BF16)
    cols = [slice(h * XA_DIM, (h + 1) * XA_DIM) for h in range(XA_HEADS)]
    s = [_dot_nt(q[:, sl], km_ref[:, sl]) for sl in cols]
    p = [jnp.exp2(sh - jnp.max(sh, axis=-1, keepdims=True)) for sh in s]
    o = [_dot(ph.astype(BF16), vm_ref[:, sl]) for ph, sl in zip(p, cols)]
    heads = jnp.concatenate([oh / jnp.sum(ph, axis=-1, keepdims=True) for oh, ph in zip(o, p)],
                            axis=1).astype(BF16)

    def route(c, logits):
        scores = 1.0 / (1.0 + jnp.exp(-logits))
        sel_c, i1, i2, gate1, gate2 = _route_rows(scores, scores + rb_ref[...])
        in_group = [jnp.where(i1 == j, gate1, 0.0) + jnp.where(i2 == j, gate2, 0.0)
                    for j in range(EXPERTS_PER_GROUP)]
        gate_rows_ref[rows[c], :] = jnp.concatenate(
            in_group + [jnp.zeros((128 - EXPERTS_PER_GROUP, chunk), F32)], axis=0).T
        return sel_c

    sel_parts = []
    pending = []

    def norm2_router(c, xa):
        if pending:
            sel_parts.append(route(c - 1, pending.pop()))
        x2 = _layer_norm(ALPHA * x1[rows[c], :] + xa, g2_ref[...], b2_ref[...])
        x2_ref[rows[c], :] = x2
        x_hi, x_lo = _split_bf16(x2)
        both = _dot_nt(w_both, x_hi)
        pending.append(both[0:N_EXPERTS] + (_dot_nt(w_hi, x_lo) + both[N_EXPERTS:2 * N_EXPERTS]))

    skewed(lambda c: _dot(heads[rows[c], :], wo_ref[...]), norm2_router)
    sel_parts.append(route(n_chunks - 1, pending.pop()))
    sel = jnp.concatenate(sel_parts, axis=1)
    hit = lax.broadcasted_iota(jnp.int32, (N_EXPERTS, tm), 0) == sel
    cnt = jnp.where(hit, 1.0, 0.0)
    base = base_ref[...]
    before = _dot(cnt.astype(BF16), tri_ref[...]) - cnt + base[:, 0:1]
    rank = jnp.sum(jnp.where(hit, before, 0.0), axis=0, keepdims=True).astype(jnp.int32)
    base = base + jnp.sum(cnt, axis=1, keepdims=True)
    base_ref[...] = base
    cnt_ref[...] = base
    ti = pl.program_id(1) % tiles_per_block
    pk_ref[ti] = sel * RANK_RADIX + rank

    @pl.when(ti == tiles_per_block - 1)
    def _():
        starts = []
        start = jnp.zeros((1, 128), F32)
        for g in range(N_GROUPS):
            starts.append(start.astype(jnp.int32)[:, 0:1])
            tiles = jnp.floor((base[g:g + 1, :] + (MOE_TILE - 1)) * (1.0 / MOE_TILE))
            start = start + tiles * MOE_TILE
        for tj in range(tiles_per_block):
            pk = pk_ref[tj]
            g_of = pk >> RANK_BITS
            slot = pk & (RANK_RADIX - 1)
            for g in range(N_GROUPS):
                slot = slot + jnp.where(g_of == g, starts[g], 0)
            pos_ref[:, tj * tm:(tj + 1) * tm] = slot


def _mix_xa(x2d, rs, at, km, vm, l, w_out, g1, b1, wq, wo, g2, b2, rw_t, rb_col, tri, batch, seq, tm, nblk):
    t = x2d.shape[0]
    nb = seq // tm
    n_tiles = t // tm
    tpb = nblk // tm
    row = lambda w: pl.BlockSpec((tm, w), lambda b, i: (b * nb + i, 0))
    lay = lambda a: pl.BlockSpec((None,) + a.shape[1:], lambda b, i: (l,) + (0,) * (a.ndim - 1))
    full = lambda a: pl.BlockSpec(a.shape, lambda b, i: (0,) * a.ndim)
    memb = pl.BlockSpec((MEM_LEN, XA_W), lambda b, i: (b, 0))
    blk = pl.BlockSpec((None, 1, nblk), lambda b, i: ((b * nb + i) // tpb, 0, 0))
    blk_i = jax.ShapeDtypeStruct((t // nblk, 1, nblk), jnp.int32)
    return pl.pallas_call(
        functools.partial(_mix_xa_kernel, tm=tm, tiles_per_block=tpb),
        grid=(batch, nb),
        in_specs=[row(D_MODEL), row(512), row(512), lay(w_out), lay(g1), lay(b1), lay(wq), memb, memb,
                  lay(wo), lay(g2), lay(b2), full(rw_t), full(rb_col), full(tri)],
        out_specs=[row(D_MODEL), blk, row(128),
                   pl.BlockSpec((None, N_EXPERTS, 128), lambda b, i: ((b * nb + i) // tpb, 0, 0))],
        out_shape=[jax.ShapeDtypeStruct((t, D_MODEL), F32), blk_i, jax.ShapeDtypeStruct((t, 128), F32),
                   jax.ShapeDtypeStruct((t // nblk, N_EXPERTS, 128), F32)],
        scratch_shapes=[pltpu.VMEM((N_EXPERTS, 128), F32), pltpu.VMEM((tpb, 1, tm), jnp.int32)],
        compiler_params=_cparams("parallel", "arbitrary"),
        name="mix_xattn_router",
    )(x2d, rs, at, w_out, g1, b1, wq, km, vm, wo, g2, b2, rw_t, rb_col, tri)


def _pack_bf16_pairs(x):
    k = x.shape[1] // 2
    hi = pltpu.bitcast(x[:, :k].astype(BF16).astype(F32), jnp.uint32)
    lo = pltpu.bitcast(x[:, k:].astype(BF16).astype(F32), jnp.uint32)
    return hi | (lo >> 16)


def _unpack_bf16_pairs(words):
    return jnp.concatenate([pltpu.bitcast(w & jnp.uint32(0xFFFF0000), F32) for w in words]
                           + [pltpu.bitcast(w << 16, F32) for w in words], axis=1)


def _group_weights_kernel(w_ref, o_ref):
    for e in range(EXPERTS_PER_GROUP):
        o_ref[:, e * D_EXPERT:(e + 1) * D_EXPERT] = w_ref[e].astype(BF16)


def _group_weights(w):
    depth = w.shape[0]
    return pl.pallas_call(
        _group_weights_kernel,
        grid=(depth, N_GROUPS),
        in_specs=[pl.BlockSpec((None, EXPERTS_PER_GROUP, D_MODEL, D_EXPERT), lambda l, g: (l, g, 0, 0))],
        out_specs=pl.BlockSpec((None, None, D_MODEL, EXPERTS_PER_GROUP * D_EXPERT), lambda l, g: (l, g, 0, 0)),
        out_shape=jax.ShapeDtypeStruct((depth, N_GROUPS, D_MODEL, EXPERTS_PER_GROUP * D_EXPERT), BF16),
        compiler_params=_cparams("parallel", "parallel"),
        name="group_weights",
    )(w)


def _moe_kernel(cnt_ref, pos_ref, x_ref, gates_ref, wg_ref, wu_ref, wd_ref, g_ref, b_ref,
                o_ref, xg_ref, xs_ref, ys_ref, og_ref, off_ref, ntile_ref, *, ts, k, mp):
    s = pl.program_id(1)
    sg = ts + 8
    unroll = 8

    @pl.when((s == 0) & (pl.program_id(0) == 0))
    def _():
        xs_ref[...] = jnp.zeros_like(xs_ref)
        xg_ref[...] = jnp.zeros_like(xg_ref)

    @pl.when(s == 0)
    def _():
        start = jnp.int32(0)
        for g in range(N_GROUPS):
            tiles = (cnt_ref[0, g] + (MOE_TILE - 1)) // MOE_TILE
            off_ref[g] = start
            ntile_ref[g] = tiles
            start = start + tiles * MOE_TILE

    @pl.when(s < k)
    def _():
        words = _pack_bf16_pairs(x_ref[...])
        for c in range(4):
            xg_ref[c * sg:c * sg + ts, :] = words[:, c * 128:(c + 1) * 128]
        xg_ref[4 * sg:4 * sg + ts, :] = pltpu.bitcast(gates_ref[...], jnp.uint32)

        def dispatch(tt, carry):
            for u in range(unroll):
                tl = tt * unroll + u
                xs_ref[pl.ds(pos_ref[s * ts + tl], 8, stride=mp), :] = xg_ref[pl.ds(tl, 8, stride=sg), :]
            return carry

        lax.fori_loop(0, ts // unroll, dispatch, 0)

    @pl.when((s >= k) & (s < k + N_GROUPS))
    def _():
        g = s - k
        seg = off_ref[g]

        def row_tiles(tiles):
            r0 = [pl.multiple_of(seg + i * MOE_TILE, MOE_TILE) for i in tiles]
            xb, weight = [], []
            for r in r0:
                words = [xs_ref[pl.ds(c * mp + r, MOE_TILE), :] for c in range(4)]
                gates = pltpu.bitcast(xs_ref[pl.ds(4 * mp + r, MOE_TILE), :], F32)
                xb.append(_unpack_bf16_pairs(words).astype(BF16))
                weight.append(jnp.concatenate([jnp.broadcast_to(gates[:, e:e + 1], (MOE_TILE, D_EXPERT))
                                               for e in range(EXPERTS_PER_GROUP)], axis=1))
            gate_act = [_dot(x, wg_ref[...]) for x in xb]
            up_act = [_dot(x, wu_ref[...]) for x in xb]
            hid = [(_silu(a) * u * w).astype(BF16) for a, u, w in zip(gate_act, up_act, weight)]
            y = [_pack_bf16_pairs(_dot(h, wd_ref[...])) for h in hid]
            for r, yt in zip(r0, y):
                for j in range(4):
                    ys_ref[pl.ds(j * mp + r, MOE_TILE), :] = yt[:, j * 128:(j + 1) * 128]

        def tile_pair(i, carry):
            row_tiles((2 * i, 2 * i + 1))
            return carry

        def tile_last(i, carry):
            row_tiles((n_tiles - 1,))
            return carry

        n_tiles = ntile_ref[g]
        lax.fori_loop(0, n_tiles // 2, tile_pair, 0)
        lax.fori_loop(0, n_tiles % 2, tile_last, 0)

    @pl.when(s >= k + N_GROUPS)
    def _():
        sub = s - (k + N_GROUPS)

        base = sub * ts
        n_chunks = ts // MOE_TILE

        def gather(c):
            for tl in range(c * MOE_TILE, (c + 1) * MOE_TILE):
                og_ref[pl.ds(tl, 4, stride=sg), :] = ys_ref[pl.ds(pos_ref[base + tl], 4, stride=mp), :]

        def norm(c):
            rows = slice(c * MOE_TILE, (c + 1) * MOE_TILE)
            ffn = _unpack_bf16_pairs([og_ref[j * sg + c * MOE_TILE:j * sg + (c + 1) * MOE_TILE, :]
                                      for j in range(4)])
            o_ref[rows, :] = _layer_norm(ALPHA * x_ref[rows, :] + ffn, g_ref[...], b_ref[...])

        gather(0)
        for c in range(n_chunks):
            if c + 1 < n_chunks:
                gather(c + 1)
            norm(c)


def _moe(x2, cnt_blk, pos, gate_rows, l, wg, wu, wd, g, b, nblk, ts):
    t = x2.shape[0]
    k = nblk // ts
    steps = 2 * k + N_GROUPS
    mp = nblk + N_GROUPS * MOE_TILE + 8
    lay = lambda a: pl.BlockSpec((None,) + a.shape[1:], lambda i, s: (l,) + (0,) * (a.ndim - 1))
    grp = lambda a: pl.BlockSpec((None, None) + a.shape[2:],
                                 lambda i, s: (l, jnp.clip(s - k, 0, N_GROUPS - 1), 0, 0))
    x_map = lambda i, s: (i * k + jnp.where(s < k, s, jnp.where(s < k + N_GROUPS, k - 1, s - k - N_GROUPS)), 0)
    g_map = lambda i, s: (i * k + jnp.minimum(s, k - 1), 0)
    o_map = lambda i, s: (i * k + jnp.maximum(s - k - N_GROUPS, 0), 0)
    return pl.pallas_call(
        functools.partial(_moe_kernel, ts=ts, k=k, mp=mp),
        grid=(t // nblk, steps),
        in_specs=[pl.BlockSpec((None, 1, N_EXPERTS), lambda i, s: (i, 0, 0), memory_space=pltpu.SMEM),
                  pl.BlockSpec((nblk,), lambda i, s: (i,), memory_space=pltpu.SMEM),
                  pl.BlockSpec((ts, D_MODEL), x_map), pl.BlockSpec((ts, 128), g_map),
                  grp(wg), grp(wu), grp(wd), lay(g), lay(b)],
        out_specs=pl.BlockSpec((ts, D_MODEL), o_map),
        out_shape=jax.ShapeDtypeStruct((t, D_MODEL), F32),
        scratch_shapes=[pltpu.VMEM((8 * (ts + 8), 128), jnp.uint32), pltpu.VMEM((8 * mp, 128), jnp.uint32),
                        pltpu.VMEM((4 * mp, 128), jnp.uint32), pltpu.VMEM((4 * (ts + 8), 128), jnp.uint32),
                        pltpu.SMEM((N_GROUPS,), jnp.int32), pltpu.SMEM((N_GROUPS,), jnp.int32)],
        compiler_params=_cparams("arbitrary", "arbitrary"),
        name="moe_experts",
    )(cnt_blk, pos, x2, gate_rows, wg, wu, wd, g, b)


def _tile(n, pref):
    t = min(n, pref)
    assert n % t == 0, (n, t)
    return t


def kernel(x, mem, positions, w_in, w_out, sgu_ln_g, sgu_ln_b, sgu_w, sgu_b, mla_q_norm_g, mla_w_uq, mla_kv_norm_g, mla_w_ukv, xa_wq, xa_wk, xa_wv, xa_wo, ln_mix_g, ln_mix_b, ln_xa_g, ln_xa_b, ln_moe_g, ln_moe_b, router_w, router_bias, expert_w_gate, expert_w_up, expert_w_down):
    batch, seq, _ = x.shape
    depth = w_in.shape[0]
    t = batch * seq
    assert seq % CHUNK == 0

    w_in_b = w_in.astype(BF16)
    w_in_p = jnp.concatenate(
        [w_in_b[:, :, :1920], jnp.zeros((depth, D_MODEL, 64), BF16), w_in_b[:, :, 1920:1952],
         jnp.zeros((depth, D_MODEL, 32), BF16)], axis=2)
    w_uq_p = jnp.pad(mla_w_uq.astype(BF16).reshape(depth, MLA_Q_RANK, MLA_HEADS, MLA_NOPE + MLA_ROPE),
                     ((0, 0), (0, 0), (0, 0), (0, HEAD_PAD - MLA_NOPE - MLA_ROPE))
                     ).reshape(depth, MLA_Q_RANK, MLA_HEADS * HEAD_PAD)
    ukv = mla_w_ukv.astype(BF16).reshape(depth, MLA_KV_RANK, MLA_HEADS, MLA_NOPE + MLA_V)
    w_uk_p = jnp.pad(ukv[..., :MLA_NOPE], ((0, 0), (0, 0), (0, 0), (0, HEAD_PAD - MLA_NOPE))
                     ).reshape(depth, MLA_KV_RANK, MLA_HEADS * HEAD_PAD)
    w_uv_p = jnp.pad(ukv[..., MLA_NOPE:], ((0, 0), (0, 0), (0, 0), (0, HEAD_PAD - MLA_V))
                     ).reshape(depth, MLA_KV_RANK, MLA_HEADS * HEAD_PAD)
    w_ukv_p = jnp.concatenate([w_uk_p, w_uv_p], axis=2)
    w_out_b = w_out.astype(BF16)
    wq_b, wk_b, wv_b, wo_b = (a.astype(BF16) for a in (xa_wq, xa_wk, xa_wv, xa_wo))
    wg_b = _group_weights(expert_w_gate)
    wu_b = _group_weights(expert_w_up)
    wd_b = expert_w_down.reshape(depth, N_GROUPS, EXPERTS_PER_GROUP * D_EXPERT, D_MODEL).astype(BF16)
    vec = lambda a: a.reshape(depth, 1, a.shape[-1])
    sgu_w_cat = jnp.transpose(sgu_w, (0, 2, 1, 3)).reshape(depth, CHUNK, SGU_GROUPS * CHUNK)
    sgu_b_lane = jnp.repeat(jnp.transpose(sgu_b, (0, 2, 1)), SGU_W // SGU_GROUPS, axis=2)
    rw_t = router_w.T
    rb_col = router_bias.reshape(N_EXPERTS, 1)
    consts = _retention_consts()

    x2d = x.reshape(t, D_MODEL)
    mem2d = mem.reshape(batch * MEM_LEN, D_MODEL)
    cr, sr, cm, sm = _rope_tables(positions, _tile(t // 2, 512))

    tm_proj = _tile(t, 1024)
    tb = _tile(seq, 1024)
    tq = _tile(seq, 1024)
    tm_mix = _tile(seq, 1024)
    nblk = _tile(seq, 4096)
    ts_moe = _tile(nblk, 512)
    tri = (jnp.arange(tm_mix)[:, None] <= jnp.arange(tm_mix)[None, :]).astype(BF16)
    for l in range(depth):
        rs, q, k, v = _projections(x2d, l, w_in_p, w_uq_p, w_ukv_p, vec(sgu_ln_g), vec(sgu_ln_b),
                                   vec(mla_q_norm_g), vec(mla_kv_norm_g), cr, sr, cm, sm, tm_proj)
        retsgu = _retention_sgu(rs, l, consts, sgu_w_cat, sgu_b_lane, batch, seq, tb)
        attn = _flash_attention(q, k, v, batch, seq, tq, 4)
        km, vm = _memory_kv(mem2d, l, wk_b, wv_b, batch)
        x2, pos, gate_rows, cnts = _mix_xa(
            x2d, retsgu, attn, km, vm, l, w_out_b, vec(ln_mix_g), vec(ln_mix_b), wq_b, wo_b,
            vec(ln_xa_g), vec(ln_xa_b), rw_t, rb_col, tri, batch, seq, tm_mix, nblk)
        cnt_blk = cnts[:, :, 0].astype(jnp.int32).reshape(-1, 1, N_EXPERTS)
        x2d = _moe(x2, cnt_blk, pos.reshape(t), gate_rows, l,
                   wg_b, wu_b, wd_b, vec(ln_moe_g), vec(ln_moe_b), nblk, ts_moe)
    return x2d.reshape(batch, seq, D_MODEL)
```

```python
import functools
import math

import jax
import jax.numpy as jnp
from jax import lax
from jax.experimental import pallas as pl
from jax.experimental.pallas import tpu as pltpu

F32 = jnp.float32
BF16 = jnp.bfloat16

D_MODEL = 1024
DEPTH = 4
MEM_LEN = 256
ROPE_THETA = 10000.0

RET_HEADS = 4
RET_DK = 64
RET_W = 256
CHUNK = 128

SGU_GROUPS = 4
SGU_W = 256

MLA_HEADS = 8
MLA_Q_RANK = 256
MLA_KV_RANK = 128
MLA_NOPE = 64
MLA_ROPE = 32
MLA_V = 64
MLA_W = MLA_HEADS * MLA_V
HEAD_PAD = 128

XA_HEADS = 4
XA_DIM = 128
XA_W = XA_HEADS * XA_DIM

N_EXPERTS = 16
N_GROUPS = 4
EXPERTS_PER_GROUP = 4
D_EXPERT = 256

ALPHA = (2 * DEPTH) ** 0.25
LN_EPS = 1e-5
IN_PAD = 2048
NEG_BIG = -1e30
PROJ_CHUNK = 256
MIX_CHUNK = 256
MOE_TILE = 128
RANK_BITS = 16
RANK_RADIX = 1 << RANK_BITS

VMEM_LIMIT = 56 * 1024 * 1024


def _cparams(*sem):
    return pltpu.CompilerParams(dimension_semantics=sem, vmem_limit_bytes=VMEM_LIMIT)


def _dot(a, b):
    return jnp.dot(a, b, preferred_element_type=F32)


def _dot_nt(a, b):
    return lax.dot_general(a, b, (((1,), (1,)), ((), ())), preferred_element_type=F32)


def _dot_tn(a, b):
    return lax.dot_general(a, b, (((0,), (0,)), ((), ())), preferred_element_type=F32)


def _layer_norm(z, g, b):
    mu = jnp.mean(z, axis=-1, keepdims=True)
    zc = z - mu
    var = jnp.mean(zc * zc, axis=-1, keepdims=True)
    return zc * lax.rsqrt(var + LN_EPS) * g + b


def _rms_norm(z, g):
    ms = jnp.mean(z * z, axis=-1, keepdims=True)
    return z * lax.rsqrt(ms + LN_EPS) * g


def _silu(z):
    return z / (1.0 + jnp.exp(-z))


def _rope(x, cos, sin_signed, half):
    w = x.shape[-1]
    lane = lax.broadcasted_iota(jnp.int32, x.shape, 1)
    rot = jnp.where((lane & half) == 0, pltpu.roll(x, w - half, 1), pltpu.roll(x, half, 1))
    return x * cos + rot * sin_signed


def _split_bf16(x):
    hi = x.astype(BF16)
    lo = (x - hi.astype(F32)).astype(BF16)
    return hi, lo


def _tables_kernel(pos_a_ref, pos_b_ref, invf_ref, cr_ref, sr_ref, cm_ref, sm_ref):
    lane = lax.broadcasted_iota(jnp.int32, (pos_a_ref.shape[0], 128), 1)
    pos = jnp.where(lane < 64, pos_a_ref[...].astype(F32), pos_b_ref[...].astype(F32))
    ang = pos * invf_ref[...]
    c_both = jnp.cos(ang)
    s_both = jnp.sin(ang)

    def tile32(v):
        v0 = jnp.where(lane < 32, v, 0.0)
        v1 = v0 + pltpu.roll(v0, 32, 1)
        return v1 + pltpu.roll(v1, 64, 1)

    for half in range(2):
        c = c_both if half == 0 else pltpu.roll(c_both, 64, 1)
        s = s_both if half == 0 else pltpu.roll(s_both, 64, 1)
        ct = tile32(c)
        st = tile32(s) * jnp.where((lane & 32) == 0, -1.0, 1.0)
        cr_ref[half] = jnp.concatenate([ct, ct], axis=1)
        sr_ref[half] = jnp.concatenate([st, st], axis=1)
        in_src = (lane >= 32) & (lane < 48)
        cmv = jnp.where(in_src, c, 0.0)
        smv = jnp.where(in_src, s, 0.0)
        in_dst = (lane >= 64) & (lane < 96)
        cm_ref[half] = jnp.where(in_dst, pltpu.roll(cmv, 32, 1) + pltpu.roll(cmv, 48, 1), 1.0)
        sm_ref[half] = jnp.where(in_dst, pltpu.roll(smv, 48, 1) - pltpu.roll(smv, 32, 1), 0.0)


def _rope_tables(positions, tm):
    t = positions.size
    half_t = t // 2
    n = half_t // tm
    pos = positions.reshape(t, 1)
    fr = ROPE_THETA ** (-jnp.arange(0, RET_DK, 2, dtype=F32) / RET_DK)
    fm = ROPE_THETA ** (-jnp.arange(0, MLA_ROPE, 2, dtype=F32) / MLA_ROPE)
    one = jnp.concatenate([fr, fm, jnp.zeros((64 - 48,), F32)])
    invf = jnp.concatenate([one, one]).reshape(1, 128)
    out = lambda w: pl.BlockSpec((2, tm, w), lambda i: (0, i, 0))
    tables = pl.pallas_call(
        _tables_kernel,
        grid=(n,),
        in_specs=[pl.BlockSpec((tm, 1), lambda i: (i, 0)),
                  pl.BlockSpec((tm, 1), lambda i: (i + n, 0)),
                  pl.BlockSpec((1, 128), lambda i: (0, 0))],
        out_specs=[out(256), out(256), out(128), out(128)],
        out_shape=[jax.ShapeDtypeStruct((2, half_t, 256), F32), jax.ShapeDtypeStruct((2, half_t, 256), F32),
                   jax.ShapeDtypeStruct((2, half_t, 128), F32), jax.ShapeDtypeStruct((2, half_t, 128), F32)],
        compiler_params=_cparams("parallel"),
        name="rope_tables",
    )(pos, pos, invf)
    return [a.reshape(t, a.shape[-1]) for a in tables]


def _proj_kernel(x_ref, win_ref, wuq_ref, wukv_ref, lng_ref, lnb_ref, qg_ref, kvg_ref,
                 cr_ref, sr_ref, cm_ref, sm_ref, rs_ref, q_ref, k_ref, v_ref, *, q_scale):
    tm = x_ref.shape[0]
    n_chunks = max(tm // PROJ_CHUNK, 1)
    chunk = tm // n_chunks
    lane = lax.broadcasted_iota(jnp.int32, (chunk, MLA_HEADS * HEAD_PAD), 1)
    ones_lane = jnp.where((lane & (HEAD_PAD - 1)) == MLA_V, 1.0, 0.0)

    def in_proj(c):
        rows = slice(c * chunk, (c + 1) * chunk)
        return _dot(x_ref[rows, :].astype(BF16), win_ref[...])

    def finish(c, h):
        rows = slice(c * chunk, (c + 1) * chunk)
        cm = cm_ref[rows, :]
        sm = sm_ref[rows, :]
        cq = _rms_norm(h[:, 1536:1792], qg_ref[...]).astype(BF16)
        q = _dot(cq, wuq_ref[...])
        ckv = _rms_norm(h[:, 1792:1920], kvg_ref[...]).astype(BF16)
        kv = _dot(ckv, wukv_ref[...])
        cr = cr_ref[rows, :]
        sr = sr_ref[rows, :]
        rs_ref[rows, 0:256] = _rope(h[:, 0:256], cr, sr, 32).astype(BF16)
        rs_ref[rows, 256:512] = (_rope(h[:, 256:512], cr, sr, 32) * (RET_DK ** -0.5)).astype(BF16)
        rs_ref[rows, 512:768] = h[:, 512:768].astype(BF16)
        rs_ref[rows, 768:1024] = _silu(h[:, 768:1024]).astype(BF16)
        rs_ref[rows, 1024:1280] = h[:, 1024:1280].astype(BF16)
        rs_ref[rows, 1280:1536] = _layer_norm(h[:, 1280:1536], lng_ref[...], lnb_ref[...]).astype(BF16)
        cm8 = jnp.concatenate([cm] * MLA_HEADS, axis=1)
        sm8 = jnp.concatenate([sm] * MLA_HEADS, axis=1)
        q_ref[rows, :] = (_rope(q, cm8, sm8, 16) * q_scale).astype(BF16)
        kr = _rope(h[:, 1920:2048], cm, sm, 16)
        k_ref[rows, :] = (kv[:, 0:1024] + jnp.concatenate([kr] * MLA_HEADS, axis=1)).astype(BF16)
        v_ref[rows, :] = (kv[:, 1024:2048] + ones_lane).astype(BF16)

    h_prev = in_proj(0)
    for c in range(1, n_chunks):
        h_next = in_proj(c)
        finish(c - 1, h_prev)
        h_prev = h_next
    finish(n_chunks - 1, h_prev)


def _projections(x2d, l, w_in, w_uq, w_ukv, sgu_g, sgu_b, q_g, kv_g, cr, sr, cm, sm, tm):
    t = x2d.shape[0]
    row = lambda w: pl.BlockSpec((tm, w), lambda i: (i, 0))
    lay = lambda a: pl.BlockSpec((None,) + a.shape[1:], lambda i: (l,) + (0,) * (a.ndim - 1))
    q_scale = (MLA_NOPE + MLA_ROPE) ** -0.5 * math.log2(math.e)
    return pl.pallas_call(
        functools.partial(_proj_kernel, q_scale=q_scale),
        grid=(t // tm,),
        in_specs=[row(D_MODEL), lay(w_in), lay(w_uq), lay(w_ukv), lay(sgu_g), lay(sgu_b),
                  lay(q_g), lay(kv_g), row(256), row(256), row(128), row(128)],
        out_specs=[row(1536), row(1024), row(1024), row(1024)],
        out_shape=[jax.ShapeDtypeStruct((t, 1536), BF16)] + [jax.ShapeDtypeStruct((t, 1024), BF16)] * 3,
        compiler_params=_cparams("parallel"),
        name="projections",
    )(x2d, w_in, w_uq, w_ukv, sgu_g, sgu_b, q_g, kv_g, cr, sr, cm, sm)


def _retsgu_kernel(rs_ref, dmat_ref, qdec_ref, kdec_ref, cdec_ref, gavg_ref, sw_ref, sb_ref,
                   out_ref, state_ref, *, n_chunks):
    @pl.when(pl.program_id(1) == 0)
    def _():
        state_ref[...] = jnp.zeros_like(state_ref)

    lane = lax.broadcasted_iota(jnp.int32, (CHUNK, RET_W), 1)
    head_of_lane = lane // RET_DK
    hmask = [head_of_lane == h for h in range(RET_HEADS)]
    row_i = lax.broadcasted_iota(jnp.int32, (CHUNK, SGU_GROUPS * CHUNK), 0)
    col_i = lax.broadcasted_iota(jnp.int32, (CHUNK, SGU_GROUPS * CHUNK), 1)
    sw = jnp.where(row_i >= (col_i & (CHUNK - 1)), sw_ref[...], 0.0).astype(BF16)
    blk = (lax.broadcasted_iota(jnp.int32, (RET_W, RET_W), 0) // RET_DK
           == lax.broadcasted_iota(jnp.int32, (RET_W, RET_W), 1) // RET_DK)
    gavg = gavg_ref[...]
    zero = jnp.zeros((), BF16)

    def group_mean(y):
        hi, lo = _split_bf16(y)
        return _dot(jnp.concatenate([hi, lo], axis=1), gavg)

    cs = range(n_chunks)
    rows = [slice(c * CHUNK, (c + 1) * CHUNK) for c in cs]
    rq = [rs_ref[r, 0:256] for r in rows]
    rk = [rs_ref[r, 256:512] for r in rows]
    rv = [rs_ref[r, 512:768] for r in rows]
    heads_of = lambda a: jnp.concatenate([jnp.where(m, a, zero) for m in hmask], axis=0)

    scores = [_dot_nt(heads_of(rq[c]), rk[c]) * dmat_ref[...] for c in cs]
    kv = [_dot_tn((rk[c].astype(F32) * kdec_ref[...]).astype(BF16), rv[c]) for c in cs]
    mixed = [_dot(sw, heads_of(rs_ref[rows[c], 1280:1536])) + sb_ref[...] for c in cs]
    intra = [_dot(jnp.concatenate([scores[c][h * CHUNK:(h + 1) * CHUNK, :] for h in range(RET_HEADS)],
                                  axis=1).astype(BF16), heads_of(rv[c])) for c in cs]
    states = []
    state = state_ref[...]
    for c in cs:
        states.append(state.astype(BF16))
        state = state * cdec_ref[...] + jnp.where(blk, kv[c], 0.0)
    state_ref[...] = state
    y = jnp.concatenate([intra[c] + _dot((rq[c].astype(F32) * qdec_ref[...]).astype(BF16), states[c])
                         for c in cs], axis=0)
    yc = y - group_mean(y)
    var = group_mean(yc * yc)
    out_ref[:, 0:256] = (rs_ref[:, 768:1024].astype(F32) * (yc * lax.rsqrt(var + LN_EPS))).astype(BF16)
    out_ref[:, 256:512] = (rs_ref[:, 1024:1280].astype(F32) * jnp.concatenate(mixed, axis=0)).astype(BF16)


def _retention_sgu(rs, l, consts, sgu_w_cat, sgu_b_lane, batch, seq, tb):
    dmat, qdec, kdec, cdec, gavg = consts
    n_blocks = seq // tb
    full = lambda a: pl.BlockSpec(a.shape, lambda b, i: (0,) * a.ndim)
    lay = lambda a: pl.BlockSpec((None,) + a.shape[1:], lambda b, i: (l,) + (0,) * (a.ndim - 1))
    return pl.pallas_call(
        functools.partial(_retsgu_kernel, n_chunks=tb // CHUNK),
        grid=(batch, n_blocks),
        in_specs=[pl.BlockSpec((tb, 1536), lambda b, i: (b * n_blocks + i, 0)),
                  full(dmat), full(qdec), full(kdec), full(cdec), full(gavg),
                  lay(sgu_w_cat), lay(sgu_b_lane)],
        out_specs=pl.BlockSpec((tb, 512), lambda b, i: (b * n_blocks + i, 0)),
        out_shape=jax.ShapeDtypeStruct((batch * seq, 512), BF16),
        scratch_shapes=[pltpu.VMEM((RET_W, RET_W), F32)],
        compiler_params=_cparams("parallel", "arbitrary"),
        name="retention_sgu",
    )(rs, dmat, qdec, kdec, cdec, gavg, sgu_w_cat, sgu_b_lane)


def _retention_consts():
    h = jnp.arange(RET_HEADS, dtype=F32)
    log_gamma = jnp.log1p(-(2.0 ** (-5.0 - h)))
    pos = jnp.arange(CHUNK, dtype=F32)
    diff = pos[:, None] - pos[None, :]
    intra = jnp.where(diff >= 0, jnp.exp(log_gamma[:, None, None] * jnp.maximum(diff, 0.0)), 0.0)
    dmat = intra.reshape(RET_HEADS * CHUNK, CHUNK)
    inner = jnp.exp(log_gamma[None, :] * (CHUNK - 1 - pos)[:, None])
    query = jnp.exp(log_gamma[None, :] * (pos + 1)[:, None])
    kdec = jnp.repeat(inner, RET_DK, axis=1)
    qdec = jnp.repeat(query, RET_DK, axis=1)
    chunk_decay = jnp.repeat(jnp.exp(log_gamma * CHUNK), RET_DK)
    blk = jnp.arange(RET_W)[:, None] // RET_DK == jnp.arange(RET_W)[None, :] // RET_DK
    cdec = jnp.where(blk, chunk_decay[:, None], 0.0)
    gavg = jnp.where(blk, 1.0 / RET_DK, 0.0).astype(BF16)
    return dmat, qdec, kdec, cdec, jnp.concatenate([gavg, gavg], axis=0)


def _flash_kernel(q_ref, k_ref, v_ref, o_ref, sa_ref, sb_ref, m_ref, acc_ref, *, tq, hps):
    i = pl.program_id(2)
    tk = tq // 2
    heads = range(hps)
    cols = [slice(hh * HEAD_PAD, (hh + 1) * HEAD_PAD) for hh in heads]

    def scores(j, hh, rows=slice(None)):
        r0 = pl.multiple_of(j * tk, tk)
        return _dot_nt(q_ref[rows, cols[hh]], k_ref[pl.ds(r0, tk), cols[hh]])

    def update(j, hh, s, rows=slice(None)):
        r0 = pl.multiple_of(j * tk, tk)
        m = m_ref[hh, rows, :]
        s_max = s[:, 0:HEAD_PAD]
        for c in range(1, tk // HEAD_PAD):
            s_max = jnp.maximum(s_max, s[:, c * HEAD_PAD:(c + 1) * HEAD_PAD])
        m_new = jnp.maximum(m, jnp.max(s_max, axis=-1, keepdims=True))
        p = jnp.exp2(s - jnp.concatenate([m_new] * (tk // HEAD_PAD), axis=1)).astype(BF16)
        acc_ref[hh, rows, :] = (jnp.exp2(m - m_new) * acc_ref[hh, rows, :]
                                + _dot(p, v_ref[pl.ds(r0, tk), cols[hh]]))
        m_ref[hh, rows, :] = m_new

    m_ref[...] = jnp.full(m_ref.shape, NEG_BIG, F32)
    acc_ref[...] = jnp.zeros(acc_ref.shape, F32)
    for hh in heads:
        sa_ref[hh] = scores(0, hh)

    def pair(t, carry):
        for hh in heads:
            sb_ref[hh] = scores(2 * t + 1, hh)
        for hh in heads:
            update(2 * t, hh, sa_ref[hh])
        for hh in heads:
            sa_ref[hh] = scores(2 * t + 2, hh)
        for hh in heads:
            update(2 * t + 1, hh, sb_ref[hh])
        return carry

    lax.fori_loop(0, i, pair, 0)
    low = slice(tk, tq)
    visible = (lax.broadcasted_iota(jnp.int32, (tq, tk), 1) <= lax.broadcasted_iota(jnp.int32, (tq, tk), 0))
    visible_low = (lax.broadcasted_iota(jnp.int32, (tk, tk), 1) <= lax.broadcasted_iota(jnp.int32, (tk, tk), 0))
    for hh in heads:
        sb_ref[hh, low, :] = scores(2 * i + 1, hh, low)
        update(2 * i, hh, jnp.where(visible, sa_ref[hh], NEG_BIG))
    for hh in heads:
        update(2 * i + 1, hh, jnp.where(visible_low, sb_ref[hh, low, :], NEG_BIG), low)
    lane = lax.broadcasted_iota(jnp.int32, (tq, HEAD_PAD), 1)
    for pr in range(hps // 2):
        o = []
        for hh in (2 * pr, 2 * pr + 1):
            acc = acc_ref[hh]
            o.append(acc / acc[:, MLA_V:MLA_V + 1])
        o_ref[:, pr * HEAD_PAD:(pr + 1) * HEAD_PAD] = jnp.where(
            lane < MLA_V, o[0], pltpu.roll(o[1], MLA_V, 1)).astype(BF16)


def _flash_attention(q, k, v, batch, seq, tq, hps):
    nq = seq // tq
    return pl.pallas_call(
        functools.partial(_flash_kernel, tq=tq, hps=hps),
        grid=(batch, MLA_HEADS // hps, nq),
        in_specs=[pl.BlockSpec((tq, hps * HEAD_PAD), lambda b, p, i: (b * nq + i, p)),
                  pl.BlockSpec((seq, hps * HEAD_PAD), lambda b, p, i: (b, p)),
                  pl.BlockSpec((seq, hps * HEAD_PAD), lambda b, p, i: (b, p))],
        out_specs=pl.BlockSpec((tq, hps * MLA_V), lambda b, p, i: (b * nq + i, p)),
        out_shape=jax.ShapeDtypeStruct((batch * seq, MLA_W), BF16),
        scratch_shapes=[pltpu.VMEM((hps, tq, tq // 2), F32), pltpu.VMEM((hps, tq, tq // 2), F32),
                        pltpu.VMEM((hps, tq, HEAD_PAD), F32), pltpu.VMEM((hps, tq, HEAD_PAD), F32)],
        compiler_params=_cparams("parallel", "parallel", "arbitrary"),
        name="flash_attention",
    )(q, k, v)


def _memkv_kernel(mem_ref, wk_ref, wv_ref, k_ref, v_ref):
    m = mem_ref[...].astype(BF16)
    k_ref[...] = _dot(m, wk_ref[...]).astype(BF16)
    v_ref[...] = _dot(m, wv_ref[...]).astype(BF16)


def _memory_kv(mem2d, l, wk, wv, batch):
    lay = lambda a: pl.BlockSpec((None,) + a.shape[1:], lambda b: (l,) + (0,) * (a.ndim - 1))
    blk = pl.BlockSpec((MEM_LEN, XA_W), lambda b: (b, 0))
    return pl.pallas_call(
        _memkv_kernel,
        grid=(batch,),
        in_specs=[pl.BlockSpec((MEM_LEN, D_MODEL), lambda b: (b, 0)), lay(wk), lay(wv)],
        out_specs=[blk, blk],
        out_shape=[jax.ShapeDtypeStruct((batch * MEM_LEN, XA_W), BF16)] * 2,
        compiler_params=_cparams("parallel"),
        name="memory_kv",
    )(mem2d, wk, wv)


def _route_rows(scores, biased):
    s = [scores[e:e + 1, :] for e in range(N_EXPERTS)]
    b = [biased[e:e + 1, :] for e in range(N_EXPERTS)]
    group_scores = []
    for g in range(N_GROUPS):
        b0, b1, b2, b3 = b[4 * g:4 * g + 4]
        hi01, lo01 = jnp.maximum(b0, b1), jnp.minimum(b0, b1)
        hi23, lo23 = jnp.maximum(b2, b3), jnp.minimum(b2, b3)
        top1 = jnp.maximum(hi01, hi23)
        top2 = jnp.maximum(jnp.minimum(hi01, hi23), jnp.maximum(lo01, lo23))
        group_scores.append(top1 + top2)
    best = group_scores[0]
    sel = jnp.zeros_like(best, dtype=jnp.int32)
    for g in range(1, N_GROUPS):
        upd = group_scores[g] > best
        sel = jnp.where(upd, g, sel)
        best = jnp.where(upd, group_scores[g], best)

    def pick(rows, j):
        out = rows[j]
        for g in range(1, N_GROUPS):
            out = jnp.where(sel == g, rows[4 * g + j], out)
        return out

    ib = [pick(b, j) for j in range(EXPERTS_PER_GROUP)]
    isc = [pick(s, j) for j in range(EXPERTS_PER_GROUP)]

    def argmax4(vals):
        bv, bi = vals[0], jnp.zeros_like(sel)
        for j in range(1, EXPERTS_PER_GROUP):
            upd = vals[j] > bv
            bi = jnp.where(upd, j, bi)
            bv = jnp.where(upd, vals[j], bv)
        return bi

    i1 = argmax4(ib)
    i2 = argmax4([jnp.where(i1 == j, -jnp.inf, ib[j]) for j in range(EXPERTS_PER_GROUP)])

    def take(vals, idx):
        out = vals[0]
        for j in range(1, EXPERTS_PER_GROUP):
            out = jnp.where(idx == j, vals[j], out)
        return out

    g1, g2 = take(isc, i1), take(isc, i2)
    den = g1 + g2
    g1, g2 = g1 / den, g2 / den
    return sel, i1, i2, g1, g2


def _mix_xa_kernel(x_ref, rs_ref, at_ref, wout_ref, g1_ref, b1_ref, wq_ref, km_ref, vm_ref, wo_ref,
                   g2_ref, b2_ref, rw_ref, rb_ref, tri_ref, x2_ref, pos_ref, gate_rows_ref,
                   cnt_ref, base_ref, pk_ref, *, tm, tiles_per_block):
    @pl.when(pl.program_id(1) % tiles_per_block == 0)
    def _():
        base_ref[...] = jnp.zeros_like(base_ref)

    w_hi, w_lo = _split_bf16(rw_ref[...])
    w_both = jnp.concatenate([w_hi, w_lo], axis=0)

    n_chunks = max(tm // MIX_CHUNK, 1)
    chunk = tm // n_chunks
    rows = [slice(c * chunk, (c + 1) * chunk) for c in range(n_chunks)]

    def skewed(matmul, finish):
        out = []
        prev = matmul(0)
        for c in range(n_chunks):
            nxt = matmul(c + 1) if c + 1 < n_chunks else None
            out.append(finish(c, prev))
            prev = nxt
        return out

    x1 = jnp.concatenate(skewed(
        lambda c: _dot(jnp.concatenate([rs_ref[rows[c], :], at_ref[rows[c], :]], axis=1), wout_ref[...]),
        lambda c, mix: _layer_norm(ALPHA * x_ref[rows[c], :] + mix, g1_ref[...], b1_ref[...])), axis=0)
    q = (_dot(x1.astype(BF16), wq_ref[...]) * (XA_DIM ** -0.5 * math.log2(math.e))).astype(BF16)
    cols = [slice(h * XA_DIM, (h + 1) * XA_DIM) for h in range(XA_HEADS)]
    s = [_dot_nt(q[:, sl], km_ref[:, sl]) for sl in cols]
    p = [jnp.exp2(sh - jnp.max(sh, axis=-1, keepdims=True)) for sh in s]
    o = [_dot(ph.astype(BF16), vm_ref[:, sl]) for ph, sl in zip(p, cols)]
    heads = jnp.concatenate([oh / jnp.sum(ph, axis=-1, keepdims=True) for oh, ph in zip(o, p)],
                            axis=1).astype(BF16)

    def route(c, logits):
        scores = 1.0 / (1.0 + jnp.exp(-logits))
        sel_c, i1, i2, gate1, gate2 = _route_rows(scores, scores + rb_ref[...])
        in_group = [jnp.where(i1 == j, gate1, 0.0) + jnp.where(i2 == j, gate2, 0.0)
                    for j in range(EXPERTS_PER_GROUP)]
        gate_rows_ref[rows[c], :] = jnp.concatenate(
            in_group + [jnp.zeros((128 - EXPERTS_PER_GROUP, chunk), F32)], axis=0).T
        return sel_c

    sel_parts = []
    pending = []

    def norm2_router(c, xa):
        if pending:
            sel_parts.append(route(c - 1, pending.pop()))
        x2 = _layer_norm(ALPHA * x1[rows[c], :] + xa, g2_ref[...], b2_ref[...])
        x2_ref[rows[c], :] = x2
        x_hi, x_lo = _split_bf16(x2)
        both = _dot_nt(w_both, x_hi)
        pending.append(both[0:N_EXPERTS] + (_dot_nt(w_hi, x_lo) + both[N_EXPERTS:2 * N_EXPERTS]))

    skewed(lambda c: _dot(heads[rows[c], :], wo_ref[...]), norm2_router)
    sel_parts.append(route(n_chunks - 1, pending.pop()))
    sel = jnp.concatenate(sel_parts, axis=1)
    hit = lax.broadcasted_iota(jnp.int32, (N_EXPERTS, tm), 0) == sel
    cnt = jnp.where(hit, 1.0, 0.0)
    base = base_ref[...]
    before = _dot(cnt.astype(BF16), tri_ref[...]) - cnt + base[:, 0:1]
    rank = jnp.sum(jnp.where(hit, before, 0.0), axis=0, keepdims=True).astype(jnp.int32)
    base = base + jnp.sum(cnt, axis=1, keepdims=True)
    base_ref[...] = base
    cnt_ref[...] = base
    ti = pl.program_id(1) % tiles_per_block
    pk_ref[ti] = sel * RANK_RADIX + rank

    @pl.when(ti == tiles_per_block - 1)
    def _():
        starts = []
        start = jnp.zeros((1, 128), F32)
        for g in range(N_GROUPS):
            starts.append(start.astype(jnp.int32)[:, 0:1])
            tiles = jnp.floor((base[g:g + 1, :] + (MOE_TILE - 1)) * (1.0 / MOE_TILE))
            start = start + tiles * MOE_TILE
        for tj in range(tiles_per_block):
            pk = pk_ref[tj]
            g_of = pk >> RANK_BITS
            slot = pk & (RANK_RADIX - 1)
            for g in range(N_GROUPS):
                slot = slot + jnp.where(g_of == g, starts[g], 0)
            pos_ref[:, tj * tm:(tj + 1) * tm] = slot


def _mix_xa(x2d, rs, at, km, vm, l, w_out, g1, b1, wq, wo, g2, b2, rw_t, rb_col, tri, batch, seq, tm, nblk):
    t = x2d.shape[0]
    nb = seq // tm
    n_tiles = t // tm
    tpb = nblk // tm
    row = lambda w: pl.BlockSpec((tm, w), lambda b, i: (b * nb + i, 0))
    lay = lambda a: pl.BlockSpec((None,) + a.shape[1:], lambda b, i: (l,) + (0,) * (a.ndim - 1))
    full = lambda a: pl.BlockSpec(a.shape, lambda b, i: (0,) * a.ndim)
    memb = pl.BlockSpec((MEM_LEN, XA_W), lambda b, i: (b, 0))
    blk = pl.BlockSpec((None, 1, nblk), lambda b, i: ((b * nb + i) // tpb, 0, 0))
    blk_i = jax.ShapeDtypeStruct((t // nblk, 1, nblk), jnp.int32)
    return pl.pallas_call(
        functools.partial(_mix_xa_kernel, tm=tm, tiles_per_block=tpb),
        grid=(batch, nb),
        in_specs=[row(D_MODEL), row(512), row(512), lay(w_out), lay(g1), lay(b1), lay(wq), memb, memb,
                  lay(wo), lay(g2), lay(b2), full(rw_t), full(rb_col), full(tri)],
        out_specs=[row(D_MODEL), blk, row(128),
                   pl.BlockSpec((None, N_EXPERTS, 128), lambda b, i: ((b * nb + i) // tpb, 0, 0))],
        out_shape=[jax.ShapeDtypeStruct((t, D_MODEL), F32), blk_i, jax.ShapeDtypeStruct((t, 128), F32),
                   jax.ShapeDtypeStruct((t // nblk, N_EXPERTS, 128), F32)],
        scratch_shapes=[pltpu.VMEM((N_EXPERTS, 128), F32), pltpu.VMEM((tpb, 1, tm), jnp.int32)],
        compiler_params=_cparams("parallel", "arbitrary"),
        name="mix_xattn_router",
    )(x2d, rs, at, w_out, g1, b1, wq, km, vm, wo, g2, b2, rw_t, rb_col, tri)


def _pack_bf16_pairs(x):
    k = x.shape[1] // 2
    hi = pltpu.bitcast(x[:, :k].astype(BF16).astype(F32), jnp.uint32)
    lo = pltpu.bitcast(x[:, k:].astype(BF16).astype(F32), jnp.uint32)
    return hi | (lo >> 16)


def _unpack_bf16_pairs(words):
    return jnp.concatenate([pltpu.bitcast(w & jnp.uint32(0xFFFF0000), F32) for w in words]
                           + [pltpu.bitcast(w << 16, F32) for w in words], axis=1)


def _group_weights_kernel(w_ref, o_ref):
    for e in range(EXPERTS_PER_GROUP):
        o_ref[:, e * D_EXPERT:(e + 1) * D_EXPERT] = w_ref[e].astype(BF16)


def _group_weights(w):
    depth = w.shape[0]
    return pl.pallas_call(
        _group_weights_kernel,
        grid=(depth, N_GROUPS),
        in_specs=[pl.BlockSpec((None, EXPERTS_PER_GROUP, D_MODEL, D_EXPERT), lambda l, g: (l, g, 0, 0))],
        out_specs=pl.BlockSpec((None, None, D_MODEL, EXPERTS_PER_GROUP * D_EXPERT), lambda l, g: (l, g, 0, 0)),
        out_shape=jax.ShapeDtypeStruct((depth, N_GROUPS, D_MODEL, EXPERTS_PER_GROUP * D_EXPERT), BF16),
        compiler_params=_cparams("parallel", "parallel"),
        name="group_weights",
    )(w)


def _moe_kernel(cnt_ref, pos_ref, x_ref, gates_ref, wg_ref, wu_ref, wd_ref, g_ref, b_ref,
                o_ref, xg_ref, xs_ref, ys_ref, og_ref, off_ref, ntile_ref, *, ts, k, mp):
    s = pl.program_id(1)
    sg = ts + 8

    @pl.when((s == 0) & (pl.program_id(0) == 0))
    def _():
        xs_ref[...] = jnp.zeros_like(xs_ref)
        xg_ref[...] = jnp.zeros_like(xg_ref)

    @pl.when(s == 0)
    def _():
        start = jnp.int32(0)
        for g in range(N_GROUPS):
            tiles = (cnt_ref[0, g] + (MOE_TILE - 1)) // MOE_TILE
            off_ref[g] = start
            ntile_ref[g] = tiles
            start = start + tiles * MOE_TILE

    @pl.when(s < k)
    def _():
        base = s * ts

        def pack(c):
            rows = slice(c * MOE_TILE, (c + 1) * MOE_TILE)
            words = _pack_bf16_pairs(x_ref[rows, :])
            for j in range(4):
                xg_ref[j * sg + c * MOE_TILE:j * sg + (c + 1) * MOE_TILE, :] = words[:, j * 128:(j + 1) * 128]
            xg_ref[4 * sg + c * MOE_TILE:4 * sg + (c + 1) * MOE_TILE, :] = pltpu.bitcast(
                gates_ref[rows, :], jnp.uint32)

        def scatter(c):
            for tl in range(c * MOE_TILE, (c + 1) * MOE_TILE):
                xs_ref[pl.ds(pos_ref[base + tl], 8, stride=mp), :] = xg_ref[pl.ds(tl, 8, stride=sg), :]

        pack(0)
        for c in range(ts // MOE_TILE):
            if c + 1 < ts // MOE_TILE:
                pack(c + 1)
            scatter(c)

    @pl.when((s >= k) & (s < k + N_GROUPS))
    def _():
        g = s - k
        seg = off_ref[g]

        def row_tiles(tiles):
            r0 = [pl.multiple_of(seg + i * MOE_TILE, MOE_TILE) for i in tiles]
            xb, weight = [], []
            for r in r0:
                words = [xs_ref[pl.ds(c * mp + r, MOE_TILE), :] for c in range(4)]
                gates = pltpu.bitcast(xs_ref[pl.ds(4 * mp + r, MOE_TILE), :], F32)
                xb.append(_unpack_bf16_pairs(words).astype(BF16))
                weight.append(jnp.concatenate([jnp.broadcast_to(gates[:, e:e + 1], (MOE_TILE, D_EXPERT))
                                               for e in range(EXPERTS_PER_GROUP)], axis=1))
            gate_act = [_dot(x, wg_ref[...]) for x in xb]
            up_act = [_dot(x, wu_ref[...]) for x in xb]
            hid = [(_silu(a) * u * w).astype(BF16) for a, u, w in zip(gate_act, up_act, weight)]
            y = [_pack_bf16_pairs(_dot(h, wd_ref[...])) for h in hid]
            for r, yt in zip(r0, y):
                for j in range(4):
                    ys_ref[pl.ds(j * mp + r, MOE_TILE), :] = yt[:, j * 128:(j + 1) * 128]

        def tile_pair(i, carry):
            row_tiles((2 * i, 2 * i + 1))
            return carry

        def tile_last(i, carry):
            row_tiles((n_tiles - 1,))
            return carry

        n_tiles = ntile_ref[g]
        lax.fori_loop(0, n_tiles // 2, tile_pair, 0)
        lax.fori_loop(0, n_tiles % 2, tile_last, 0)

    @pl.when(s >= k + N_GROUPS)
    def _():
        sub = s - (k + N_GROUPS)

        base = sub * ts
        n_chunks = ts // MOE_TILE

        def gather(c):
            for tl in range(c * MOE_TILE, (c + 1) * MOE_TILE):
                og_ref[pl.ds(tl, 4, stride=sg), :] = ys_ref[pl.ds(pos_ref[base + tl], 4, stride=mp), :]

        def norm(c):
            rows = slice(c * MOE_TILE, (c + 1) * MOE_TILE)
            ffn = _unpack_bf16_pairs([og_ref[j * sg + c * MOE_TILE:j * sg + (c + 1) * MOE_TILE, :]
                                      for j in range(4)])
            o_ref[rows, :] = _layer_norm(ALPHA * x_ref[rows, :] + ffn, g_ref[...], b_ref[...])

        gather(0)
        for c in range(n_chunks):
            if c + 1 < n_chunks:
                gather(c + 1)
            norm(c)


def _moe(x2, cnt_blk, pos, gate_rows, l, wg, wu, wd, g, b, nblk, ts):
    t = x2.shape[0]
    k = nblk // ts
    steps = 2 * k + N_GROUPS
    mp = nblk + N_GROUPS * MOE_TILE + 8
    lay = lambda a: pl.BlockSpec((None,) + a.shape[1:], lambda i, s: (l,) + (0,) * (a.ndim - 1))
    grp = lambda a: pl.BlockSpec((None, None) + a.shape[2:],
                                 lambda i, s: (l, jnp.clip(s - k, 0, N_GROUPS - 1), 0, 0))
    x_map = lambda i, s: (i * k + jnp.where(s < k, s, jnp.where(s < k + N_GROUPS, k - 1, s - k - N_GROUPS)), 0)
    g_map = lambda i, s: (i * k + jnp.minimum(s, k - 1), 0)
    o_map = lambda i, s: (i * k + jnp.maximum(s - k - N_GROUPS, 0), 0)
    return pl.pallas_call(
        functools.partial(_moe_kernel, ts=ts, k=k, mp=mp),
        grid=(t // nblk, steps),
        in_specs=[pl.BlockSpec((None, 1, N_EXPERTS), lambda i, s: (i, 0, 0), memory_space=pltpu.SMEM),
                  pl.BlockSpec((nblk,), lambda i, s: (i,), memory_space=pltpu.SMEM),
                  pl.BlockSpec((ts, D_MODEL), x_map), pl.BlockSpec((ts, 128), g_map),
                  grp(wg), grp(wu), grp(wd), lay(g), lay(b)],
        out_specs=pl.BlockSpec((ts, D_MODEL), o_map),
        out_shape=jax.ShapeDtypeStruct((t, D_MODEL), F32),
        scratch_shapes=[pltpu.VMEM((8 * (ts + 8), 128), jnp.uint32), pltpu.VMEM((8 * mp, 128), jnp.uint32),
                        pltpu.VMEM((4 * mp, 128), jnp.uint32), pltpu.VMEM((4 * (ts + 8), 128), jnp.uint32),
                        pltpu.SMEM((N_GROUPS,), jnp.int32), pltpu.SMEM((N_GROUPS,), jnp.int32)],
        compiler_params=_cparams("arbitrary", "arbitrary"),
        name="moe_experts",
    )(cnt_blk, pos, x2, gate_rows, wg, wu, wd, g, b)


def _tile(n, pref):
    t = min(n, pref)
    assert n % t == 0, (n, t)
    return t


def kernel(x, mem, positions, w_in, w_out, sgu_ln_g, sgu_ln_b, sgu_w, sgu_b, mla_q_norm_g, mla_w_uq, mla_kv_norm_g, mla_w_ukv, xa_wq, xa_wk, xa_wv, xa_wo, ln_mix_g, ln_mix_b, ln_xa_g, ln_xa_b, ln_moe_g, ln_moe_b, router_w, router_bias, expert_w_gate, expert_w_up, expert_w_down):
    batch, seq, _ = x.shape
    depth = w_in.shape[0]
    t = batch * seq
    assert seq % CHUNK == 0

    w_in_b = w_in.astype(BF16)
    w_in_p = jnp.concatenate(
        [w_in_b[:, :, :1920], jnp.zeros((depth, D_MODEL, 64), BF16), w_in_b[:, :, 1920:1952],
         jnp.zeros((depth, D_MODEL, 32), BF16)], axis=2)
    w_uq_p = jnp.pad(mla_w_uq.astype(BF16).reshape(depth, MLA_Q_RANK, MLA_HEADS, MLA_NOPE + MLA_ROPE),
                     ((0, 0), (0, 0), (0, 0), (0, HEAD_PAD - MLA_NOPE - MLA_ROPE))
                     ).reshape(depth, MLA_Q_RANK, MLA_HEADS * HEAD_PAD)
    ukv = mla_w_ukv.astype(BF16).reshape(depth, MLA_KV_RANK, MLA_HEADS, MLA_NOPE + MLA_V)
    w_uk_p = jnp.pad(ukv[..., :MLA_NOPE], ((0, 0), (0, 0), (0, 0), (0, HEAD_PAD - MLA_NOPE))
                     ).reshape(depth, MLA_KV_RANK, MLA_HEADS * HEAD_PAD)
    w_uv_p = jnp.pad(ukv[..., MLA_NOPE:], ((0, 0), (0, 0), (0, 0), (0, HEAD_PAD - MLA_V))
                     ).reshape(depth, MLA_KV_RANK, MLA_HEADS * HEAD_PAD)
    w_ukv_p = jnp.concatenate([w_uk_p, w_uv_p], axis=2)
    w_out_b = w_out.astype(BF16)
    wq_b, wk_b, wv_b, wo_b = (a.astype(BF16) for a in (xa_wq, xa_wk, xa_wv, xa_wo))
    wg_b = _group_weights(expert_w_gate)
    wu_b = _group_weights(expert_w_up)
    wd_b = expert_w_down.reshape(depth, N_GROUPS, EXPERTS_PER_GROUP * D_EXPERT, D_MODEL).astype(BF16)
    vec = lambda a: a.reshape(depth, 1, a.shape[-1])
    sgu_w_cat = jnp.transpose(sgu_w, (0, 2, 1, 3)).reshape(depth, CHUNK, SGU_GROUPS * CHUNK)
    sgu_b_lane = jnp.repeat(jnp.transpose(sgu_b, (0, 2, 1)), SGU_W // SGU_GROUPS, axis=2)
    rw_t = router_w.T
    rb_col = router_bias.reshape(N_EXPERTS, 1)
    consts = _retention_consts()

    x2d = x.reshape(t, D_MODEL)
    mem2d = mem.reshape(batch * MEM_LEN, D_MODEL)
    cr, sr, cm, sm = _rope_tables(positions, _tile(t // 2, 512))

    tm_proj = _tile(t, 1024)
    tb = _tile(seq, 1024)
    tq = _tile(seq, 1024)
    tm_mix = _tile(seq, 1024)
    nblk = _tile(seq, 4096)
    ts_moe = _tile(nblk, 512)
    tri = (jnp.arange(tm_mix)[:, None] <= jnp.arange(tm_mix)[None, :]).astype(BF16)
    for l in range(depth):
        rs, q, k, v = _projections(x2d, l, w_in_p, w_uq_p, w_ukv_p, vec(sgu_ln_g), vec(sgu_ln_b),
                                   vec(mla_q_norm_g), vec(mla_kv_norm_g), cr, sr, cm, sm, tm_proj)
        retsgu = _retention_sgu(rs, l, consts, sgu_w_cat, sgu_b_lane, batch, seq, tb)
        attn = _flash_attention(q, k, v, batch, seq, tq, 4)
        km, vm = _memory_kv(mem2d, l, wk_b, wv_b, batch)
        x2, pos, gate_rows, cnts = _mix_xa(
            x2d, retsgu, attn, km, vm, l, w_out_b, vec(ln_mix_g), vec(ln_mix_b), wq_b, wo_b,
            vec(ln_xa_g), vec(ln_xa_b), rw_t, rb_col, tri, batch, seq, tm_mix, nblk)
        cnt_blk = cnts[:, :, 0].astype(jnp.int32).reshape(-1, 1, N_EXPERTS)
        x2d = _moe(x2, cnt_blk, pos.reshape(t), gate_rows, l,
                   wg_b, wu_b, wd_b, vec(ln_moe_g), vec(ln_moe_b), nblk, ts_moe)
    return x2d.reshape(batch, seq, D_MODEL)
```

```python
import functools
import math

import jax
import jax.numpy as jnp
from jax import lax
from jax.experimental import pallas as pl
from jax.experimental.pallas import tpu as pltpu

F32 = jnp.float32
BF16 = jnp.bfloat16

D_MODEL = 1024
DEPTH = 4
MEM_LEN = 256
ROPE_THETA = 10000.0

RET_HEADS = 4
RET_DK = 64
RET_W = 256
CHUNK = 128

SGU_GROUPS = 4
SGU_W = 256

MLA_HEADS = 8
MLA_Q_RANK = 256
MLA_KV_RANK = 128
MLA_NOPE = 64
MLA_ROPE = 32
MLA_V = 64
MLA_W = MLA_HEADS * MLA_V
HEAD_PAD = 128

XA_HEADS = 4
XA_DIM = 128
XA_W = XA_HEADS * XA_DIM

N_EXPERTS = 16
N_GROUPS = 4
EXPERTS_PER_GROUP = 4
D_EXPERT = 256

ALPHA = (2 * DEPTH) ** 0.25
LN_EPS = 1e-5
IN_PAD = 2048
NEG_BIG = -1e30
PROJ_CHUNK = 256
MIX_CHUNK = 256
MOE_TILE = 128
RANK_BITS = 16
RANK_RADIX = 1 << RANK_BITS

VMEM_LIMIT = 56 * 1024 * 1024


def _cparams(*sem):
    return pltpu.CompilerParams(dimension_semantics=sem, vmem_limit_bytes=VMEM_LIMIT)


def _dot(a, b):
    return jnp.dot(a, b, preferred_element_type=F32)


def _dot_nt(a, b):
    return lax.dot_general(a, b, (((1,), (1,)), ((), ())), preferred_element_type=F32)


def _dot_tn(a, b):
    return lax.dot_general(a, b, (((0,), (0,)), ((), ())), preferred_element_type=F32)


def _layer_norm(z, g, b):
    mu = jnp.mean(z, axis=-1, keepdims=True)
    zc = z - mu
    var = jnp.mean(zc * zc, axis=-1, keepdims=True)
    return zc * lax.rsqrt(var + LN_EPS) * g + b


def _rms_norm(z, g):
    ms = jnp.mean(z * z, axis=-1, keepdims=True)
    return z * lax.rsqrt(ms + LN_EPS) * g


def _silu(z):
    return z / (1.0 + jnp.exp(-z))


def _rope(x, cos, sin_signed, half):
    w = x.shape[-1]
    lane = lax.broadcasted_iota(jnp.int32, x.shape, 1)
    rot = jnp.where((lane & half) == 0, pltpu.roll(x, w - half, 1), pltpu.roll(x, half, 1))
    return x * cos + rot * sin_signed


def _split_bf16(x):
    hi = x.astype(BF16)
    lo = (x - hi.astype(F32)).astype(BF16)
    return hi, lo


def _tables_kernel(pos_a_ref, pos_b_ref, invf_ref, cr_ref, sr_ref, cm_ref, sm_ref):
    lane = lax.broadcasted_iota(jnp.int32, (pos_a_ref.shape[0], 128), 1)
    pos = jnp.where(lane < 64, pos_a_ref[...].astype(F32), pos_b_ref[...].astype(F32))
    ang = pos * invf_ref[...]
    c_both = jnp.cos(ang)
    s_both = jnp.sin(ang)

    def tile32(v):
        v0 = jnp.where(lane < 32, v, 0.0)
        v1 = v0 + pltpu.roll(v0, 32, 1)
        return v1 + pltpu.roll(v1, 64, 1)

    for half in range(2):
        c = c_both if half == 0 else pltpu.roll(c_both, 64, 1)
        s = s_both if half == 0 else pltpu.roll(s_both, 64, 1)
        ct = tile32(c)
        st = tile32(s) * jnp.where((lane & 32) == 0, -1.0, 1.0)
        cr_ref[half] = jnp.concatenate([ct, ct], axis=1)
        sr_ref[half] = jnp.concatenate([st, st], axis=1)
        in_src = (lane >= 32) & (lane < 48)
        cmv = jnp.where(in_src, c, 0.0)
        smv = jnp.where(in_src, s, 0.0)
        in_dst = (lane >= 64) & (lane < 96)
        cm_ref[half] = jnp.where(in_dst, pltpu.roll(cmv, 32, 1) + pltpu.roll(cmv, 48, 1), 1.0)
        sm_ref[half] = jnp.where(in_dst, pltpu.roll(smv, 48, 1) - pltpu.roll(smv, 32, 1), 0.0)


def _rope_tables(positions, tm):
    t = positions.size
    half_t = t // 2
    n = half_t // tm
    pos = positions.reshape(t, 1)
    fr = ROPE_THETA ** (-jnp.arange(0, RET_DK, 2, dtype=F32) / RET_DK)
    fm = ROPE_THETA ** (-jnp.arange(0, MLA_ROPE, 2, dtype=F32) / MLA_ROPE)
    one = jnp.concatenate([fr, fm, jnp.zeros((64 - 48,), F32)])
    invf = jnp.concatenate([one, one]).reshape(1, 128)
    out = lambda w: pl.BlockSpec((2, tm, w), lambda i: (0, i, 0))
    tables = pl.pallas_call(
        _tables_kernel,
        grid=(n,),
        in_specs=[pl.BlockSpec((tm, 1), lambda i: (i, 0)),
                  pl.BlockSpec((tm, 1), lambda i: (i + n, 0)),
                  pl.BlockSpec((1, 128), lambda i: (0, 0))],
        out_specs=[out(256), out(256), out(128), out(128)],
        out_shape=[jax.ShapeDtypeStruct((2, half_t, 256), F32), jax.ShapeDtypeStruct((2, half_t, 256), F32),
                   jax.ShapeDtypeStruct((2, half_t, 128), F32), jax.ShapeDtypeStruct((2, half_t, 128), F32)],
        compiler_params=_cparams("parallel"),
        name="rope_tables",
    )(pos, pos, invf)
    return [a.reshape(t, a.shape[-1]) for a in tables]


def _proj_kernel(x_ref, win_ref, wuq_ref, wukv_ref, lng_ref, lnb_ref, qg_ref, kvg_ref,
                 cr_ref, sr_ref, cm_ref, sm_ref, rs_ref, q_ref, k_ref, v_ref, *, q_scale):
    tm = x_ref.shape[0]
    n_chunks = max(tm // PROJ_CHUNK, 1)
    chunk = tm // n_chunks
    lane = lax.broadcasted_iota(jnp.int32, (chunk, MLA_HEADS * HEAD_PAD), 1)
    ones_lane = jnp.where((lane & (HEAD_PAD - 1)) == MLA_V, 1.0, 0.0)

    def in_proj(c):
        rows = slice(c * chunk, (c + 1) * chunk)
        return _dot(x_ref[rows, :].astype(BF16), win_ref[...])

    def finish(c, h):
        rows = slice(c * chunk, (c + 1) * chunk)
        cm = cm_ref[rows, :]
        sm = sm_ref[rows, :]
        cq = _rms_norm(h[:, 1536:1792], qg_ref[...]).astype(BF16)
        q = _dot(cq, wuq_ref[...])
        ckv = _rms_norm(h[:, 1792:1920], kvg_ref[...]).astype(BF16)
        kv = _dot(ckv, wukv_ref[...])
        cr = cr_ref[rows, :]
        sr = sr_ref[rows, :]
        rs_ref[rows, 0:256] = _rope(h[:, 0:256], cr, sr, 32).astype(BF16)
        rs_ref[rows, 256:512] = (_rope(h[:, 256:512], cr, sr, 32) * (RET_DK ** -0.5)).astype(BF16)
        rs_ref[rows, 512:768] = h[:, 512:768].astype(BF16)
        rs_ref[rows, 768:1024] = _silu(h[:, 768:1024]).astype(BF16)
        rs_ref[rows, 1024:1280] = h[:, 1024:1280].astype(BF16)
        rs_ref[rows, 1280:1536] = _layer_norm(h[:, 1280:1536], lng_ref[...], lnb_ref[...]).astype(BF16)
        cm8 = jnp.concatenate([cm] * MLA_HEADS, axis=1)
        sm8 = jnp.concatenate([sm] * MLA_HEADS, axis=1)
        q_ref[rows, :] = (_rope(q, cm8, sm8, 16) * q_scale).astype(BF16)
        kr = _rope(h[:, 1920:2048], cm, sm, 16)
        k_ref[rows, :] = (kv[:, 0:1024] + jnp.concatenate([kr] * MLA_HEADS, axis=1)).astype(BF16)
        v_ref[rows, :] = (kv[:, 1024:2048] + ones_lane).astype(BF16)

    h_prev = in_proj(0)
    for c in range(1, n_chunks):
        h_next = in_proj(c)
        finish(c - 1, h_prev)
        h_prev = h_next
    finish(n_chunks - 1, h_prev)


def _projections(x2d, l, w_in, w_uq, w_ukv, sgu_g, sgu_b, q_g, kv_g, cr, sr, cm, sm, tm):
    t = x2d.shape[0]
    row = lambda w: pl.BlockSpec((tm, w), lambda i: (i, 0))
    lay = lambda a: pl.BlockSpec((None,) + a.shape[1:], lambda i: (l,) + (0,) * (a.ndim - 1))
    q_scale = (MLA_NOPE + MLA_ROPE) ** -0.5 * math.log2(math.e)
    return pl.pallas_call(
        functools.partial(_proj_kernel, q_scale=q_scale),
        grid=(t // tm,),
        in_specs=[row(D_MODEL), lay(w_in), lay(w_uq), lay(w_ukv), lay(sgu_g), lay(sgu_b),
                  lay(q_g), lay(kv_g), row(256), row(256), row(128), row(128)],
        out_specs=[row(1536), row(1024), row(1024), row(1024)],
        out_shape=[jax.ShapeDtypeStruct((t, 1536), BF16)] + [jax.ShapeDtypeStruct((t, 1024), BF16)] * 3,
        compiler_params=_cparams("parallel"),
        name="projections",
    )(x2d, w_in, w_uq, w_ukv, sgu_g, sgu_b, q_g, kv_g, cr, sr, cm, sm)


def _retsgu_kernel(rs_ref, dmat_ref, qdec_ref, kdec_ref, cdec_ref, gavg_ref, sw_ref, sb_ref,
                   out_ref, state_ref, *, n_chunks):
    @pl.when(pl.program_id(1) == 0)
    def _():
        state_ref[...] = jnp.zeros_like(state_ref)

    lane = lax.broadcasted_iota(jnp.int32, (CHUNK, RET_W), 1)
    head_of_lane = lane // RET_DK
    hmask = [head_of_lane == h for h in range(RET_HEADS)]
    row_i = lax.broadcasted_iota(jnp.int32, (CHUNK, SGU_GROUPS * CHUNK), 0)
    col_i = lax.broadcasted_iota(jnp.int32, (CHUNK, SGU_GROUPS * CHUNK), 1)
    sw = jnp.where(row_i >= (col_i & (CHUNK - 1)), sw_ref[...], 0.0).astype(BF16)
    blk = (lax.broadcasted_iota(jnp.int32, (RET_W, RET_W), 0) // RET_DK
           == lax.broadcasted_iota(jnp.int32, (RET_W, RET_W), 1) // RET_DK)
    gavg = gavg_ref[...]
    zero = jnp.zeros((), BF16)

    def group_mean(y):
        hi, lo = _split_bf16(y)
        return _dot(jnp.concatenate([hi, lo], axis=1), gavg)

    cs = range(n_chunks)
    rows = [slice(c * CHUNK, (c + 1) * CHUNK) for c in cs]
    rq = [rs_ref[r, 0:256] for r in rows]
    rk = [rs_ref[r, 256:512] for r in rows]
    rv = [rs_ref[r, 512:768] for r in rows]
    heads_of = lambda a: jnp.concatenate([jnp.where(m, a, zero) for m in hmask], axis=0)

    scores = [_dot_nt(heads_of(rq[c]), rk[c]) * dmat_ref[...] for c in cs]
    kv = [_dot_tn((rk[c].astype(F32) * kdec_ref[...]).astype(BF16), rv[c]) for c in cs]
    mixed = [_dot(sw, heads_of(rs_ref[rows[c], 1280:1536])) + sb_ref[...] for c in cs]
    intra = [_dot(jnp.concatenate([scores[c][h * CHUNK:(h + 1) * CHUNK, :] for h in range(RET_HEADS)],
                                  axis=1).astype(BF16), heads_of(rv[c])) for c in cs]
    states = []
    state = state_ref[...]
    for c in cs:
        states.append(state.astype(BF16))
        state = state * cdec_ref[...] + jnp.where(blk, kv[c], 0.0)
    state_ref[...] = state
    y = jnp.concatenate([intra[c] + _dot((rq[c].astype(F32) * qdec_ref[...]).astype(BF16), states[c])
                         for c in cs], axis=0)
    yc = y - group_mean(y)
    var = group_mean(yc * yc)
    out_ref[:, 0:256] = (rs_ref[:, 768:1024].astype(F32) * (yc * lax.rsqrt(var + LN_EPS))).astype(BF16)
    out_ref[:, 256:512] = (rs_ref[:, 1024:1280].astype(F32) * jnp.concatenate(mixed, axis=0)).astype(BF16)


def _retention_sgu(rs, l, consts, sgu_w_cat, sgu_b_lane, batch, seq, tb):
    dmat, qdec, kdec, cdec, gavg = consts
    n_blocks = seq // tb
    full = lambda a: pl.BlockSpec(a.shape, lambda b, i: (0,) * a.ndim)
    lay = lambda a: pl.BlockSpec((None,) + a.shape[1:], lambda b, i: (l,) + (0,) * (a.ndim - 1))
    return pl.pallas_call(
        functools.partial(_retsgu_kernel, n_chunks=tb // CHUNK),
        grid=(batch, n_blocks),
        in_specs=[pl.BlockSpec((tb, 1536), lambda b, i: (b * n_blocks + i, 0)),
                  full(dmat), full(qdec), full(kdec), full(cdec), full(gavg),
                  lay(sgu_w_cat), lay(sgu_b_lane)],
        out_specs=pl.BlockSpec((tb, 512), lambda b, i: (b * n_blocks + i, 0)),
        out_shape=jax.ShapeDtypeStruct((batch * seq, 512), BF16),
        scratch_shapes=[pltpu.VMEM((RET_W, RET_W), F32)],
        compiler_params=_cparams("parallel", "arbitrary"),
        name="retention_sgu",
    )(rs, dmat, qdec, kdec, cdec, gavg, sgu_w_cat, sgu_b_lane)


def _retention_consts():
    h = jnp.arange(RET_HEADS, dtype=F32)
    log_gamma = jnp.log1p(-(2.0 ** (-5.0 - h)))
    pos = jnp.arange(CHUNK, dtype=F32)
    diff = pos[:, None] - pos[None, :]
    intra = jnp.where(diff >= 0, jnp.exp(log_gamma[:, None, None] * jnp.maximum(diff, 0.0)), 0.0)
    dmat = intra.reshape(RET_HEADS * CHUNK, CHUNK)
    inner = jnp.exp(log_gamma[None, :] * (CHUNK - 1 - pos)[:, None])
    query = jnp.exp(log_gamma[None, :] * (pos + 1)[:, None])
    kdec = jnp.repeat(inner, RET_DK, axis=1)
    qdec = jnp.repeat(query, RET_DK, axis=1)
    chunk_decay = jnp.repeat(jnp.exp(log_gamma * CHUNK), RET_DK)
    blk = jnp.arange(RET_W)[:, None] // RET_DK == jnp.arange(RET_W)[None, :] // RET_DK
    cdec = jnp.where(blk, chunk_decay[:, None], 0.0)
    gavg = jnp.where(blk, 1.0 / RET_DK, 0.0).astype(BF16)
    return dmat, qdec, kdec, cdec, jnp.concatenate([gavg, gavg], axis=0)


def _flash_kernel(q_ref, k_ref, v_ref, o_ref, sa_ref, sb_ref, m_ref, acc_ref, *, tq, hps):
    i = pl.program_id(2)
    tk = tq // 2
    heads = range(hps)
    cols = [slice(hh * HEAD_PAD, (hh + 1) * HEAD_PAD) for hh in heads]

    def scores(j, hh, rows=slice(None)):
        r0 = pl.multiple_of(j * tk, tk)
        return _dot_nt(q_ref[rows, cols[hh]], k_ref[pl.ds(r0, tk), cols[hh]])

    def update(j, hh, s, rows=slice(None)):
        r0 = pl.multiple_of(j * tk, tk)
        m = m_ref[hh, rows, :]
        s_max = s[:, 0:HEAD_PAD]
        for c in range(1, tk // HEAD_PAD):
            s_max = jnp.maximum(s_max, s[:, c * HEAD_PAD:(c + 1) * HEAD_PAD])
        m_new = jnp.maximum(m, jnp.max(s_max, axis=-1, keepdims=True))
        p = jnp.exp2(s - jnp.concatenate([m_new] * (tk // HEAD_PAD), axis=1)).astype(BF16)
        acc_ref[hh, rows, :] = (jnp.exp2(m - m_new) * acc_ref[hh, rows, :]
                                + _dot(p, v_ref[pl.ds(r0, tk), cols[hh]]))
        m_ref[hh, rows, :] = m_new

    m_ref[...] = jnp.full(m_ref.shape, NEG_BIG, F32)
    acc_ref[...] = jnp.zeros(acc_ref.shape, F32)
    for hh in heads:
        sa_ref[hh] = scores(0, hh)

    def pair(t, carry):
        for hh in heads:
            sb_ref[hh] = scores(2 * t + 1, hh)
        for hh in heads:
            update(2 * t, hh, sa_ref[hh])
        for hh in heads:
            sa_ref[hh] = scores(2 * t + 2, hh)
        for hh in heads:
            update(2 * t + 1, hh, sb_ref[hh])
        return carry

    lax.fori_loop(0, i, pair, 0)
    low = slice(tk, tq)
    visible = (lax.broadcasted_iota(jnp.int32, (tq, tk), 1) <= lax.broadcasted_iota(jnp.int32, (tq, tk), 0))
    visible_low = (lax.broadcasted_iota(jnp.int32, (tk, tk), 1) <= lax.broadcasted_iota(jnp.int32, (tk, tk), 0))
    for hh in heads:
        sb_ref[hh, low, :] = scores(2 * i + 1, hh, low)
        update(2 * i, hh, jnp.where(visible, sa_ref[hh], NEG_BIG))
    for hh in heads:
        update(2 * i + 1, hh, jnp.where(visible_low, sb_ref[hh, low, :], NEG_BIG), low)
    lane = lax.broadcasted_iota(jnp.int32, (tq, HEAD_PAD), 1)
    for pr in range(hps // 2):
        o = []
        for hh in (2 * pr, 2 * pr + 1):
            acc = acc_ref[hh]
            o.append(acc / acc[:, MLA_V:MLA_V + 1])
        o_ref[:, pr * HEAD_PAD:(pr + 1) * HEAD_PAD] = jnp.where(
            lane < MLA_V, o[0], pltpu.roll(o[1], MLA_V, 1)).astype(BF16)


def _flash_attention(q, k, v, batch, seq, tq, hps):
    nq = seq // tq
    return pl.pallas_call(
        functools.partial(_flash_kernel, tq=tq, hps=hps),
        grid=(batch, MLA_HEADS // hps, nq),
        in_specs=[pl.BlockSpec((tq, hps * HEAD_PAD), lambda b, p, i: (b * nq + i, p)),
                  pl.BlockSpec((seq, hps * HEAD_PAD), lambda b, p, i: (b, p)),
                  pl.BlockSpec((seq, hps * HEAD_PAD), lambda b, p, i: (b, p))],
        out_specs=pl.BlockSpec((tq, hps * MLA_V), lambda b, p, i: (b * nq + i, p)),
        out_shape=jax.ShapeDtypeStruct((batch * seq, MLA_W), BF16),
        scratch_shapes=[pltpu.VMEM((hps, tq, tq // 2), F32), pltpu.VMEM((hps, tq, tq // 2), F32),
                        pltpu.VMEM((hps, tq, HEAD_PAD), F32), pltpu.VMEM((hps, tq, HEAD_PAD), F32)],
        compiler_params=_cparams("parallel", "parallel", "arbitrary"),
        name="flash_attention",
    )(q, k, v)


def _memkv_kernel(mem_ref, wk_ref, wv_ref, k_ref, v_ref):
    m = mem_ref[...].astype(BF16)
    k_ref[...] = _dot(m, wk_ref[...]).astype(BF16)
    v_ref[...] = _dot(m, wv_ref[...]).astype(BF16)


def _memory_kv(mem2d, l, wk, wv, batch):
    lay = lambda a: pl.BlockSpec((None,) + a.shape[1:], lambda b: (l,) + (0,) * (a.ndim - 1))
    blk = pl.BlockSpec((MEM_LEN, XA_W), lambda b: (b, 0))
    return pl.pallas_call(
        _memkv_kernel,
        grid=(batch,),
        in_specs=[pl.BlockSpec((MEM_LEN, D_MODEL), lambda b: (b, 0)), lay(wk), lay(wv)],
        out_specs=[blk, blk],
        out_shape=[jax.ShapeDtypeStruct((batch * MEM_LEN, XA_W), BF16)] * 2,
        compiler_params=_cparams("parallel"),
        name="memory_kv",
    )(mem2d, wk, wv)


def _route_rows(scores, biased):
    s = [scores[e:e + 1, :] for e in range(N_EXPERTS)]
    b = [biased[e:e + 1, :] for e in range(N_EXPERTS)]
    group_scores = []
    for g in range(N_GROUPS):
        b0, b1, b2, b3 = b[4 * g:4 * g + 4]
        hi01, lo01 = jnp.maximum(b0, b1), jnp.minimum(b0, b1)
        hi23, lo23 = jnp.maximum(b2, b3), jnp.minimum(b2, b3)
        top1 = jnp.maximum(hi01, hi23)
        top2 = jnp.maximum(jnp.minimum(hi01, hi23), jnp.maximum(lo01, lo23))
        group_scores.append(top1 + top2)
    best = group_scores[0]
    sel = jnp.zeros_like(best, dtype=jnp.int32)
    for g in range(1, N_GROUPS):
        upd = group_scores[g] > best
        sel = jnp.where(upd, g, sel)
        best = jnp.where(upd, group_scores[g], best)

    def pick(rows, j):
        out = rows[j]
        for g in range(1, N_GROUPS):
            out = jnp.where(sel == g, rows[4 * g + j], out)
        return out

    ib = [pick(b, j) for j in range(EXPERTS_PER_GROUP)]
    isc = [pick(s, j) for j in range(EXPERTS_PER_GROUP)]

    def argmax4(vals):
        bv, bi = vals[0], jnp.zeros_like(sel)
        for j in range(1, EXPERTS_PER_GROUP):
            upd = vals[j] > bv
            bi = jnp.where(upd, j, bi)
            bv = jnp.where(upd, vals[j], bv)
        return bi

    i1 = argmax4(ib)
    i2 = argmax4([jnp.where(i1 == j, -jnp.inf, ib[j]) for j in range(EXPERTS_PER_GROUP)])

    def take(vals, idx):
        out = vals[0]
        for j in range(1, EXPERTS_PER_GROUP):
            out = jnp.where(idx == j, vals[j], out)
        return out

    g1, g2 = take(isc, i1), take(isc, i2)
    den = g1 + g2
    g1, g2 = g1 / den, g2 / den
    return sel, i1, i2, g1, g2


def _mix_xa_kernel(x_ref, rs_ref, at_ref, wout_ref, g1_ref, b1_ref, wq_ref, km_ref, vm_ref, wo_ref,
                   g2_ref, b2_ref, rw_ref, rb_ref, tri_ref, x2_ref, pos_ref, gate_rows_ref,
                   cnt_ref, base_ref, pk_ref, *, tm, tiles_per_block):
    @pl.when(pl.program_id(1) % tiles_per_block == 0)
    def _():
        base_ref[...] = jnp.zeros_like(base_ref)

    w_hi, w_lo = _split_bf16(rw_ref[...])
    w_both = jnp.concatenate([w_hi, w_lo], axis=0)

    n_chunks = max(tm // MIX_CHUNK, 1)
    chunk = tm // n_chunks
    rows = [slice(c * chunk, (c + 1) * chunk) for c in range(n_chunks)]

    def skewed(matmul, finish):
        out = []
        prev = matmul(0)
        for c in range(n_chunks):
            nxt = matmul(c + 1) if c + 1 < n_chunks else None
            out.append(finish(c, prev))
            prev = nxt
        return out

    x1 = jnp.concatenate(skewed(
        lambda c: _dot(jnp.concatenate([rs_ref[rows[c], :], at_ref[rows[c], :]], axis=1), wout_ref[...]),
        lambda c, mix: _layer_norm(ALPHA * x_ref[rows[c], :] + mix, g1_ref[...], b1_ref[...])), axis=0)
    q = (_dot(x1.astype(BF16), wq_ref[...]) * (XA_DIM ** -0.5 * math.log2(math.e))).astype(BF16)
    cols = [slice(h * XA_DIM, (h + 1) * XA_DIM) for h in range(XA_HEADS)]
    score = lambda h: _dot_nt(q[:, cols[h]], km_ref[:, cols[h]])
    head_out = []
    s_prev = score(0)
    for h in range(XA_HEADS):
        s_next = score(h + 1) if h + 1 < XA_HEADS else None
        p = jnp.exp2(s_prev - jnp.max(s_prev, axis=-1, keepdims=True))
        head_out.append(_dot(p.astype(BF16), vm_ref[:, cols[h]]) / jnp.sum(p, axis=-1, keepdims=True))
        s_prev = s_next
    heads = jnp.concatenate(head_out, axis=1).astype(BF16)

    def route(c, logits):
        scores = 1.0 / (1.0 + jnp.exp(-logits))
        sel_c, i1, i2, gate1, gate2 = _route_rows(scores, scores + rb_ref[...])
        in_group = [jnp.where(i1 == j, gate1, 0.0) + jnp.where(i2 == j, gate2, 0.0)
                    for j in range(EXPERTS_PER_GROUP)]
        gate_rows_ref[rows[c], :] = jnp.concatenate(
            in_group + [jnp.zeros((128 - EXPERTS_PER_GROUP, chunk), F32)], axis=0).T
        return sel_c

    sel_parts = []
    pending = []

    def norm2_router(c, xa):
        if pending:
            sel_parts.append(route(c - 1, pending.pop()))
        x2 = _layer_norm(ALPHA * x1[rows[c], :] + xa, g2_ref[...], b2_ref[...])
        x2_ref[rows[c], :] = x2
        x_hi, x_lo = _split_bf16(x2)
        both = _dot_nt(w_both, x_hi)
        pending.append(both[0:N_EXPERTS] + (_dot_nt(w_hi, x_lo) + both[N_EXPERTS:2 * N_EXPERTS]))

    skewed(lambda c: _dot(heads[rows[c], :], wo_ref[...]), norm2_router)
    sel_parts.append(route(n_chunks - 1, pending.pop()))
    sel = jnp.concatenate(sel_parts, axis=1)
    hit = lax.broadcasted_iota(jnp.int32, (N_EXPERTS, tm), 0) == sel
    cnt = jnp.where(hit, 1.0, 0.0)
    base = base_ref[...]
    before = _dot(cnt.astype(BF16), tri_ref[...]) - cnt + base[:, 0:1]
    rank = jnp.sum(jnp.where(hit, before, 0.0), axis=0, keepdims=True).astype(jnp.int32)
    base = base + jnp.sum(cnt, axis=1, keepdims=True)
    base_ref[...] = base
    cnt_ref[...] = base
    ti = pl.program_id(1) % tiles_per_block
    pk_ref[ti] = sel * RANK_RADIX + rank

    @pl.when(ti == tiles_per_block - 1)
    def _():
        starts = []
        start = jnp.zeros((1, 128), F32)
        for g in range(N_GROUPS):
            starts.append(start.astype(jnp.int32)[:, 0:1])
            tiles = jnp.floor((base[g:g + 1, :] + (MOE_TILE - 1)) * (1.0 / MOE_TILE))
            start = start + tiles * MOE_TILE
        for tj in range(tiles_per_block):
            pk = pk_ref[tj]
            g_of = pk >> RANK_BITS
            slot = pk & (RANK_RADIX - 1)
            for g in range(N_GROUPS):
                slot = slot + jnp.where(g_of == g, starts[g], 0)
            pos_ref[:, tj * tm:(tj + 1) * tm] = slot


def _mix_xa(x2d, rs, at, km, vm, l, w_out, g1, b1, wq, wo, g2, b2, rw_t, rb_col, tri, batch, seq, tm, nblk):
    t = x2d.shape[0]
    nb = seq // tm
    n_tiles = t // tm
    tpb = nblk // tm
    row = lambda w: pl.BlockSpec((tm, w), lambda b, i: (b * nb + i, 0))
    lay = lambda a: pl.BlockSpec((None,) + a.shape[1:], lambda b, i: (l,) + (0,) * (a.ndim - 1))
    full = lambda a: pl.BlockSpec(a.shape, lambda b, i: (0,) * a.ndim)
    memb = pl.BlockSpec((MEM_LEN, XA_W), lambda b, i: (b, 0))
    blk = pl.BlockSpec((None, 1, nblk), lambda b, i: ((b * nb + i) // tpb, 0, 0))
    blk_i = jax.ShapeDtypeStruct((t // nblk, 1, nblk), jnp.int32)
    return pl.pallas_call(
        functools.partial(_mix_xa_kernel, tm=tm, tiles_per_block=tpb),
        grid=(batch, nb),
        in_specs=[row(D_MODEL), row(512), row(512), lay(w_out), lay(g1), lay(b1), lay(wq), memb, memb,
                  lay(wo), lay(g2), lay(b2), full(rw_t), full(rb_col), full(tri)],
        out_specs=[row(D_MODEL), blk, row(128),
                   pl.BlockSpec((None, N_EXPERTS, 128), lambda b, i: ((b * nb + i) // tpb, 0, 0))],
        out_shape=[jax.ShapeDtypeStruct((t, D_MODEL), F32), blk_i, jax.ShapeDtypeStruct((t, 128), F32),
                   jax.ShapeDtypeStruct((t // nblk, N_EXPERTS, 128), F32)],
        scratch_shapes=[pltpu.VMEM((N_EXPERTS, 128), F32), pltpu.VMEM((tpb, 1, tm), jnp.int32)],
        compiler_params=_cparams("parallel", "arbitrary"),
        name="mix_xattn_router",
    )(x2d, rs, at, w_out, g1, b1, wq, km, vm, wo, g2, b2, rw_t, rb_col, tri)


def _pack_bf16_pairs(x):
    k = x.shape[1] // 2
    hi = pltpu.bitcast(x[:, :k].astype(BF16).astype(F32), jnp.uint32)
    lo = pltpu.bitcast(x[:, k:].astype(BF16).astype(F32), jnp.uint32)
    return hi | (lo >> 16)


def _unpack_bf16_pairs(words):
    return jnp.concatenate([pltpu.bitcast(w & jnp.uint32(0xFFFF0000), F32) for w in words]
                           + [pltpu.bitcast(w << 16, F32) for w in words], axis=1)


def _group_weights_kernel(w_ref, o_ref):
    for e in range(EXPERTS_PER_GROUP):
        o_ref[:, e * D_EXPERT:(e + 1) * D_EXPERT] = w_ref[e].astype(BF16)


def _group_weights(w):
    depth = w.shape[0]
    return pl.pallas_call(
        _group_weights_kernel,
        grid=(depth, N_GROUPS),
        in_specs=[pl.BlockSpec((None, EXPERTS_PER_GROUP, D_MODEL, D_EXPERT), lambda l, g: (l, g, 0, 0))],
        out_specs=pl.BlockSpec((None, None, D_MODEL, EXPERTS_PER_GROUP * D_EXPERT), lambda l, g: (l, g, 0, 0)),
        out_shape=jax.ShapeDtypeStruct((depth, N_GROUPS, D_MODEL, EXPERTS_PER_GROUP * D_EXPERT), BF16),
        compiler_params=_cparams("parallel", "parallel"),
        name="group_weights",
    )(w)


def _moe_kernel(cnt_ref, pos_ref, x_ref, gates_ref, wg_ref, wu_ref, wd_ref, g_ref, b_ref,
                o_ref, xg_ref, xs_ref, ys_ref, og_ref, off_ref, ntile_ref, *, ts, k, mp):
    s = pl.program_id(1)
    sg = ts + 8

    @pl.when((s == 0) & (pl.program_id(0) == 0))
    def _():
        xs_ref[...] = jnp.zeros_like(xs_ref)
        xg_ref[...] = jnp.zeros_like(xg_ref)

    @pl.when(s == 0)
    def _():
        start = jnp.int32(0)
        for g in range(N_GROUPS):
            tiles = (cnt_ref[0, g] + (MOE_TILE - 1)) // MOE_TILE
            off_ref[g] = start
            ntile_ref[g] = tiles
            start = start + tiles * MOE_TILE

    @pl.when(s < k)
    def _():
        base = s * ts

        def pack(c):
            rows = slice(c * MOE_TILE, (c + 1) * MOE_TILE)
            words = _pack_bf16_pairs(x_ref[rows, :])
            for j in range(4):
                xg_ref[j * sg + c * MOE_TILE:j * sg + (c + 1) * MOE_TILE, :] = words[:, j * 128:(j + 1) * 128]
            xg_ref[4 * sg + c * MOE_TILE:4 * sg + (c + 1) * MOE_TILE, :] = pltpu.bitcast(
                gates_ref[rows, :], jnp.uint32)

        def scatter(c):
            for tl in range(c * MOE_TILE, (c + 1) * MOE_TILE):
                xs_ref[pl.ds(pos_ref[base + tl], 8, stride=mp), :] = xg_ref[pl.ds(tl, 8, stride=sg), :]

        pack(0)
        for c in range(ts // MOE_TILE):
            if c + 1 < ts // MOE_TILE:
                pack(c + 1)
            scatter(c)

    @pl.when((s >= k) & (s < k + N_GROUPS))
    def _():
        g = s - k
        seg = off_ref[g]

        def row_tiles(tiles):
            r0 = [pl.multiple_of(seg + i * MOE_TILE, MOE_TILE) for i in tiles]
            xb, weight = [], []
            for r in r0:
                words = [xs_ref[pl.ds(c * mp + r, MOE_TILE), :] for c in range(4)]
                gates = pltpu.bitcast(xs_ref[pl.ds(4 * mp + r, MOE_TILE), :], F32)
                xb.append(_unpack_bf16_pairs(words).astype(BF16))
                weight.append(jnp.concatenate([jnp.broadcast_to(gates[:, e:e + 1], (MOE_TILE, D_EXPERT))
                                               for e in range(EXPERTS_PER_GROUP)], axis=1))
            gate_act = [_dot(x, wg_ref[...]) for x in xb]
            up_act = [_dot(x, wu_ref[...]) for x in xb]
            hid = [(_silu(a) * u * w).astype(BF16) for a, u, w in zip(gate_act, up_act, weight)]
            y = [_pack_bf16_pairs(_dot(h, wd_ref[...])) for h in hid]
            for r, yt in zip(r0, y):
                for j in range(4):
                    ys_ref[pl.ds(j * mp + r, MOE_TILE), :] = yt[:, j * 128:(j + 1) * 128]

        def tile_pair(i, carry):
            row_tiles((2 * i, 2 * i + 1))
            return carry

        def tile_last(i, carry):
            row_tiles((n_tiles - 1,))
            return carry

        n_tiles = ntile_ref[g]
        lax.fori_loop(0, n_tiles // 2, tile_pair, 0)
        lax.fori_loop(0, n_tiles % 2, tile_last, 0)

    @pl.when(s >= k + N_GROUPS)
    def _():
        sub = s - (k + N_GROUPS)

        base = sub * ts
        n_chunks = ts // MOE_TILE

        def gather(c):
            for tl in range(c * MOE_TILE, (c + 1) * MOE_TILE):
                og_ref[pl.ds(tl, 4, stride=sg), :] = ys_ref[pl.ds(pos_ref[base + tl], 4, stride=mp), :]

        def norm(c):
            rows = slice(c * MOE_TILE, (c + 1) * MOE_TILE)
            ffn = _unpack_bf16_pairs([og_ref[j * sg + c * MOE_TILE:j * sg + (c + 1) * MOE_TILE, :]
                                      for j in range(4)])
            o_ref[rows, :] = _layer_norm(ALPHA * x_ref[rows, :] + ffn, g_ref[...], b_ref[...])

        gather(0)
        for c in range(n_chunks):
            if c + 1 < n_chunks:
                gather(c + 1)
            norm(c)


def _moe(x2, cnt_blk, pos, gate_rows, l, wg, wu, wd, g, b, nblk, ts):
    t = x2.shape[0]
    k = nblk // ts
    steps = 2 * k + N_GROUPS
    mp = nblk + N_GROUPS * MOE_TILE + 8
    lay = lambda a: pl.BlockSpec((None,) + a.shape[1:], lambda i, s: (l,) + (0,) * (a.ndim - 1))
    grp = lambda a: pl.BlockSpec((None, None) + a.shape[2:],
                                 lambda i, s: (l, jnp.clip(s - k, 0, N_GROUPS - 1), 0, 0))
    x_map = lambda i, s: (i * k + jnp.where(s < k, s, jnp.where(s < k + N_GROUPS, k - 1, s - k - N_GROUPS)), 0)
    g_map = lambda i, s: (i * k + jnp.minimum(s, k - 1), 0)
    o_map = lambda i, s: (i * k + jnp.maximum(s - k - N_GROUPS, 0), 0)
    return pl.pallas_call(
        functools.partial(_moe_kernel, ts=ts, k=k, mp=mp),
        grid=(t // nblk, steps),
        in_specs=[pl.BlockSpec((None, 1, N_EXPERTS), lambda i, s: (i, 0, 0), memory_space=pltpu.SMEM),
                  pl.BlockSpec((nblk,), lambda i, s: (i,), memory_space=pltpu.SMEM),
                  pl.BlockSpec((ts, D_MODEL), x_map), pl.BlockSpec((ts, 128), g_map),
                  grp(wg), grp(wu), grp(wd), lay(g), lay(b)],
        out_specs=pl.BlockSpec((ts, D_MODEL), o_map),
        out_shape=jax.ShapeDtypeStruct((t, D_MODEL), F32),
        scratch_shapes=[pltpu.VMEM((8 * (ts + 8), 128), jnp.uint32), pltpu.VMEM((8 * mp, 128), jnp.uint32),
                        pltpu.VMEM((4 * mp, 128), jnp.uint32), pltpu.VMEM((4 * (ts + 8), 128), jnp.uint32),
                        pltpu.SMEM((N_GROUPS,), jnp.int32), pltpu.SMEM((N_GROUPS,), jnp.int32)],
        compiler_params=_cparams("arbitrary", "arbitrary"),
        name="moe_experts",
    )(cnt_blk, pos, x2, gate_rows, wg, wu, wd, g, b)


def _tile(n, pref):
    t = min(n, pref)
    assert n % t == 0, (n, t)
    return t


def kernel(x, mem, positions, w_in, w_out, sgu_ln_g, sgu_ln_b, sgu_w, sgu_b, mla_q_norm_g, mla_w_uq, mla_kv_norm_g, mla_w_ukv, xa_wq, xa_wk, xa_wv, xa_wo, ln_mix_g, ln_mix_b, ln_xa_g, ln_xa_b, ln_moe_g, ln_moe_b, router_w, router_bias, expert_w_gate, expert_w_up, expert_w_down):
    batch, seq, _ = x.shape
    depth = w_in.shape[0]
    t = batch * seq
    assert seq % CHUNK == 0

    w_in_b = w_in.astype(BF16)
    w_in_p = jnp.concatenate(
        [w_in_b[:, :, :1920], jnp.zeros((depth, D_MODEL, 64), BF16), w_in_b[:, :, 1920:1952],
         jnp.zeros((depth, D_MODEL, 32), BF16)], axis=2)
    w_uq_p = jnp.pad(mla_w_uq.astype(BF16).reshape(depth, MLA_Q_RANK, MLA_HEADS, MLA_NOPE + MLA_ROPE),
                     ((0, 0), (0, 0), (0, 0), (0, HEAD_PAD - MLA_NOPE - MLA_ROPE))
                     ).reshape(depth, MLA_Q_RANK, MLA_HEADS * HEAD_PAD)
    ukv = mla_w_ukv.astype(BF16).reshape(depth, MLA_KV_RANK, MLA_HEADS, MLA_NOPE + MLA_V)
    w_uk_p = jnp.pad(ukv[..., :MLA_NOPE], ((0, 0), (0, 0), (0, 0), (0, HEAD_PAD - MLA_NOPE))
                     ).reshape(depth, MLA_KV_RANK, MLA_HEADS * HEAD_PAD)
    w_uv_p = jnp.pad(ukv[..., MLA_NOPE:], ((0, 0), (0, 0), (0, 0), (0, HEAD_PAD - MLA_V))
                     ).reshape(depth, MLA_KV_RANK, MLA_HEADS * HEAD_PAD)
    w_ukv_p = jnp.concatenate([w_uk_p, w_uv_p], axis=2)
    w_out_b = w_out.astype(BF16)
    wq_b, wk_b, wv_b, wo_b = (a.astype(BF16) for a in (xa_wq, xa_wk, xa_wv, xa_wo))
    wg_b = _group_weights(expert_w_gate)
    wu_b = _group_weights(expert_w_up)
    wd_b = expert_w_down.reshape(depth, N_GROUPS, EXPERTS_PER_GROUP * D_EXPERT, D_MODEL).astype(BF16)
    vec = lambda a: a.reshape(depth, 1, a.shape[-1])
    sgu_w_cat = jnp.transpose(sgu_w, (0, 2, 1, 3)).reshape(depth, CHUNK, SGU_GROUPS * CHUNK)
    sgu_b_lane = jnp.repeat(jnp.transpose(sgu_b, (0, 2, 1)), SGU_W // SGU_GROUPS, axis=2)
    rw_t = router_w.T
    rb_col = router_bias.reshape(N_EXPERTS, 1)
    consts = _retention_consts()

    x2d = x.reshape(t, D_MODEL)
    mem2d = mem.reshape(batch * MEM_LEN, D_MODEL)
    cr, sr, cm, sm = _rope_tables(positions, _tile(t // 2, 512))

    tm_proj = _tile(t, 1024)
    tb = _tile(seq, 2048)
    tq = _tile(seq, 1024)
    tm_mix = _tile(seq, 1024)
    nblk = _tile(seq, 4096)
    ts_moe = _tile(nblk, 512)
    tri = (jnp.arange(tm_mix)[:, None] <= jnp.arange(tm_mix)[None, :]).astype(BF16)
    for l in range(depth):
        rs, q, k, v = _projections(x2d, l, w_in_p, w_uq_p, w_ukv_p, vec(sgu_ln_g), vec(sgu_ln_b),
                                   vec(mla_q_norm_g), vec(mla_kv_norm_g), cr, sr, cm, sm, tm_proj)
        retsgu = _retention_sgu(rs, l, consts, sgu_w_cat, sgu_b_lane, batch, seq, tb)
        attn = _flash_attention(q, k, v, batch, seq, tq, 4)
        km, vm = _memory_kv(mem2d, l, wk_b, wv_b, batch)
        x2, pos, gate_rows, cnts = _mix_xa(
            x2d, retsgu, attn, km, vm, l, w_out_b, vec(ln_mix_g), vec(ln_mix_b), wq_b, wo_b,
            vec(ln_xa_g), vec(ln_xa_b), rw_t, rb_col, tri, batch, seq, tm_mix, nblk)
        cnt_blk = cnts[:, :, 0].astype(jnp.int32).reshape(-1, 1, N_EXPERTS)
        x2d = _moe(x2, cnt_blk, pos.reshape(t), gate_rows, l,
                   wg_b, wu_b, wd_b, vec(ln_moe_g), vec(ln_moe_b), nblk, ts_moe)
    return x2d.reshape(batch, seq, D_MODEL)
```

```python
import functools
import math

import jax
import jax.numpy as jnp
from jax import lax
from jax.experimental import pallas as pl
from jax.experimental.pallas import tpu as pltpu

F32 = jnp.float32
BF16 = jnp.bfloat16

D_MODEL = 1024
DEPTH = 4
MEM_LEN = 256
ROPE_THETA = 10000.0

RET_HEADS = 4
RET_DK = 64
RET_W = 256
CHUNK = 128

SGU_GROUPS = 4
SGU_W = 256

MLA_HEADS = 8
MLA_Q_RANK = 256
MLA_KV_RANK = 128
MLA_NOPE = 64
MLA_ROPE = 32
MLA_V = 64
MLA_W = MLA_HEADS * MLA_V
HEAD_PAD = 128

XA_HEADS = 4
XA_DIM = 128
XA_W = XA_HEADS * XA_DIM

N_EXPERTS = 16
N_GROUPS = 4
EXPERTS_PER_GROUP = 4
D_EXPERT = 256

ALPHA = (2 * DEPTH) ** 0.25
LN_EPS = 1e-5
IN_PAD = 2048
NEG_BIG = -1e30
PROJ_CHUNK = 256
MIX_CHUNK = 256
MOE_TILE = 128
RANK_BITS = 16
RANK_RADIX = 1 << RANK_BITS

VMEM_LIMIT = 56 * 1024 * 1024


def _cparams(*sem):
    return pltpu.CompilerParams(dimension_semantics=sem, vmem_limit_bytes=VMEM_LIMIT)


def _dot(a, b):
    return jnp.dot(a, b, preferred_element_type=F32)


def _dot_nt(a, b):
    return lax.dot_general(a, b, (((1,), (1,)), ((), ())), preferred_element_type=F32)


def _dot_tn(a, b):
    return lax.dot_general(a, b, (((0,), (0,)), ((), ())), preferred_element_type=F32)


def _layer_norm(z, g, b):
    mu = jnp.mean(z, axis=-1, keepdims=True)
    zc = z - mu
    var = jnp.mean(zc * zc, axis=-1, keepdims=True)
    return zc * lax.rsqrt(var + LN_EPS) * g + b


def _rms_norm(z, g):
    ms = jnp.mean(z * z, axis=-1, keepdims=True)
    return z * lax.rsqrt(ms + LN_EPS) * g


def _silu(z):
    return z / (1.0 + jnp.exp(-z))


def _rope(x, cos, sin_signed, half):
    w = x.shape[-1]
    lane = lax.broadcasted_iota(jnp.int32, x.shape, 1)
    rot = jnp.where((lane & half) == 0, pltpu.roll(x, w - half, 1), pltpu.roll(x, half, 1))
    return x * cos + rot * sin_signed


def _split_bf16(x):
    hi = x.astype(BF16)
    lo = (x - hi.astype(F32)).astype(BF16)
    return hi, lo


def _tables_kernel(pos_a_ref, pos_b_ref, invf_ref, cr_ref, sr_ref, cm_ref, sm_ref):
    lane = lax.broadcasted_iota(jnp.int32, (pos_a_ref.shape[0], 128), 1)
    pos = jnp.where(lane < 64, pos_a_ref[...].astype(F32), pos_b_ref[...].astype(F32))
    ang = pos * invf_ref[...]
    c_both = jnp.cos(ang)
    s_both = jnp.sin(ang)

    def tile32(v):
        v0 = jnp.where(lane < 32, v, 0.0)
        v1 = v0 + pltpu.roll(v0, 32, 1)
        return v1 + pltpu.roll(v1, 64, 1)

    for half in range(2):
        c = c_both if half == 0 else pltpu.roll(c_both, 64, 1)
        s = s_both if half == 0 else pltpu.roll(s_both, 64, 1)
        ct = tile32(c)
        st = tile32(s) * jnp.where((lane & 32) == 0, -1.0, 1.0)
        cr_ref[half] = jnp.concatenate([ct, ct], axis=1)
        sr_ref[half] = jnp.concatenate([st, st], axis=1)
        in_src = (lane >= 32) & (lane < 48)
        cmv = jnp.where(in_src, c, 0.0)
        smv = jnp.where(in_src, s, 0.0)
        in_dst = (lane >= 64) & (lane < 96)
        cm_ref[half] = jnp.where(in_dst, pltpu.roll(cmv, 32, 1) + pltpu.roll(cmv, 48, 1), 1.0)
        sm_ref[half] = jnp.where(in_dst, pltpu.roll(smv, 48, 1) - pltpu.roll(smv, 32, 1), 0.0)


def _rope_tables(positions, tm):
    t = positions.size
    half_t = t // 2
    n = half_t // tm
    pos = positions.reshape(t, 1)
    fr = ROPE_THETA ** (-jnp.arange(0, RET_DK, 2, dtype=F32) / RET_DK)
    fm = ROPE_THETA ** (-jnp.arange(0, MLA_ROPE, 2, dtype=F32) / MLA_ROPE)
    one = jnp.concatenate([fr, fm, jnp.zeros((64 - 48,), F32)])
    invf = jnp.concatenate([one, one]).reshape(1, 128)
    out = lambda w: pl.BlockSpec((2, tm, w), lambda i: (0, i, 0))
    tables = pl.pallas_call(
        _tables_kernel,
        grid=(n,),
        in_specs=[pl.BlockSpec((tm, 1), lambda i: (i, 0)),
                  pl.BlockSpec((tm, 1), lambda i: (i + n, 0)),
                  pl.BlockSpec((1, 128), lambda i: (0, 0))],
        out_specs=[out(256), out(256), out(128), out(128)],
        out_shape=[jax.ShapeDtypeStruct((2, half_t, 256), F32), jax.ShapeDtypeStruct((2, half_t, 256), F32),
                   jax.ShapeDtypeStruct((2, half_t, 128), F32), jax.ShapeDtypeStruct((2, half_t, 128), F32)],
        compiler_params=_cparams("parallel"),
        name="rope_tables",
    )(pos, pos, invf)
    return [a.reshape(t, a.shape[-1]) for a in tables]


def _proj_kernel(x_ref, win_ref, wuq_ref, wukv_ref, lng_ref, lnb_ref, qg_ref, kvg_ref,
                 cr_ref, sr_ref, cm_ref, sm_ref, rs_ref, q_ref, k_ref, v_ref, *, q_scale):
    tm = x_ref.shape[0]
    n_chunks = max(tm // PROJ_CHUNK, 1)
    chunk = tm // n_chunks
    lane = lax.broadcasted_iota(jnp.int32, (chunk, MLA_HEADS * HEAD_PAD), 1)
    ones_lane = jnp.where((lane & (HEAD_PAD - 1)) == MLA_V, 1.0, 0.0)

    def in_proj(c):
        rows = slice(c * chunk, (c + 1) * chunk)
        return _dot(x_ref[rows, :].astype(BF16), win_ref[...])

    def finish(c, h):
        rows = slice(c * chunk, (c + 1) * chunk)
        cm = cm_ref[rows, :]
        sm = sm_ref[rows, :]
        cq = _rms_norm(h[:, 1536:1792], qg_ref[...]).astype(BF16)
        q = _dot(cq, wuq_ref[...])
        ckv = _rms_norm(h[:, 1792:1920], kvg_ref[...]).astype(BF16)
        kv = _dot(ckv, wukv_ref[...])
        cr = cr_ref[rows, :]
        sr = sr_ref[rows, :]
        rs_ref[rows, 0:256] = _rope(h[:, 0:256], cr, sr, 32).astype(BF16)
        rs_ref[rows, 256:512] = (_rope(h[:, 256:512], cr, sr, 32) * (RET_DK ** -0.5)).astype(BF16)
        rs_ref[rows, 512:768] = h[:, 512:768].astype(BF16)
        rs_ref[rows, 768:1024] = _silu(h[:, 768:1024]).astype(BF16)
        rs_ref[rows, 1024:1280] = h[:, 1024:1280].astype(BF16)
        rs_ref[rows, 1280:1536] = _layer_norm(h[:, 1280:1536], lng_ref[...], lnb_ref[...]).astype(BF16)
        cm8 = jnp.concatenate([cm] * MLA_HEADS, axis=1)
        sm8 = jnp.concatenate([sm] * MLA_HEADS, axis=1)
        q_ref[rows, :] = (_rope(q, cm8, sm8, 16) * q_scale).astype(BF16)
        kr = _rope(h[:, 1920:2048], cm, sm, 16)
        k_ref[rows, :] = (kv[:, 0:1024] + jnp.concatenate([kr] * MLA_HEADS, axis=1)).astype(BF16)
        v_ref[rows, :] = (kv[:, 1024:2048] + ones_lane).astype(BF16)

    h_prev = in_proj(0)
    for c in range(1, n_chunks):
        h_next = in_proj(c)
        finish(c - 1, h_prev)
        h_prev = h_next
    finish(n_chunks - 1, h_prev)


def _projections(x2d, l, w_in, w_uq, w_ukv, sgu_g, sgu_b, q_g, kv_g, cr, sr, cm, sm, tm):
    t = x2d.shape[0]
    row = lambda w: pl.BlockSpec((tm, w), lambda i: (i, 0))
    lay = lambda a: pl.BlockSpec((None,) + a.shape[1:], lambda i: (l,) + (0,) * (a.ndim - 1))
    q_scale = (MLA_NOPE + MLA_ROPE) ** -0.5 * math.log2(math.e)
    return pl.pallas_call(
        functools.partial(_proj_kernel, q_scale=q_scale),
        grid=(t // tm,),
        in_specs=[row(D_MODEL), lay(w_in), lay(w_uq), lay(w_ukv), lay(sgu_g), lay(sgu_b),
                  lay(q_g), lay(kv_g), row(256), row(256), row(128), row(128)],
        out_specs=[row(1536), row(1024), row(1024), row(1024)],
        out_shape=[jax.ShapeDtypeStruct((t, 1536), BF16)] + [jax.ShapeDtypeStruct((t, 1024), BF16)] * 3,
        compiler_params=_cparams("parallel"),
        name="projections",
    )(x2d, w_in, w_uq, w_ukv, sgu_g, sgu_b, q_g, kv_g, cr, sr, cm, sm)


def _retsgu_kernel(rs_ref, dmat_ref, qdec_ref, kdec_ref, cdec_ref, gavg_ref, sw_ref, sb_ref,
                   out_ref, state_ref, *, n_chunks):
    @pl.when(pl.program_id(1) == 0)
    def _():
        state_ref[...] = jnp.zeros_like(state_ref)

    lane = lax.broadcasted_iota(jnp.int32, (CHUNK, RET_W), 1)
    head_of_lane = lane // RET_DK
    hmask = [head_of_lane == h for h in range(RET_HEADS)]
    row_i = lax.broadcasted_iota(jnp.int32, (CHUNK, SGU_GROUPS * CHUNK), 0)
    col_i = lax.broadcasted_iota(jnp.int32, (CHUNK, SGU_GROUPS * CHUNK), 1)
    sw = jnp.where(row_i >= (col_i & (CHUNK - 1)), sw_ref[...], 0.0).astype(BF16)
    blk = (lax.broadcasted_iota(jnp.int32, (RET_W, RET_W), 0) // RET_DK
           == lax.broadcasted_iota(jnp.int32, (RET_W, RET_W), 1) // RET_DK)
    gavg = gavg_ref[...]
    zero = jnp.zeros((), BF16)

    def group_mean(y):
        hi, lo = _split_bf16(y)
        return _dot(jnp.concatenate([hi, lo], axis=1), gavg)

    cs = range(n_chunks)
    rows = [slice(c * CHUNK, (c + 1) * CHUNK) for c in cs]
    rq = [rs_ref[r, 0:256] for r in rows]
    rk = [rs_ref[r, 256:512] for r in rows]
    rv = [rs_ref[r, 512:768] for r in rows]
    heads_of = lambda a: jnp.concatenate([jnp.where(m, a, zero) for m in hmask], axis=0)

    scores = [_dot_nt(heads_of(rq[c]), rk[c]) * dmat_ref[...] for c in cs]
    kv = [_dot_tn((rk[c].astype(F32) * kdec_ref[...]).astype(BF16), rv[c]) for c in cs]
    mixed = [_dot(sw, heads_of(rs_ref[rows[c], 1280:1536])) + sb_ref[...] for c in cs]
    intra = [_dot(jnp.concatenate([scores[c][h * CHUNK:(h + 1) * CHUNK, :] for h in range(RET_HEADS)],
                                  axis=1).astype(BF16), heads_of(rv[c])) for c in cs]
    states = []
    state = state_ref[...]
    for c in cs:
        states.append(state.astype(BF16))
        state = state * cdec_ref[...] + jnp.where(blk, kv[c], 0.0)
    state_ref[...] = state
    y = jnp.concatenate([intra[c] + _dot((rq[c].astype(F32) * qdec_ref[...]).astype(BF16), states[c])
                         for c in cs], axis=0)
    yc = y - group_mean(y)
    var = group_mean(yc * yc)
    out_ref[:, 0:256] = (rs_ref[:, 768:1024].astype(F32) * (yc * lax.rsqrt(var + LN_EPS))).astype(BF16)
    out_ref[:, 256:512] = (rs_ref[:, 1024:1280].astype(F32) * jnp.concatenate(mixed, axis=0)).astype(BF16)


def _retention_sgu(rs, l, consts, sgu_w_cat, sgu_b_lane, batch, seq, tb):
    dmat, qdec, kdec, cdec, gavg = consts
    n_blocks = seq // tb
    full = lambda a: pl.BlockSpec(a.shape, lambda b, i: (0,) * a.ndim)
    lay = lambda a: pl.BlockSpec((None,) + a.shape[1:], lambda b, i: (l,) + (0,) * (a.ndim - 1))
    return pl.pallas_call(
        functools.partial(_retsgu_kernel, n_chunks=tb // CHUNK),
        grid=(batch, n_blocks),
        in_specs=[pl.BlockSpec((tb, 1536), lambda b, i: (b * n_blocks + i, 0)),
                  full(dmat), full(qdec), full(kdec), full(cdec), full(gavg),
                  lay(sgu_w_cat), lay(sgu_b_lane)],
        out_specs=pl.BlockSpec((tb, 512), lambda b, i: (b * n_blocks + i, 0)),
        out_shape=jax.ShapeDtypeStruct((batch * seq, 512), BF16),
        scratch_shapes=[pltpu.VMEM((RET_W, RET_W), F32)],
        compiler_params=_cparams("parallel", "arbitrary"),
        name="retention_sgu",
    )(rs, dmat, qdec, kdec, cdec, gavg, sgu_w_cat, sgu_b_lane)


def _proj_ret_kernel(x_ref, win_ref, wuq_ref, wukv_ref, lng_ref, lnb_ref, qg_ref, kvg_ref, cr_ref, sr_ref,
                     cm_ref, sm_ref, dmat_ref, qdec_ref, kdec_ref, cdec_ref, gavg_ref, sw_ref, sb_ref,
                     q_ref, k_ref, v_ref, out_ref, rs_ref, state_ref, *, q_scale, n_chunks):
    _proj_kernel(x_ref, win_ref, wuq_ref, wukv_ref, lng_ref, lnb_ref, qg_ref, kvg_ref, cr_ref, sr_ref,
                 cm_ref, sm_ref, rs_ref, q_ref, k_ref, v_ref, q_scale=q_scale)
    _retsgu_kernel(rs_ref, dmat_ref, qdec_ref, kdec_ref, cdec_ref, gavg_ref, sw_ref, sb_ref,
                   out_ref, state_ref, n_chunks=n_chunks)


def _proj_retention(x2d, l, w_in, w_uq, w_ukv, sgu_g, sgu_b, q_g, kv_g, cr, sr, cm, sm, consts,
                    sgu_w_cat, sgu_b_lane, batch, seq, tm):
    t = x2d.shape[0]
    nb = seq // tm
    dmat, qdec, kdec, cdec, gavg = consts
    row = lambda w: pl.BlockSpec((tm, w), lambda b, i: (b * nb + i, 0))
    lay = lambda a: pl.BlockSpec((None,) + a.shape[1:], lambda b, i: (l,) + (0,) * (a.ndim - 1))
    full = lambda a: pl.BlockSpec(a.shape, lambda b, i: (0,) * a.ndim)
    q_scale = (MLA_NOPE + MLA_ROPE) ** -0.5 * math.log2(math.e)
    return pl.pallas_call(
        functools.partial(_proj_ret_kernel, q_scale=q_scale, n_chunks=tm // CHUNK),
        grid=(batch, nb),
        in_specs=[row(D_MODEL), lay(w_in), lay(w_uq), lay(w_ukv), lay(sgu_g), lay(sgu_b),
                  lay(q_g), lay(kv_g), row(256), row(256), row(128), row(128),
                  full(dmat), full(qdec), full(kdec), full(cdec), full(gavg), lay(sgu_w_cat), lay(sgu_b_lane)],
        out_specs=[row(1024), row(1024), row(1024), row(512)],
        out_shape=[jax.ShapeDtypeStruct((t, 1024), BF16)] * 3 + [jax.ShapeDtypeStruct((t, 512), BF16)],
        scratch_shapes=[pltpu.VMEM((tm, 1536), BF16), pltpu.VMEM((RET_W, RET_W), F32)],
        compiler_params=_cparams("parallel", "arbitrary"),
        name="projections_retention",
    )(x2d, w_in, w_uq, w_ukv, sgu_g, sgu_b, q_g, kv_g, cr, sr, cm, sm,
      dmat, qdec, kdec, cdec, gavg, sgu_w_cat, sgu_b_lane)


def _retention_consts():
    h = jnp.arange(RET_HEADS, dtype=F32)
    log_gamma = jnp.log1p(-(2.0 ** (-5.0 - h)))
    pos = jnp.arange(CHUNK, dtype=F32)
    diff = pos[:, None] - pos[None, :]
    intra = jnp.where(diff >= 0, jnp.exp(log_gamma[:, None, None] * jnp.maximum(diff, 0.0)), 0.0)
    dmat = intra.reshape(RET_HEADS * CHUNK, CHUNK)
    inner = jnp.exp(log_gamma[None, :] * (CHUNK - 1 - pos)[:, None])
    query = jnp.exp(log_gamma[None, :] * (pos + 1)[:, None])
    kdec = jnp.repeat(inner, RET_DK, axis=1)
    qdec = jnp.repeat(query, RET_DK, axis=1)
    chunk_decay = jnp.repeat(jnp.exp(log_gamma * CHUNK), RET_DK)
    blk = jnp.arange(RET_W)[:, None] // RET_DK == jnp.arange(RET_W)[None, :] // RET_DK
    cdec = jnp.where(blk, chunk_decay[:, None], 0.0)
    gavg = jnp.where(blk, 1.0 / RET_DK, 0.0).astype(BF16)
    return dmat, qdec, kdec, cdec, jnp.concatenate([gavg, gavg], axis=0)


def _flash_kernel(q_ref, k_ref, v_ref, o_ref, sa_ref, sb_ref, m_ref, acc_ref, *, tq, hps):
    i = pl.program_id(2)
    tk = tq // 2
    heads = range(hps)
    cols = [slice(hh * HEAD_PAD, (hh + 1) * HEAD_PAD) for hh in heads]

    def scores(j, hh, rows=slice(None)):
        r0 = pl.multiple_of(j * tk, tk)
        return _dot_nt(q_ref[rows, cols[hh]], k_ref[pl.ds(r0, tk), cols[hh]])

    def update(j, hh, s, rows=slice(None)):
        r0 = pl.multiple_of(j * tk, tk)
        m = m_ref[hh, rows, :]
        s_max = s[:, 0:HEAD_PAD]
        for c in range(1, tk // HEAD_PAD):
            s_max = jnp.maximum(s_max, s[:, c * HEAD_PAD:(c + 1) * HEAD_PAD])
        m_new = jnp.maximum(m, jnp.max(s_max, axis=-1, keepdims=True))
        p = jnp.exp2(s - jnp.concatenate([m_new] * (tk // HEAD_PAD), axis=1)).astype(BF16)
        acc_ref[hh, rows, :] = (jnp.exp2(m - m_new) * acc_ref[hh, rows, :]
                                + _dot(p, v_ref[pl.ds(r0, tk), cols[hh]]))
        m_ref[hh, rows, :] = m_new

    m_ref[...] = jnp.full(m_ref.shape, NEG_BIG, F32)
    acc_ref[...] = jnp.zeros(acc_ref.shape, F32)
    for hh in heads:
        sa_ref[hh] = scores(0, hh)

    def pair(t, carry):
        for hh in heads:
            sb_ref[hh] = scores(2 * t + 1, hh)
        for hh in heads:
            update(2 * t, hh, sa_ref[hh])
        for hh in heads:
            sa_ref[hh] = scores(2 * t + 2, hh)
        for hh in heads:
            update(2 * t + 1, hh, sb_ref[hh])
        return carry

    lax.fori_loop(0, i, pair, 0)
    low = slice(tk, tq)
    visible = (lax.broadcasted_iota(jnp.int32, (tq, tk), 1) <= lax.broadcasted_iota(jnp.int32, (tq, tk), 0))
    visible_low = (lax.broadcasted_iota(jnp.int32, (tk, tk), 1) <= lax.broadcasted_iota(jnp.int32, (tk, tk), 0))
    for hh in heads:
        sb_ref[hh, low, :] = scores(2 * i + 1, hh, low)
        update(2 * i, hh, jnp.where(visible, sa_ref[hh], NEG_BIG))
    for hh in heads:
        update(2 * i + 1, hh, jnp.where(visible_low, sb_ref[hh, low, :], NEG_BIG), low)
    lane = lax.broadcasted_iota(jnp.int32, (tq, HEAD_PAD), 1)
    for pr in range(hps // 2):
        o = []
        for hh in (2 * pr, 2 * pr + 1):
            acc = acc_ref[hh]
            o.append(acc / acc[:, MLA_V:MLA_V + 1])
        o_ref[:, pr * HEAD_PAD:(pr + 1) * HEAD_PAD] = jnp.where(
            lane < MLA_V, o[0], pltpu.roll(o[1], MLA_V, 1)).astype(BF16)


def _flash_attention(q, k, v, batch, seq, tq, hps):
    nq = seq // tq
    return pl.pallas_call(
        functools.partial(_flash_kernel, tq=tq, hps=hps),
        grid=(batch, MLA_HEADS // hps, nq),
        in_specs=[pl.BlockSpec((tq, hps * HEAD_PAD), lambda b, p, i: (b * nq + i, p)),
                  pl.BlockSpec((seq, hps * HEAD_PAD), lambda b, p, i: (b, p)),
                  pl.BlockSpec((seq, hps * HEAD_PAD), lambda b, p, i: (b, p))],
        out_specs=pl.BlockSpec((tq, hps * MLA_V), lambda b, p, i: (b * nq + i, p)),
        out_shape=jax.ShapeDtypeStruct((batch * seq, MLA_W), BF16),
        scratch_shapes=[pltpu.VMEM((hps, tq, tq // 2), F32), pltpu.VMEM((hps, tq, tq // 2), F32),
                        pltpu.VMEM((hps, tq, HEAD_PAD), F32), pltpu.VMEM((hps, tq, HEAD_PAD), F32)],
        compiler_params=_cparams("parallel", "parallel", "arbitrary"),
        name="flash_attention",
    )(q, k, v)


def _memkv_kernel(mem_ref, wk_ref, wv_ref, k_ref, v_ref):
    m = mem_ref[...].astype(BF16)
    k_ref[...] = _dot(m, wk_ref[...]).astype(BF16)
    v_ref[...] = _dot(m, wv_ref[...]).astype(BF16)


def _memory_kv(mem2d, l, wk, wv, batch):
    lay = lambda a: pl.BlockSpec((None,) + a.shape[1:], lambda b: (l,) + (0,) * (a.ndim - 1))
    blk = pl.BlockSpec((MEM_LEN, XA_W), lambda b: (b, 0))
    return pl.pallas_call(
        _memkv_kernel,
        grid=(batch,),
        in_specs=[pl.BlockSpec((MEM_LEN, D_MODEL), lambda b: (b, 0)), lay(wk), lay(wv)],
        out_specs=[blk, blk],
        out_shape=[jax.ShapeDtypeStruct((batch * MEM_LEN, XA_W), BF16)] * 2,
        compiler_params=_cparams("parallel"),
        name="memory_kv",
    )(mem2d, wk, wv)


def _route_rows(scores, biased):
    s = [scores[e:e + 1, :] for e in range(N_EXPERTS)]
    b = [biased[e:e + 1, :] for e in range(N_EXPERTS)]
    group_scores = []
    for g in range(N_GROUPS):
        b0, b1, b2, b3 = b[4 * g:4 * g + 4]
        hi01, lo01 = jnp.maximum(b0, b1), jnp.minimum(b0, b1)
        hi23, lo23 = jnp.maximum(b2, b3), jnp.minimum(b2, b3)
        top1 = jnp.maximum(hi01, hi23)
        top2 = jnp.maximum(jnp.minimum(hi01, hi23), jnp.maximum(lo01, lo23))
        group_scores.append(top1 + top2)
    best = group_scores[0]
    sel = jnp.zeros_like(best, dtype=jnp.int32)
    for g in range(1, N_GROUPS):
        upd = group_scores[g] > best
        sel = jnp.where(upd, g, sel)
        best = jnp.where(upd, group_scores[g], best)

    def pick(rows, j):
        out = rows[j]
        for g in range(1, N_GROUPS):
            out = jnp.where(sel == g, rows[4 * g + j], out)
        return out

    ib = [pick(b, j) for j in range(EXPERTS_PER_GROUP)]
    isc = [pick(s, j) for j in range(EXPERTS_PER_GROUP)]

    def argmax4(vals):
        bv, bi = vals[0], jnp.zeros_like(sel)
        for j in range(1, EXPERTS_PER_GROUP):
            upd = vals[j] > bv
            bi = jnp.where(upd, j, bi)
            bv = jnp.where(upd, vals[j], bv)
        return bi

    i1 = argmax4(ib)
    i2 = argmax4([jnp.where(i1 == j, -jnp.inf, ib[j]) for j in range(EXPERTS_PER_GROUP)])

    def take(vals, idx):
        out = vals[0]
        for j in range(1, EXPERTS_PER_GROUP):
            out = jnp.where(idx == j, vals[j], out)
        return out

    g1, g2 = take(isc, i1), take(isc, i2)
    den = g1 + g2
    g1, g2 = g1 / den, g2 / den
    return sel, i1, i2, g1, g2


def _mix_xa_kernel(x_ref, rs_ref, at_ref, wout_ref, g1_ref, b1_ref, wq_ref, km_ref, vm_ref, wo_ref,
                   g2_ref, b2_ref, rw_ref, rb_ref, tri_ref, x2_ref, pos_ref, gate_rows_ref,
                   cnt_ref, base_ref, pk_ref, *, tm, tiles_per_block):
    @pl.when(pl.program_id(1) % tiles_per_block == 0)
    def _():
        base_ref[...] = jnp.zeros_like(base_ref)

    w_hi, w_lo = _split_bf16(rw_ref[...])
    w_both = jnp.concatenate([w_hi, w_lo], axis=0)

    n_chunks = max(tm // MIX_CHUNK, 1)
    chunk = tm // n_chunks
    rows = [slice(c * chunk, (c + 1) * chunk) for c in range(n_chunks)]

    def skewed(matmul, finish):
        out = []
        prev = matmul(0)
        for c in range(n_chunks):
            nxt = matmul(c + 1) if c + 1 < n_chunks else None
            out.append(finish(c, prev))
            prev = nxt
        return out

    x1 = jnp.concatenate(skewed(
        lambda c: _dot(jnp.concatenate([rs_ref[rows[c], :], at_ref[rows[c], :]], axis=1), wout_ref[...]),
        lambda c, mix: _layer_norm(ALPHA * x_ref[rows[c], :] + mix, g1_ref[...], b1_ref[...])), axis=0)
    q = (_dot(x1.astype(BF16), wq_ref[...]) * (XA_DIM ** -0.5 * math.log2(math.e))).astype(BF16)
    cols = [slice(h * XA_DIM, (h + 1) * XA_DIM) for h in range(XA_HEADS)]
    score = lambda h: _dot_nt(q[:, cols[h]], km_ref[:, cols[h]])
    head_out = []
    s_prev = score(0)
    for h in range(XA_HEADS):
        s_next = score(h + 1) if h + 1 < XA_HEADS else None
        p = jnp.exp2(s_prev - jnp.max(s_prev, axis=-1, keepdims=True))
        head_out.append(_dot(p.astype(BF16), vm_ref[:, cols[h]]) / jnp.sum(p, axis=-1, keepdims=True))
        s_prev = s_next
    heads = jnp.concatenate(head_out, axis=1).astype(BF16)

    def route(c, logits):
        scores = 1.0 / (1.0 + jnp.exp(-logits))
        sel_c, i1, i2, gate1, gate2 = _route_rows(scores, scores + rb_ref[...])
        in_group = [jnp.where(i1 == j, gate1, 0.0) + jnp.where(i2 == j, gate2, 0.0)
                    for j in range(EXPERTS_PER_GROUP)]
        gate_rows_ref[rows[c], :] = jnp.concatenate(
            in_group + [jnp.zeros((128 - EXPERTS_PER_GROUP, chunk), F32)], axis=0).T
        return sel_c

    sel_parts = []
    pending = []

    def norm2_router(c, xa):
        if pending:
            sel_parts.append(route(c - 1, pending.pop()))
        x2 = _layer_norm(ALPHA * x1[rows[c], :] + xa, g2_ref[...], b2_ref[...])
        x2_ref[rows[c], :] = x2
        x_hi, x_lo = _split_bf16(x2)
        both = _dot_nt(w_both, x_hi)
        pending.append(both[0:N_EXPERTS] + (_dot_nt(w_hi, x_lo) + both[N_EXPERTS:2 * N_EXPERTS]))

    skewed(lambda c: _dot(heads[rows[c], :], wo_ref[...]), norm2_router)
    sel_parts.append(route(n_chunks - 1, pending.pop()))
    sel = jnp.concatenate(sel_parts, axis=1)
    hit = lax.broadcasted_iota(jnp.int32, (N_EXPERTS, tm), 0) == sel
    cnt = jnp.where(hit, 1.0, 0.0)
    base = base_ref[...]
    before = _dot(cnt.astype(BF16), tri_ref[...]) - cnt + base[:, 0:1]
    rank = jnp.sum(jnp.where(hit, before, 0.0), axis=0, keepdims=True).astype(jnp.int32)
    base = base + jnp.sum(cnt, axis=1, keepdims=True)
    base_ref[...] = base
    cnt_ref[...] = base
    ti = pl.program_id(1) % tiles_per_block
    pk_ref[ti] = sel * RANK_RADIX + rank

    @pl.when(ti == tiles_per_block - 1)
    def _():
        starts = []
        start = jnp.zeros((1, 128), F32)
        for g in range(N_GROUPS):
            starts.append(start.astype(jnp.int32)[:, 0:1])
            tiles = jnp.floor((base[g:g + 1, :] + (MOE_TILE - 1)) * (1.0 / MOE_TILE))
            start = start + tiles * MOE_TILE
        for tj in range(tiles_per_block):
            pk = pk_ref[tj]
            g_of = pk >> RANK_BITS
            slot = pk & (RANK_RADIX - 1)
            for g in range(N_GROUPS):
                slot = slot + jnp.where(g_of == g, starts[g], 0)
            pos_ref[:, tj * tm:(tj + 1) * tm] = slot


def _mix_xa(x2d, rs, at, km, vm, l, w_out, g1, b1, wq, wo, g2, b2, rw_t, rb_col, tri, batch, seq, tm, nblk):
    t = x2d.shape[0]
    nb = seq // tm
    n_tiles = t // tm
    tpb = nblk // tm
    row = lambda w: pl.BlockSpec((tm, w), lambda b, i: (b * nb + i, 0))
    lay = lambda a: pl.BlockSpec((None,) + a.shape[1:], lambda b, i: (l,) + (0,) * (a.ndim - 1))
    full = lambda a: pl.BlockSpec(a.shape, lambda b, i: (0,) * a.ndim)
    memb = pl.BlockSpec((MEM_LEN, XA_W), lambda b, i: (b, 0))
    blk = pl.BlockSpec((None, 1, nblk), lambda b, i: ((b * nb + i) // tpb, 0, 0))
    blk_i = jax.ShapeDtypeStruct((t // nblk, 1, nblk), jnp.int32)
    return pl.pallas_call(
        functools.partial(_mix_xa_kernel, tm=tm, tiles_per_block=tpb),
        grid=(batch, nb),
        in_specs=[row(D_MODEL), row(512), row(512), lay(w_out), lay(g1), lay(b1), lay(wq), memb, memb,
                  lay(wo), lay(g2), lay(b2), full(rw_t), full(rb_col), full(tri)],
        out_specs=[row(D_MODEL), blk, row(128),
                   pl.BlockSpec((None, N_EXPERTS, 128), lambda b, i: ((b * nb + i) // tpb, 0, 0))],
        out_shape=[jax.ShapeDtypeStruct((t, D_MODEL), F32), blk_i, jax.ShapeDtypeStruct((t, 128), F32),
                   jax.ShapeDtypeStruct((t // nblk, N_EXPERTS, 128), F32)],
        scratch_shapes=[pltpu.VMEM((N_EXPERTS, 128), F32), pltpu.VMEM((tpb, 1, tm), jnp.int32)],
        compiler_params=_cparams("parallel", "arbitrary"),
        name="mix_xattn_router",
    )(x2d, rs, at, w_out, g1, b1, wq, km, vm, wo, g2, b2, rw_t, rb_col, tri)


def _pack_bf16_pairs(x):
    k = x.shape[1] // 2
    hi = pltpu.bitcast(x[:, :k].astype(BF16).astype(F32), jnp.uint32)
    lo = pltpu.bitcast(x[:, k:].astype(BF16).astype(F32), jnp.uint32)
    return hi | (lo >> 16)


def _unpack_bf16_pairs(words):
    return jnp.concatenate([pltpu.bitcast(w & jnp.uint32(0xFFFF0000), F32) for w in words]
                           + [pltpu.bitcast(w << 16, F32) for w in words], axis=1)


def _group_weights_kernel(w_ref, o_ref):
    for e in range(EXPERTS_PER_GROUP):
        o_ref[:, e * D_EXPERT:(e + 1) * D_EXPERT] = w_ref[e].astype(BF16)


def _group_weights(w):
    depth = w.shape[0]
    return pl.pallas_call(
        _group_weights_kernel,
        grid=(depth, N_GROUPS),
        in_specs=[pl.BlockSpec((None, EXPERTS_PER_GROUP, D_MODEL, D_EXPERT), lambda l, g: (l, g, 0, 0))],
        out_specs=pl.BlockSpec((None, None, D_MODEL, EXPERTS_PER_GROUP * D_EXPERT), lambda l, g: (l, g, 0, 0)),
        out_shape=jax.ShapeDtypeStruct((depth, N_GROUPS, D_MODEL, EXPERTS_PER_GROUP * D_EXPERT), BF16),
        compiler_params=_cparams("parallel", "parallel"),
        name="group_weights",
    )(w)


def _moe_kernel(cnt_ref, pos_ref, x_ref, gates_ref, wg_ref, wu_ref, wd_ref, g_ref, b_ref,
                o_ref, xg_ref, xs_ref, ys_ref, og_ref, off_ref, ntile_ref, *, ts, k, mp):
    s = pl.program_id(1)
    sg = ts + 8

    @pl.when((s == 0) & (pl.program_id(0) == 0))
    def _():
        xs_ref[...] = jnp.zeros_like(xs_ref)
        xg_ref[...] = jnp.zeros_like(xg_ref)

    @pl.when(s == 0)
    def _():
        start = jnp.int32(0)
        for g in range(N_GROUPS):
            tiles = (cnt_ref[0, g] + (MOE_TILE - 1)) // MOE_TILE
            off_ref[g] = start
            ntile_ref[g] = tiles
            start = start + tiles * MOE_TILE

    @pl.when(s < k)
    def _():
        base = s * ts

        def pack(c):
            rows = slice(c * MOE_TILE, (c + 1) * MOE_TILE)
            words = _pack_bf16_pairs(x_ref[rows, :])
            for j in range(4):
                xg_ref[j * sg + c * MOE_TILE:j * sg + (c + 1) * MOE_TILE, :] = words[:, j * 128:(j + 1) * 128]
            xg_ref[4 * sg + c * MOE_TILE:4 * sg + (c + 1) * MOE_TILE, :] = pltpu.bitcast(
                gates_ref[rows, :], jnp.uint32)

        def scatter(c):
            for tl in range(c * MOE_TILE, (c + 1) * MOE_TILE):
                xs_ref[pl.ds(pos_ref[base + tl], 8, stride=mp), :] = xg_ref[pl.ds(tl, 8, stride=sg), :]

        pack(0)
        for c in range(ts // MOE_TILE):
            if c + 1 < ts // MOE_TILE:
                pack(c + 1)
            scatter(c)

    @pl.when((s >= k) & (s < k + N_GROUPS))
    def _():
        g = s - k
        seg = off_ref[g]

        def row_tiles(tiles):
            r0 = [pl.multiple_of(seg + i * MOE_TILE, MOE_TILE) for i in tiles]
            xb, weight = [], []
            for r in r0:
                words = [xs_ref[pl.ds(c * mp + r, MOE_TILE), :] for c in range(4)]
                gates = pltpu.bitcast(xs_ref[pl.ds(4 * mp + r, MOE_TILE), :], F32)
                xb.append(_unpack_bf16_pairs(words).astype(BF16))
                weight.append(jnp.concatenate([jnp.broadcast_to(gates[:, e:e + 1], (MOE_TILE, D_EXPERT))
                                               for e in range(EXPERTS_PER_GROUP)], axis=1))
            gate_act = [_dot(x, wg_ref[...]) for x in xb]
            up_act = [_dot(x, wu_ref[...]) for x in xb]
            hid = [(_silu(a) * u * w).astype(BF16) for a, u, w in zip(gate_act, up_act, weight)]
            y = [_pack_bf16_pairs(_dot(h, wd_ref[...])) for h in hid]
            for r, yt in zip(r0, y):
                for j in range(4):
                    ys_ref[pl.ds(j * mp + r, MOE_TILE), :] = yt[:, j * 128:(j + 1) * 128]

        def tile_pair(i, carry):
            row_tiles((2 * i, 2 * i + 1))
            return carry

        def tile_last(i, carry):
            row_tiles((n_tiles - 1,))
            return carry

        n_tiles = ntile_ref[g]
        lax.fori_loop(0, n_tiles // 2, tile_pair, 0)
        lax.fori_loop(0, n_tiles % 2, tile_last, 0)

    @pl.when(s >= k + N_GROUPS)
    def _():
        sub = s - (k + N_GROUPS)

        base = sub * ts
        n_chunks = ts // MOE_TILE

        def gather(c):
            for tl in range(c * MOE_TILE, (c + 1) * MOE_TILE):
                og_ref[pl.ds(tl, 4, stride=sg), :] = ys_ref[pl.ds(pos_ref[base + tl], 4, stride=mp), :]

        def norm(c):
            rows = slice(c * MOE_TILE, (c + 1) * MOE_TILE)
            ffn = _unpack_bf16_pairs([og_ref[j * sg + c * MOE_TILE:j * sg + (c + 1) * MOE_TILE, :]
                                      for j in range(4)])
            o_ref[rows, :] = _layer_norm(ALPHA * x_ref[rows, :] + ffn, g_ref[...], b_ref[...])

        gather(0)
        for c in range(n_chunks):
            if c + 1 < n_chunks:
                gather(c + 1)
            norm(c)


def _moe(x2, cnt_blk, pos, gate_rows, l, wg, wu, wd, g, b, nblk, ts):
    t = x2.shape[0]
    k = nblk // ts
    steps = 2 * k + N_GROUPS
    mp = nblk + N_GROUPS * MOE_TILE + 8
    lay = lambda a: pl.BlockSpec((None,) + a.shape[1:], lambda i, s: (l,) + (0,) * (a.ndim - 1))
    grp = lambda a: pl.BlockSpec((None, None) + a.shape[2:],
                                 lambda i, s: (l, jnp.clip(s - k, 0, N_GROUPS - 1), 0, 0))
    x_map = lambda i, s: (i * k + jnp.where(s < k, s, jnp.where(s < k + N_GROUPS, k - 1, s - k - N_GROUPS)), 0)
    g_map = lambda i, s: (i * k + jnp.minimum(s, k - 1), 0)
    o_map = lambda i, s: (i * k + jnp.maximum(s - k - N_GROUPS, 0), 0)
    return pl.pallas_call(
        functools.partial(_moe_kernel, ts=ts, k=k, mp=mp),
        grid=(t // nblk, steps),
        in_specs=[pl.BlockSpec((None, 1, N_EXPERTS), lambda i, s: (i, 0, 0), memory_space=pltpu.SMEM),
                  pl.BlockSpec((nblk,), lambda i, s: (i,), memory_space=pltpu.SMEM),
                  pl.BlockSpec((ts, D_MODEL), x_map), pl.BlockSpec((ts, 128), g_map),
                  grp(wg), grp(wu), grp(wd), lay(g), lay(b)],
        out_specs=pl.BlockSpec((ts, D_MODEL), o_map),
        out_shape=jax.ShapeDtypeStruct((t, D_MODEL), F32),
        scratch_shapes=[pltpu.VMEM((8 * (ts + 8), 128), jnp.uint32), pltpu.VMEM((8 * mp, 128), jnp.uint32),
                        pltpu.VMEM((4 * mp, 128), jnp.uint32), pltpu.VMEM((4 * (ts + 8), 128), jnp.uint32),
                        pltpu.SMEM((N_GROUPS,), jnp.int32), pltpu.SMEM((N_GROUPS,), jnp.int32)],
        compiler_params=_cparams("arbitrary", "arbitrary"),
        name="moe_experts",
    )(cnt_blk, pos, x2, gate_rows, wg, wu, wd, g, b)


def _tile(n, pref):
    t = min(n, pref)
    assert n % t == 0, (n, t)
    return t


def kernel(x, mem, positions, w_in, w_out, sgu_ln_g, sgu_ln_b, sgu_w, sgu_b, mla_q_norm_g, mla_w_uq, mla_kv_norm_g, mla_w_ukv, xa_wq, xa_wk, xa_wv, xa_wo, ln_mix_g, ln_mix_b, ln_xa_g, ln_xa_b, ln_moe_g, ln_moe_b, router_w, router_bias, expert_w_gate, expert_w_up, expert_w_down):
    batch, seq, _ = x.shape
    depth = w_in.shape[0]
    t = batch * seq
    assert seq % CHUNK == 0

    w_in_b = w_in.astype(BF16)
    w_in_p = jnp.concatenate(
        [w_in_b[:, :, :1920], jnp.zeros((depth, D_MODEL, 64), BF16), w_in_b[:, :, 1920:1952],
         jnp.zeros((depth, D_MODEL, 32), BF16)], axis=2)
    w_uq_p = jnp.pad(mla_w_uq.astype(BF16).reshape(depth, MLA_Q_RANK, MLA_HEADS, MLA_NOPE + MLA_ROPE),
                     ((0, 0), (0, 0), (0, 0), (0, HEAD_PAD - MLA_NOPE - MLA_ROPE))
                     ).reshape(depth, MLA_Q_RANK, MLA_HEADS * HEAD_PAD)
    ukv = mla_w_ukv.astype(BF16).reshape(depth, MLA_KV_RANK, MLA_HEADS, MLA_NOPE + MLA_V)
    w_uk_p = jnp.pad(ukv[..., :MLA_NOPE], ((0, 0), (0, 0), (0, 0), (0, HEAD_PAD - MLA_NOPE))
                     ).reshape(depth, MLA_KV_RANK, MLA_HEADS * HEAD_PAD)
    w_uv_p = jnp.pad(ukv[..., MLA_NOPE:], ((0, 0), (0, 0), (0, 0), (0, HEAD_PAD - MLA_V))
                     ).reshape(depth, MLA_KV_RANK, MLA_HEADS * HEAD_PAD)
    w_ukv_p = jnp.concatenate([w_uk_p, w_uv_p], axis=2)
    w_out_b = w_out.astype(BF16)
    wq_b, wk_b, wv_b, wo_b = (a.astype(BF16) for a in (xa_wq, xa_wk, xa_wv, xa_wo))
    wg_b = _group_weights(expert_w_gate)
    wu_b = _group_weights(expert_w_up)
    wd_b = expert_w_down.reshape(depth, N_GROUPS, EXPERTS_PER_GROUP * D_EXPERT, D_MODEL).astype(BF16)
    vec = lambda a: a.reshape(depth, 1, a.shape[-1])
    sgu_w_cat = jnp.transpose(sgu_w, (0, 2, 1, 3)).reshape(depth, CHUNK, SGU_GROUPS * CHUNK)
    sgu_b_lane = jnp.repeat(jnp.transpose(sgu_b, (0, 2, 1)), SGU_W // SGU_GROUPS, axis=2)
    rw_t = router_w.T
    rb_col = router_bias.reshape(N_EXPERTS, 1)
    consts = _retention_consts()

    x2d = x.reshape(t, D_MODEL)
    mem2d = mem.reshape(batch * MEM_LEN, D_MODEL)
    cr, sr, cm, sm = _rope_tables(positions, _tile(t // 2, 512))

    tm_proj = _tile(t, 1024)
    tb = _tile(seq, 2048)
    tq = _tile(seq, 1024)
    tm_mix = _tile(seq, 1024)
    nblk = _tile(seq, 4096)
    ts_moe = _tile(nblk, 512)
    tri = (jnp.arange(tm_mix)[:, None] <= jnp.arange(tm_mix)[None, :]).astype(BF16)
    for l in range(depth):
        q, k, v, retsgu = _proj_retention(
            x2d, l, w_in_p, w_uq_p, w_ukv_p, vec(sgu_ln_g), vec(sgu_ln_b), vec(mla_q_norm_g), vec(mla_kv_norm_g),
            cr, sr, cm, sm, consts, sgu_w_cat, sgu_b_lane, batch, seq, _tile(seq, 1024))
        attn = _flash_attention(q, k, v, batch, seq, tq, 4)
        km, vm = _memory_kv(mem2d, l, wk_b, wv_b, batch)
        x2, pos, gate_rows, cnts = _mix_xa(
            x2d, retsgu, attn, km, vm, l, w_out_b, vec(ln_mix_g), vec(ln_mix_b), wq_b, wo_b,
            vec(ln_xa_g), vec(ln_xa_b), rw_t, rb_col, tri, batch, seq, tm_mix, nblk)
        cnt_blk = cnts[:, :, 0].astype(jnp.int32).reshape(-1, 1, N_EXPERTS)
        x2d = _moe(x2, cnt_blk, pos.reshape(t), gate_rows, l,
                   wg_b, wu_b, wd_b, vec(ln_moe_g), vec(ln_moe_b), nblk, ts_moe)
    return x2d.reshape(batch, seq, D_MODEL)
```

```python
import functools
import math

import jax
import jax.numpy as jnp
from jax import lax
from jax.experimental import pallas as pl
from jax.experimental.pallas import tpu as pltpu

F32 = jnp.float32
BF16 = jnp.bfloat16

D_MODEL = 1024
DEPTH = 4
MEM_LEN = 256
ROPE_THETA = 10000.0

RET_HEADS = 4
RET_DK = 64
RET_W = 256
CHUNK = 128

SGU_GROUPS = 4
SGU_W = 256

MLA_HEADS = 8
MLA_Q_RANK = 256
MLA_KV_RANK = 128
MLA_NOPE = 64
MLA_ROPE = 32
MLA_V = 64
MLA_W = MLA_HEADS * MLA_V
HEAD_PAD = 128

XA_HEADS = 4
XA_DIM = 128
XA_W = XA_HEADS * XA_DIM

N_EXPERTS = 16
N_GROUPS = 4
EXPERTS_PER_GROUP = 4
D_EXPERT = 256

ALPHA = (2 * DEPTH) ** 0.25
LN_EPS = 1e-5
IN_PAD = 2048
NEG_BIG = -1e30
PROJ_CHUNK = 256
MIX_CHUNK = 256
MOE_TILE = 128
RANK_BITS = 16
RANK_RADIX = 1 << RANK_BITS

VMEM_LIMIT = 56 * 1024 * 1024


def _cparams(*sem):
    return pltpu.CompilerParams(dimension_semantics=sem, vmem_limit_bytes=VMEM_LIMIT)


def _dot(a, b):
    return jnp.dot(a, b, preferred_element_type=F32)


def _dot_nt(a, b):
    return lax.dot_general(a, b, (((1,), (1,)), ((), ())), preferred_element_type=F32)


def _dot_tn(a, b):
    return lax.dot_general(a, b, (((0,), (0,)), ((), ())), preferred_element_type=F32)


def _layer_norm(z, g, b):
    mu = jnp.mean(z, axis=-1, keepdims=True)
    zc = z - mu
    var = jnp.mean(zc * zc, axis=-1, keepdims=True)
    return zc * lax.rsqrt(var + LN_EPS) * g + b


def _rms_norm(z, g):
    ms = jnp.mean(z * z, axis=-1, keepdims=True)
    return z * lax.rsqrt(ms + LN_EPS) * g


def _silu(z):
    return z / (1.0 + jnp.exp(-z))


def _rope(x, cos, sin_signed, half):
    w = x.shape[-1]
    lane = lax.broadcasted_iota(jnp.int32, x.shape, 1)
    rot = jnp.where((lane & half) == 0, pltpu.roll(x, w - half, 1), pltpu.roll(x, half, 1))
    return x * cos + rot * sin_signed


def _split_bf16(x):
    hi = x.astype(BF16)
    lo = (x - hi.astype(F32)).astype(BF16)
    return hi, lo


def _tables_kernel(pos_a_ref, pos_b_ref, invf_ref, cr_ref, sr_ref, cm_ref, sm_ref):
    lane = lax.broadcasted_iota(jnp.int32, (pos_a_ref.shape[0], 128), 1)
    pos = jnp.where(lane < 64, pos_a_ref[...].astype(F32), pos_b_ref[...].astype(F32))
    ang = pos * invf_ref[...]
    c_both = jnp.cos(ang)
    s_both = jnp.sin(ang)

    def tile32(v):
        v0 = jnp.where(lane < 32, v, 0.0)
        v1 = v0 + pltpu.roll(v0, 32, 1)
        return v1 + pltpu.roll(v1, 64, 1)

    for half in range(2):
        c = c_both if half == 0 else pltpu.roll(c_both, 64, 1)
        s = s_both if half == 0 else pltpu.roll(s_both, 64, 1)
        ct = tile32(c)
        st = tile32(s) * jnp.where((lane & 32) == 0, -1.0, 1.0)
        cr_ref[half] = jnp.concatenate([ct, ct], axis=1)
        sr_ref[half] = jnp.concatenate([st, st], axis=1)
        in_src = (lane >= 32) & (lane < 48)
        cmv = jnp.where(in_src, c, 0.0)
        smv = jnp.where(in_src, s, 0.0)
        in_dst = (lane >= 64) & (lane < 96)
        cm_ref[half] = jnp.where(in_dst, pltpu.roll(cmv, 32, 1) + pltpu.roll(cmv, 48, 1), 1.0)
        sm_ref[half] = jnp.where(in_dst, pltpu.roll(smv, 48, 1) - pltpu.roll(smv, 32, 1), 0.0)


def _rope_tables(positions, tm):
    t = positions.size
    half_t = t // 2
    n = half_t // tm
    pos = positions.reshape(t, 1)
    fr = ROPE_THETA ** (-jnp.arange(0, RET_DK, 2, dtype=F32) / RET_DK)
    fm = ROPE_THETA ** (-jnp.arange(0, MLA_ROPE, 2, dtype=F32) / MLA_ROPE)
    one = jnp.concatenate([fr, fm, jnp.zeros((64 - 48,), F32)])
    invf = jnp.concatenate([one, one]).reshape(1, 128)
    out = lambda w: pl.BlockSpec((2, tm, w), lambda i: (0, i, 0))
    tables = pl.pallas_call(
        _tables_kernel,
        grid=(n,),
        in_specs=[pl.BlockSpec((tm, 1), lambda i: (i, 0)),
                  pl.BlockSpec((tm, 1), lambda i: (i + n, 0)),
                  pl.BlockSpec((1, 128), lambda i: (0, 0))],
        out_specs=[out(256), out(256), out(128), out(128)],
        out_shape=[jax.ShapeDtypeStruct((2, half_t, 256), F32), jax.ShapeDtypeStruct((2, half_t, 256), F32),
                   jax.ShapeDtypeStruct((2, half_t, 128), F32), jax.ShapeDtypeStruct((2, half_t, 128), F32)],
        compiler_params=_cparams("parallel"),
        name="rope_tables",
    )(pos, pos, invf)
    return [a.reshape(t, a.shape[-1]) for a in tables]


def _proj_kernel(x_ref, win_ref, wuq_ref, wukv_ref, lng_ref, lnb_ref, qg_ref, kvg_ref,
                 cr_ref, sr_ref, cm_ref, sm_ref, rs_ref, q_ref, k_ref, v_ref, *, q_scale):
    tm = x_ref.shape[0]
    n_chunks = max(tm // PROJ_CHUNK, 1)
    chunk = tm // n_chunks
    lane = lax.broadcasted_iota(jnp.int32, (chunk, MLA_HEADS * HEAD_PAD), 1)
    ones_lane = jnp.where((lane & (HEAD_PAD - 1)) == MLA_V, 1.0, 0.0)

    def in_proj(c):
        rows = slice(c * chunk, (c + 1) * chunk)
        return _dot(x_ref[rows, :].astype(BF16), win_ref[...])

    def finish(c, h):
        rows = slice(c * chunk, (c + 1) * chunk)
        cm = cm_ref[rows, :]
        sm = sm_ref[rows, :]
        cq = _rms_norm(h[:, 1536:1792], qg_ref[...]).astype(BF16)
        q = _dot(cq, wuq_ref[...])
        ckv = _rms_norm(h[:, 1792:1920], kvg_ref[...]).astype(BF16)
        kv = _dot(ckv, wukv_ref[...])
        cr = cr_ref[rows, :]
        sr = sr_ref[rows, :]
        rs_ref[rows, 0:256] = _rope(h[:, 0:256], cr, sr, 32).astype(BF16)
        rs_ref[rows, 256:512] = (_rope(h[:, 256:512], cr, sr, 32) * (RET_DK ** -0.5)).astype(BF16)
        rs_ref[rows, 512:768] = h[:, 512:768].astype(BF16)
        rs_ref[rows, 768:1024] = _silu(h[:, 768:1024]).astype(BF16)
        rs_ref[rows, 1024:1280] = h[:, 1024:1280].astype(BF16)
        rs_ref[rows, 1280:1536] = _layer_norm(h[:, 1280:1536], lng_ref[...], lnb_ref[...]).astype(BF16)
        cm8 = jnp.concatenate([cm] * MLA_HEADS, axis=1)
        sm8 = jnp.concatenate([sm] * MLA_HEADS, axis=1)
        q_ref[rows, :] = (_rope(q, cm8, sm8, 16) * q_scale).astype(BF16)
        kr = _rope(h[:, 1920:2048], cm, sm, 16)
        k_ref[rows, :] = (kv[:, 0:1024] + jnp.concatenate([kr] * MLA_HEADS, axis=1)).astype(BF16)
        v_ref[rows, :] = (kv[:, 1024:2048] + ones_lane).astype(BF16)

    h_prev = in_proj(0)
    for c in range(1, n_chunks):
        h_next = in_proj(c)
        finish(c - 1, h_prev)
        h_prev = h_next
    finish(n_chunks - 1, h_prev)


def _projections(x2d, l, w_in, w_uq, w_ukv, sgu_g, sgu_b, q_g, kv_g, cr, sr, cm, sm, tm):
    t = x2d.shape[0]
    row = lambda w: pl.BlockSpec((tm, w), lambda i: (i, 0))
    lay = lambda a: pl.BlockSpec((None,) + a.shape[1:], lambda i: (l,) + (0,) * (a.ndim - 1))
    q_scale = (MLA_NOPE + MLA_ROPE) ** -0.5 * math.log2(math.e)
    return pl.pallas_call(
        functools.partial(_proj_kernel, q_scale=q_scale),
        grid=(t // tm,),
        in_specs=[row(D_MODEL), lay(w_in), lay(w_uq), lay(w_ukv), lay(sgu_g), lay(sgu_b),
                  lay(q_g), lay(kv_g), row(256), row(256), row(128), row(128)],
        out_specs=[row(1536), row(1024), row(1024), row(1024)],
        out_shape=[jax.ShapeDtypeStruct((t, 1536), BF16)] + [jax.ShapeDtypeStruct((t, 1024), BF16)] * 3,
        compiler_params=_cparams("parallel"),
        name="projections",
    )(x2d, w_in, w_uq, w_ukv, sgu_g, sgu_b, q_g, kv_g, cr, sr, cm, sm)


def _retsgu_kernel(rs_ref, dmat_ref, qdec_ref, kdec_ref, cdec_ref, gavg_ref, sw_ref, sb_ref,
                   out_ref, state_ref, *, n_chunks):
    @pl.when(pl.program_id(1) == 0)
    def _():
        state_ref[...] = jnp.zeros_like(state_ref)

    lane = lax.broadcasted_iota(jnp.int32, (CHUNK, RET_W), 1)
    head_of_lane = lane // RET_DK
    hmask = [head_of_lane == h for h in range(RET_HEADS)]
    row_i = lax.broadcasted_iota(jnp.int32, (CHUNK, SGU_GROUPS * CHUNK), 0)
    col_i = lax.broadcasted_iota(jnp.int32, (CHUNK, SGU_GROUPS * CHUNK), 1)
    sw = jnp.where(row_i >= (col_i & (CHUNK - 1)), sw_ref[...], 0.0).astype(BF16)
    blk = (lax.broadcasted_iota(jnp.int32, (RET_W, RET_W), 0) // RET_DK
           == lax.broadcasted_iota(jnp.int32, (RET_W, RET_W), 1) // RET_DK)
    gavg = gavg_ref[...]
    zero = jnp.zeros((), BF16)

    def group_mean(y):
        hi, lo = _split_bf16(y)
        return _dot(jnp.concatenate([hi, lo], axis=1), gavg)

    cs = range(n_chunks)
    rows = [slice(c * CHUNK, (c + 1) * CHUNK) for c in cs]
    rq = [rs_ref[r, 0:256] for r in rows]
    rk = [rs_ref[r, 256:512] for r in rows]
    rv = [rs_ref[r, 512:768] for r in rows]
    heads_of = lambda a: jnp.concatenate([jnp.where(m, a, zero) for m in hmask], axis=0)

    scores = [_dot_nt(heads_of(rq[c]), rk[c]) * dmat_ref[...] for c in cs]
    kv = [_dot_tn((rk[c].astype(F32) * kdec_ref[...]).astype(BF16), rv[c]) for c in cs]
    mixed = [_dot(sw, heads_of(rs_ref[rows[c], 1280:1536])) + sb_ref[...] for c in cs]
    intra = [_dot(jnp.concatenate([scores[c][h * CHUNK:(h + 1) * CHUNK, :] for h in range(RET_HEADS)],
                                  axis=1).astype(BF16), heads_of(rv[c])) for c in cs]
    states = []
    state = state_ref[...]
    for c in cs:
        states.append(state.astype(BF16))
        state = state * cdec_ref[...] + jnp.where(blk, kv[c], 0.0)
    state_ref[...] = state
    y = jnp.concatenate([intra[c] + _dot((rq[c].astype(F32) * qdec_ref[...]).astype(BF16), states[c])
                         for c in cs], axis=0)
    yc = y - group_mean(y)
    var = group_mean(yc * yc)
    out_ref[:, 0:256] = (rs_ref[:, 768:1024].astype(F32) * (yc * lax.rsqrt(var + LN_EPS))).astype(BF16)
    out_ref[:, 256:512] = (rs_ref[:, 1024:1280].astype(F32) * jnp.concatenate(mixed, axis=0)).astype(BF16)


def _retention_sgu(rs, l, consts, sgu_w_cat, sgu_b_lane, batch, seq, tb):
    dmat, qdec, kdec, cdec, gavg = consts
    n_blocks = seq // tb
    full = lambda a: pl.BlockSpec(a.shape, lambda b, i: (0,) * a.ndim)
    lay = lambda a: pl.BlockSpec((None,) + a.shape[1:], lambda b, i: (l,) + (0,) * (a.ndim - 1))
    return pl.pallas_call(
        functools.partial(_retsgu_kernel, n_chunks=tb // CHUNK),
        grid=(batch, n_blocks),
        in_specs=[pl.BlockSpec((tb, 1536), lambda b, i: (b * n_blocks + i, 0)),
                  full(dmat), full(qdec), full(kdec), full(cdec), full(gavg),
                  lay(sgu_w_cat), lay(sgu_b_lane)],
        out_specs=pl.BlockSpec((tb, 512), lambda b, i: (b * n_blocks + i, 0)),
        out_shape=jax.ShapeDtypeStruct((batch * seq, 512), BF16),
        scratch_shapes=[pltpu.VMEM((RET_W, RET_W), F32)],
        compiler_params=_cparams("parallel", "arbitrary"),
        name="retention_sgu",
    )(rs, dmat, qdec, kdec, cdec, gavg, sgu_w_cat, sgu_b_lane)


def _retention_consts():
    h = jnp.arange(RET_HEADS, dtype=F32)
    log_gamma = jnp.log1p(-(2.0 ** (-5.0 - h)))
    pos = jnp.arange(CHUNK, dtype=F32)
    diff = pos[:, None] - pos[None, :]
    intra = jnp.where(diff >= 0, jnp.exp(log_gamma[:, None, None] * jnp.maximum(diff, 0.0)), 0.0)
    dmat = intra.reshape(RET_HEADS * CHUNK, CHUNK)
    inner = jnp.exp(log_gamma[None, :] * (CHUNK - 1 - pos)[:, None])
    query = jnp.exp(log_gamma[None, :] * (pos + 1)[:, None])
    kdec = jnp.repeat(inner, RET_DK, axis=1)
    qdec = jnp.repeat(query, RET_DK, axis=1)
    chunk_decay = jnp.repeat(jnp.exp(log_gamma * CHUNK), RET_DK)
    blk = jnp.arange(RET_W)[:, None] // RET_DK == jnp.arange(RET_W)[None, :] // RET_DK
    cdec = jnp.where(blk, chunk_decay[:, None], 0.0)
    gavg = jnp.where(blk, 1.0 / RET_DK, 0.0).astype(BF16)
    return dmat, qdec, kdec, cdec, jnp.concatenate([gavg, gavg], axis=0)


def _flash_kernel(q_ref, k_ref, v_ref, o_ref, sa_ref, sb_ref, m_ref, acc_ref, *, tq, hps):
    i = pl.program_id(2)
    tk = tq // 2
    heads = range(hps)
    cols = [slice(hh * HEAD_PAD, (hh + 1) * HEAD_PAD) for hh in heads]

    def scores(j, hh, rows=slice(None)):
        r0 = pl.multiple_of(j * tk, tk)
        return _dot_nt(q_ref[rows, cols[hh]], k_ref[pl.ds(r0, tk), cols[hh]])

    def update(j, hh, s, rows=slice(None)):
        r0 = pl.multiple_of(j * tk, tk)
        m = m_ref[hh, rows, :]
        s_max = s[:, 0:HEAD_PAD]
        for c in range(1, tk // HEAD_PAD):
            s_max = jnp.maximum(s_max, s[:, c * HEAD_PAD:(c + 1) * HEAD_PAD])
        m_new = jnp.maximum(m, jnp.max(s_max, axis=-1, keepdims=True))
        p = jnp.exp2(s - jnp.concatenate([m_new] * (tk // HEAD_PAD), axis=1)).astype(BF16)
        acc_ref[hh, rows, :] = (jnp.exp2(m - m_new) * acc_ref[hh, rows, :]
                                + _dot(p, v_ref[pl.ds(r0, tk), cols[hh]]))
        m_ref[hh, rows, :] = m_new

    m_ref[...] = jnp.full(m_ref.shape, NEG_BIG, F32)
    acc_ref[...] = jnp.zeros(acc_ref.shape, F32)
    for hh in heads:
        sa_ref[hh] = scores(0, hh)

    def pair(t, carry):
        for hh in heads:
            sb_ref[hh] = scores(2 * t + 1, hh)
        for hh in heads:
            update(2 * t, hh, sa_ref[hh])
        for hh in heads:
            sa_ref[hh] = scores(2 * t + 2, hh)
        for hh in heads:
            update(2 * t + 1, hh, sb_ref[hh])
        return carry

    lax.fori_loop(0, i, pair, 0)
    low = slice(tk, tq)
    visible = (lax.broadcasted_iota(jnp.int32, (tq, tk), 1) <= lax.broadcasted_iota(jnp.int32, (tq, tk), 0))
    visible_low = (lax.broadcasted_iota(jnp.int32, (tk, tk), 1) <= lax.broadcasted_iota(jnp.int32, (tk, tk), 0))
    for hh in heads:
        sb_ref[hh, low, :] = scores(2 * i + 1, hh, low)
        update(2 * i, hh, jnp.where(visible, sa_ref[hh], NEG_BIG))
    for hh in heads:
        update(2 * i + 1, hh, jnp.where(visible_low, sb_ref[hh, low, :], NEG_BIG), low)
    lane = lax.broadcasted_iota(jnp.int32, (tq, HEAD_PAD), 1)
    for pr in range(hps // 2):
        o = []
        for hh in (2 * pr, 2 * pr + 1):
            acc = acc_ref[hh]
            o.append(acc / acc[:, MLA_V:MLA_V + 1])
        o_ref[:, pr * HEAD_PAD:(pr + 1) * HEAD_PAD] = jnp.where(
            lane < MLA_V, o[0], pltpu.roll(o[1], MLA_V, 1)).astype(BF16)


def _flash_attention(q, k, v, batch, seq, tq, hps):
    nq = seq // tq
    return pl.pallas_call(
        functools.partial(_flash_kernel, tq=tq, hps=hps),
        grid=(batch, MLA_HEADS // hps, nq),
        in_specs=[pl.BlockSpec((tq, hps * HEAD_PAD), lambda b, p, i: (b * nq + i, p)),
                  pl.BlockSpec((seq, hps * HEAD_PAD), lambda b, p, i: (b, p)),
                  pl.BlockSpec((seq, hps * HEAD_PAD), lambda b, p, i: (b, p))],
        out_specs=pl.BlockSpec((tq, hps * MLA_V), lambda b, p, i: (b * nq + i, p)),
        out_shape=jax.ShapeDtypeStruct((batch * seq, MLA_W), BF16),
        scratch_shapes=[pltpu.VMEM((hps, tq, tq // 2), F32), pltpu.VMEM((hps, tq, tq // 2), F32),
                        pltpu.VMEM((hps, tq, HEAD_PAD), F32), pltpu.VMEM((hps, tq, HEAD_PAD), F32)],
        compiler_params=_cparams("parallel", "parallel", "arbitrary"),
        name="flash_attention",
    )(q, k, v)


def _memkv_kernel(mem_ref, wk_ref, wv_ref, k_ref, v_ref):
    m = mem_ref[...].astype(BF16)
    k_ref[...] = _dot(m, wk_ref[...]).astype(BF16)
    v_ref[...] = _dot(m, wv_ref[...]).astype(BF16)


def _memory_kv(mem2d, l, wk, wv, batch):
    lay = lambda a: pl.BlockSpec((None,) + a.shape[1:], lambda b: (l,) + (0,) * (a.ndim - 1))
    blk = pl.BlockSpec((MEM_LEN, XA_W), lambda b: (b, 0))
    return pl.pallas_call(
        _memkv_kernel,
        grid=(batch,),
        in_specs=[pl.BlockSpec((MEM_LEN, D_MODEL), lambda b: (b, 0)), lay(wk), lay(wv)],
        out_specs=[blk, blk],
        out_shape=[jax.ShapeDtypeStruct((batch * MEM_LEN, XA_W), BF16)] * 2,
        compiler_params=_cparams("parallel"),
        name="memory_kv",
    )(mem2d, wk, wv)


def _route_rows(scores, biased):
    s = [scores[e:e + 1, :] for e in range(N_EXPERTS)]
    b = [biased[e:e + 1, :] for e in range(N_EXPERTS)]
    group_scores = []
    for g in range(N_GROUPS):
        b0, b1, b2, b3 = b[4 * g:4 * g + 4]
        hi01, lo01 = jnp.maximum(b0, b1), jnp.minimum(b0, b1)
        hi23, lo23 = jnp.maximum(b2, b3), jnp.minimum(b2, b3)
        top1 = jnp.maximum(hi01, hi23)
        top2 = jnp.maximum(jnp.minimum(hi01, hi23), jnp.maximum(lo01, lo23))
        group_scores.append(top1 + top2)
    best = group_scores[0]
    sel = jnp.zeros_like(best, dtype=jnp.int32)
    for g in range(1, N_GROUPS):
        upd = group_scores[g] > best
        sel = jnp.where(upd, g, sel)
        best = jnp.where(upd, group_scores[g], best)

    def pick(rows, j):
        out = rows[j]
        for g in range(1, N_GROUPS):
            out = jnp.where(sel == g, rows[4 * g + j], out)
        return out

    ib = [pick(b, j) for j in range(EXPERTS_PER_GROUP)]
    isc = [pick(s, j) for j in range(EXPERTS_PER_GROUP)]

    def argmax4(vals):
        bv, bi = vals[0], jnp.zeros_like(sel)
        for j in range(1, EXPERTS_PER_GROUP):
            upd = vals[j] > bv
            bi = jnp.where(upd, j, bi)
            bv = jnp.where(upd, vals[j], bv)
        return bi

    i1 = argmax4(ib)
    i2 = argmax4([jnp.where(i1 == j, -jnp.inf, ib[j]) for j in range(EXPERTS_PER_GROUP)])

    def take(vals, idx):
        out = vals[0]
        for j in range(1, EXPERTS_PER_GROUP):
            out = jnp.where(idx == j, vals[j], out)
        return out

    g1, g2 = take(isc, i1), take(isc, i2)
    den = g1 + g2
    g1, g2 = g1 / den, g2 / den
    return sel, i1, i2, g1, g2


def _mix_xa_kernel(x_ref, rs_ref, at_ref, wout_ref, g1_ref, b1_ref, wq_ref, km_ref, vm_ref, wo_ref,
                   g2_ref, b2_ref, rw_ref, rb_ref, tri_ref, x2_ref, pos_ref, gate_rows_ref,
                   cnt_ref, base_ref, pk_ref, *, tm, tiles_per_block):
    @pl.when(pl.program_id(1) % tiles_per_block == 0)
    def _():
        base_ref[...] = jnp.zeros_like(base_ref)

    w_hi, w_lo = _split_bf16(rw_ref[...])
    w_both = jnp.concatenate([w_hi, w_lo], axis=0)

    n_chunks = max(tm // MIX_CHUNK, 1)
    chunk = tm // n_chunks
    rows = [slice(c * chunk, (c + 1) * chunk) for c in range(n_chunks)]

    def skewed(matmul, finish):
        out = []
        prev = matmul(0)
        for c in range(n_chunks):
            nxt = matmul(c + 1) if c + 1 < n_chunks else None
            out.append(finish(c, prev))
            prev = nxt
        return out

    x1 = jnp.concatenate(skewed(
        lambda c: _dot(jnp.concatenate([rs_ref[rows[c], :], at_ref[rows[c], :]], axis=1), wout_ref[...]),
        lambda c, mix: _layer_norm(ALPHA * x_ref[rows[c], :] + mix, g1_ref[...], b1_ref[...])), axis=0)
    q = (_dot(x1.astype(BF16), wq_ref[...]) * (XA_DIM ** -0.5 * math.log2(math.e))).astype(BF16)
    cols = [slice(h * XA_DIM, (h + 1) * XA_DIM) for h in range(XA_HEADS)]
    score = lambda h: _dot_nt(q[:, cols[h]], km_ref[:, cols[h]])
    head_out = []
    s_prev = score(0)
    for h in range(XA_HEADS):
        s_next = score(h + 1) if h + 1 < XA_HEADS else None
        p = jnp.exp2(s_prev - jnp.max(s_prev, axis=-1, keepdims=True))
        head_out.append(_dot(p.astype(BF16), vm_ref[:, cols[h]]) / jnp.sum(p, axis=-1, keepdims=True))
        s_prev = s_next
    heads = jnp.concatenate(head_out, axis=1).astype(BF16)

    def route(c, logits):
        scores = 1.0 / (1.0 + jnp.exp(-logits))
        sel_c, i1, i2, gate1, gate2 = _route_rows(scores, scores + rb_ref[...])
        in_group = [jnp.where(i1 == j, gate1, 0.0) + jnp.where(i2 == j, gate2, 0.0)
                    for j in range(EXPERTS_PER_GROUP)]
        gate_rows_ref[rows[c], :] = jnp.concatenate(
            in_group + [jnp.zeros((128 - EXPERTS_PER_GROUP, chunk), F32)], axis=0).T
        return sel_c

    sel_parts = []
    pending = []

    def norm2_router(c, xa):
        if pending:
            sel_parts.append(route(c - 1, pending.pop()))
        x2 = _layer_norm(ALPHA * x1[rows[c], :] + xa, g2_ref[...], b2_ref[...])
        x2_ref[rows[c], :] = x2
        x_hi, x_lo = _split_bf16(x2)
        both = _dot_nt(w_both, x_hi)
        pending.append(both[0:N_EXPERTS] + (_dot_nt(w_hi, x_lo) + both[N_EXPERTS:2 * N_EXPERTS]))

    skewed(lambda c: _dot(heads[rows[c], :], wo_ref[...]), norm2_router)
    sel_parts.append(route(n_chunks - 1, pending.pop()))
    sel = jnp.concatenate(sel_parts, axis=1)
    hit = lax.broadcasted_iota(jnp.int32, (N_EXPERTS, tm), 0) == sel
    cnt = jnp.where(hit, 1.0, 0.0)
    base = base_ref[...]
    before = _dot(cnt.astype(BF16), tri_ref[...]) - cnt + base[:, 0:1]
    rank = jnp.sum(jnp.where(hit, before, 0.0), axis=0, keepdims=True).astype(jnp.int32)
    base = base + jnp.sum(cnt, axis=1, keepdims=True)
    base_ref[...] = base
    cnt_ref[...] = base
    ti = pl.program_id(1) % tiles_per_block
    pk_ref[ti] = sel * RANK_RADIX + rank

    @pl.when(ti == tiles_per_block - 1)
    def _():
        starts = []
        start = jnp.zeros((1, 128), F32)
        for g in range(N_GROUPS):
            starts.append(start.astype(jnp.int32)[:, 0:1])
            tiles = jnp.floor((base[g:g + 1, :] + (MOE_TILE - 1)) * (1.0 / MOE_TILE))
            start = start + tiles * MOE_TILE
        for tj in range(tiles_per_block):
            pk = pk_ref[tj]
            g_of = pk >> RANK_BITS
            slot = pk & (RANK_RADIX - 1)
            for g in range(N_GROUPS):
                slot = slot + jnp.where(g_of == g, starts[g], 0)
            pos_ref[:, tj * tm:(tj + 1) * tm] = slot


def _mix_xa(x2d, rs, at, km, vm, l, w_out, g1, b1, wq, wo, g2, b2, rw_t, rb_col, tri, batch, seq, tm, nblk):
    t = x2d.shape[0]
    nb = seq // tm
    n_tiles = t // tm
    tpb = nblk // tm
    row = lambda w: pl.BlockSpec((tm, w), lambda b, i: (b * nb + i, 0))
    lay = lambda a: pl.BlockSpec((None,) + a.shape[1:], lambda b, i: (l,) + (0,) * (a.ndim - 1))
    full = lambda a: pl.BlockSpec(a.shape, lambda b, i: (0,) * a.ndim)
    memb = pl.BlockSpec((MEM_LEN, XA_W), lambda b, i: (b, 0))
    blk = pl.BlockSpec((None, 1, nblk), lambda b, i: ((b * nb + i) // tpb, 0, 0))
    blk_i = jax.ShapeDtypeStruct((t // nblk, 1, nblk), jnp.int32)
    return pl.pallas_call(
        functools.partial(_mix_xa_kernel, tm=tm, tiles_per_block=tpb),
        grid=(batch, nb),
        in_specs=[row(D_MODEL), row(512), row(512), lay(w_out), lay(g1), lay(b1), lay(wq), memb, memb,
                  lay(wo), lay(g2), lay(b2), full(rw_t), full(rb_col), full(tri)],
        out_specs=[row(D_MODEL), blk, row(128),
                   pl.BlockSpec((None, N_EXPERTS, 128), lambda b, i: ((b * nb + i) // tpb, 0, 0))],
        out_shape=[jax.ShapeDtypeStruct((t, D_MODEL), F32), blk_i, jax.ShapeDtypeStruct((t, 128), F32),
                   jax.ShapeDtypeStruct((t // nblk, N_EXPERTS, 128), F32)],
        scratch_shapes=[pltpu.VMEM((N_EXPERTS, 128), F32), pltpu.VMEM((tpb, 1, tm), jnp.int32)],
        compiler_params=_cparams("parallel", "arbitrary"),
        name="mix_xattn_router",
    )(x2d, rs, at, w_out, g1, b1, wq, km, vm, wo, g2, b2, rw_t, rb_col, tri)


def _pack_bf16_pairs(x):
    k = x.shape[1] // 2
    hi = pltpu.bitcast(x[:, :k].astype(BF16).astype(F32), jnp.uint32)
    lo = pltpu.bitcast(x[:, k:].astype(BF16).astype(F32), jnp.uint32)
    return hi | (lo >> 16)


def _unpack_bf16_pairs(words):
    return jnp.concatenate([pltpu.bitcast(w & jnp.uint32(0xFFFF0000), F32) for w in words]
                           + [pltpu.bitcast(w << 16, F32) for w in words], axis=1)


def _group_weights_kernel(w_ref, o_ref):
    for e in range(EXPERTS_PER_GROUP):
        o_ref[:, e * D_EXPERT:(e + 1) * D_EXPERT] = w_ref[e].astype(BF16)


def _group_weights(w):
    depth = w.shape[0]
    return pl.pallas_call(
        _group_weights_kernel,
        grid=(depth, N_GROUPS),
        in_specs=[pl.BlockSpec((None, EXPERTS_PER_GROUP, D_MODEL, D_EXPERT), lambda l, g: (l, g, 0, 0))],
        out_specs=pl.BlockSpec((None, None, D_MODEL, EXPERTS_PER_GROUP * D_EXPERT), lambda l, g: (l, g, 0, 0)),
        out_shape=jax.ShapeDtypeStruct((depth, N_GROUPS, D_MODEL, EXPERTS_PER_GROUP * D_EXPERT), BF16),
        compiler_params=_cparams("parallel", "parallel"),
        name="group_weights",
    )(w)


def _moe_kernel(cnt_ref, pos_ref, x_ref, gates_ref, wg_ref, wu_ref, wd_ref, g_ref, b_ref,
                o_ref, xg_ref, xs_ref, ys_ref, og_ref, off_ref, ntile_ref, *, ts, k, mp):
    s = pl.program_id(1)
    sg = ts + 8

    @pl.when((s == 0) & (pl.program_id(0) == 0))
    def _():
        xs_ref[...] = jnp.zeros_like(xs_ref)
        xg_ref[...] = jnp.zeros_like(xg_ref)

    @pl.when(s == 0)
    def _():
        start = jnp.int32(0)
        for g in range(N_GROUPS):
            tiles = (cnt_ref[0, g] + (MOE_TILE - 1)) // MOE_TILE
            off_ref[g] = start
            ntile_ref[g] = tiles
            start = start + tiles * MOE_TILE

    @pl.when(s < k)
    def _():
        base = s * ts

        def pack(c):
            rows = slice(c * MOE_TILE, (c + 1) * MOE_TILE)
            words = _pack_bf16_pairs(x_ref[rows, :])
            for j in range(4):
                xg_ref[j * sg + c * MOE_TILE:j * sg + (c + 1) * MOE_TILE, :] = words[:, j * 128:(j + 1) * 128]
            xg_ref[4 * sg + c * MOE_TILE:4 * sg + (c + 1) * MOE_TILE, :] = pltpu.bitcast(
                gates_ref[rows, :], jnp.uint32)

        def scatter(c):
            for tl in range(c * MOE_TILE, (c + 1) * MOE_TILE):
                xs_ref[pl.ds(pos_ref[base + tl], 8, stride=mp), :] = xg_ref[pl.ds(tl, 8, stride=sg), :]

        pack(0)
        for c in range(ts // MOE_TILE):
            if c + 1 < ts // MOE_TILE:
                pack(c + 1)
            scatter(c)

    @pl.when((s >= k) & (s < k + N_GROUPS))
    def _():
        g = s - k
        seg = off_ref[g]

        def row_tiles(tiles):
            r0 = [pl.multiple_of(seg + i * MOE_TILE, MOE_TILE) for i in tiles]
            xb, weight = [], []
            for r in r0:
                words = [xs_ref[pl.ds(c * mp + r, MOE_TILE), :] for c in range(4)]
                gates = pltpu.bitcast(xs_ref[pl.ds(4 * mp + r, MOE_TILE), :], F32)
                xb.append(_unpack_bf16_pairs(words).astype(BF16))
                weight.append(jnp.concatenate([jnp.broadcast_to(gates[:, e:e + 1], (MOE_TILE, D_EXPERT))
                                               for e in range(EXPERTS_PER_GROUP)], axis=1))
            gate_act = [_dot(x, wg_ref[...]) for x in xb]
            up_act = [_dot(x, wu_ref[...]) for x in xb]
            hid = [(_silu(a) * u * w).astype(BF16) for a, u, w in zip(gate_act, up_act, weight)]
            y = [_pack_bf16_pairs(_dot(h, wd_ref[...])) for h in hid]
            for r, yt in zip(r0, y):
                for j in range(4):
                    ys_ref[pl.ds(j * mp + r, MOE_TILE), :] = yt[:, j * 128:(j + 1) * 128]

        def tile_triple(i, carry):
            row_tiles((3 * i, 3 * i + 1, 3 * i + 2))
            return carry

        def tile_pair(i, carry):
            row_tiles((n_tiles - 2, n_tiles - 1))
            return carry

        def tile_last(i, carry):
            row_tiles((n_tiles - 1,))
            return carry

        n_tiles = ntile_ref[g]
        rest = n_tiles % 3
        lax.fori_loop(0, n_tiles // 3, tile_triple, 0)
        lax.fori_loop(0, rest // 2, tile_pair, 0)
        lax.fori_loop(0, rest % 2, tile_last, 0)

    @pl.when(s >= k + N_GROUPS)
    def _():
        sub = s - (k + N_GROUPS)

        base = sub * ts
        n_chunks = ts // MOE_TILE

        def gather(c):
            for tl in range(c * MOE_TILE, (c + 1) * MOE_TILE):
                og_ref[pl.ds(tl, 4, stride=sg), :] = ys_ref[pl.ds(pos_ref[base + tl], 4, stride=mp), :]

        def norm(c):
            rows = slice(c * MOE_TILE, (c + 1) * MOE_TILE)
            ffn = _unpack_bf16_pairs([og_ref[j * sg + c * MOE_TILE:j * sg + (c + 1) * MOE_TILE, :]
                                      for j in range(4)])
            o_ref[rows, :] = _layer_norm(ALPHA * x_ref[rows, :] + ffn, g_ref[...], b_ref[...])

        gather(0)
        for c in range(n_chunks):
            if c + 1 < n_chunks:
                gather(c + 1)
            norm(c)


def _moe(x2, cnt_blk, pos, gate_rows, l, wg, wu, wd, g, b, nblk, ts):
    t = x2.shape[0]
    k = nblk // ts
    steps = 2 * k + N_GROUPS
    mp = nblk + N_GROUPS * MOE_TILE + 8
    lay = lambda a: pl.BlockSpec((None,) + a.shape[1:], lambda i, s: (l,) + (0,) * (a.ndim - 1))
    grp = lambda a: pl.BlockSpec((None, None) + a.shape[2:],
                                 lambda i, s: (l, jnp.clip(s - k, 0, N_GROUPS - 1), 0, 0))
    x_map = lambda i, s: (i * k + jnp.where(s < k, s, jnp.where(s < k + N_GROUPS, k - 1, s - k - N_GROUPS)), 0)
    g_map = lambda i, s: (i * k + jnp.minimum(s, k - 1), 0)
    o_map = lambda i, s: (i * k + jnp.maximum(s - k - N_GROUPS, 0), 0)
    return pl.pallas_call(
        functools.partial(_moe_kernel, ts=ts, k=k, mp=mp),
        grid=(t // nblk, steps),
        in_specs=[pl.BlockSpec((None, 1, N_EXPERTS), lambda i, s: (i, 0, 0), memory_space=pltpu.SMEM),
                  pl.BlockSpec((nblk,), lambda i, s: (i,), memory_space=pltpu.SMEM),
                  pl.BlockSpec((ts, D_MODEL), x_map), pl.BlockSpec((ts, 128), g_map),
                  grp(wg), grp(wu), grp(wd), lay(g), lay(b)],
        out_specs=pl.BlockSpec((ts, D_MODEL), o_map),
        out_shape=jax.ShapeDtypeStruct((t, D_MODEL), F32),
        scratch_shapes=[pltpu.VMEM((8 * (ts + 8), 128), jnp.uint32), pltpu.VMEM((8 * mp, 128), jnp.uint32),
                        pltpu.VMEM((4 * mp, 128), jnp.uint32), pltpu.VMEM((4 * (ts + 8), 128), jnp.uint32),
                        pltpu.SMEM((N_GROUPS,), jnp.int32), pltpu.SMEM((N_GROUPS,), jnp.int32)],
        compiler_params=_cparams("arbitrary", "arbitrary"),
        name="moe_experts",
    )(cnt_blk, pos, x2, gate_rows, wg, wu, wd, g, b)


def _tile(n, pref):
    t = min(n, pref)
    assert n % t == 0, (n, t)
    return t


def kernel(x, mem, positions, w_in, w_out, sgu_ln_g, sgu_ln_b, sgu_w, sgu_b, mla_q_norm_g, mla_w_uq, mla_kv_norm_g, mla_w_ukv, xa_wq, xa_wk, xa_wv, xa_wo, ln_mix_g, ln_mix_b, ln_xa_g, ln_xa_b, ln_moe_g, ln_moe_b, router_w, router_bias, expert_w_gate, expert_w_up, expert_w_down):
    batch, seq, _ = x.shape
    depth = w_in.shape[0]
    t = batch * seq
    assert seq % CHUNK == 0

    w_in_b = w_in.astype(BF16)
    w_in_p = jnp.concatenate(
        [w_in_b[:, :, :1920], jnp.zeros((depth, D_MODEL, 64), BF16), w_in_b[:, :, 1920:1952],
         jnp.zeros((depth, D_MODEL, 32), BF16)], axis=2)
    w_uq_p = jnp.pad(mla_w_uq.astype(BF16).reshape(depth, MLA_Q_RANK, MLA_HEADS, MLA_NOPE + MLA_ROPE),
                     ((0, 0), (0, 0), (0, 0), (0, HEAD_PAD - MLA_NOPE - MLA_ROPE))
                     ).reshape(depth, MLA_Q_RANK, MLA_HEADS * HEAD_PAD)
    ukv = mla_w_ukv.astype(BF16).reshape(depth, MLA_KV_RANK, MLA_HEADS, MLA_NOPE + MLA_V)
    w_uk_p = jnp.pad(ukv[..., :MLA_NOPE], ((0, 0), (0, 0), (0, 0), (0, HEAD_PAD - MLA_NOPE))
                     ).reshape(depth, MLA_KV_RANK, MLA_HEADS * HEAD_PAD)
    w_uv_p = jnp.pad(ukv[..., MLA_NOPE:], ((0, 0), (0, 0), (0, 0), (0, HEAD_PAD - MLA_V))
                     ).reshape(depth, MLA_KV_RANK, MLA_HEADS * HEAD_PAD)
    w_ukv_p = jnp.concatenate([w_uk_p, w_uv_p], axis=2)
    w_out_b = w_out.astype(BF16)
    wq_b, wk_b, wv_b, wo_b = (a.astype(BF16) for a in (xa_wq, xa_wk, xa_wv, xa_wo))
    wg_b = _group_weights(expert_w_gate)
    wu_b = _group_weights(expert_w_up)
    wd_b = expert_w_down.reshape(depth, N_GROUPS, EXPERTS_PER_GROUP * D_EXPERT, D_MODEL).astype(BF16)
    vec = lambda a: a.reshape(depth, 1, a.shape[-1])
    sgu_w_cat = jnp.transpose(sgu_w, (0, 2, 1, 3)).reshape(depth, CHUNK, SGU_GROUPS * CHUNK)
    sgu_b_lane = jnp.repeat(jnp.transpose(sgu_b, (0, 2, 1)), SGU_W // SGU_GROUPS, axis=2)
    rw_t = router_w.T
    rb_col = router_bias.reshape(N_EXPERTS, 1)
    consts = _retention_consts()

    x2d = x.reshape(t, D_MODEL)
    mem2d = mem.reshape(batch * MEM_LEN, D_MODEL)
    cr, sr, cm, sm = _rope_tables(positions, _tile(t // 2, 512))

    tm_proj = _tile(t, 1024)
    tb = _tile(seq, 2048)
    tq = _tile(seq, 1024)
    tm_mix = _tile(seq, 1024)
    nblk = _tile(seq, 4096)
    ts_moe = _tile(nblk, 512)
    tri = (jnp.arange(tm_mix)[:, None] <= jnp.arange(tm_mix)[None, :]).astype(BF16)
    for l in range(depth):
        rs, q, k, v = _projections(x2d, l, w_in_p, w_uq_p, w_ukv_p, vec(sgu_ln_g), vec(sgu_ln_b),
                                   vec(mla_q_norm_g), vec(mla_kv_norm_g), cr, sr, cm, sm, tm_proj)
        retsgu = _retention_sgu(rs, l, consts, sgu_w_cat, sgu_b_lane, batch, seq, tb)
        attn = _flash_attention(q, k, v, batch, seq, tq, 4)
        km, vm = _memory_kv(mem2d, l, wk_b, wv_b, batch)
        x2, pos, gate_rows, cnts = _mix_xa(
            x2d, retsgu, attn, km, vm, l, w_out_b, vec(ln_mix_g), vec(ln_mix_b), wq_b, wo_b,
            vec(ln_xa_g), vec(ln_xa_b), rw_t, rb_col, tri, batch, seq, tm_mix, nblk)
        cnt_blk = cnts[:, :, 0].astype(jnp.int32).reshape(-1, 1, N_EXPERTS)
        x2d = _moe(x2, cnt_blk, pos.reshape(t), gate_rows, l,
                   wg_b, wu_b, wd_b, vec(ln_moe_g), vec(ln_moe_b), nblk, ts_moe)
    return x2d.reshape(batch, seq, D_MODEL)
```
